```python
import math
import jax
import jax.numpy as jnp
from jax import lax
import numpy as np

D_MODEL = 1024
BATCH = 32
SEQ = 2048
DEPTH = 4

CTX_LEN = 256
GRID_W = 64

GDN_HEADS = 6
GDN_DK = 64
GDN_DV = 64
GDN_W = GDN_HEADS * GDN_DV
GDN_QKV = GDN_HEADS * (2 * GDN_DK + GDN_DV)
CONV_K = 5
CHUNK = 64
ATT_HEADS = 6
ATT_KV_HEADS = 2
ATT_DH = 64
ATT_W = ATT_HEADS * ATT_DH
ATT_GROUP = ATT_HEADS // ATT_KV_HEADS
Q_BLOCK = 128
ROPE_THETA = 10000.0
ROPE_PAIRS = ATT_DH // 4
S5_GROUPS = 16
S5_GH = 16
S5_P = 64
S5_W = S5_GROUPS * S5_GH
D_MIX = GDN_W + ATT_W + S5_W
D_FF = 2816
N_MOD = 9
ALPHA = (2.0 * DEPTH) ** 0.25
BETA_INIT = (8.0 * DEPTH) ** -0.25
EPS = 1e-6

OFF_GDN_Z = GDN_QKV
OFF_GDN_B = OFF_GDN_Z + GDN_W
OFF_GDN_A = OFF_GDN_B + 2 * GDN_HEADS
OFF_ATT_Q = OFF_GDN_A + 2 * GDN_HEADS
OFF_ATT_K = OFF_ATT_Q + ATT_W
OFF_ATT_V = OFF_ATT_K + ATT_KV_HEADS * ATT_DH
OFF_S5 = OFF_ATT_V + ATT_KV_HEADS * ATT_DH
IN_COLS = OFF_S5 + S5_W
IN_CUTS = (OFF_GDN_Z, OFF_GDN_B, OFF_GDN_A, OFF_ATT_Q, OFF_ATT_K, OFF_ATT_V, OFF_S5)

kernel_name = 'hybrid_gdn_gqa_s5_dit_layer'


def _layer_norm(x, g, b):
    xf = x.astype(jnp.float32)
    mu = jnp.mean(xf, axis=-1, keepdims=True)
    xc = xf - mu
    var = jnp.mean(xc * xc, axis=-1, keepdims=True)
    return (xc * lax.rsqrt(var + EPS) * g.astype(jnp.float32) + b.astype(jnp.float32)).astype(x.dtype)


def _rms_norm(x, w):
    xf = x.astype(jnp.float32)
    y = xf * lax.rsqrt(jnp.mean(xf * xf, axis=-1, keepdims=True) + EPS) * w.astype(jnp.float32)
    return y.astype(x.dtype)


def _l2_normalize(x):
    return x * lax.rsqrt(jnp.sum(x * x, axis=-1, keepdims=True) + EPS)


def _swiglu(h, w1, w3, w2):
    return (jax.nn.silu(h @ w1) * (h @ w3)) @ w2


def _post_norm_residual(x, y, gate, res_w, g, b):
    return _layer_norm(ALPHA * x + res_w * gate * y, g, b)


def _dwconv_centred(x, w):
    pad = CONV_K // 2
    return lax.conv_general_dilated(
        x, w[:, None, :].astype(x.dtype), window_strides=(1,), padding=[(pad, pad)],
        dimension_numbers=('NWC', 'WIO', 'NWC'), feature_group_count=x.shape[-1])


def _gated_delta_chunked(q, k, v, g, beta, s0):
    Bn, H, T, dk = q.shape
    dv = v.shape[-1]
    n = T // CHUNK
    q = (q * dk ** -0.5).reshape(Bn, H, n, CHUNK, dk)
    k = k.reshape(Bn, H, n, CHUNK, dk)
    v = v.reshape(Bn, H, n, CHUNK, dv)
    g = jnp.cumsum(g.reshape(Bn, H, n, CHUNK), axis=-1)
    beta = beta.reshape(Bn, H, n, CHUNK)[..., None]
    incl = jnp.tril(jnp.ones((CHUNK, CHUNK), dtype=bool))
    strict = jnp.tril(jnp.ones((CHUNK, CHUNK), dtype=bool), -1)
    diff = g[..., :, None] - g[..., None, :]
    decay = jnp.where(incl, jnp.exp(jnp.where(incl, diff, 0.0)), 0.0)
    kb = k * beta
    lower = jnp.where(strict, jnp.einsum('bhnid,bhnjd->bhnij', kb, k) * decay, 0.0)
    eye = jnp.eye(CHUNK, dtype=q.dtype)
    rhs = jnp.concatenate([v * beta, kb * jnp.exp(g)[..., None]], axis=-1)
    sol = lax.linalg.triangular_solve(eye + lower, rhs, left_side=True, lower=True, unit_diagonal=True)
    u, w = sol[..., :dv], sol[..., dv:]
    intra = jnp.where(incl, jnp.einsum('bhnid,bhnjd->bhnij', q, k) * decay, 0.0)

    def step(state, inp):
        q_i, k_i, u_i, w_i, g_i, a_i = inp
        v_new = u_i - jnp.einsum('bhcd,bhde->bhce', w_i, state)
        o_i = (jnp.einsum('bhcd,bhde->bhce', q_i * jnp.exp(g_i)[..., None], state)
               + jnp.einsum('bhij,bhje->bhie', a_i, v_new))
        g_last = g_i[..., -1:]
        state = (state * jnp.exp(g_last)[..., None]
                 + jnp.einsum('bhcd,bhce->bhde', k_i * jnp.exp(g_last - g_i)[..., None], v_new))
        return state, o_i

    xs = tuple(jnp.moveaxis(t, 2, 0) for t in (q, k, u, w, g, intra))
    s_final, o = lax.scan(step, s0, xs)
    return jnp.moveaxis(o, 0, 2).reshape(Bn, H, T, dv), s_final


def _gdn_inputs(qkv, b, a, conv_w, a_log, dt_bias):
    Bn, T, _ = qkv.shape
    qkv = jax.nn.silu(_dwconv_centred(qkv, conv_w))
    q, k, v = jnp.split(qkv, [GDN_HEADS * GDN_DK, 2 * GDN_HEADS * GDN_DK], axis=-1)

    def to_heads(t, d):
        return t.reshape(Bn, T, GDN_HEADS, d).transpose(0, 2, 1, 3).astype(jnp.float32)

    def dir_heads(t):
        return t.reshape(Bn, T, 2, GDN_HEADS).transpose(2, 0, 3, 1).astype(jnp.float32)

    q = _l2_normalize(to_heads(q, GDN_DK))
    k = _l2_normalize(to_heads(k, GDN_DK))
    v = to_heads(v, GDN_DV)
    beta = jax.nn.sigmoid(dir_heads(b))
    g = (-jnp.exp(a_log.astype(jnp.float32))[:, None, :, None]
         * jax.nn.softplus(dir_heads(a) + dt_bias.astype(jnp.float32)[:, None, :, None]))
    return q, k, v, g, beta


def _gdn_gated_out(o, z, norm_w):
    Bn, H, T, dv = o.shape
    o = _rms_norm(o.transpose(0, 2, 1, 3), norm_w)
    o = o * jax.nn.silu(z.reshape(Bn, T, H, dv).astype(jnp.float32))
    return o.reshape(Bn, T, H * dv).astype(z.dtype)


def _gdn_group(qkv, z, b, a, qkv_c, z_c, b_c, a_c, conv_w, a_log, dt_bias, norm_w, with_ctx_out):
    q, k, v, g, beta = _gdn_inputs(qkv, b, a, conv_w, a_log, dt_bias)
    qc, kc, vc, gc, betac = _gdn_inputs(qkv_c, b_c, a_c, conv_w, a_log, dt_bias)
    s0 = jnp.zeros((q.shape[0], GDN_HEADS, GDN_DK, GDN_DV), jnp.float32)

    def flip(t):
        return jnp.flip(t, axis=2)

    oc_f, sc_f = _gated_delta_chunked(qc, kc, vc, gc[0], betac[0], s0)
    o_f, _ = _gated_delta_chunked(q, k, v, g[0], beta[0], sc_f)
    oc_b, sc_b = _gated_delta_chunked(flip(qc), flip(kc), flip(vc), flip(gc[1]), flip(betac[1]), s0)
    o_b, _ = _gated_delta_chunked(flip(q), flip(k), flip(v), flip(g[1]), flip(beta[1]), sc_b)
    out = _gdn_gated_out(o_f + flip(o_b), z, norm_w)
    out_c = _gdn_gated_out(oc_f + flip(oc_b), z_c, norm_w) if with_ctx_out else None
    return out, out_c


def _axial_rope_tables(rows):
    row = jnp.repeat(jnp.arange(rows), GRID_W)
    col = jnp.tile(jnp.arange(GRID_W), rows)
    inv_freq = ROPE_THETA ** (-jnp.arange(ROPE_PAIRS, dtype=jnp.float32) / ROPE_PAIRS)
    ang = jnp.stack([row, col], axis=-1).astype(jnp.float32)[..., None] * inv_freq
    return jnp.cos(ang), jnp.sin(ang)


def _rope_2d(x, cos, sin):
    shp = x.shape
    xr = x.reshape(*shp[:-1], 2, 2, ROPE_PAIRS).astype(jnp.float32)
    x1, x2 = xr[..., 0, :], xr[..., 1, :]
    bshape = (shp[1],) + (1,) * (x.ndim - 3) + (2, ROPE_PAIRS)
    c, s = cos.reshape(bshape), sin.reshape(bshape)
    out = jnp.stack([x1 * c - x2 * s, x2 * c + x1 * s], axis=-2)
    return out.reshape(shp).astype(x.dtype)


def _attend(qb, keys, vals):
    s = jnp.einsum('bqhgd,bkhd->bhgqk', qb, keys).astype(jnp.float32) * (ATT_DH ** -0.5)
    p = jax.nn.softmax(s, axis=-1).astype(vals.dtype)
    return jnp.einsum('bhgqk,bkhd->bqhgd', p, vals)


def _attention_group(q, k, v, q_c, k_c, v_c, q_norm_w, k_norm_w, cos, sin, with_ctx_out):
    Bn, T, _ = q.shape
    Tc = q_c.shape[1]
    q = _rope_2d(_rms_norm(q.reshape(Bn, T, ATT_KV_HEADS, ATT_GROUP, ATT_DH), q_norm_w), cos, sin)
    k = _rope_2d(_rms_norm(k.reshape(Bn, T, ATT_KV_HEADS, ATT_DH), k_norm_w), cos, sin)
    v = v.reshape(Bn, T, ATT_KV_HEADS, ATT_DH)
    qc = _rms_norm(q_c.reshape(Bn, Tc, ATT_KV_HEADS, ATT_GROUP, ATT_DH), q_norm_w)
    kc = _rms_norm(k_c.reshape(Bn, Tc, ATT_KV_HEADS, ATT_DH), k_norm_w)
    vc = v_c.reshape(Bn, Tc, ATT_KV_HEADS, ATT_DH)
    keys = jnp.concatenate([kc, k], axis=1)
    vals = jnp.concatenate([vc, v], axis=1)
    nb = T // Q_BLOCK
    q_blocks = q.reshape(Bn, nb, Q_BLOCK, ATT_KV_HEADS, ATT_GROUP, ATT_DH).transpose(1, 0, 2, 3, 4, 5)
    o = lax.map(lambda qb: _attend(qb, keys, vals), q_blocks)
    o = o.transpose(1, 0, 2, 3, 4, 5).reshape(Bn, T, ATT_W)
    o_c = _attend(qc, kc, vc).reshape(Bn, Tc, ATT_W) if with_ctx_out else None
    return o, o_c


def _s5_discretize(lam_re, lam_im, log_dt, b_re, b_im):
    lam = lax.complex(lam_re.astype(jnp.float32), lam_im.astype(jnp.float32))
    dt = jnp.exp(log_dt.astype(jnp.float32))[:, None]
    lam_bar = jnp.exp(lam * dt)
    b = lax.complex(b_re.astype(jnp.float32), b_im.astype(jnp.float32))
    b_bar = ((lam_bar - 1.0) / lam)[..., None] * b
    return lam_bar, b_bar


def _s5_scan(u, lam_bar, b_bar, h0):
    bu = jnp.einsum('gph,btgh->tbgp', b_bar, u.astype(jnp.complex64))
    if h0 is not None:
        bu = bu.at[0].add(lam_bar * h0)
    a = jnp.broadcast_to(lam_bar, (bu.shape[0], 1) + lam_bar.shape)

    def combine(e1, e2):
        a1, b1 = e1
        a2, b2 = e2
        return a2 * a1, a2 * b1 + b2

    _, h = lax.associative_scan(combine, (a, bu), axis=0)
    return h


def _s5_readout(h, c_re, c_im):
    return (jnp.einsum('ghp,tbgp->btgh', c_re.astype(jnp.float32), h.real)
            - jnp.einsum('ghp,tbgp->btgh', c_im.astype(jnp.float32), h.imag))


def _s5_group(u, u_c, lam_re, lam_im, log_dt, b_re, b_im, c_re, c_im, d_skip, glu_w, glu_b, with_ctx_out):
    def grp(t):
        return t.reshape(t.shape[0], t.shape[1], S5_GROUPS, S5_GH).astype(jnp.float32)

    ul, uc = grp(u), grp(u_c)
    dg = d_skip.astype(jnp.float32).reshape(S5_GROUPS, S5_GH)
    yl, yc = dg * ul, dg * uc
    for d in range(2):
        lam_bar, b_bar = _s5_discretize(lam_re[d], lam_im[d], log_dt[d], b_re[d], b_im[d])
        if d == 1:
            fl = lambda t: jnp.flip(t, axis=1)
        else:
            fl = lambda t: t
        hc = _s5_scan(fl(uc), lam_bar, b_bar, None)
        hl = _s5_scan(fl(ul), lam_bar, b_bar, hc[-1])
        yl = yl + fl(_s5_readout(hl, c_re[d], c_im[d]))
        if with_ctx_out:
            yc = yc + fl(_s5_readout(hc, c_re[d], c_im[d]))

    def glu(y):
        zz = jax.nn.gelu(y.reshape(y.shape[0], y.shape[1], S5_W))
        return (zz * jax.nn.sigmoid(zz @ glu_w.astype(jnp.float32) + glu_b.astype(jnp.float32))).astype(u.dtype)

    return glu(yl), (glu(yc) if with_ctx_out else None)


def _modulate(t, m, i):
    return t * (1.0 + m[:, 3 * i + 1]) + m[:, 3 * i]


def _trunk_layer(x, xc, mod, mod_c, cos, sin, ln_g, ln_b, ffn_w1, ffn_w3, ffn_w2, w_in, w_out,
                 gdn_conv_w, gdn_a_log, gdn_dt_bias, gdn_norm_w, q_norm_w, k_norm_w,
                 s5_lam_re, s5_lam_im, s5_log_dt, s5_b_re, s5_b_im, s5_c_re, s5_c_im, s5_d, glu_w, glu_b,
                 with_ctx_out):
    x = _post_norm_residual(x, _swiglu(_modulate(x, mod, 0), ffn_w1[0], ffn_w3[0], ffn_w2[0]),
                            mod[:, 2], 0.5, ln_g[0], ln_b[0])
    xc = _post_norm_residual(xc, _swiglu(_modulate(xc, mod_c, 0), ffn_w1[0], ffn_w3[0], ffn_w2[0]),
                             mod_c[:, 2], 0.5, ln_g[0], ln_b[0])
    h, hc = _modulate(x, mod, 1), _modulate(xc, mod_c, 1)
    g_qkv, g_z, g_b, g_a, a_q, a_k, a_v, s_u = jnp.split(h @ w_in, list(IN_CUTS), axis=-1)
    c_qkv, c_z, c_b, c_a, c_q, c_k, c_v, c_u = jnp.split(hc @ w_in, list(IN_CUTS), axis=-1)
    o_gdn, oc_gdn = _gdn_group(g_qkv, g_z, g_b, g_a, c_qkv, c_z, c_b, c_a,
                               gdn_conv_w, gdn_a_log, gdn_dt_bias, gdn_norm_w, with_ctx_out)
    o_att, oc_att = _attention_group(a_q, a_k, a_v, c_q, c_k, c_v, q_norm_w, k_norm_w, cos, sin, with_ctx_out)
    o_s5, oc_s5 = _s5_group(s_u, c_u, s5_lam_re, s5_lam_im, s5_log_dt, s5_b_re, s5_b_im,
                            s5_c_re, s5_c_im, s5_d, glu_w, glu_b, with_ctx_out)
    y = jnp.concatenate([o_gdn, o_att, o_s5], axis=-1) @ w_out
    x = _post_norm_residual(x, y, mod[:, 5], 1.0, ln_g[1], ln_b[1])
    x = _post_norm_residual(x, _swiglu(_modulate(x, mod, 2), ffn_w1[1], ffn_w3[1], ffn_w2[1]),
                            mod[:, 8], 0.5, ln_g[2], ln_b[2])
    if not with_ctx_out:
        return x, None
    yc = jnp.concatenate([oc_gdn, oc_att, oc_s5], axis=-1) @ w_out
    xc = _post_norm_residual(xc, yc, mod_c[:, 5], 1.0, ln_g[1], ln_b[1])
    xc = _post_norm_residual(xc, _swiglu(_modulate(xc, mod_c, 2), ffn_w1[1], ffn_w3[1], ffn_w2[1]),
                             mod_c[:, 8], 0.5, ln_g[2], ln_b[2])
    return x, xc


def _fwd_setup_inputs(seed: int = 0) -> dict:
    key = jax.random.key(seed)
    ks = jax.random.split(key, 32)
    f32 = jnp.float32
    L, D = DEPTH, D_MODEL

    def nrm(k, shape, s):
        return jax.random.normal(k, shape, f32) * s

    def unif(k, shape, lo, hi):
        return jax.random.uniform(k, shape, f32, lo, hi)

    dt_g = jnp.exp(unif(ks[15], (L, 2, GDN_HEADS), math.log(1e-3), math.log(1e-1)))
    return {
        'x': nrm(ks[0], (BATCH, SEQ, D), 1.0),
        'c': nrm(ks[1], (BATCH, D), 1.0),
        'ctx': nrm(ks[2], (BATCH, CTX_LEN, D), 1.0),
        'c_ctx': nrm(ks[3], (D,), 1.0),
        'w_ada': nrm(ks[4], (L, D, N_MOD * D), 0.5 * D ** -0.5),
        'b_ada': nrm(ks[5], (L, N_MOD * D), 0.02),
        'ln_g': 1.0 + nrm(ks[6], (L, 3, D), 0.05),
        'ln_b': nrm(ks[7], (L, 3, D), 0.02),
        'ffn_w1': nrm(ks[8], (L, 2, D, D_FF), D ** -0.5),
        'ffn_w3': nrm(ks[9], (L, 2, D, D_FF), D ** -0.5),
        'ffn_w2': nrm(ks[10], (L, 2, D_FF, D), D_FF ** -0.5 * BETA_INIT),
        'w_in': nrm(ks[11], (L, D, IN_COLS), D ** -0.5),
        'w_out': nrm(ks[12], (L, D_MIX, D), D_MIX ** -0.5 * BETA_INIT),
        'gdn_conv_w': nrm(ks[13], (L, CONV_K, GDN_QKV), CONV_K ** -0.5),
        'gdn_a_log': jnp.log(unif(ks[14], (L, 2, GDN_HEADS), 1.0, 16.0)),
        'gdn_dt_bias': dt_g + jnp.log(-jnp.expm1(-dt_g)),
        'gdn_norm_w': 1.0 + nrm(ks[16], (L, GDN_DV), 0.05),
        'q_norm_w': 1.0 + nrm(ks[17], (L, ATT_DH), 0.05),
        'k_norm_w': 1.0 + nrm(ks[18], (L, ATT_DH), 0.05),
        's5_lam_re': -0.5 * (1.0 + nrm(ks[19], (L, 2, S5_GROUPS, S5_P), 0.01)),
        's5_lam_im': jnp.pi * jnp.arange(S5_P, dtype=f32) + nrm(ks[20], (L, 2, S5_GROUPS, S5_P), 0.01),
        's5_log_dt': unif(ks[21], (L, 2, S5_GROUPS), math.log(1e-3), math.log(1e-1)),
        's5_b_re': nrm(ks[22], (L, 2, S5_GROUPS, S5_P, S5_GH), (2.0 * S5_GH) ** -0.5),
        's5_b_im': nrm(ks[23], (L, 2, S5_GROUPS, S5_P, S5_GH), (2.0 * S5_GH) ** -0.5),
        's5_c_re': nrm(ks[24], (L, 2, S5_GROUPS, S5_GH, S5_P), S5_P ** -0.5),
        's5_c_im': nrm(ks[25], (L, 2, S5_GROUPS, S5_GH, S5_P), S5_P ** -0.5),
        's5_d': nrm(ks[26], (L, S5_W), 1.0),
        'glu_w': nrm(ks[27], (L, S5_W, S5_W), S5_W ** -0.5),
        'glu_b': nrm(ks[28], (L, S5_W), 0.02),
    }


def _fwd_reference(x, c, ctx, c_ctx, w_ada, b_ada, ln_g, ln_b, ffn_w1, ffn_w3, ffn_w2, w_in, w_out,
              gdn_conv_w, gdn_a_log, gdn_dt_bias, gdn_norm_w, q_norm_w, k_norm_w,
              s5_lam_re, s5_lam_im, s5_log_dt, s5_b_re, s5_b_im, s5_c_re, s5_c_im, s5_d, glu_w, glu_b):
    Bn, T, D = x.shape
    ROWS = T // GRID_W
    cos, sin = _axial_rope_tables(ROWS)
    sc = jax.nn.silu(c)
    scc = jax.nn.silu(c_ctx)
    xc = ctx
    for layer in range(DEPTH):
        mod = (sc @ w_ada[layer] + b_ada[layer]).reshape(Bn, N_MOD, 1, D)
        mod_c = (scc @ w_ada[layer] + b_ada[layer]).reshape(1, N_MOD, 1, D)
        x, xc = _trunk_layer(
            x, xc, mod, mod_c, cos, sin, ln_g[layer], ln_b[layer],
            ffn_w1[layer], ffn_w3[layer], ffn_w2[layer], w_in[layer], w_out[layer],
            gdn_conv_w[layer], gdn_a_log[layer], gdn_dt_bias[layer], gdn_norm_w[layer],
            q_norm_w[layer], k_norm_w[layer],
            s5_lam_re[layer], s5_lam_im[layer], s5_log_dt[layer], s5_b_re[layer], s5_b_im[layer],
            s5_c_re[layer], s5_c_im[layer], s5_d[layer], glu_w[layer], glu_b[layer],
            with_ctx_out=layer < DEPTH - 1)
    return x


import jax as _jax
import jax.numpy as _jnp

TWIN_FORMAT = 'train_step'
FWD_PARAMS = ['x', 'c', 'ctx', 'c_ctx', 'w_ada', 'b_ada', 'ln_g', 'ln_b', 'ffn_w1', 'ffn_w3', 'ffn_w2', 'w_in', 'w_out', 'gdn_conv_w', 'gdn_a_log', 'gdn_dt_bias', 'gdn_norm_w', 'q_norm_w', 'k_norm_w', 's5_lam_re', 's5_lam_im', 's5_log_dt', 's5_b_re', 's5_b_im', 's5_c_re', 's5_c_im', 's5_d', 'glu_w', 'glu_b']
TWIN_WEIGHTS = ['c_ctx', 'w_ada', 'b_ada', 'ln_g', 'ln_b', 'ffn_w1', 'ffn_w3', 'ffn_w2', 'w_in', 'w_out', 'gdn_conv_w', 'gdn_a_log', 'gdn_dt_bias', 'gdn_norm_w', 'q_norm_w', 'k_norm_w', 's5_lam_re', 's5_lam_im', 's5_log_dt', 's5_b_re', 's5_b_im', 's5_c_re', 's5_c_im', 's5_d', 'glu_w', 'glu_b']
TWIN_DIFF_INPUT = 'x'
TWIN_INPUTS = ['x', 'c', 'ctx', 'c_ctx', 'w_ada', 'b_ada', 'ln_g', 'ln_b', 'ffn_w1', 'ffn_w3', 'ffn_w2', 'w_in', 'w_out', 'gdn_conv_w', 'gdn_a_log', 'gdn_dt_bias', 'gdn_norm_w', 'q_norm_w', 'k_norm_w', 's5_lam_re', 's5_lam_im', 's5_log_dt', 's5_b_re', 's5_b_im', 's5_c_re', 's5_c_im', 's5_d', 'glu_w', 'glu_b', 'loss_target', 'm_c_ctx', 'm_w_ada', 'm_b_ada', 'm_ln_g', 'm_ln_b', 'm_ffn_w1', 'm_ffn_w3', 'm_ffn_w2', 'm_w_in', 'm_w_out', 'm_gdn_conv_w', 'm_gdn_a_log', 'm_gdn_dt_bias', 'm_gdn_norm_w', 'm_q_norm_w', 'm_k_norm_w', 'm_s5_lam_re', 'm_s5_lam_im', 'm_s5_log_dt', 'm_s5_b_re', 'm_s5_b_im', 'm_s5_c_re', 'm_s5_c_im', 'm_s5_d', 'm_glu_w', 'm_glu_b', 'v_c_ctx', 'v_w_ada', 'v_b_ada', 'v_ln_g', 'v_ln_b', 'v_ffn_w1', 'v_ffn_w3', 'v_ffn_w2', 'v_w_in', 'v_w_out', 'v_gdn_conv_w', 'v_gdn_a_log', 'v_gdn_dt_bias', 'v_gdn_norm_w', 'v_q_norm_w', 'v_k_norm_w', 'v_s5_lam_re', 'v_s5_lam_im', 'v_s5_log_dt', 'v_s5_b_re', 'v_s5_b_im', 'v_s5_c_re', 'v_s5_c_im', 'v_s5_d', 'v_glu_w', 'v_glu_b']
TWIN_OUTPUTS = ['loss', 'grad_x', 'grad_c_ctx', 'grad_w_ada', 'grad_b_ada', 'grad_ln_g', 'grad_ln_b', 'grad_ffn_w1', 'grad_ffn_w3', 'grad_ffn_w2', 'grad_w_in', 'grad_w_out', 'grad_gdn_conv_w', 'grad_gdn_a_log', 'grad_gdn_dt_bias', 'grad_gdn_norm_w', 'grad_q_norm_w', 'grad_k_norm_w', 'grad_s5_lam_re', 'grad_s5_lam_im', 'grad_s5_log_dt', 'grad_s5_b_re', 'grad_s5_b_im', 'grad_s5_c_re', 'grad_s5_c_im', 'grad_s5_d', 'grad_glu_w', 'grad_glu_b', 'delta_c_ctx', 'delta_w_ada', 'delta_b_ada', 'delta_ln_g', 'delta_ln_b', 'delta_ffn_w1', 'delta_ffn_w3', 'delta_ffn_w2', 'delta_w_in', 'delta_w_out', 'delta_gdn_conv_w', 'delta_gdn_a_log', 'delta_gdn_dt_bias', 'delta_gdn_norm_w', 'delta_q_norm_w', 'delta_k_norm_w', 'delta_s5_lam_re', 'delta_s5_lam_im', 'delta_s5_log_dt', 'delta_s5_b_re', 'delta_s5_b_im', 'delta_s5_c_re', 'delta_s5_c_im', 'delta_s5_d', 'delta_glu_w', 'delta_glu_b', 'new_m_c_ctx', 'new_m_w_ada', 'new_m_b_ada', 'new_m_ln_g', 'new_m_ln_b', 'new_m_ffn_w1', 'new_m_ffn_w3', 'new_m_ffn_w2', 'new_m_w_in', 'new_m_w_out', 'new_m_gdn_conv_w', 'new_m_gdn_a_log', 'new_m_gdn_dt_bias', 'new_m_gdn_norm_w', 'new_m_q_norm_w', 'new_m_k_norm_w', 'new_m_s5_lam_re', 'new_m_s5_lam_im', 'new_m_s5_log_dt', 'new_m_s5_b_re', 'new_m_s5_b_im', 'new_m_s5_c_re', 'new_m_s5_c_im', 'new_m_s5_d', 'new_m_glu_w', 'new_m_glu_b', 'new_v_c_ctx', 'new_v_w_ada', 'new_v_b_ada', 'new_v_ln_g', 'new_v_ln_b', 'new_v_ffn_w1', 'new_v_ffn_w3', 'new_v_ffn_w2', 'new_v_w_in', 'new_v_w_out', 'new_v_gdn_conv_w', 'new_v_gdn_a_log', 'new_v_gdn_dt_bias', 'new_v_gdn_norm_w', 'new_v_q_norm_w', 'new_v_k_norm_w', 'new_v_s5_lam_re', 'new_v_s5_lam_im', 'new_v_s5_log_dt', 'new_v_s5_b_re', 'new_v_s5_b_im', 'new_v_s5_c_re', 'new_v_s5_c_im', 'new_v_s5_d', 'new_v_glu_w', 'new_v_glu_b']
TWIN_LEAF_KINDS = {'loss': 'loss', 'grad_x': 'grad_x', 'grad_c_ctx': 'grad_w', 'grad_w_ada': 'grad_w', 'grad_b_ada': 'grad_w', 'grad_ln_g': 'grad_w', 'grad_ln_b': 'grad_w', 'grad_ffn_w1': 'grad_w', 'grad_ffn_w3': 'grad_w', 'grad_ffn_w2': 'grad_w', 'grad_w_in': 'grad_w', 'grad_w_out': 'grad_w', 'grad_gdn_conv_w': 'grad_w', 'grad_gdn_a_log': 'grad_w', 'grad_gdn_dt_bias': 'grad_w', 'grad_gdn_norm_w': 'grad_w', 'grad_q_norm_w': 'grad_w', 'grad_k_norm_w': 'grad_w', 'grad_s5_lam_re': 'grad_w', 'grad_s5_lam_im': 'grad_w', 'grad_s5_log_dt': 'grad_w', 'grad_s5_b_re': 'grad_w', 'grad_s5_b_im': 'grad_w', 'grad_s5_c_re': 'grad_w', 'grad_s5_c_im': 'grad_w', 'grad_s5_d': 'grad_w', 'grad_glu_w': 'grad_w', 'grad_glu_b': 'grad_w', 'delta_c_ctx': 'delta_w', 'delta_w_ada': 'delta_w', 'delta_b_ada': 'delta_w', 'delta_ln_g': 'delta_w', 'delta_ln_b': 'delta_w', 'delta_ffn_w1': 'delta_w', 'delta_ffn_w3': 'delta_w', 'delta_ffn_w2': 'delta_w', 'delta_w_in': 'delta_w', 'delta_w_out': 'delta_w', 'delta_gdn_conv_w': 'delta_w', 'delta_gdn_a_log': 'delta_w', 'delta_gdn_dt_bias': 'delta_w', 'delta_gdn_norm_w': 'delta_w', 'delta_q_norm_w': 'delta_w', 'delta_k_norm_w': 'delta_w', 'delta_s5_lam_re': 'delta_w', 'delta_s5_lam_im': 'delta_w', 'delta_s5_log_dt': 'delta_w', 'delta_s5_b_re': 'delta_w', 'delta_s5_b_im': 'delta_w', 'delta_s5_c_re': 'delta_w', 'delta_s5_c_im': 'delta_w', 'delta_s5_d': 'delta_w', 'delta_glu_w': 'delta_w', 'delta_glu_b': 'delta_w', 'new_m_c_ctx': 'new_m', 'new_m_w_ada': 'new_m', 'new_m_b_ada': 'new_m', 'new_m_ln_g': 'new_m', 'new_m_ln_b': 'new_m', 'new_m_ffn_w1': 'new_m', 'new_m_ffn_w3': 'new_m', 'new_m_ffn_w2': 'new_m', 'new_m_w_in': 'new_m', 'new_m_w_out': 'new_m', 'new_m_gdn_conv_w': 'new_m', 'new_m_gdn_a_log': 'new_m', 'new_m_gdn_dt_bias': 'new_m', 'new_m_gdn_norm_w': 'new_m', 'new_m_q_norm_w': 'new_m', 'new_m_k_norm_w': 'new_m', 'new_m_s5_lam_re': 'new_m', 'new_m_s5_lam_im': 'new_m', 'new_m_s5_log_dt': 'new_m', 'new_m_s5_b_re': 'new_m', 'new_m_s5_b_im': 'new_m', 'new_m_s5_c_re': 'new_m', 'new_m_s5_c_im': 'new_m', 'new_m_s5_d': 'new_m', 'new_m_glu_w': 'new_m', 'new_m_glu_b': 'new_m', 'new_v_c_ctx': 'new_v', 'new_v_w_ada': 'new_v', 'new_v_b_ada': 'new_v', 'new_v_ln_g': 'new_v', 'new_v_ln_b': 'new_v', 'new_v_ffn_w1': 'new_v', 'new_v_ffn_w3': 'new_v', 'new_v_ffn_w2': 'new_v', 'new_v_w_in': 'new_v', 'new_v_w_out': 'new_v', 'new_v_gdn_conv_w': 'new_v', 'new_v_gdn_a_log': 'new_v', 'new_v_gdn_dt_bias': 'new_v', 'new_v_gdn_norm_w': 'new_v', 'new_v_q_norm_w': 'new_v', 'new_v_k_norm_w': 'new_v', 'new_v_s5_lam_re': 'new_v', 'new_v_s5_lam_im': 'new_v', 'new_v_s5_log_dt': 'new_v', 'new_v_s5_b_re': 'new_v', 'new_v_s5_b_im': 'new_v', 'new_v_s5_c_re': 'new_v', 'new_v_s5_c_im': 'new_v', 'new_v_s5_d': 'new_v', 'new_v_glu_w': 'new_v', 'new_v_glu_b': 'new_v'}


def _forward(args):
    return _fwd_reference(*[args[k] for k in FWD_PARAMS])


def _output_shape():
    out = _jax.eval_shape(lambda: _forward(_fwd_setup_inputs(0)))
    return out.shape, out.dtype

N_MICROBATCH = 1
ADAM_LR = 0.001
ADAM_B1 = 0.9
ADAM_B2 = 0.999
ADAM_EPS = 1e-08
ADAM_WD = 0.01
ADAM_STEP = 10
PER_EXAMPLE_BATCH_AXIS = {'x': 0, 'c': 0, 'ctx': 0, 'loss_target': 0}
SHARED_INPUTS = []
_WEIGHT_DTYPES = {'c_ctx': _jnp.float32, 'w_ada': _jnp.float32, 'b_ada': _jnp.float32, 'ln_g': _jnp.float32, 'ln_b': _jnp.float32, 'ffn_w1': _jnp.float32, 'ffn_w3': _jnp.float32, 'ffn_w2': _jnp.float32, 'w_in': _jnp.float32, 'w_out': _jnp.float32, 'gdn_conv_w': _jnp.float32, 'gdn_a_log': _jnp.float32, 'gdn_dt_bias': _jnp.float32, 'gdn_norm_w': _jnp.float32, 'q_norm_w': _jnp.float32, 'k_norm_w': _jnp.float32, 's5_lam_re': _jnp.float32, 's5_lam_im': _jnp.float32, 's5_log_dt': _jnp.float32, 's5_b_re': _jnp.float32, 's5_b_im': _jnp.float32, 's5_c_re': _jnp.float32, 's5_c_im': _jnp.float32, 's5_d': _jnp.float32, 'glu_w': _jnp.float32, 'glu_b': _jnp.float32}
MOMENT_SCALE = {'c_ctx': 4.345942e-03, 'w_ada': 1.198527e-02, 'b_ada': 2.060772e-02, 'ln_g': 2.021442e+01, 'ln_b': 1.211428e+00, 'ffn_w1': 4.364951e-03, 'ffn_w3': 4.269385e-03, 'ffn_w2': 1.684248e-02, 'w_in': 1.031934e-02, 'w_out': 2.594863e-02, 'gdn_conv_w': 1.121381e-02, 'gdn_a_log': 2.545525e-02, 'gdn_dt_bias': 2.512555e-02, 'gdn_norm_w': 3.336662e-02, 'q_norm_w': 3.845689e-03, 'k_norm_w': 3.947773e-03, 's5_lam_re': 8.699366e-04, 's5_lam_im': 9.205931e-04, 's5_log_dt': 5.061282e-01, 's5_b_re': 4.831868e-04, 's5_b_im': 4.918247e-04, 's5_c_re': 7.060039e-04, 's5_c_im': 6.441343e-04, 's5_d': 8.969455e-03, 'glu_w': 2.621932e-03, 'glu_b': 3.816913e-03}


def _to_microbatches(a, axis):
    t = _jnp.moveaxis(a, axis, 0)
    t = t.reshape((N_MICROBATCH, t.shape[0] // N_MICROBATCH) + t.shape[1:])
    return _jnp.moveaxis(t, 1, axis + 1)


def setup_inputs(seed: int = 0) -> dict:
    inp = _fwd_setup_inputs(seed)
    key = _jax.random.fold_in(_jax.random.key(seed), 7919)
    shape, _ = _output_shape()
    out = dict(inp)
    out["loss_target"] = _jax.random.normal(_jax.random.fold_in(key, 0), shape, _jnp.float32)
    for i, name in enumerate(TWIN_WEIGHTS):
        w = inp[name].astype(_jnp.float32)
        if MOMENT_SCALE is None:
            s = _jnp.sqrt(_jnp.mean(_jnp.square(w)) + 1e-30)
        else:
            s = MOMENT_SCALE[name]
        km, kv = _jax.random.split(_jax.random.fold_in(key, i + 1))
        out[name] = w
        out["m_" + name] = s * _jax.random.normal(km, w.shape, _jnp.float32)
        out["v_" + name] = (s * s) * _jax.random.uniform(kv, w.shape, _jnp.float32, 0.5, 1.5)
    if N_MICROBATCH > 1:
        for name, axis in PER_EXAMPLE_BATCH_AXIS.items():
            out[name] = _to_microbatches(out[name], axis)
    return {'x': out['x'], 'c': out['c'], 'ctx': out['ctx'], 'c_ctx': out['c_ctx'], 'w_ada': out['w_ada'], 'b_ada': out['b_ada'], 'ln_g': out['ln_g'], 'ln_b': out['ln_b'], 'ffn_w1': out['ffn_w1'], 'ffn_w3': out['ffn_w3'], 'ffn_w2': out['ffn_w2'], 'w_in': out['w_in'], 'w_out': out['w_out'], 'gdn_conv_w': out['gdn_conv_w'], 'gdn_a_log': out['gdn_a_log'], 'gdn_dt_bias': out['gdn_dt_bias'], 'gdn_norm_w': out['gdn_norm_w'], 'q_norm_w': out['q_norm_w'], 'k_norm_w': out['k_norm_w'], 's5_lam_re': out['s5_lam_re'], 's5_lam_im': out['s5_lam_im'], 's5_log_dt': out['s5_log_dt'], 's5_b_re': out['s5_b_re'], 's5_b_im': out['s5_b_im'], 's5_c_re': out['s5_c_re'], 's5_c_im': out['s5_c_im'], 's5_d': out['s5_d'], 'glu_w': out['glu_w'], 'glu_b': out['glu_b'], 'loss_target': out['loss_target'], 'm_c_ctx': out['m_c_ctx'], 'm_w_ada': out['m_w_ada'], 'm_b_ada': out['m_b_ada'], 'm_ln_g': out['m_ln_g'], 'm_ln_b': out['m_ln_b'], 'm_ffn_w1': out['m_ffn_w1'], 'm_ffn_w3': out['m_ffn_w3'], 'm_ffn_w2': out['m_ffn_w2'], 'm_w_in': out['m_w_in'], 'm_w_out': out['m_w_out'], 'm_gdn_conv_w': out['m_gdn_conv_w'], 'm_gdn_a_log': out['m_gdn_a_log'], 'm_gdn_dt_bias': out['m_gdn_dt_bias'], 'm_gdn_norm_w': out['m_gdn_norm_w'], 'm_q_norm_w': out['m_q_norm_w'], 'm_k_norm_w': out['m_k_norm_w'], 'm_s5_lam_re': out['m_s5_lam_re'], 'm_s5_lam_im': out['m_s5_lam_im'], 'm_s5_log_dt': out['m_s5_log_dt'], 'm_s5_b_re': out['m_s5_b_re'], 'm_s5_b_im': out['m_s5_b_im'], 'm_s5_c_re': out['m_s5_c_re'], 'm_s5_c_im': out['m_s5_c_im'], 'm_s5_d': out['m_s5_d'], 'm_glu_w': out['m_glu_w'], 'm_glu_b': out['m_glu_b'], 'v_c_ctx': out['v_c_ctx'], 'v_w_ada': out['v_w_ada'], 'v_b_ada': out['v_b_ada'], 'v_ln_g': out['v_ln_g'], 'v_ln_b': out['v_ln_b'], 'v_ffn_w1': out['v_ffn_w1'], 'v_ffn_w3': out['v_ffn_w3'], 'v_ffn_w2': out['v_ffn_w2'], 'v_w_in': out['v_w_in'], 'v_w_out': out['v_w_out'], 'v_gdn_conv_w': out['v_gdn_conv_w'], 'v_gdn_a_log': out['v_gdn_a_log'], 'v_gdn_dt_bias': out['v_gdn_dt_bias'], 'v_gdn_norm_w': out['v_gdn_norm_w'], 'v_q_norm_w': out['v_q_norm_w'], 'v_k_norm_w': out['v_k_norm_w'], 'v_s5_lam_re': out['v_s5_lam_re'], 'v_s5_lam_im': out['v_s5_lam_im'], 'v_s5_log_dt': out['v_s5_log_dt'], 'v_s5_b_re': out['v_s5_b_re'], 'v_s5_b_im': out['v_s5_b_im'], 'v_s5_c_re': out['v_s5_c_re'], 'v_s5_c_im': out['v_s5_c_im'], 'v_s5_d': out['v_s5_d'], 'v_glu_w': out['v_glu_w'], 'v_glu_b': out['v_glu_b']}


def _loss(weights, diff, rest, loss_target):
    with _jax.named_scope("forward"):
        args = {**rest, TWIN_DIFF_INPUT: diff, **{k: w.astype(_WEIGHT_DTYPES[k]) for k, w in weights.items()}}
        y = _forward(args)
    with _jax.named_scope("loss_head"):
        err = _jnp.square(y.astype(_jnp.float32) - loss_target)
        return 0.5 * _jnp.sum(_jnp.mean(err, axis=-1)) if err.ndim else 0.5 * err


def _adamw(w, g, m, v):
    m = ADAM_B1 * m + (1.0 - ADAM_B1) * g
    v = ADAM_B2 * v + (1.0 - ADAM_B2) * _jnp.square(g)
    m_hat = m / (1.0 - ADAM_B1 ** ADAM_STEP)
    v_hat = v / (1.0 - ADAM_B2 ** ADAM_STEP)
    delta = -ADAM_LR * (m_hat / (_jnp.sqrt(v_hat) + ADAM_EPS) + ADAM_WD * w)
    return delta, m, v


def reference(x, c, ctx, c_ctx, w_ada, b_ada, ln_g, ln_b, ffn_w1, ffn_w3, ffn_w2, w_in, w_out, gdn_conv_w, gdn_a_log, gdn_dt_bias, gdn_norm_w, q_norm_w, k_norm_w, s5_lam_re, s5_lam_im, s5_log_dt, s5_b_re, s5_b_im, s5_c_re, s5_c_im, s5_d, glu_w, glu_b, loss_target, m_c_ctx, m_w_ada, m_b_ada, m_ln_g, m_ln_b, m_ffn_w1, m_ffn_w3, m_ffn_w2, m_w_in, m_w_out, m_gdn_conv_w, m_gdn_a_log, m_gdn_dt_bias, m_gdn_norm_w, m_q_norm_w, m_k_norm_w, m_s5_lam_re, m_s5_lam_im, m_s5_log_dt, m_s5_b_re, m_s5_b_im, m_s5_c_re, m_s5_c_im, m_s5_d, m_glu_w, m_glu_b, v_c_ctx, v_w_ada, v_b_ada, v_ln_g, v_ln_b, v_ffn_w1, v_ffn_w3, v_ffn_w2, v_w_in, v_w_out, v_gdn_conv_w, v_gdn_a_log, v_gdn_dt_bias, v_gdn_norm_w, v_q_norm_w, v_k_norm_w, v_s5_lam_re, v_s5_lam_im, v_s5_log_dt, v_s5_b_re, v_s5_b_im, v_s5_c_re, v_s5_c_im, v_s5_d, v_glu_w, v_glu_b):
    given = dict(x=x, c=c, ctx=ctx, c_ctx=c_ctx, w_ada=w_ada, b_ada=b_ada, ln_g=ln_g, ln_b=ln_b, ffn_w1=ffn_w1, ffn_w3=ffn_w3, ffn_w2=ffn_w2, w_in=w_in, w_out=w_out, gdn_conv_w=gdn_conv_w, gdn_a_log=gdn_a_log, gdn_dt_bias=gdn_dt_bias, gdn_norm_w=gdn_norm_w, q_norm_w=q_norm_w, k_norm_w=k_norm_w, s5_lam_re=s5_lam_re, s5_lam_im=s5_lam_im, s5_log_dt=s5_log_dt, s5_b_re=s5_b_re, s5_b_im=s5_b_im, s5_c_re=s5_c_re, s5_c_im=s5_c_im, s5_d=s5_d, glu_w=glu_w, glu_b=glu_b, loss_target=loss_target, m_c_ctx=m_c_ctx, m_w_ada=m_w_ada, m_b_ada=m_b_ada, m_ln_g=m_ln_g, m_ln_b=m_ln_b, m_ffn_w1=m_ffn_w1, m_ffn_w3=m_ffn_w3, m_ffn_w2=m_ffn_w2, m_w_in=m_w_in, m_w_out=m_w_out, m_gdn_conv_w=m_gdn_conv_w, m_gdn_a_log=m_gdn_a_log, m_gdn_dt_bias=m_gdn_dt_bias, m_gdn_norm_w=m_gdn_norm_w, m_q_norm_w=m_q_norm_w, m_k_norm_w=m_k_norm_w, m_s5_lam_re=m_s5_lam_re, m_s5_lam_im=m_s5_lam_im, m_s5_log_dt=m_s5_log_dt, m_s5_b_re=m_s5_b_re, m_s5_b_im=m_s5_b_im, m_s5_c_re=m_s5_c_re, m_s5_c_im=m_s5_c_im, m_s5_d=m_s5_d, m_glu_w=m_glu_w, m_glu_b=m_glu_b, v_c_ctx=v_c_ctx, v_w_ada=v_w_ada, v_b_ada=v_b_ada, v_ln_g=v_ln_g, v_ln_b=v_ln_b, v_ffn_w1=v_ffn_w1, v_ffn_w3=v_ffn_w3, v_ffn_w2=v_ffn_w2, v_w_in=v_w_in, v_w_out=v_w_out, v_gdn_conv_w=v_gdn_conv_w, v_gdn_a_log=v_gdn_a_log, v_gdn_dt_bias=v_gdn_dt_bias, v_gdn_norm_w=v_gdn_norm_w, v_q_norm_w=v_q_norm_w, v_k_norm_w=v_k_norm_w, v_s5_lam_re=v_s5_lam_re, v_s5_lam_im=v_s5_lam_im, v_s5_log_dt=v_s5_log_dt, v_s5_b_re=v_s5_b_re, v_s5_b_im=v_s5_b_im, v_s5_c_re=v_s5_c_re, v_s5_c_im=v_s5_c_im, v_s5_d=v_s5_d, v_glu_w=v_glu_w, v_glu_b=v_glu_b)
    weights = {n: given[n] for n in TWIN_WEIGHTS}
    shared = {n: given[n] for n in SHARED_INPUTS}
    per_example = {n: given[n] for n in ['x', 'c', 'ctx']}
    grad_fn = _jax.value_and_grad(_loss, argnums=(0, 1))

    def one_microbatch(ex, loss_target):
        ex = dict(ex)
        diff = ex.pop(TWIN_DIFF_INPUT)
        return grad_fn(weights, diff, {**shared, **ex}, loss_target)

    if N_MICROBATCH == 1:
        loss, (grad_w, grad_x) = one_microbatch(per_example, given["loss_target"])
    else:
        def body(carry, xs):
            loss_sum, grad_sum = carry
            l_k, (gw_k, gx_k) = one_microbatch(xs[0], xs[1])
            with _jax.named_scope("update"):
                return (loss_sum + l_k, _jax.tree.map(_jnp.add, grad_sum, gw_k)), gx_k

        init = (_jnp.zeros((), _jnp.float32), _jax.tree.map(_jnp.zeros_like, weights))
        (loss, grad_w), grad_x = _jax.lax.scan(body, init, (per_example, given["loss_target"]))
    with _jax.named_scope("update"):
        delta_w, new_m, new_v = {}, {}, {}
        for n in TWIN_WEIGHTS:
            delta_w[n], new_m[n], new_v[n] = _adamw(weights[n], grad_w[n], given["m_" + n], given["v_" + n])
    return (loss, grad_x, *[grad_w[n] for n in TWIN_WEIGHTS], *[delta_w[n] for n in TWIN_WEIGHTS],
            *[new_m[n] for n in TWIN_WEIGHTS], *[new_v[n] for n in TWIN_WEIGHTS])
```

```python
import functools
import math

import numpy as np
import jax
import jax.numpy as jnp
from jax import lax
from jax.experimental import pallas as pl
from jax.experimental.pallas import tpu as pltpu

F32 = jnp.float32
BF16 = jnp.bfloat16
MESH = pl.DeviceIdType.MESH

NDEV = 8
D = 1024
DFF = 2816
DEPTH = 4
B_LOC = 4
T_CTX = 256
T_LAT = 2048
GRID_W = 64
N_MOD = 9
GDN_H = 6
HD = 64
GDN_QKV = 3 * GDN_H * HD
GDN_W = GDN_H * HD
ATT_HKV = 2
ATT_G = 3
ATT_W = ATT_HKV * ATT_G * HD
ATT_KW = ATT_HKV * HD
S5_G = 16
S5_H = 16
S5_P = 64
S5_W = S5_G * S5_H
S5_N = S5_G * S5_P
IN_COLS = 2456
IN_PAD = 2560
ROPE_THETA = 10000.0
ROPE_PAIRS = 16
ALPHA = (2.0 * 4) ** 0.25
EPS = 1e-6
CHUNK = 64
ADAM_LR, ADAM_B1, ADAM_B2, ADAM_EPS, ADAM_WD, ADAM_STEP = 0.001, 0.9, 0.999, 1e-08, 0.01, 10

TT = 256
ATT_TQ = 128
S5_LB = 256
VMEM_LIMIT = 56 * 1024 * 1024

WEIGHTS = ['c_ctx', 'w_ada', 'b_ada', 'ln_g', 'ln_b', 'ffn_w1', 'ffn_w3', 'ffn_w2', 'w_in', 'w_out', 'gdn_conv_w',
           'gdn_a_log', 'gdn_dt_bias', 'gdn_norm_w', 'q_norm_w', 'k_norm_w', 's5_lam_re', 's5_lam_im', 's5_log_dt',
           's5_b_re', 's5_b_im', 's5_c_re', 's5_c_im', 's5_d', 'glu_w', 'glu_b']
INPUTS = ['x', 'c', 'ctx'] + WEIGHTS + ['loss_target'] + ['m_' + n for n in WEIGHTS] + ['v_' + n for n in WEIGHTS]
BIG = ['ffn_w1', 'ffn_w3', 'ffn_w2', 'w_in', 'w_out']
SMALL_SHARDED = ['ln_g', 'ln_b', 'gdn_conv_w', 'glu_w']
SMALL_REPL = ['gdn_a_log', 'gdn_dt_bias', 'gdn_norm_w', 'q_norm_w', 'k_norm_w', 's5_lam_re', 's5_lam_im',
              's5_log_dt', 's5_b_re', 's5_b_im', 's5_c_re', 's5_c_im', 's5_d', 'glu_b']


def _cparams(sem=None):
    return pltpu.CompilerParams(dimension_semantics=sem, vmem_limit_bytes=VMEM_LIMIT)


def _ntok():
    return T_CTX + T_LAT


def _my_pos():
    return lax.axis_index("x"), lax.axis_index("y"), lax.axis_index("c")


def _my_index():
    x, y, c = _my_pos()
    return 4 * x + 2 * y + c


def _all_gather(shard, name):
    def body(x_ref, out_ref, send_sems, recv_sems, local_sem):
        x, y, c = _my_pos()
        me, sibling = (x, y, c), (x, y, 1 - c)
        chips = [(1 - x, y), (x, 1 - y), (1 - x, 1 - y)]

        def slab(px, py, pc):
            return out_ref.at[4 * px + 2 * py + pc]

        def copy(k, block, to, src=None):
            return pltpu.make_async_remote_copy(
                src_ref=slab(*block) if src is None else src, dst_ref=slab(*block),
                send_sem=send_sems.at[k], recv_sem=recv_sems.at[k], device_id=to, device_id_type=MESH)

        mine = pltpu.make_async_copy(x_ref, slab(*me), local_sem)
        mine.start()
        first = [copy(0, me, sibling, src=x_ref)]
        first += [copy(1 + j, me, (*chip, c), src=x_ref) for j, chip in enumerate(chips)]
        for cp in first:
            cp.start()
        passed = [copy(4 + j, (*chip, c), sibling) for j, chip in enumerate(chips)]
        for j, chip in enumerate(chips):
            copy(1 + j, (*chip, c), me).wait_recv()
            passed[j].start()
        copy(0, sibling, me).wait_recv()
        for j, chip in enumerate(chips):
            copy(4 + j, (*chip, 1 - c), me).wait_recv()
        for cp in first + passed:
            cp.wait_send()
        mine.wait()

    return pl.pallas_call(
        body, name=name,
        out_shape=jax.ShapeDtypeStruct((NDEV,) + shard.shape, shard.dtype),
        in_specs=[pl.BlockSpec(memory_space=pl.ANY)],
        out_specs=pl.BlockSpec(memory_space=pl.ANY),
        scratch_shapes=[pltpu.SemaphoreType.DMA((7,)), pltpu.SemaphoreType.DMA((7,)), pltpu.SemaphoreType.DMA],
    )(shard)


def _all_to_all(pieces, name):
    def body(x_ref, out_ref, send_sems, recv_sems, local_sem):
        x, y, c = _my_pos()
        me_i = 4 * x + 2 * y + c
        mine = pltpu.make_async_copy(x_ref.at[me_i], out_ref.at[me_i], local_sem)
        mine.start()
        sends, recvs = [], []
        for k in range(1, NDEV):
            px = 1 - x if (k >> 2) & 1 else x
            py = 1 - y if (k >> 1) & 1 else y
            pc = 1 - c if k & 1 else c
            peer_i = 4 * px + 2 * py + pc
            sends.append(pltpu.make_async_remote_copy(
                src_ref=x_ref.at[peer_i], dst_ref=out_ref.at[me_i], send_sem=send_sems.at[k - 1],
                recv_sem=recv_sems.at[k - 1], device_id=(px, py, pc), device_id_type=MESH))
            recvs.append(pltpu.make_async_remote_copy(
                src_ref=x_ref.at[peer_i], dst_ref=out_ref.at[peer_i], send_sem=send_sems.at[k - 1],
                recv_sem=recv_sems.at[k - 1], device_id=(px, py, pc), device_id_type=MESH))
        for cp in sends:
            cp.start()
        for cp in recvs:
            cp.wait_recv()
        for cp in sends:
            cp.wait_send()
        mine.wait()

    return pl.pallas_call(
        body, name=name,
        out_shape=jax.ShapeDtypeStruct(pieces.shape, pieces.dtype),
        in_specs=[pl.BlockSpec(memory_space=pl.ANY)],
        out_specs=pl.BlockSpec(memory_space=pl.ANY),
        scratch_shapes=[pltpu.SemaphoreType.DMA((7,)), pltpu.SemaphoreType.DMA((7,)), pltpu.SemaphoreType.DMA],
    )(pieces)


def _dims(ta, tb):
    return (((0 if ta else 1,), (1 if tb else 0,)), ((), ()))


def _raw_dot(a, b, ta, tb):
    return lax.dot_general(a, b, _dims(ta, tb), preferred_element_type=F32)


def _split2(x):
    hi = x.astype(BF16)
    return hi, (x - hi.astype(F32)).astype(BF16)


def _dot_impl(a, b, ta, tb, prec):
    if prec == "bf16":
        return _raw_dot(a.astype(BF16), b.astype(BF16), ta, tb)
    if prec == "bx":
        ah, al = _split2(a)
        bb = b.astype(BF16)
        return _raw_dot(ah, bb, ta, tb) + _raw_dot(al, bb, ta, tb)
    ah, al = _split2(a)
    bh, bl = _split2(b)
    return _raw_dot(ah, bh, ta, tb) + (_raw_dot(ah, bl, ta, tb) + _raw_dot(al, bh, ta, tb))


@functools.lru_cache(maxsize=None)
def _mm_fn(ta, tb, prec):
    @jax.custom_vjp
    def mm(a, b):
        return _dot_impl(a, b, ta, tb, prec)

    def fwd(a, b):
        return mm(a, b), (a, b)

    def bwd(res, dc):
        a, b = res
        bprec = "f32" if prec == "f32" else "bf16"
        if prec == "bx":
            assert not ta
            return _mm_fn(False, not tb, "bx")(dc, b).astype(a.dtype), jnp.zeros_like(b)
        da = _mm_fn(tb, True, bprec)(b, dc) if ta else _mm_fn(False, not tb, bprec)(dc, b)
        db = _mm_fn(True, ta, bprec)(dc, a) if tb else _mm_fn(not ta, False, bprec)(a, dc)
        return da.astype(a.dtype), db.astype(b.dtype)

    mm.defvjp(fwd, bwd)
    return mm


def _mm(a, b, ta=False, tb=False, prec="bf16"):
    return _mm_fn(ta, tb, prec)(a, b)


@functools.lru_cache(maxsize=None)
def _shift_fn(k):
    @jax.custom_vjp
    def shift(x):
        n = x.shape[0]
        r = pltpu.roll(x, (-k) % n, 0)
        t = lax.broadcasted_iota(jnp.int32, x.shape, 0)
        ok = (t + k >= 0) & (t + k < n)
        return jnp.where(ok, r, 0.0)

    shift.defvjp(lambda x: (shift(x), None), lambda _, dy: (_shift_fn(-k)(dy),))
    return shift


def _sigmoid(x):
    return 1.0 / (1.0 + jnp.exp(-x))


@jax.custom_vjp
def _softplus(x):
    y = jnp.exp(-jnp.abs(x))
    u = 1.0 + y
    l1p = jnp.where(u == 1.0, y, jnp.log(u) * y / jnp.where(u == 1.0, 1.0, u - 1.0))
    return jnp.maximum(x, 0.0) + l1p


_softplus.defvjp(lambda x: (_softplus(x), x), lambda x, dy: (dy * _sigmoid(x),))


def _silu(x):
    return x * _sigmoid(x)


def _gelu_tanh(x):
    return 0.5 * x * (1.0 + jnp.tanh(math.sqrt(2.0 / math.pi) * (x + 0.044715 * (x * x * x))))


def _block_op(name, f, grid, in_specs, out_specs, out_shapes, diff, acc=None):
    n_in, n_out = len(in_specs), len(out_specs)
    acc = acc or [None] * n_in
    didx = [i for i in range(n_in) if diff[i]]
    sem = ("arbitrary",) * len(grid)

    def run_fwd(*xs):
        def body(*refs):
            outs = f(*[r[...] for r in refs[:n_in]])
            for r, o in zip(refs[n_in:], outs):
                r[...] = o.astype(r.dtype)
        return pl.pallas_call(body, name=name + "_fwd", grid=grid, in_specs=in_specs, out_specs=out_specs,
                              out_shape=out_shapes, compiler_params=_cparams(sem))(*xs)

    def run_bwd(xs, douts):
        def body(*refs):
            ins = [r[...] for r in refs[:n_in]]
            dos = [r[...] for r in refs[n_in:n_in + n_out]]

            def g(*dv):
                full = list(ins)
                for i, v in zip(didx, dv):
                    full[i] = v
                return tuple(f(*full))

            outs, vjp = jax.vjp(g, *[ins[i] for i in didx])
            dins = vjp(tuple(d.astype(o.dtype) for d, o in zip(dos, outs)))
            for r, i, dv in zip(refs[n_in + n_out:], didx, dins):
                dv = dv.astype(r.dtype)
                if acc[i] is None:
                    r[...] = dv
                else:
                    if acc[i] == 'last':
                        first = pl.program_id(len(grid) - 1) == 0
                    else:
                        first = functools.reduce(jnp.logical_and, [pl.program_id(a) == 0 for a in range(len(grid))])

                    @pl.when(first)
                    def _(r=r, dv=dv):
                        r[...] = dv

                    @pl.when(jnp.logical_not(first))
                    def _(r=r, dv=dv):
                        r[...] += dv
        return pl.pallas_call(
            body, name=name + "_bwd", grid=grid, in_specs=list(in_specs) + list(out_specs),
            out_specs=[in_specs[i] for i in didx],
            out_shape=[jax.ShapeDtypeStruct(xs[i].shape, xs[i].dtype) for i in didx],
            compiler_params=_cparams(sem))(*xs, *douts)

    @jax.custom_vjp
    def op(*xs):
        return tuple(run_fwd(*xs))

    def op_fwd(*xs):
        return tuple(run_fwd(*xs)), xs

    def op_bwd(xs, douts):
        dins = run_bwd(xs, douts)
        full = [jnp.zeros_like(x) for x in xs]
        for i, dv in zip(didx, dins):
            full[i] = dv
        return tuple(full)

    op.defvjp(op_fwd, op_bwd)
    return op


def _sds(shape, dtype):
    return jax.ShapeDtypeStruct(tuple(shape), dtype)


def _pick(n, cands):
    for c in cands:
        if n % c == 0:
            return c
    return n


def _mm_call(name, a, b, mode, out_dtype):
    if mode == "tn":
        m, k = a.shape
        n = b.shape[1]
        tm = _pick(m, (512, 256, 128, 64))
        tn = _pick(n, (1408, 1280, 1152, 1024, 512, 256, 128))
        steps = m // tm

        def body(a_ref, b_ref, o_ref, acc_ref):
            i = pl.program_id(1)

            @pl.when(i == 0)
            def _():
                acc_ref[...] = jnp.zeros_like(acc_ref)

            acc_ref[...] += _raw_dot(a_ref[...].astype(BF16), b_ref[...].astype(BF16), True, False)

            @pl.when(i == steps - 1)
            def _():
                o_ref[...] = acc_ref[...].astype(o_ref.dtype)

        return pl.pallas_call(
            body, name=name, grid=(n // tn, steps),
            in_specs=[pl.BlockSpec((tm, k), lambda j, i: (i, 0)), pl.BlockSpec((tm, tn), lambda j, i: (i, j))],
            out_specs=pl.BlockSpec((k, tn), lambda j, i: (0, j)),
            out_shape=_sds((k, n), out_dtype),
            scratch_shapes=[pltpu.VMEM((k, tn), F32)],
            compiler_params=_cparams(("arbitrary", "arbitrary")))(a, b)

    m, k = a.shape
    n = b.shape[1] if mode == "nn" else b.shape[0]
    tm = _pick(m, (512, 256, 128, 64))
    tn = _pick(n, (1408, 1280, 1152, 1024, 512, 256, 128))

    def body(a_ref, b_ref, o_ref):
        o_ref[...] = _raw_dot(a_ref[...].astype(BF16), b_ref[...].astype(BF16), False, mode == "nt").astype(o_ref.dtype)

    b_spec = pl.BlockSpec((k, tn), lambda j, i: (0, j)) if mode == "nn" else pl.BlockSpec((tn, k), lambda j, i: (j, 0))
    return pl.pallas_call(
        body, name=name, grid=(n // tn, m // tm),
        in_specs=[pl.BlockSpec((tm, k), lambda j, i: (i, 0)), b_spec],
        out_specs=pl.BlockSpec((tm, tn), lambda j, i: (i, j)),
        out_shape=_sds((m, n), out_dtype),
        compiler_params=_cparams(("arbitrary", "arbitrary")))(a, b)


def _matmul(name, a, w, out_dtype):
    @jax.custom_vjp
    def mm(a, w):
        return _mm_call(name + "_nn", a, w, "nn", out_dtype)

    def fwd(a, w):
        return mm(a, w), (a, w)

    def bwd(res, dy):
        a, w = res
        return (_mm_call(name + "_nt", dy, w, "nt", a.dtype), _mm_call(name + "_tn", a, dy, "tn", w.dtype))

    mm.defvjp(fwd, bwd)
    return mm(a, w)


def _tok(width):
    return pl.BlockSpec((None, TT, width), lambda b, t: (b, t, 0))


def _row(width):
    return pl.BlockSpec((None, None, 1, width), lambda b, t: (b, t, 0, 0))


def _const2(shape):
    return pl.BlockSpec(shape, lambda b, t: (0,) * len(shape))


def _tok_grid():
    return (B_LOC, _ntok() // TT)


def _modulate(x, shift, scale):
    def f(x, sh, sc):
        return ((x * (1.0 + sc) + sh),)
    op = _block_op("modulate", f, _tok_grid(), [_tok(D), _row(D), _row(D)], [_tok(D)],
                   [_sds(x.shape, BF16)], [True, True, True])
    return op(x, shift, scale)[0]


def _post_norm(x, y, gate, g, b, res_w):
    def f(x, y, gate, g, b):
        z = ALPHA * x + res_w * gate * y
        mu = jnp.mean(z, axis=-1, keepdims=True)
        zc = z - mu
        var = jnp.mean(zc * zc, axis=-1, keepdims=True)
        return (zc * lax.rsqrt(var + EPS) * g + b,)
    op = _block_op("post_norm", f, _tok_grid(), [_tok(D), _tok(D), _row(D), _row(D), _row(D)], [_tok(D)],
                   [_sds(x.shape, F32)], [True] * 5)
    return op(x, y, gate, g, b)[0]


def _swiglu_gate(a, b):
    m = a.shape[0]
    tm = _pick(m, (256, 128, 64))

    def f(a, b):
        a = a.astype(F32)
        return (_silu(a) * b.astype(F32),)
    spec = pl.BlockSpec((tm, DFF), lambda i: (i, 0))
    op = _block_op("swiglu_gate", f, (m // tm,), [spec, spec], [spec], [_sds(a.shape, BF16)], [True, True])
    return op(a, b)[0]


def _seg_ones(width):
    i = np.arange(width)
    return jnp.asarray((i[:, None] // HD) == (i[None, :] // HD), BF16)


def _rope_perm(width):
    p = np.zeros((width, width), np.float32)
    for j in range(width):
        if (j % 32) < 16:
            p[j + 16, j] = -1.0
        else:
            p[j - 16, j] = 1.0
    return jnp.asarray(p, BF16)


def _rope_tables(width):
    t = jnp.arange(T_LAT)
    pos = jnp.stack([t // GRID_W, t % GRID_W], axis=-1).astype(F32)
    inv_freq = ROPE_THETA ** (-jnp.arange(ROPE_PAIRS, dtype=F32) / ROPE_PAIRS)
    ang = pos[..., None] * inv_freq
    ang = jnp.broadcast_to(ang[:, :, None, :], (T_LAT, 2, 2, ROPE_PAIRS)).reshape(T_LAT, HD)
    ang = jnp.tile(ang, (1, width // HD))
    cos = jnp.concatenate([jnp.ones((T_CTX, width), F32), jnp.cos(ang)], axis=0)
    sin = jnp.concatenate([jnp.zeros((T_CTX, width), F32), jnp.sin(ang)], axis=0)
    return cos, sin


def _att_pre(x, w_row, name):
    width = x.shape[-1]
    cos, sin = _rope_tables(width)

    def f(x, w, cos, sin, seg, perm):
        ms = _mm(x * x, seg, prec="bx") * (1.0 / HD)
        xn = x * lax.rsqrt(ms + EPS) * w
        return (xn * cos + _mm(xn, perm, prec="bx") * sin,)
    tab = pl.BlockSpec((TT, width), lambda b, t: (t, 0))
    op = _block_op(name, f, _tok_grid(),
                   [_tok(width), _row(width), tab, tab, _const2((width, width)), _const2((width, width))],
                   [_tok(width)], [_sds(x.shape, F32)], [True, True, False, False, False, False])
    return op(x, w_row, cos, sin, _seg_ones(width), _rope_perm(width))[0]


def _attention(q, k, v, name, tq):
    b_, hk, g_, tq_all, _ = q.shape
    tk = k.shape[2]

    def f(q, k, v):
        outs = []
        for gi in range(g_):
            s = _mm(q[gi], k, tb=True) * (HD ** -0.5)
            m = lax.stop_gradient(jnp.max(s, axis=-1, keepdims=True))
            e = jnp.exp(s - m)
            p = e / jnp.sum(e, axis=-1, keepdims=True)
            outs.append(_mm(p, v))
        return (jnp.stack(outs, axis=0),)
    qs = pl.BlockSpec((None, None, g_, tq, HD), lambda b, h, i: (b, h, 0, i, 0))
    ks = pl.BlockSpec((None, None, tk, HD), lambda b, h, i: (b, h, 0, 0))
    op = _block_op(name, f, (b_, hk, tq_all // tq), [qs, ks, ks], [qs], [_sds(q.shape, F32)],
                   [True, True, True], acc=[None, 'last', 'last'])
    return op(q, k, v)[0]


def _gdn_pre(qkv, conv_w):
    nt_c = GDN_QKV // 128
    flag = jnp.asarray((np.arange(nt_c) < 2 * GDN_W // 128).astype(np.float32)[:, None, None] * np.ones((1, 1, 128), np.float32))
    cw = jnp.broadcast_to(conv_w[None], (B_LOC,) + conv_w.shape)

    def f(x, cw, flag, seg):
        def conv(s):
            acc = cw[2:3, :] * s
            for j in (0, 1, 3, 4):
                acc = acc + cw[j:j + 1, :] * _shift_fn(j - 2)(s)
            return acc
        y = jnp.concatenate([conv(x[:T_CTX]), conv(x[T_CTX:])], axis=0)
        s = _silu(y)
        ss = _mm(s * s, seg, prec="bx")
        return (s * (flag * lax.rsqrt(ss + EPS) + (1.0 - flag)),)
    xs = pl.BlockSpec((None, _ntok(), 128), lambda b, j: (b, 0, j))
    op = _block_op("gdn_pre", f, (B_LOC, nt_c),
                   [xs, pl.BlockSpec((None, 5, 128), lambda b, j: (b, 0, j)),
                    pl.BlockSpec((None, 1, 128), lambda b, j: (j, 0, 0)), pl.BlockSpec((128, 128), lambda b, j: (0, 0))],
                   [xs], [_sds(qkv.shape, F32)], [True, True, False, False])
    return op(qkv, cw, flag, _seg_ones(128))[0]


def _gdn_gates(ba, a_log, dt_bias):
    pad = jnp.zeros((12,), F32)
    al = jnp.broadcast_to(jnp.concatenate([pad, a_log.reshape(12), jnp.zeros((104,), F32)])[None, None], (B_LOC, 1, 128))
    db = jnp.broadcast_to(jnp.concatenate([pad, dt_bias.reshape(12), jnp.zeros((104,), F32)])[None, None], (B_LOC, 1, 128))

    def f(x, al, db):
        lane = lax.broadcasted_iota(jnp.int32, x.shape, 1)
        return (jnp.where(lane < 12, _sigmoid(x), -jnp.exp(al) * _softplus(x + db)),)
    xs = pl.BlockSpec((None, _ntok(), 128), lambda b: (b, 0, 0))
    ps = pl.BlockSpec((None, 1, 128), lambda b: (b, 0, 0))
    op = _block_op("gdn_gates", f, (B_LOC,), [xs, ps, ps], [xs], [_sds(ba.shape, F32)], [True, True, True])
    return op(ba, al, db)[0]


@jax.custom_vjp
def _unit_lower_solve(lower, rhs):
    return _solve_fwd(lower, rhs)[0]


def _solve_inv(lower):
    n = lower.shape[0]
    eye = (lax.broadcasted_iota(jnp.int32, (n, n), 0) == lax.broadcasted_iota(jnp.int32, (n, n), 1)).astype(F32)
    nk = -lower
    inv = eye + nk
    for _ in range(int(math.log2(n)) - 1):
        nk = _dot_impl(nk, nk, False, False, "f32")
        inv = inv + _dot_impl(inv, nk, False, False, "f32")
    return inv


def _solve_fwd(lower, rhs):
    inv = _solve_inv(lower)
    sol = _dot_impl(inv, rhs, False, False, "f32")
    return sol, (inv, sol)


def _solve_bwd(res, dsol):
    inv, sol = res
    drhs = _dot_impl(inv, dsol, True, False, "f32")
    return -_dot_impl(drhs, sol, False, True, "f32"), drhs


_unit_lower_solve.defvjp(_solve_fwd, _solve_bwd)


def _gdn_prep(q, k, v, g, beta):
    s_, t_, _ = q.shape
    nc = t_ // CHUNK
    cb = max(d for d in (1, 2, 3, 4) if nc % d == 0)

    def f(q, k, v, g, beta):
        ii = lax.broadcasted_iota(jnp.int32, (CHUNK, CHUNK), 0)
        jj = lax.broadcasted_iota(jnp.int32, (CHUNK, CHUNK), 1)
        incl, strict, eye = jj <= ii, jj < ii, (ii == jj).astype(F32)
        us, ws, qgs, kds, ins, egs = [], [], [], [], [], []
        for c in range(cb):
            r = slice(c * CHUNK, (c + 1) * CHUNK)
            qc, kc, vc = q[r] * (HD ** -0.5), k[r], v[r]
            g_row, b_row = g[c], beta[c]
            g_col = jnp.sum(eye * g_row, axis=1, keepdims=True)
            b_col = jnp.sum(eye * b_row, axis=1, keepdims=True)
            gc_col = jnp.sum(jnp.where(incl, g_row, 0.0), axis=1, keepdims=True)
            gc_row = jnp.sum(jnp.where(ii <= jj, g_col, 0.0), axis=0, keepdims=True)
            g_tot = jnp.sum(g_row, axis=1, keepdims=True)
            diff = gc_col - gc_row
            decay = jnp.where(incl, jnp.exp(jnp.where(incl, diff, 0.0)), 0.0)
            kb = kc * b_col
            lower = jnp.where(strict, _mm(kb, kc, tb=True) * decay, 0.0)
            rhs = jnp.concatenate([vc * b_col, kb * jnp.exp(gc_col)], axis=1)
            sol = _unit_lower_solve(lower, rhs)
            us.append(sol[:, :HD])
            ws.append(sol[:, HD:])
            ins.append(jnp.where(incl, _mm(qc, kc, tb=True) * decay, 0.0))
            qgs.append(qc * jnp.exp(gc_col))
            kds.append(kc * jnp.exp(g_tot - gc_col))
            egs.append(jnp.broadcast_to(jnp.exp(g_tot), (1, CHUNK)))
        cat = lambda xs: jnp.concatenate(xs, axis=0)
        return cat(us), cat(ws), cat(qgs), cat(kds), cat(ins), jnp.stack(egs, axis=0)
    ts = pl.BlockSpec((None, cb * CHUNK, HD), lambda s, i: (s, i, 0))
    rs = pl.BlockSpec((None, cb, 1, CHUNK), lambda s, i: (s, i, 0, 0))
    big = _sds(q.shape, F32)
    op = _block_op("gdn_prep", f, (s_, nc // cb), [ts, ts, ts, rs, rs], [ts, ts, ts, ts, ts, rs],
                   [big, big, big, big, big, _sds(g.shape, F32)], [True] * 5)
    return op(q, k, v, g, beta)


def _gdn_scan_call(u, w, qg, kd, intra, eg):
    s_, t_, _ = u.shape
    nc = t_ // CHUNK
    hb = 2

    def body(u_ref, w_ref, qg_ref, kd_ref, in_ref, eg_ref, o_ref, st_ref):
        def step(c, states):
            rows = pl.ds(pl.multiple_of(c * CHUNK, CHUNK), CHUNK)
            new = []
            for h in range(hb):
                st = states[h]
                st_ref[h, rows, :] = st
                sb = st.astype(BF16)
                vnew = u_ref[h, rows, :] - _raw_dot(w_ref[h, rows, :].astype(BF16), sb, False, False)
                vb = vnew.astype(BF16)
                o_ref[h, rows, :] = (_raw_dot(qg_ref[h, rows, :].astype(BF16), sb, False, False)
                                     + _raw_dot(in_ref[h, rows, :].astype(BF16), vb, False, False))
                e = eg_ref[h, pl.ds(c, 1), :, :].reshape(1, CHUNK)
                new.append(st * e + _raw_dot(kd_ref[h, rows, :].astype(BF16), vb, True, False))
            return tuple(new)
        lax.fori_loop(0, nc, step, tuple(jnp.zeros((HD, HD), F32) for _ in range(hb)))
    ts = pl.BlockSpec((hb, t_, HD), lambda s: (s, 0, 0))
    es = pl.BlockSpec((hb, nc, 1, CHUNK), lambda s: (s, 0, 0, 0))
    return pl.pallas_call(body, name="gdn_scan_fwd", grid=(s_ // hb,), in_specs=[ts] * 5 + [es], out_specs=[ts, ts],
                          out_shape=[_sds(u.shape, F32), _sds(u.shape, F32)],
                          compiler_params=_cparams(("arbitrary",)))(u, w, qg, kd, intra, eg)


def _gdn_scan_bwd_call(u, w, qg, kd, intra, eg, states, do):
    s_, t_, _ = u.shape
    nc = t_ // CHUNK
    hb = 1

    def body(u_ref, w_ref, qg_ref, kd_ref, in_ref, eg_ref, st_ref, do_ref, du_ref, dw_ref, dqg_ref, dkd_ref, din_ref, deg_ref):
        def step(i, dstates):
            c = nc - 1 - i
            rows = pl.ds(pl.multiple_of(c * CHUNK, CHUNK), CHUNK)
            new = []
            for h in range(hb):
                ds = dstates[h]
                st = st_ref[h, rows, :]
                sb, dsb = st.astype(BF16), ds.astype(BF16)
                wb, kdb = w_ref[h, rows, :].astype(BF16), kd_ref[h, rows, :].astype(BF16)
                inb, qgb = in_ref[h, rows, :].astype(BF16), qg_ref[h, rows, :].astype(BF16)
                dob = do_ref[h, rows, :].astype(BF16)
                vnew = u_ref[h, rows, :] - _raw_dot(wb, sb, False, False)
                vb = vnew.astype(BF16)
                din_ref[h, rows, :] = _raw_dot(dob, vb, False, True)
                dvn = _raw_dot(inb, dob, True, False) + _raw_dot(kdb, dsb, False, False)
                dvb = dvn.astype(BF16)
                dqg_ref[h, rows, :] = _raw_dot(dob, sb, False, True)
                dkd_ref[h, rows, :] = _raw_dot(vb, dsb, False, True)
                deg_ref[h, pl.ds(c, 1), :, :] = jnp.sum(st * ds, axis=0, keepdims=True).reshape(1, 1, CHUNK)
                du_ref[h, rows, :] = dvn
                dw_ref[h, rows, :] = -_raw_dot(dvb, sb, False, True)
                e = eg_ref[h, pl.ds(c, 1), :, :].reshape(1, CHUNK)
                new.append(ds * e + _raw_dot(qgb, dob, True, False) - _raw_dot(wb, dvb, True, False))
            return tuple(new)
        lax.fori_loop(0, nc, step, tuple(jnp.zeros((HD, HD), F32) for _ in range(hb)))
    ts = pl.BlockSpec((hb, t_, HD), lambda s: (s, 0, 0))
    es = pl.BlockSpec((hb, nc, 1, CHUNK), lambda s: (s, 0, 0, 0))
    big = _sds(u.shape, F32)
    return pl.pallas_call(body, name="gdn_scan_bwd", grid=(s_ // hb,), in_specs=[ts] * 5 + [es, ts, ts],
                          out_specs=[ts] * 5 + [es], out_shape=[big] * 5 + [_sds(eg.shape, F32)],
                          compiler_params=_cparams(("arbitrary",)))(u, w, qg, kd, intra, eg, states, do)


@jax.custom_vjp
def _gdn_scan(u, w, qg, kd, intra, eg):
    return _gdn_scan_call(u, w, qg, kd, intra, eg)[0]


def _gdn_scan_f(u, w, qg, kd, intra, eg):
    o, states = _gdn_scan_call(u, w, qg, kd, intra, eg)
    return o, (u, w, qg, kd, intra, eg, states)


def _gdn_scan_b(res, do):
    return tuple(_gdn_scan_bwd_call(*res, do))


_gdn_scan.defvjp(_gdn_scan_f, _gdn_scan_b)


def _gdn_post(o, z, w_row):
    def f(o, z, w, seg):
        ms = _mm(o * o, seg, prec="bx") * (1.0 / HD)
        return (o * lax.rsqrt(ms + EPS) * w * _silu(z),)
    op = _block_op("gdn_post", f, _tok_grid(), [_tok(GDN_W), _tok(GDN_W), _row(GDN_W), _const2((GDN_W, GDN_W))],
                   [_tok(GDN_W)], [_sds(o.shape, BF16)], [True, True, True, False])
    return op(o, z, w_row, _seg_ones(GDN_W))[0]


def _s5_tables(ar, ai, rev):
    pr, pi = [ar], [ai]
    for _ in range(7):
        pr, pi = pr + [pr[-1] * ar - pi[-1] * ai], pi + [pr[-1] * ai + pi[-1] * ar]
    if rev:
        pr, pi = pr[::-1], pi[::-1]
    return jnp.concatenate(pr, axis=0), jnp.concatenate(pi, axis=0)


def _s5_scan_call(bu, ar, ai, rev, h=None):
    b_, t_, n2 = bu.shape
    nblk = n2 // (2 * S5_LB)
    tr, ti = _s5_tables(ar, ai, rev)
    tab = jnp.concatenate([tr.reshape(8, nblk, 1, S5_LB), ti.reshape(8, nblk, 1, S5_LB)], axis=2).reshape(8, n2)
    ntile = t_ // 8
    with_grad = h is not None

    def scan_tile(xr, xi, tabr, tabi, cr, ci):
        row = lax.broadcasted_iota(jnp.int32, xr.shape, 0)
        for k in (1, 2, 4):
            idx = (8 - k) if rev else (k - 1)
            akr, aki = tabr[idx:idx + 1, :], tabi[idx:idx + 1, :]
            sh = (8 - k) if rev else k
            sr, si = pltpu.roll(xr, sh, 0), pltpu.roll(xi, sh, 0)
            ok = (row < 8 - k) if rev else (row >= k)
            xr, xi = (xr + jnp.where(ok, akr * sr - aki * si, 0.0), xi + jnp.where(ok, akr * si + aki * sr, 0.0))
        return xr + tabr * cr - tabi * ci, xi + tabr * ci + tabi * cr

    def body(*refs):
        if with_grad:
            bu_ref, tab_ref, h_ref, o_ref, da_ref = refs
        else:
            bu_ref, tab_ref, o_ref = refs
        tabr, tabi = tab_ref[:, :S5_LB], tab_ref[:, S5_LB:]
        zero = jnp.zeros((1, S5_LB), F32)
        row = lax.broadcasted_iota(jnp.int32, (8, S5_LB), 0)

        def step(i, carry):
            cr, ci = carry[0], carry[1]
            tix = (ntile - 1 - i) if rev else i
            rows = pl.ds(pl.multiple_of(tix * 8, 8), 8)
            hr, hi = scan_tile(bu_ref[rows, :S5_LB], bu_ref[rows, S5_LB:], tabr, tabi, cr, ci)
            o_ref[rows, :S5_LB] = hr
            o_ref[rows, S5_LB:] = hi
            out = (hr[0:1, :], hi[0:1, :]) if rev else (hr[7:8, :], hi[7:8, :])
            if with_grad:
                prev = pl.ds(pl.multiple_of(jnp.maximum(tix - 1, 0) * 8, 8), 8)
                live = jnp.where(tix > 0, 1.0, 0.0)
                pr = jnp.where(row == 0, pltpu.roll(h_ref[prev, :S5_LB], 1, 0) * live, pltpu.roll(h_ref[rows, :S5_LB], 1, 0))
                pi = jnp.where(row == 0, pltpu.roll(h_ref[prev, S5_LB:], 1, 0) * live, pltpu.roll(h_ref[rows, S5_LB:], 1, 0))
                out = out + (carry[2] + hr * pr + hi * pi, carry[3] + hi * pr - hr * pi)
            return out
        init = (zero, zero) + ((jnp.zeros((8, S5_LB), F32),) * 2 if with_grad else ())
        fin = lax.fori_loop(0, ntile, step, init)
        if with_grad:
            da_ref[:, :S5_LB] = fin[2]
            da_ref[:, S5_LB:] = fin[3]
    xs = pl.BlockSpec((None, t_, 2 * S5_LB), lambda b, j: (b, 0, j))
    tb = pl.BlockSpec((8, 2 * S5_LB), lambda b, j: (0, j))
    if with_grad:
        return pl.pallas_call(body, name="s5_scan_bwd", grid=(b_, nblk), in_specs=[xs, tb, xs],
                              out_specs=[xs, pl.BlockSpec((None, 8, 2 * S5_LB), lambda b, j: (b, 0, j))],
                              out_shape=[_sds(bu.shape, F32), _sds((b_, 8, n2), F32)],
                              compiler_params=_cparams(("arbitrary", "arbitrary")))(bu, tab, h)
    return pl.pallas_call(body, name="s5_scan_fwd", grid=(b_, nblk), in_specs=[xs, tb], out_specs=xs,
                          out_shape=_sds(bu.shape, F32), compiler_params=_cparams(("arbitrary", "arbitrary")))(bu, tab)


@jax.custom_vjp
def _s5_scan(bu, ar, ai):
    return _s5_scan_call(bu, ar, ai, False)


def _s5_scan_f(bu, ar, ai):
    h = _s5_scan_call(bu, ar, ai, False)
    return h, (h, ar, ai)


def _s5_scan_b(res, dh):
    h, ar, ai = res
    lam, da = _s5_scan_call(dh, ar, -ai, True, h=h)
    nblk = da.shape[-1] // (2 * S5_LB)
    da = jnp.sum(da, axis=(0, 1)).reshape(nblk, 2, S5_LB)
    return lam, da[:, 0].reshape(1, -1), da[:, 1].reshape(1, -1)


_s5_scan.defvjp(_s5_scan_f, _s5_scan_b)


def _s5_post(u, y0, y1, d_row, glu_w, glu_b_row):
    def f(u, y0, y1, d, gw, gb):
        zz = _gelu_tanh(d * u + y0 + y1)
        return (zz * _sigmoid(_mm(zz, gw) + gb),)
    op = _block_op("s5_post", f, _tok_grid(),
                   [_tok(S5_W), _tok(S5_W), _tok(S5_W), _row(S5_W), _const2((S5_W, S5_W)), _row(S5_W)],
                   [_tok(S5_W)], [_sds(u.shape, BF16)], [True] * 6, acc=[None, None, None, None, 'all', None])
    return op(u, y0, y1, d_row, glu_w, glu_b_row)[0]


def _loss_rows(x, target):
    def f(x, t):
        e = x - t
        return (jnp.sum(e * e, axis=0, keepdims=True),)
    grid = (B_LOC, T_LAT // TT)
    op = _block_op("loss_rows", f, grid, [_tok(D), _tok(D)], [_row(D)], [_sds((B_LOC, T_LAT // TT, 1, D), F32)],
                   [True, False])
    return op(x, target)[0]


def _adamw_call(name, w, g, m, v, pieces):
    r, c = w.shape
    tr = _pick(r, (256, 128, 64, 32, 16, 8))
    c1, c2 = 1.0 - ADAM_B1 ** ADAM_STEP, 1.0 - ADAM_B2 ** ADAM_STEP

    def body(w_ref, g_ref, m_ref, v_ref, go_ref, d_ref, mo_ref, vo_ref):
        if pieces:
            g = g_ref[0].astype(F32)
            for i in range(1, NDEV):
                g = g + g_ref[i].astype(F32)
        else:
            g = g_ref[...]
        m = ADAM_B1 * m_ref[...] + (1.0 - ADAM_B1) * g
        v = ADAM_B2 * v_ref[...] + (1.0 - ADAM_B2) * (g * g)
        go_ref[...] = g
        mo_ref[...] = m
        vo_ref[...] = v
        d_ref[...] = -ADAM_LR * ((m / c1) / (jnp.sqrt(v / c2) + ADAM_EPS) + ADAM_WD * w_ref[...])
    spec = pl.BlockSpec((tr, c), lambda i: (i, 0))
    gspec = pl.BlockSpec((NDEV, tr, c), lambda i: (0, i, 0)) if pieces else spec
    return pl.pallas_call(body, name=name, grid=(r // tr,), in_specs=[spec, gspec, spec, spec], out_specs=[spec] * 4,
                          out_shape=[_sds((r, c), F32)] * 4, compiler_params=_cparams(("arbitrary",)))(w, g, m, v)


def _sum_pieces(x, name):
    _, r, c = x.shape
    tr = _pick(r, (512, 256, 128, 64, 32, 16, 8))

    def body(x_ref, o_ref):
        acc = x_ref[0]
        for i in range(1, NDEV):
            acc = acc + x_ref[i]
        o_ref[...] = acc
    return pl.pallas_call(body, name=name, grid=(r // tr,), in_specs=[pl.BlockSpec((NDEV, tr, c), lambda i: (0, i, 0))],
                          out_specs=pl.BlockSpec((tr, c), lambda i: (i, 0)), out_shape=_sds((r, c), F32),
                          compiler_params=_cparams(("arbitrary",)))(x)


def _pack_flat(arrs, lanes, row_mult):
    flat = jnp.concatenate([a.reshape(-1).astype(F32) for a in arrs])
    n = flat.shape[0]
    rows = -(-n // lanes)
    rows = -(-rows // row_mult) * row_mult
    return jnp.pad(flat, (0, rows * lanes - n)).reshape(rows, lanes)


def _unpack_flat(packed, shapes, lead=()):
    flat = packed.reshape(lead + (-1,))
    out, off = [], 0
    for s in shapes:
        n = int(np.prod(s))
        out.append(flat[..., off:off + n].reshape(lead + tuple(s)))
        off += n
    return out


FS_ROWS = DFF // NDEV
WI_ROWS = IN_COLS // NDEV
WI_PAD = -(-WI_ROWS // 16) * 16
WO_ROWS = D // NDEV


def _pad_rows(blk, rows, axis):
    pad = [(0, 0)] * blk.ndim
    pad[axis] = (0, rows - blk.shape[axis])
    return jnp.pad(blk, pad)


def _pack_big_shards(t, dtype):
    parts = []
    for l in range(DEPTH):
        for n in ('ffn_w1', 'ffn_w3', 'ffn_w2'):
            for i in range(2):
                parts.append(t[n][l, i].reshape(-1, D))
        parts.append(_pad_rows(t['w_in'][l].reshape(-1, D), WI_PAD, 0))
        parts.append(t['w_out'][l].reshape(-1, D))
    return jnp.concatenate(parts, axis=0).astype(dtype)


def _unpack_big_shards(p):
    out = {n: [] for n in BIG}
    off = 0
    for l in range(DEPTH):
        for n in ('ffn_w1', 'ffn_w3', 'ffn_w2'):
            pair = []
            for i in range(2):
                blk = p[off:off + FS_ROWS]
                off += FS_ROWS
                pair.append(blk.reshape(FS_ROWS, D) if n == 'ffn_w2' else blk.reshape(D, FS_ROWS))
            out[n].append(jnp.stack(pair))
        out['w_in'].append(p[off:off + WI_ROWS].reshape(D, WI_ROWS))
        off += WI_PAD
        out['w_out'].append(p[off:off + WO_ROWS].reshape(WO_ROWS, D))
        off += WO_ROWS
    return {n: jnp.stack(v) for n, v in out.items()}


def _full_from_gathered(g):
    layers, off = [], 0

    def cols(blk, n):
        return blk.reshape(NDEV, D, n).transpose(1, 0, 2).reshape(D, NDEV * n)
    for l in range(DEPTH):
        lw = {}
        for n in ('ffn_w1', 'ffn_w3', 'ffn_w2'):
            pair = []
            for i in range(2):
                blk = g[:, off:off + FS_ROWS]
                off += FS_ROWS
                pair.append(blk.reshape(DFF, D) if n == 'ffn_w2' else cols(blk, FS_ROWS))
            lw[n] = pair
        lw['w_in'] = cols(g[:, off:off + WI_ROWS], WI_ROWS)
        off += WI_PAD
        lw['w_out'] = g[:, off:off + WO_ROWS].reshape(D, D)
        off += WO_ROWS
        layers.append(lw)
    return layers


def _pieces_from_full(layers):
    def cols(wf, n):
        return wf.reshape(D, NDEV, n).transpose(1, 0, 2).reshape(NDEV, n, D)
    parts = []
    for lw in layers:
        for n in ('ffn_w1', 'ffn_w3', 'ffn_w2'):
            for i in range(2):
                parts.append(lw[n][i].reshape(NDEV, FS_ROWS, D) if n == 'ffn_w2' else cols(lw[n][i], FS_ROWS))
        parts.append(_pad_rows(cols(lw['w_in'], WI_ROWS), WI_PAD, 1))
        parts.append(lw['w_out'].reshape(NDEV, WO_ROWS, D))
    return jnp.concatenate(parts, axis=1)


def _flip_segments(a, axis):
    ctx, lat = lax.slice_in_dim(a, 0, T_CTX, axis=axis), lax.slice_in_dim(a, T_CTX, _ntok(), axis=axis)
    return jnp.concatenate([jnp.flip(ctx, axis), jnp.flip(lat, axis)], axis=axis)


def _rows_of(vec_ctx, vec_lat):
    w = vec_lat.shape[-1]
    ntc, ntl = T_CTX // TT, T_LAT // TT
    return jnp.concatenate([jnp.broadcast_to(vec_ctx[None, None, None, :], (B_LOC, ntc, 1, w)),
                            jnp.broadcast_to(vec_lat[:, None, None, :], (B_LOC, ntl, 1, w))], axis=1)


def _rows_const(vec):
    return jnp.broadcast_to(vec[None, None, None, :], (B_LOC, _ntok() // TT, 1, vec.shape[-1]))


def _ffn_sublayer(xt, mrow, w1, w3, w2, g, b):
    h = _modulate(xt, mrow[0], mrow[1]).reshape(-1, D)
    a = _matmul("ffn_up", h, w1, BF16)
    bb = _matmul("ffn_up", h, w3, BF16)
    u = _swiglu_gate(a, bb)
    y = _matmul("ffn_down", u, w2, F32).reshape(xt.shape)
    return _post_norm(xt, y, mrow[2], _rows_const(g), _rows_const(b), 0.5)


def _s5_discretize(lam_re, lam_im, log_dt, b_re, b_im):
    dt = jnp.exp(log_dt)[:, None]
    zr, zi = lam_re * dt, lam_im * dt
    er = jnp.exp(zr)
    lbr, lbi = er * jnp.cos(zi), er * jnp.sin(zi)
    dd = lam_re * lam_re + lam_im * lam_im
    qr = ((lbr - 1.0) * lam_re + lbi * lam_im) / dd
    qi = (lbi * lam_re - (lbr - 1.0) * lam_im) / dd
    bbr = qr[..., None] * b_re - qi[..., None] * b_im
    bbi = qr[..., None] * b_im + qi[..., None] * b_re
    return lbr, lbi, bbr, bbi


def _s5_cols(a):
    return a.reshape(a.shape[:-1] + (S5_N // S5_LB, S5_LB))


def _s5_group(su, p):
    b_, t_, _ = su.shape
    eye = jnp.eye(S5_G, dtype=F32)
    ys = []
    for d in range(2):
        lbr, lbi, bbr, bbi = _s5_discretize(p['s5_lam_re'][d], p['s5_lam_im'][d], p['s5_log_dt'][d],
                                            p['s5_b_re'][d], p['s5_b_im'][d])
        bre = jnp.einsum('gph,gk->ghkp', bbr, eye).reshape(S5_W, S5_N)
        bim = jnp.einsum('gph,gk->ghkp', bbi, eye).reshape(S5_W, S5_N)
        bmat = jnp.stack([_s5_cols(bre), _s5_cols(bim)], axis=2).reshape(S5_W, 2 * S5_N)
        cre = jnp.einsum('ghp,gk->kpgh', p['s5_c_re'][d], eye).reshape(S5_N, S5_W)
        cim = -jnp.einsum('ghp,gk->kpgh', p['s5_c_im'][d], eye).reshape(S5_N, S5_W)
        cmat = jnp.stack([cre.reshape(S5_N // S5_LB, S5_LB, S5_W), cim.reshape(S5_N // S5_LB, S5_LB, S5_W)],
                         axis=1).reshape(2 * S5_N, S5_W)
        ud = su if d == 0 else _flip_segments(su, 1)
        bu = _matmul("s5_in", ud.reshape(-1, S5_W), bmat, F32).reshape(b_, t_, 2 * S5_N)
        hs = _s5_scan(bu, lbr.reshape(1, S5_N), lbi.reshape(1, S5_N))
        y = _matmul("s5_out", hs.reshape(-1, 2 * S5_N), cmat, F32).reshape(b_, t_, S5_W)
        ys.append(y if d == 0 else _flip_segments(y, 1))
    return _s5_post(su, ys[0], ys[1], _rows_const(p['s5_d']), p['glu_w'], _rows_const(p['glu_b']))


def _gdn_group(qkv, z, ba, p):
    b_, t_, _ = qkv.shape
    nc = t_ // CHUNK
    qkvn = _gdn_pre(qkv, p['gdn_conv_w'])
    bg = _gdn_gates(ba, p['gdn_a_log'], p['gdn_dt_bias'])

    def heads(a):
        a = a.reshape(b_, t_, GDN_H, HD).transpose(0, 2, 1, 3)
        return jnp.stack([a, _flip_segments(a, 2)], axis=1).reshape(b_ * 2 * GDN_H, t_, HD)

    def gates(a):
        a = a.reshape(b_, t_, 2, GDN_H).transpose(0, 2, 3, 1)
        a = jnp.stack([a[:, 0], _flip_segments(a[:, 1], 2)], axis=1)
        return a.reshape(b_ * 2 * GDN_H, nc, 1, CHUNK)
    q, k, v = heads(qkvn[..., :GDN_W]), heads(qkvn[..., GDN_W:2 * GDN_W]), heads(qkvn[..., 2 * GDN_W:])
    beta, g = gates(bg[..., 0:12]), gates(bg[..., 12:24])
    u, w, qg, kd, intra, eg = _gdn_prep(q, k, v, g, beta)
    o = _gdn_scan(u, w, qg, kd, intra, eg).reshape(b_, 2, GDN_H, t_, HD)
    o = o[:, 0] + _flip_segments(o[:, 1], 2)
    o = o.transpose(0, 2, 1, 3).reshape(b_, t_, GDN_W)
    return _gdn_post(o, z, _rows_const(jnp.tile(p['gdn_norm_w'], GDN_H)))


def _att_group(aq, ak, av, p):
    b_, t_, _ = aq.shape
    qn = _att_pre(aq, _rows_const(jnp.tile(p['q_norm_w'], ATT_W // HD)), "att_pre_q")
    kn = _att_pre(ak, _rows_const(jnp.tile(p['k_norm_w'], ATT_KW // HD)), "att_pre_k")
    q = qn.reshape(b_, t_, ATT_HKV, ATT_G, HD).transpose(0, 2, 3, 1, 4)
    k = kn.reshape(b_, t_, ATT_HKV, HD).transpose(0, 2, 1, 3)
    v = av.reshape(b_, t_, ATT_HKV, HD).transpose(0, 2, 1, 3)
    o_lat = _attention(q[:, :, :, T_CTX:], k, v, "att_lat", ATT_TQ)
    o_ctx = _attention(q[:, :, :, :T_CTX], k[:, :, :T_CTX], v[:, :, :T_CTX], "att_ctx", T_CTX)
    o = jnp.concatenate([o_ctx, o_lat], axis=3)
    return o.transpose(0, 3, 1, 2, 4).reshape(b_, t_, ATT_W)


def _permute_w_in(w):
    return jnp.concatenate([w[:, :1536], w[:, 1560:], w[:, 1536:1560], jnp.zeros((D, IN_PAD - IN_COLS), w.dtype)], axis=1)


def _mixer_sublayer(xt, mrow, lw, p, g, b):
    h = _modulate(xt, mrow[3], mrow[4]).reshape(-1, D)
    proj = _matmul("mix_in", h, _permute_w_in(lw['w_in']), F32).reshape(xt.shape[:2] + (IN_PAD,))
    o_gdn = _gdn_group(proj[..., :1152], proj[..., 1152:1536], proj[..., 2432:2560], p)
    o_att = _att_group(proj[..., 1536:1920], proj[..., 1920:2048], proj[..., 2048:2176], p)
    o_s5 = _s5_group(proj[..., 2176:2432], p)
    cat = jnp.concatenate([o_gdn, o_att.astype(BF16), o_s5], axis=-1).reshape(-1, D)
    y = _matmul("mix_out", cat, lw['w_out'], F32).reshape(xt.shape)
    return _post_norm(xt, y, mrow[5], _rows_const(g), _rows_const(b), 1.0)


def _local_loss(x, mod, modc, big, small, ctx, target):
    xt = jnp.concatenate([ctx, x], axis=1)
    for l in range(DEPTH):
        mrow = [_rows_of(modc[l, k * D:(k + 1) * D], mod[l, :, k * D:(k + 1) * D]) for k in range(N_MOD)]
        p = {n: small[n][l] for n in small}
        lw = big[l]
        xt = _ffn_sublayer(xt, mrow[0:3], lw['ffn_w1'][0], lw['ffn_w3'][0], lw['ffn_w2'][0], p['ln_g'][0], p['ln_b'][0])
        xt = _mixer_sublayer(xt, mrow, lw, p, p['ln_g'][1], p['ln_b'][1])
        xt = _ffn_sublayer(xt, mrow[6:9], lw['ffn_w1'][1], lw['ffn_w3'][1], lw['ffn_w2'][1], p['ln_g'][2], p['ln_b'][2])
    part = _loss_rows(xt[:, T_CTX:], target)
    return (0.5 / D) * jnp.sum(part)


def _silu_plain(x):
    return x * jax.nn.sigmoid(x)


def _dsilu_plain(x):
    s = jax.nn.sigmoid(x)
    return s * (1.0 + x * (1.0 - s))


def _small_shapes():
    return {'ln_g': (DEPTH, 3, D), 'ln_b': (DEPTH, 3, D), 'gdn_conv_w': (DEPTH, 5, GDN_QKV), 'glu_w': (DEPTH, S5_W, S5_W),
            'gdn_a_log': (DEPTH, 2, GDN_H), 'gdn_dt_bias': (DEPTH, 2, GDN_H), 'gdn_norm_w': (DEPTH, HD),
            'q_norm_w': (DEPTH, HD), 'k_norm_w': (DEPTH, HD), 's5_lam_re': (DEPTH, 2, S5_G, S5_P),
            's5_lam_im': (DEPTH, 2, S5_G, S5_P), 's5_log_dt': (DEPTH, 2, S5_G),
            's5_b_re': (DEPTH, 2, S5_G, S5_P, S5_H), 's5_b_im': (DEPTH, 2, S5_G, S5_P, S5_H),
            's5_c_re': (DEPTH, 2, S5_G, S5_H, S5_P), 's5_c_im': (DEPTH, 2, S5_G, S5_H, S5_P),
            's5_d': (DEPTH, S5_W), 'glu_b': (DEPTH, S5_W)}


def _gather_small_sharded(gathered, name):
    if name == 'glu_w':
        return gathered.transpose(1, 0, 2, 3).reshape(DEPTH, S5_W, S5_W)
    lead = gathered.shape[1:-1]
    return jnp.moveaxis(gathered, 0, -2).reshape(lead + (-1,))


def _my_small_shard(full, name, me):
    if name == 'glu_w':
        return lax.dynamic_slice_in_dim(full, me * (S5_W // NDEV), S5_W // NDEV, axis=1)
    n = full.shape[-1] // NDEV
    return lax.dynamic_slice_in_dim(full, me * n, n, axis=full.ndim - 1)


def kernel(x, c, ctx, c_ctx, w_ada, b_ada, ln_g, ln_b, ffn_w1, ffn_w3, ffn_w2, w_in, w_out, gdn_conv_w, gdn_a_log, gdn_dt_bias, gdn_norm_w, q_norm_w, k_norm_w, s5_lam_re, s5_lam_im, s5_log_dt, s5_b_re, s5_b_im, s5_c_re, s5_c_im, s5_d, glu_w, glu_b, loss_target, m_c_ctx, m_w_ada, m_b_ada, m_ln_g, m_ln_b, m_ffn_w1, m_ffn_w3, m_ffn_w2, m_w_in, m_w_out, m_gdn_conv_w, m_gdn_a_log, m_gdn_dt_bias, m_gdn_norm_w, m_q_norm_w, m_k_norm_w, m_s5_lam_re, m_s5_lam_im, m_s5_log_dt, m_s5_b_re, m_s5_b_im, m_s5_c_re, m_s5_c_im, m_s5_d, m_glu_w, m_glu_b, v_c_ctx, v_w_ada, v_b_ada, v_ln_g, v_ln_b, v_ffn_w1, v_ffn_w3, v_ffn_w2, v_w_in, v_w_out, v_gdn_conv_w, v_gdn_a_log, v_gdn_dt_bias, v_gdn_norm_w, v_q_norm_w, v_k_norm_w, v_s5_lam_re, v_s5_lam_im, v_s5_log_dt, v_s5_b_re, v_s5_b_im, v_s5_c_re, v_s5_c_im, v_s5_d, v_glu_w, v_glu_b):
    a = dict(locals())
    me = _my_index()
    ada_cols = N_MOD * D // NDEV

    sc = _silu_plain(a['c'])
    scc = _silu_plain(a['c_ctx'])
    small_in = [sc] + [a[n] for n in SMALL_SHARDED]
    got = _all_gather(_pack_flat(small_in, 128, 8), "gather_small")
    parts = _unpack_flat(got, [t.shape for t in small_in], lead=(NDEV,))
    sc_all = parts[0].reshape(NDEV * B_LOC, D)
    small = {n: _gather_small_sharded(parts[1 + i], n) for i, n in enumerate(SMALL_SHARDED)}
    for n in SMALL_REPL:
        small[n] = a[n]

    nb = NDEV * B_LOC
    rows_pad = 8
    sc_rows = jnp.concatenate([sc_all, scc[None], jnp.zeros((rows_pad - 1, D), F32)], axis=0)
    mod_part = jnp.stack([_mm_call("ada_fwd", sc_rows, a['w_ada'][l], "nn", F32) for l in range(DEPTH)])
    mod_all = _all_gather(mod_part.reshape(DEPTH * (nb + rows_pad), ada_cols), "gather_mod")
    mod_all = mod_all.reshape(NDEV, DEPTH, nb + rows_pad, ada_cols).transpose(1, 2, 0, 3).reshape(DEPTH, nb + rows_pad, N_MOD * D)
    mod_all = mod_all + a['b_ada'][:, None, :]
    mod = lax.dynamic_slice_in_dim(mod_all, me * B_LOC, B_LOC, axis=1)
    modc = mod_all[:, nb]

    big = _full_from_gathered(_all_gather(_pack_big_shards(a, BF16), "gather_weights"))

    loss_part, grads = jax.value_and_grad(_local_loss, argnums=(0, 1, 2, 3, 4))(
        a['x'], mod, modc, big, small, a['ctx'], a['loss_target'])
    gx, gmod, gmodc, gbig, gsmall = grads

    gm_rows = jnp.concatenate([gmod, gmodc[:, None], jnp.zeros((DEPTH, rows_pad - B_LOC - 1, N_MOD * D), F32)], axis=1)
    gm_all = _all_gather(gm_rows.reshape(DEPTH * rows_pad, N_MOD * D), "gather_dmod").reshape(NDEV, DEPTH, rows_pad, N_MOD * D)
    gm_all = gm_all.transpose(1, 0, 2, 3).reshape(DEPTH, NDEV * rows_pad, N_MOD * D)
    g_b_ada = jnp.sum(gm_all, axis=1)
    sc_dev = jnp.concatenate([sc_all.reshape(NDEV, B_LOC, D), jnp.broadcast_to(scc[None, None], (NDEV, 1, D)),
                              jnp.zeros((NDEV, rows_pad - B_LOC - 1, D), F32)], axis=1).reshape(NDEV * rows_pad, D)
    gm_mine = lax.dynamic_slice_in_dim(gm_all, me * ada_cols, ada_cols, axis=2)
    g_w_ada = jnp.stack([_mm_call("ada_dw", sc_dev, gm_mine[l], "tn", F32) for l in range(DEPTH)])
    gmc = gm_mine.reshape(DEPTH, NDEV, rows_pad, ada_cols)[:, :, B_LOC].sum(axis=1)
    gmc = jnp.concatenate([gmc[:, None], jnp.zeros((DEPTH, 7, ada_cols), F32)], axis=1)
    dscc_part = sum(_mm_call("ada_dx", gmc[l], a['w_ada'][l], "nt", F32)[0] for l in range(DEPTH))

    small_names = SMALL_SHARDED + SMALL_REPL
    sums_in = [loss_part.reshape(1), dscc_part] + [gsmall[n] for n in small_names]
    tot = _sum_pieces(_all_gather(_pack_flat(sums_in, 128, 8), "gather_sums"), "sum_small")
    tparts = _unpack_flat(tot, [t.shape for t in sums_in])
    loss = tparts[0].reshape(())
    g_c_ctx = tparts[1] * _dsilu_plain(a['c_ctx'])
    g = {'c_ctx': g_c_ctx, 'b_ada': g_b_ada, 'w_ada': g_w_ada}
    for i, n in enumerate(small_names):
        g[n] = _my_small_shard(tparts[2 + i], n, me) if n in SMALL_SHARDED else tparts[2 + i]

    recv = _all_to_all(_pieces_from_full(gbig), "scatter_grads")
    gb, db, mb, vb = _adamw_call("adamw_big", _pack_big_shards(a, F32), recv,
                                 _pack_big_shards({n: a['m_' + n] for n in BIG}, F32),
                                 _pack_big_shards({n: a['v_' + n] for n in BIG}, F32), True)
    res = {'g': {}, 'd': {}, 'm': {}, 'v': {}}
    for key, packed in (('g', gb), ('d', db), ('m', mb), ('v', vb)):
        res[key].update(_unpack_big_shards(packed))

    shp = a['w_ada'].shape
    flat2 = lambda t: t.reshape(-1, shp[-1])
    ga, da, ma, va = _adamw_call("adamw_ada", flat2(a['w_ada']), flat2(g['w_ada']), flat2(a['m_w_ada']), flat2(a['v_w_ada']), False)
    for key, val in (('g', ga), ('d', da), ('m', ma), ('v', va)):
        res[key]['w_ada'] = val.reshape(shp)
    rest = [n for n in WEIGHTS if n not in BIG and n != 'w_ada']
    shapes = [a[n].shape for n in rest]
    pk = lambda d: _pack_flat([d[n] for n in rest], 128, 8)
    outs = _adamw_call("adamw_small", pk(a), pk(g), pk({n: a['m_' + n] for n in rest}), pk({n: a['v_' + n] for n in rest}), False)
    for key, val in zip(('g', 'd', 'm', 'v'), outs):
        for n, t in zip(rest, _unpack_flat(val, shapes)):
            res[key][n] = t

    out = [loss, gx]
    for key in ('g', 'd', 'm', 'v'):
        out += [res[key][n] for n in WEIGHTS]
    return tuple(out)
```

```python
import functools
import math

import numpy as np
import jax
import jax.numpy as jnp
from jax import lax
from jax.experimental import pallas as pl
from jax.experimental.pallas import tpu as pltpu

F32 = jnp.float32
BF16 = jnp.bfloat16
MESH = pl.DeviceIdType.MESH

NDEV = 8
D = 1024
DFF = 2816
DEPTH = 4
B_LOC = 4
T_CTX = 256
T_LAT = 2048
GRID_W = 64
N_MOD = 9
GDN_H = 6
HD = 64
GDN_QKV = 3 * GDN_H * HD
GDN_W = GDN_H * HD
ATT_HKV = 2
ATT_G = 3
ATT_W = ATT_HKV * ATT_G * HD
ATT_KW = ATT_HKV * HD
S5_G = 16
S5_H = 16
S5_P = 64
S5_W = S5_G * S5_H
S5_N = S5_G * S5_P
IN_COLS = 2456
IN_PAD = 2560
ROPE_THETA = 10000.0
ROPE_PAIRS = 16
ALPHA = (2.0 * 4) ** 0.25
EPS = 1e-6
CHUNK = 64
ADAM_LR, ADAM_B1, ADAM_B2, ADAM_EPS, ADAM_WD, ADAM_STEP = 0.001, 0.9, 0.999, 1e-08, 0.01, 10

TT = 256
ATT_TQ = 128
S5_LB = 256
VMEM_LIMIT = 56 * 1024 * 1024

WEIGHTS = ['c_ctx', 'w_ada', 'b_ada', 'ln_g', 'ln_b', 'ffn_w1', 'ffn_w3', 'ffn_w2', 'w_in', 'w_out', 'gdn_conv_w',
           'gdn_a_log', 'gdn_dt_bias', 'gdn_norm_w', 'q_norm_w', 'k_norm_w', 's5_lam_re', 's5_lam_im', 's5_log_dt',
           's5_b_re', 's5_b_im', 's5_c_re', 's5_c_im', 's5_d', 'glu_w', 'glu_b']
INPUTS = ['x', 'c', 'ctx'] + WEIGHTS + ['loss_target'] + ['m_' + n for n in WEIGHTS] + ['v_' + n for n in WEIGHTS]
BIG = ['ffn_w1', 'ffn_w3', 'ffn_w2', 'w_in', 'w_out']
SMALL_SHARDED = ['ln_g', 'ln_b', 'gdn_conv_w', 'glu_w']
SMALL_REPL = ['gdn_a_log', 'gdn_dt_bias', 'gdn_norm_w', 'q_norm_w', 'k_norm_w', 's5_lam_re', 's5_lam_im',
              's5_log_dt', 's5_b_re', 's5_b_im', 's5_c_re', 's5_c_im', 's5_d', 'glu_b']


def _cparams(sem=None):
    return pltpu.CompilerParams(dimension_semantics=sem, vmem_limit_bytes=VMEM_LIMIT)


def _ntok():
    return T_CTX + T_LAT


def _my_pos():
    return lax.axis_index("x"), lax.axis_index("y"), lax.axis_index("c")


def _my_index():
    x, y, c = _my_pos()
    return 4 * x + 2 * y + c


def _all_gather(shard, name):
    def body(x_ref, out_ref, send_sems, recv_sems, local_sem):
        x, y, c = _my_pos()
        me, sibling = (x, y, c), (x, y, 1 - c)
        chips = [(1 - x, y), (x, 1 - y), (1 - x, 1 - y)]

        def slab(px, py, pc):
            return out_ref.at[4 * px + 2 * py + pc]

        def copy(k, block, to, src=None):
            return pltpu.make_async_remote_copy(
                src_ref=slab(*block) if src is None else src, dst_ref=slab(*block),
                send_sem=send_sems.at[k], recv_sem=recv_sems.at[k], device_id=to, device_id_type=MESH)

        mine = pltpu.make_async_copy(x_ref, slab(*me), local_sem)
        mine.start()
        first = [copy(0, me, sibling, src=x_ref)]
        first += [copy(1 + j, me, (*chip, c), src=x_ref) for j, chip in enumerate(chips)]
        for cp in first:
            cp.start()
        passed = [copy(4 + j, (*chip, c), sibling) for j, chip in enumerate(chips)]
        for j, chip in enumerate(chips):
            copy(1 + j, (*chip, c), me).wait_recv()
            passed[j].start()
        copy(0, sibling, me).wait_recv()
        for j, chip in enumerate(chips):
            copy(4 + j, (*chip, 1 - c), me).wait_recv()
        for cp in first + passed:
            cp.wait_send()
        mine.wait()

    return pl.pallas_call(
        body, name=name,
        out_shape=jax.ShapeDtypeStruct((NDEV,) + shard.shape, shard.dtype),
        in_specs=[pl.BlockSpec(memory_space=pl.ANY)],
        out_specs=pl.BlockSpec(memory_space=pl.ANY),
        scratch_shapes=[pltpu.SemaphoreType.DMA((7,)), pltpu.SemaphoreType.DMA((7,)), pltpu.SemaphoreType.DMA],
    )(shard)


def _all_to_all(pieces, name):
    def body(x_ref, out_ref, send_sems, recv_sems, local_sem):
        x, y, c = _my_pos()
        me_i = 4 * x + 2 * y + c
        mine = pltpu.make_async_copy(x_ref.at[me_i], out_ref.at[me_i], local_sem)
        mine.start()
        sends, recvs = [], []
        for k in range(1, NDEV):
            px = 1 - x if (k >> 2) & 1 else x
            py = 1 - y if (k >> 1) & 1 else y
            pc = 1 - c if k & 1 else c
            peer_i = 4 * px + 2 * py + pc
            sends.append(pltpu.make_async_remote_copy(
                src_ref=x_ref.at[peer_i], dst_ref=out_ref.at[me_i], send_sem=send_sems.at[k - 1],
                recv_sem=recv_sems.at[k - 1], device_id=(px, py, pc), device_id_type=MESH))
            recvs.append(pltpu.make_async_remote_copy(
                src_ref=x_ref.at[peer_i], dst_ref=out_ref.at[peer_i], send_sem=send_sems.at[k - 1],
                recv_sem=recv_sems.at[k - 1], device_id=(px, py, pc), device_id_type=MESH))
        for cp in sends:
            cp.start()
        for cp in recvs:
            cp.wait_recv()
        for cp in sends:
            cp.wait_send()
        mine.wait()

    return pl.pallas_call(
        body, name=name,
        out_shape=jax.ShapeDtypeStruct(pieces.shape, pieces.dtype),
        in_specs=[pl.BlockSpec(memory_space=pl.ANY)],
        out_specs=pl.BlockSpec(memory_space=pl.ANY),
        scratch_shapes=[pltpu.SemaphoreType.DMA((7,)), pltpu.SemaphoreType.DMA((7,)), pltpu.SemaphoreType.DMA],
    )(pieces)


def _dims(ta, tb):
    return (((0 if ta else 1,), (1 if tb else 0,)), ((), ()))


def _raw_dot(a, b, ta, tb):
    return lax.dot_general(a, b, _dims(ta, tb), preferred_element_type=F32)


def _split2(x):
    hi = x.astype(BF16)
    return hi, (x - hi.astype(F32)).astype(BF16)


def _dot_impl(a, b, ta, tb, prec):
    if prec == "bf16":
        return _raw_dot(a.astype(BF16), b.astype(BF16), ta, tb)
    ka, kb = (0 if ta else 1), (1 if tb else 0)
    ah, al = _split2(a)
    if prec == "bx":
        bb = b.astype(BF16)
        return _raw_dot(jnp.concatenate([ah, al], axis=ka), jnp.concatenate([bb, bb], axis=kb), ta, tb)
    bh, bl = _split2(b)
    return _raw_dot(ah, bh, ta, tb) + (_raw_dot(ah, bl, ta, tb) + _raw_dot(al, bh, ta, tb))


@functools.lru_cache(maxsize=None)
def _mm_fn(ta, tb, prec):
    @jax.custom_vjp
    def mm(a, b):
        return _dot_impl(a, b, ta, tb, prec)

    def fwd(a, b):
        return mm(a, b), (a, b)

    def bwd(res, dc):
        a, b = res
        bprec = "f32" if prec == "f32" else "bf16"
        if prec == "bx":
            assert not ta
            return _mm_fn(False, not tb, "bx")(dc, b).astype(a.dtype), jnp.zeros_like(b)
        da = _mm_fn(tb, True, bprec)(b, dc) if ta else _mm_fn(False, not tb, bprec)(dc, b)
        db = _mm_fn(True, ta, bprec)(dc, a) if tb else _mm_fn(not ta, False, bprec)(a, dc)
        return da.astype(a.dtype), db.astype(b.dtype)

    mm.defvjp(fwd, bwd)
    return mm


def _mm(a, b, ta=False, tb=False, prec="bf16"):
    return _mm_fn(ta, tb, prec)(a, b)


@functools.lru_cache(maxsize=None)
def _shift_fn(k):
    @jax.custom_vjp
    def shift(x):
        n = x.shape[0]
        r = pltpu.roll(x, (-k) % n, 0)
        t = lax.broadcasted_iota(jnp.int32, x.shape, 0)
        ok = (t + k >= 0) & (t + k < n)
        return jnp.where(ok, r, 0.0)

    shift.defvjp(lambda x: (shift(x), None), lambda _, dy: (_shift_fn(-k)(dy),))
    return shift


def _sigmoid(x):
    return 1.0 / (1.0 + jnp.exp(-x))


@jax.custom_vjp
def _softplus(x):
    y = jnp.exp(-jnp.abs(x))
    u = 1.0 + y
    l1p = jnp.where(u == 1.0, y, jnp.log(u) * y / jnp.where(u == 1.0, 1.0, u - 1.0))
    return jnp.maximum(x, 0.0) + l1p


_softplus.defvjp(lambda x: (_softplus(x), x), lambda x, dy: (dy * _sigmoid(x),))


def _silu(x):
    return x * _sigmoid(x)


def _gelu_tanh(x):
    return 0.5 * x * (1.0 + jnp.tanh(math.sqrt(2.0 / math.pi) * (x + 0.044715 * (x * x * x))))


def _block_op(name, f, grid, in_specs, out_specs, out_shapes, diff, acc=None):
    n_in, n_out = len(in_specs), len(out_specs)
    acc = acc or [None] * n_in
    didx = [i for i in range(n_in) if diff[i]]
    sem = ("arbitrary",) * len(grid)

    def run_fwd(*xs):
        def body(*refs):
            outs = f(*[r[...] for r in refs[:n_in]])
            for r, o in zip(refs[n_in:], outs):
                r[...] = o.astype(r.dtype)
        return pl.pallas_call(body, name=name + "_fwd", grid=grid, in_specs=in_specs, out_specs=out_specs,
                              out_shape=out_shapes, compiler_params=_cparams(sem))(*xs)

    def run_bwd(xs, douts):
        def body(*refs):
            ins = [r[...] for r in refs[:n_in]]
            dos = [r[...] for r in refs[n_in:n_in + n_out]]

            def g(*dv):
                full = list(ins)
                for i, v in zip(didx, dv):
                    full[i] = v
                return tuple(f(*full))

            outs, vjp = jax.vjp(g, *[ins[i] for i in didx])
            dins = vjp(tuple(d.astype(o.dtype) for d, o in zip(dos, outs)))
            for r, i, dv in zip(refs[n_in + n_out:], didx, dins):
                dv = dv.astype(r.dtype)
                if acc[i] is None:
                    r[...] = dv
                else:
                    if acc[i] == 'last':
                        first = pl.program_id(len(grid) - 1) == 0
                    else:
                        first = functools.reduce(jnp.logical_and, [pl.program_id(a) == 0 for a in range(len(grid))])

                    @pl.when(first)
                    def _(r=r, dv=dv):
                        r[...] = dv

                    @pl.when(jnp.logical_not(first))
                    def _(r=r, dv=dv):
                        r[...] += dv
        return pl.pallas_call(
            body, name=name + "_bwd", grid=grid, in_specs=list(in_specs) + list(out_specs),
            out_specs=[in_specs[i] for i in didx],
            out_shape=[jax.ShapeDtypeStruct(xs[i].shape, xs[i].dtype) for i in didx],
            compiler_params=_cparams(sem))(*xs, *douts)

    @jax.custom_vjp
    def op(*xs):
        return tuple(run_fwd(*xs))

    def op_fwd(*xs):
        return tuple(run_fwd(*xs)), xs

    def op_bwd(xs, douts):
        dins = run_bwd(xs, douts)
        full = [jnp.zeros_like(x) for x in xs]
        for i, dv in zip(didx, dins):
            full[i] = dv
        return tuple(full)

    op.defvjp(op_fwd, op_bwd)
    return op


def _sds(shape, dtype):
    return jax.ShapeDtypeStruct(tuple(shape), dtype)


def _pick(n, cands):
    for c in cands:
        if n % c == 0:
            return c
    return n


def _mm_call(name, a, b, mode, out_dtype):
    if mode == "tn":
        m, k = a.shape
        n = b.shape[1]
        tm = _pick(m, (512, 256, 128, 64))
        tn = _pick(n, (1408, 1280, 1152, 1024, 512, 256, 128))
        steps = m // tm

        def body(a_ref, b_ref, o_ref, acc_ref):
            i = pl.program_id(1)

            @pl.when(i == 0)
            def _():
                acc_ref[...] = jnp.zeros_like(acc_ref)

            acc_ref[...] += _raw_dot(a_ref[...].astype(BF16), b_ref[...].astype(BF16), True, False)

            @pl.when(i == steps - 1)
            def _():
                o_ref[...] = acc_ref[...].astype(o_ref.dtype)

        return pl.pallas_call(
            body, name=name, grid=(n // tn, steps),
            in_specs=[pl.BlockSpec((tm, k), lambda j, i: (i, 0)), pl.BlockSpec((tm, tn), lambda j, i: (i, j))],
            out_specs=pl.BlockSpec((k, tn), lambda j, i: (0, j)),
            out_shape=_sds((k, n), out_dtype),
            scratch_shapes=[pltpu.VMEM((k, tn), F32)],
            compiler_params=_cparams(("arbitrary", "arbitrary")))(a, b)

    m, k = a.shape
    n = b.shape[1] if mode == "nn" else b.shape[0]
    tm = _pick(m, (512, 256, 128, 64))
    tn = _pick(n, (1408, 1280, 1152, 1024, 512, 256, 128))

    def body(a_ref, b_ref, o_ref):
        o_ref[...] = _raw_dot(a_ref[...].astype(BF16), b_ref[...].astype(BF16), False, mode == "nt").astype(o_ref.dtype)

    b_spec = pl.BlockSpec((k, tn), lambda j, i: (0, j)) if mode == "nn" else pl.BlockSpec((tn, k), lambda j, i: (j, 0))
    return pl.pallas_call(
        body, name=name, grid=(n // tn, m // tm),
        in_specs=[pl.BlockSpec((tm, k), lambda j, i: (i, 0)), b_spec],
        out_specs=pl.BlockSpec((tm, tn), lambda j, i: (i, j)),
        out_shape=_sds((m, n), out_dtype),
        compiler_params=_cparams(("arbitrary", "arbitrary")))(a, b)


def _matmul(name, a, w, out_dtype):
    @jax.custom_vjp
    def mm(a, w):
        return _mm_call(name + "_nn", a, w, "nn", out_dtype)

    def fwd(a, w):
        return mm(a, w), (a, w)

    def bwd(res, dy):
        a, w = res
        return (_mm_call(name + "_nt", dy, w, "nt", a.dtype), _mm_call(name + "_tn", a, dy, "tn", w.dtype))

    mm.defvjp(fwd, bwd)
    return mm(a, w)


def _tok(width):
    return pl.BlockSpec((None, TT, width), lambda b, t: (b, t, 0))


def _row(width):
    return pl.BlockSpec((None, None, 1, width), lambda b, t: (b, t, 0, 0))


def _const2(shape):
    return pl.BlockSpec(shape, lambda b, t: (0,) * len(shape))


def _tok_grid():
    return (B_LOC, _ntok() // TT)


def _modulate(x, shift, scale):
    def f(x, sh, sc):
        return ((x * (1.0 + sc) + sh),)
    op = _block_op("modulate", f, _tok_grid(), [_tok(D), _row(D), _row(D)], [_tok(D)],
                   [_sds(x.shape, BF16)], [True, True, True])
    return op(x, shift, scale)[0]


def _post_norm(x, y, gate, g, b, res_w):
    def f(x, y, gate, g, b):
        z = ALPHA * x + res_w * gate * y
        mu = jnp.mean(z, axis=-1, keepdims=True)
        zc = z - mu
        var = jnp.mean(zc * zc, axis=-1, keepdims=True)
        return (zc * lax.rsqrt(var + EPS) * g + b,)
    op = _block_op("post_norm", f, _tok_grid(), [_tok(D), _tok(D), _row(D), _row(D), _row(D)], [_tok(D)],
                   [_sds(x.shape, F32)], [True] * 5)
    return op(x, y, gate, g, b)[0]


def _swiglu_gate(a, b):
    m = a.shape[0]
    tm = _pick(m, (256, 128, 64))

    def f(a, b):
        a = a.astype(F32)
        return (_silu(a) * b.astype(F32),)
    spec = pl.BlockSpec((tm, DFF), lambda i: (i, 0))
    op = _block_op("swiglu_gate", f, (m // tm,), [spec, spec], [spec], [_sds(a.shape, BF16)], [True, True])
    return op(a, b)[0]


def _seg_ones(width):
    i = np.arange(width)
    return jnp.asarray((i[:, None] // HD) == (i[None, :] // HD), BF16)


def _rope_perm(width):
    p = np.zeros((width, width), np.float32)
    for j in range(width):
        if (j % 32) < 16:
            p[j + 16, j] = -1.0
        else:
            p[j - 16, j] = 1.0
    return jnp.asarray(p, BF16)


def _rope_tables(width):
    t = jnp.arange(T_LAT)
    pos = jnp.stack([t // GRID_W, t % GRID_W], axis=-1).astype(F32)
    inv_freq = ROPE_THETA ** (-jnp.arange(ROPE_PAIRS, dtype=F32) / ROPE_PAIRS)
    ang = pos[..., None] * inv_freq
    ang = jnp.broadcast_to(ang[:, :, None, :], (T_LAT, 2, 2, ROPE_PAIRS)).reshape(T_LAT, HD)
    ang = jnp.tile(ang, (1, width // HD))
    cos = jnp.concatenate([jnp.ones((T_CTX, width), F32), jnp.cos(ang)], axis=0)
    sin = jnp.concatenate([jnp.zeros((T_CTX, width), F32), jnp.sin(ang)], axis=0)
    return cos, sin


def _att_pre(x, w_row, name):
    width = x.shape[-1]
    cos, sin = _rope_tables(width)

    def f(x, w, cos, sin, seg, perm):
        ms = _mm(x * x, seg, prec="bx") * (1.0 / HD)
        xn = x * lax.rsqrt(ms + EPS) * w
        return (xn * cos + _mm(xn, perm, prec="bx") * sin,)
    tab = pl.BlockSpec((TT, width), lambda b, t: (t, 0))
    op = _block_op(name, f, _tok_grid(),
                   [_tok(width), _row(width), tab, tab, _const2((width, width)), _const2((width, width))],
                   [_tok(width)], [_sds(x.shape, F32)], [True, True, False, False, False, False])
    return op(x, w_row, cos, sin, _seg_ones(width), _rope_perm(width))[0]


def _attention(q, k, v, name, tq):
    b_, hk, g_, tq_all, _ = q.shape
    tk = k.shape[2]

    def f(q, k, v):
        outs = []
        for gi in range(g_):
            s = _mm(q[gi], k, tb=True) * (HD ** -0.5)
            m = lax.stop_gradient(jnp.max(s, axis=-1, keepdims=True))
            e = jnp.exp(s - m)
            p = e / jnp.sum(e, axis=-1, keepdims=True)
            outs.append(_mm(p, v))
        return (jnp.stack(outs, axis=0),)
    qs = pl.BlockSpec((None, None, g_, tq, HD), lambda b, h, i: (b, h, 0, i, 0))
    ks = pl.BlockSpec((None, None, tk, HD), lambda b, h, i: (b, h, 0, 0))
    op = _block_op(name, f, (b_, hk, tq_all // tq), [qs, ks, ks], [qs], [_sds(q.shape, F32)],
                   [True, True, True], acc=[None, 'last', 'last'])
    return op(q, k, v)[0]


def _gdn_pre(qkv, conv_w):
    nt_c = GDN_QKV // 128
    flag = jnp.asarray((np.arange(nt_c) % 3 < 2).astype(np.float32)[:, None, None] * np.ones((1, 1, 128), np.float32))
    cw = jnp.broadcast_to(conv_w[None], (B_LOC,) + conv_w.shape)

    def f(x, cw, flag, seg):
        def conv(s):
            acc = cw[2:3, :] * s
            for j in (0, 1, 3, 4):
                acc = acc + cw[j:j + 1, :] * _shift_fn(j - 2)(s)
            return acc
        y = jnp.concatenate([conv(x[:T_CTX]), conv(x[T_CTX:])], axis=0)
        s = _silu(y)
        ss = _mm(s * s, seg, prec="bx")
        return (s * (flag * lax.rsqrt(ss + EPS) + (1.0 - flag)),)
    xs = pl.BlockSpec((None, _ntok(), 128), lambda b, j: (b, 0, j))
    op = _block_op("gdn_pre", f, (B_LOC, nt_c),
                   [xs, pl.BlockSpec((None, 5, 128), lambda b, j: (b, 0, j)),
                    pl.BlockSpec((None, 1, 128), lambda b, j: (j, 0, 0)), pl.BlockSpec((128, 128), lambda b, j: (0, 0))],
                   [xs], [_sds(qkv.shape, F32)], [True, True, False, False])
    return op(qkv, cw, flag, _seg_ones(128))[0]


def _gdn_gates(ba, a_log, dt_bias):
    pad = jnp.zeros((12,), F32)
    al = jnp.broadcast_to(jnp.concatenate([pad, a_log.reshape(12), jnp.zeros((104,), F32)])[None, None], (B_LOC, 1, 128))
    db = jnp.broadcast_to(jnp.concatenate([pad, dt_bias.reshape(12), jnp.zeros((104,), F32)])[None, None], (B_LOC, 1, 128))

    def f(x, al, db):
        lane = lax.broadcasted_iota(jnp.int32, x.shape, 1)
        return (jnp.where(lane < 12, _sigmoid(x), -jnp.exp(al) * _softplus(x + db)),)
    xs = pl.BlockSpec((None, _ntok(), 128), lambda b: (b, 0, 0))
    ps = pl.BlockSpec((None, 1, 128), lambda b: (b, 0, 0))
    op = _block_op("gdn_gates", f, (B_LOC,), [xs, ps, ps], [xs], [_sds(ba.shape, F32)], [True, True, True])
    return op(ba, al, db)[0]


@jax.custom_vjp
def _unit_lower_solve(lowers, rhss):
    return _solve_fwd(lowers, rhss)[0]


def _solve_fwd(lowers, rhss):
    n = lowers[0].shape[0]
    eye = (lax.broadcasted_iota(jnp.int32, (n, n), 0) == lax.broadcasted_iota(jnp.int32, (n, n), 1)).astype(F32)
    nks = [-l for l in lowers]
    invs = [eye + nk for nk in nks]
    for _ in range(int(math.log2(n)) - 1):
        nks = [_dot_impl(nk, nk, False, False, "f32") for nk in nks]
        invs = [inv + _dot_impl(inv, nk, False, False, "f32") for inv, nk in zip(invs, nks)]
    sols = tuple(_dot_impl(inv, rhs, False, False, "f32") for inv, rhs in zip(invs, rhss))
    return sols, (tuple(invs), sols)


def _solve_bwd(res, dsols):
    invs, sols = res
    drhss = tuple(_dot_impl(inv, d, True, False, "f32") for inv, d in zip(invs, dsols))
    return tuple(-_dot_impl(dr, s, False, True, "f32") for dr, s in zip(drhss, sols)), drhss


_unit_lower_solve.defvjp(_solve_fwd, _solve_bwd)


def _gdn_masks():
    ii, jj = np.arange(CHUNK)[:, None], np.arange(CHUNK)[None, :]
    fwd = [jj <= ii, jj < ii, ii <= jj]
    bwd = [jj >= ii, jj > ii, ii >= jj]
    return jnp.asarray(np.stack([np.stack(fwd), np.stack(bwd)]).astype(np.float32))


def _gdn_prep(qkv, g, beta):
    b_, t_, _ = qkv.shape
    nc = t_ // CHUNK
    cb = max(d for d in (1, 2, 3, 4, 6) if nc % d == 0)
    npair = GDN_H // 2

    def f(x, g, beta, masks):
        q2, k2, v2 = x[:, :2 * HD], x[:, 2 * HD:4 * HD], x[:, 4 * HD:]
        ii = lax.broadcasted_iota(jnp.int32, (CHUNK, CHUNK), 0)
        jj = lax.broadcasted_iota(jnp.int32, (CHUNK, CHUNK), 1)
        eye = (ii == jj).astype(F32)
        incl, strict, incl_t = masks[0] > 0.5, masks[1] > 0.5, masks[2] > 0.5
        items = [(hh, c) for hh in range(2) for c in range(cb)]
        qs, ks, vs, decays, kbs, rhss, qgs, kds, egs = [], [], [], [], [], [], [], [], []
        for hh, c in items:
            r = slice(c * CHUNK, (c + 1) * CHUNK)
            lanes = slice(hh * HD, (hh + 1) * HD)
            qc, kc, vc = q2[r, lanes] * (HD ** -0.5), k2[r, lanes], v2[r, lanes]
            g_row, b_row = g[hh, c], beta[hh, c]
            g_col = jnp.sum(eye * g_row, axis=1, keepdims=True)
            b_col = jnp.sum(eye * b_row, axis=1, keepdims=True)
            gc_col = jnp.sum(jnp.where(incl, g_row, 0.0), axis=1, keepdims=True)
            gc_row = jnp.sum(jnp.where(incl_t, g_col, 0.0), axis=0, keepdims=True)
            g_tot = jnp.sum(g_row, axis=1, keepdims=True)
            decays.append(jnp.where(incl, jnp.exp(jnp.where(incl, gc_col - gc_row, 0.0)), 0.0))
            kb = kc * b_col
            qs.append(qc)
            ks.append(kc)
            kbs.append(kb)
            rhss.append(jnp.concatenate([vc * b_col, kb * jnp.exp(gc_col)], axis=1))
            qgs.append(qc * jnp.exp(gc_col))
            kds.append(kc * jnp.exp(g_tot - gc_col))
            egs.append(jnp.broadcast_to(jnp.exp(g_tot), (1, CHUNK)))
        lowers = tuple(jnp.where(strict, _mm(kb, kc, tb=True) * dec, 0.0) for kb, kc, dec in zip(kbs, ks, decays))
        ins = [jnp.where(incl, _mm(qc, kc, tb=True) * dec, 0.0) for qc, kc, dec in zip(qs, ks, decays)]
        sols = _unit_lower_solve(lowers, tuple(rhss))
        us, ws = [s[:, :HD] for s in sols], [s[:, HD:] for s in sols]

        def heads(xs, joiner):
            return jnp.stack([joiner(xs[:cb]), joiner(xs[cb:])], axis=0)
        cat = lambda xs: jnp.concatenate(xs, axis=0)
        return (heads(us, cat), heads(ws, cat), heads(qgs, cat), heads(kds, cat), heads(ins, cat),
                heads(egs, lambda xs: jnp.stack(xs, axis=0)))

    xs = pl.BlockSpec((None, cb * CHUNK, 6 * HD), lambda b, p, i, d: (b, i, p))
    rs = pl.BlockSpec((None, None, 2, None, cb, 1, CHUNK), lambda b, p, i, d: (b, p, 0, d, i, 0, 0))
    ts = pl.BlockSpec((None, None, 2, None, cb * CHUNK, HD), lambda b, p, i, d: (b, p, 0, d, i, 0))
    ms = pl.BlockSpec((None, 3, CHUNK, CHUNK), lambda b, p, i, d: (d, 0, 0, 0))
    big = _sds((b_, npair, 2, 2, t_, HD), F32)
    op = _block_op("gdn_prep", f, (b_, npair, nc // cb, 2), [xs, rs, rs, ms],
                   [ts, ts, ts, ts, ts, rs], [big, big, big, big, big, _sds(g.shape, F32)],
                   [True, True, True, False], acc=['last', None, None, None])
    return op(qkv, g, beta, _gdn_masks())


def _gdn_chunk_of(i, seq, nc):
    ncc = T_CTX // CHUNK
    back = jnp.where(i < ncc, ncc - 1 - i, nc + ncc - 1 - i)
    return jnp.where(seq % 2 == 0, i, back)


def _gdn_scan_call(u, w, qg, kd, intra, eg):
    s_, t_, _ = u.shape
    nc = t_ // CHUNK
    hb = 2

    def body(u_ref, w_ref, qg_ref, kd_ref, in_ref, eg_ref, o_ref, st_ref):
        def step(i, states):
            new = []
            for h in range(hb):
                c = _gdn_chunk_of(i, pl.program_id(0) * hb + h, nc)
                rows = pl.ds(pl.multiple_of(c * CHUNK, CHUNK), CHUNK)
                st = states[h]
                st_ref[h, rows, :] = st
                sb = st.astype(BF16)
                vnew = u_ref[h, rows, :] - _raw_dot(w_ref[h, rows, :].astype(BF16), sb, False, False)
                vb = vnew.astype(BF16)
                o_ref[h, rows, :] = (_raw_dot(qg_ref[h, rows, :].astype(BF16), sb, False, False)
                                     + _raw_dot(in_ref[h, rows, :].astype(BF16), vb, False, False))
                e = eg_ref[h, pl.ds(c, 1), :, :].reshape(1, CHUNK)
                new.append(st * e + _raw_dot(kd_ref[h, rows, :].astype(BF16), vb, True, False))
            return tuple(new)
        lax.fori_loop(0, nc, step, tuple(jnp.zeros((HD, HD), F32) for _ in range(hb)))
    ts = pl.BlockSpec((hb, t_, HD), lambda s: (s, 0, 0))
    es = pl.BlockSpec((hb, nc, 1, CHUNK), lambda s: (s, 0, 0, 0))
    return pl.pallas_call(body, name="gdn_scan_fwd", grid=(s_ // hb,), in_specs=[ts] * 5 + [es], out_specs=[ts, ts],
                          out_shape=[_sds(u.shape, F32), _sds(u.shape, F32)],
                          compiler_params=_cparams(("arbitrary",)))(u, w, qg, kd, intra, eg)


def _gdn_scan_bwd_call(u, w, qg, kd, intra, eg, states, do):
    s_, t_, _ = u.shape
    nc = t_ // CHUNK
    hb = 1

    def body(u_ref, w_ref, qg_ref, kd_ref, in_ref, eg_ref, st_ref, do_ref, du_ref, dw_ref, dqg_ref, dkd_ref, din_ref, deg_ref):
        def step(i, dstates):
            new = []
            for h in range(hb):
                c = _gdn_chunk_of(nc - 1 - i, pl.program_id(0) * hb + h, nc)
                rows = pl.ds(pl.multiple_of(c * CHUNK, CHUNK), CHUNK)
                ds = dstates[h]
                st = st_ref[h, rows, :]
                sb, dsb = st.astype(BF16), ds.astype(BF16)
                wb, kdb = w_ref[h, rows, :].astype(BF16), kd_ref[h, rows, :].astype(BF16)
                inb, qgb = in_ref[h, rows, :].astype(BF16), qg_ref[h, rows, :].astype(BF16)
                dob = do_ref[h, rows, :].astype(BF16)
                vnew = u_ref[h, rows, :] - _raw_dot(wb, sb, False, False)
                vb = vnew.astype(BF16)
                din_ref[h, rows, :] = _raw_dot(dob, vb, False, True)
                dvn = _raw_dot(inb, dob, True, False) + _raw_dot(kdb, dsb, False, False)
                dvb = dvn.astype(BF16)
                dqg_ref[h, rows, :] = _raw_dot(dob, sb, False, True)
                dkd_ref[h, rows, :] = _raw_dot(vb, dsb, False, True)
                deg_ref[h, pl.ds(c, 1), :, :] = jnp.sum(st * ds, axis=0, keepdims=True).reshape(1, 1, CHUNK)
                du_ref[h, rows, :] = dvn
                dw_ref[h, rows, :] = -_raw_dot(dvb, sb, False, True)
                e = eg_ref[h, pl.ds(c, 1), :, :].reshape(1, CHUNK)
                new.append(ds * e + _raw_dot(qgb, dob, True, False) - _raw_dot(wb, dvb, True, False))
            return tuple(new)
        lax.fori_loop(0, nc, step, tuple(jnp.zeros((HD, HD), F32) for _ in range(hb)))
    ts = pl.BlockSpec((hb, t_, HD), lambda s: (s, 0, 0))
    es = pl.BlockSpec((hb, nc, 1, CHUNK), lambda s: (s, 0, 0, 0))
    big = _sds(u.shape, F32)
    return pl.pallas_call(body, name="gdn_scan_bwd", grid=(s_ // hb,), in_specs=[ts] * 5 + [es, ts, ts],
                          out_specs=[ts] * 5 + [es], out_shape=[big] * 5 + [_sds(eg.shape, F32)],
                          compiler_params=_cparams(("arbitrary",)))(u, w, qg, kd, intra, eg, states, do)


@jax.custom_vjp
def _gdn_scan(u, w, qg, kd, intra, eg):
    return _gdn_scan_call(u, w, qg, kd, intra, eg)[0]


def _gdn_scan_f(u, w, qg, kd, intra, eg):
    o, states = _gdn_scan_call(u, w, qg, kd, intra, eg)
    return o, (u, w, qg, kd, intra, eg, states)


def _gdn_scan_b(res, do):
    return tuple(_gdn_scan_bwd_call(*res, do))


_gdn_scan.defvjp(_gdn_scan_f, _gdn_scan_b)


def _gdn_post(o, z, w_row):
    def f(o, z, w, seg):
        ms = _mm(o * o, seg, prec="bx") * (1.0 / HD)
        return (o * lax.rsqrt(ms + EPS) * w * _silu(z),)
    op = _block_op("gdn_post", f, _tok_grid(), [_tok(GDN_W), _tok(GDN_W), _row(GDN_W), _const2((GDN_W, GDN_W))],
                   [_tok(GDN_W)], [_sds(o.shape, BF16)], [True, True, True, False])
    return op(o, z, w_row, _seg_ones(GDN_W))[0]


def _s5_tables(ar, ai, rev):
    pr, pi = [ar], [ai]
    for _ in range(7):
        pr, pi = pr + [pr[-1] * ar - pi[-1] * ai], pi + [pr[-1] * ai + pi[-1] * ar]
    if rev:
        pr, pi = pr[::-1], pi[::-1]
    return jnp.concatenate(pr, axis=0), jnp.concatenate(pi, axis=0)


def _s5_scan_call(bu, ar, ai, rev, h=None):
    b_, t_, n2 = bu.shape
    nblk = n2 // (2 * S5_LB)
    tr, ti = _s5_tables(ar, ai, rev)
    tab = jnp.concatenate([tr.reshape(8, nblk, 1, S5_LB), ti.reshape(8, nblk, 1, S5_LB)], axis=2).reshape(8, n2)
    ntile = t_ // 8
    with_grad = h is not None

    def scan_tile(xr, xi, tabr, tabi, cr, ci):
        row = lax.broadcasted_iota(jnp.int32, xr.shape, 0)
        for k in (1, 2, 4):
            idx = (8 - k) if rev else (k - 1)
            akr, aki = tabr[idx:idx + 1, :], tabi[idx:idx + 1, :]
            sh = (8 - k) if rev else k
            sr, si = pltpu.roll(xr, sh, 0), pltpu.roll(xi, sh, 0)
            ok = (row < 8 - k) if rev else (row >= k)
            xr, xi = (xr + jnp.where(ok, akr * sr - aki * si, 0.0), xi + jnp.where(ok, akr * si + aki * sr, 0.0))
        return xr + tabr * cr - tabi * ci, xi + tabr * ci + tabi * cr

    def body(*refs):
        if with_grad:
            bu_ref, tab_ref, h_ref, o_ref, da_ref = refs
        else:
            bu_ref, tab_ref, o_ref = refs
        tabr, tabi = tab_ref[:, :S5_LB], tab_ref[:, S5_LB:]
        zero = jnp.zeros((1, S5_LB), F32)
        row = lax.broadcasted_iota(jnp.int32, (8, S5_LB), 0)

        def step(i, carry):
            cr, ci = carry[0], carry[1]
            tix = (ntile - 1 - i) if rev else i
            rows = pl.ds(pl.multiple_of(tix * 8, 8), 8)
            hr, hi = scan_tile(bu_ref[rows, :S5_LB], bu_ref[rows, S5_LB:], tabr, tabi, cr, ci)
            o_ref[rows, :S5_LB] = hr
            o_ref[rows, S5_LB:] = hi
            out = (hr[0:1, :], hi[0:1, :]) if rev else (hr[7:8, :], hi[7:8, :])
            if with_grad:
                prev = pl.ds(pl.multiple_of(jnp.maximum(tix - 1, 0) * 8, 8), 8)
                live = jnp.where(tix > 0, 1.0, 0.0)
                pr = jnp.where(row == 0, pltpu.roll(h_ref[prev, :S5_LB], 1, 0) * live, pltpu.roll(h_ref[rows, :S5_LB], 1, 0))
                pi = jnp.where(row == 0, pltpu.roll(h_ref[prev, S5_LB:], 1, 0) * live, pltpu.roll(h_ref[rows, S5_LB:], 1, 0))
                out = out + (carry[2] + hr * pr + hi * pi, carry[3] + hi * pr - hr * pi)
            return out
        init = (zero, zero) + ((jnp.zeros((8, S5_LB), F32),) * 2 if with_grad else ())
        fin = lax.fori_loop(0, ntile, step, init)
        if with_grad:
            da_ref[:, :S5_LB] = fin[2]
            da_ref[:, S5_LB:] = fin[3]
    xs = pl.BlockSpec((None, t_, 2 * S5_LB), lambda b, j: (b, 0, j))
    tb = pl.BlockSpec((8, 2 * S5_LB), lambda b, j: (0, j))
    if with_grad:
        return pl.pallas_call(body, name="s5_scan_bwd", grid=(b_, nblk), in_specs=[xs, tb, xs],
                              out_specs=[xs, pl.BlockSpec((None, 8, 2 * S5_LB), lambda b, j: (b, 0, j))],
                              out_shape=[_sds(bu.shape, F32), _sds((b_, 8, n2), F32)],
                              compiler_params=_cparams(("arbitrary", "arbitrary")))(bu, tab, h)
    return pl.pallas_call(body, name="s5_scan_fwd", grid=(b_, nblk), in_specs=[xs, tb], out_specs=xs,
                          out_shape=_sds(bu.shape, F32), compiler_params=_cparams(("arbitrary", "arbitrary")))(bu, tab)


@jax.custom_vjp
def _s5_scan(bu, ar, ai):
    return _s5_scan_call(bu, ar, ai, False)


def _s5_scan_f(bu, ar, ai):
    h = _s5_scan_call(bu, ar, ai, False)
    return h, (h, ar, ai)


def _s5_scan_b(res, dh):
    h, ar, ai = res
    lam, da = _s5_scan_call(dh, ar, -ai, True, h=h)
    nblk = da.shape[-1] // (2 * S5_LB)
    da = jnp.sum(da, axis=(0, 1)).reshape(nblk, 2, S5_LB)
    return lam, da[:, 0].reshape(1, -1), da[:, 1].reshape(1, -1)


_s5_scan.defvjp(_s5_scan_f, _s5_scan_b)


def _s5_post(u, y0, y1, d_row, glu_w, glu_b_row):
    def f(u, y0, y1, d, gw, gb):
        zz = _gelu_tanh(d * u + y0 + y1)
        return (zz * _sigmoid(_mm(zz, gw) + gb),)
    op = _block_op("s5_post", f, _tok_grid(),
                   [_tok(S5_W), _tok(S5_W), _tok(S5_W), _row(S5_W), _const2((S5_W, S5_W)), _row(S5_W)],
                   [_tok(S5_W)], [_sds(u.shape, BF16)], [True] * 6, acc=[None, None, None, None, 'all', None])
    return op(u, y0, y1, d_row, glu_w, glu_b_row)[0]


def _loss_rows(x, target):
    def f(x, t):
        e = x - t
        return (jnp.sum(e * e, axis=0, keepdims=True),)
    grid = (B_LOC, T_LAT // TT)
    op = _block_op("loss_rows", f, grid, [_tok(D), _tok(D)], [_row(D)], [_sds((B_LOC, T_LAT // TT, 1, D), F32)],
                   [True, False])
    return op(x, target)[0]


def _adamw_call(name, w, g, m, v, pieces):
    r, c = w.shape
    tr = _pick(r, (256, 128, 64, 32, 16, 8))
    c1, c2 = 1.0 - ADAM_B1 ** ADAM_STEP, 1.0 - ADAM_B2 ** ADAM_STEP

    def body(w_ref, g_ref, m_ref, v_ref, go_ref, d_ref, mo_ref, vo_ref):
        if pieces:
            g = g_ref[0].astype(F32)
            for i in range(1, NDEV):
                g = g + g_ref[i].astype(F32)
        else:
            g = g_ref[...]
        m = ADAM_B1 * m_ref[...] + (1.0 - ADAM_B1) * g
        v = ADAM_B2 * v_ref[...] + (1.0 - ADAM_B2) * (g * g)
        go_ref[...] = g
        mo_ref[...] = m
        vo_ref[...] = v
        d_ref[...] = -ADAM_LR * ((m / c1) / (jnp.sqrt(v / c2) + ADAM_EPS) + ADAM_WD * w_ref[...])
    spec = pl.BlockSpec((tr, c), lambda i: (i, 0))
    gspec = pl.BlockSpec((NDEV, tr, c), lambda i: (0, i, 0)) if pieces else spec
    return pl.pallas_call(body, name=name, grid=(r // tr,), in_specs=[spec, gspec, spec, spec], out_specs=[spec] * 4,
                          out_shape=[_sds((r, c), F32)] * 4, compiler_params=_cparams(("arbitrary",)))(w, g, m, v)


def _sum_pieces(x, name):
    _, r, c = x.shape
    tr = _pick(r, (512, 256, 128, 64, 32, 16, 8))

    def body(x_ref, o_ref):
        acc = x_ref[0]
        for i in range(1, NDEV):
            acc = acc + x_ref[i]
        o_ref[...] = acc
    return pl.pallas_call(body, name=name, grid=(r // tr,), in_specs=[pl.BlockSpec((NDEV, tr, c), lambda i: (0, i, 0))],
                          out_specs=pl.BlockSpec((tr, c), lambda i: (i, 0)), out_shape=_sds((r, c), F32),
                          compiler_params=_cparams(("arbitrary",)))(x)


def _pack_flat(arrs, lanes, row_mult):
    flat = jnp.concatenate([a.reshape(-1).astype(F32) for a in arrs])
    n = flat.shape[0]
    rows = -(-n // lanes)
    rows = -(-rows // row_mult) * row_mult
    return jnp.pad(flat, (0, rows * lanes - n)).reshape(rows, lanes)


def _unpack_flat(packed, shapes, lead=()):
    flat = packed.reshape(lead + (-1,))
    out, off = [], 0
    for s in shapes:
        n = int(np.prod(s))
        out.append(flat[..., off:off + n].reshape(lead + tuple(s)))
        off += n
    return out


FS_ROWS = DFF // NDEV
WI_ROWS = IN_COLS // NDEV
WI_PAD = -(-WI_ROWS // 16) * 16
WO_ROWS = D // NDEV


def _pad_rows(blk, rows, axis):
    pad = [(0, 0)] * blk.ndim
    pad[axis] = (0, rows - blk.shape[axis])
    return jnp.pad(blk, pad)


def _pack_big_shards(t, dtype):
    parts = []
    for l in range(DEPTH):
        for n in ('ffn_w1', 'ffn_w3', 'ffn_w2'):
            for i in range(2):
                parts.append(t[n][l, i].reshape(-1, D))
        parts.append(_pad_rows(t['w_in'][l].reshape(-1, D), WI_PAD, 0))
        parts.append(t['w_out'][l].reshape(-1, D))
    return jnp.concatenate(parts, axis=0).astype(dtype)


def _unpack_big_shards(p):
    out = {n: [] for n in BIG}
    off = 0
    for l in range(DEPTH):
        for n in ('ffn_w1', 'ffn_w3', 'ffn_w2'):
            pair = []
            for i in range(2):
                blk = p[off:off + FS_ROWS]
                off += FS_ROWS
                pair.append(blk.reshape(FS_ROWS, D) if n == 'ffn_w2' else blk.reshape(D, FS_ROWS))
            out[n].append(jnp.stack(pair))
        out['w_in'].append(p[off:off + WI_ROWS].reshape(D, WI_ROWS))
        off += WI_PAD
        out['w_out'].append(p[off:off + WO_ROWS].reshape(WO_ROWS, D))
        off += WO_ROWS
    return {n: jnp.stack(v) for n, v in out.items()}


def _full_from_gathered(g):
    layers, off = [], 0

    def cols(blk, n):
        return blk.reshape(NDEV, D, n).transpose(1, 0, 2).reshape(D, NDEV * n)
    for l in range(DEPTH):
        lw = {}
        for n in ('ffn_w1', 'ffn_w3', 'ffn_w2'):
            pair = []
            for i in range(2):
                blk = g[:, off:off + FS_ROWS]
                off += FS_ROWS
                pair.append(blk.reshape(DFF, D) if n == 'ffn_w2' else cols(blk, FS_ROWS))
            lw[n] = pair
        lw['w_in'] = cols(g[:, off:off + WI_ROWS], WI_ROWS)
        off += WI_PAD
        lw['w_out'] = g[:, off:off + WO_ROWS].reshape(D, D)
        off += WO_ROWS
        layers.append(lw)
    return layers


def _pieces_from_full(layers):
    def cols(wf, n):
        return wf.reshape(D, NDEV, n).transpose(1, 0, 2).reshape(NDEV, n, D)
    parts = []
    for lw in layers:
        for n in ('ffn_w1', 'ffn_w3', 'ffn_w2'):
            for i in range(2):
                parts.append(lw[n][i].reshape(NDEV, FS_ROWS, D) if n == 'ffn_w2' else cols(lw[n][i], FS_ROWS))
        parts.append(_pad_rows(cols(lw['w_in'], WI_ROWS), WI_PAD, 1))
        parts.append(lw['w_out'].reshape(NDEV, WO_ROWS, D))
    return jnp.concatenate(parts, axis=1)


def _flip_segments(a, axis):
    ctx, lat = lax.slice_in_dim(a, 0, T_CTX, axis=axis), lax.slice_in_dim(a, T_CTX, _ntok(), axis=axis)
    return jnp.concatenate([jnp.flip(ctx, axis), jnp.flip(lat, axis)], axis=axis)


def _rows_of(vec_ctx, vec_lat):
    w = vec_lat.shape[-1]
    ntc, ntl = T_CTX // TT, T_LAT // TT
    return jnp.concatenate([jnp.broadcast_to(vec_ctx[None, None, None, :], (B_LOC, ntc, 1, w)),
                            jnp.broadcast_to(vec_lat[:, None, None, :], (B_LOC, ntl, 1, w))], axis=1)


def _rows_const(vec):
    return jnp.broadcast_to(vec[None, None, None, :], (B_LOC, _ntok() // TT, 1, vec.shape[-1]))


def _ffn_sublayer(xt, mrow, w1, w3, w2, g, b):
    h = _modulate(xt, mrow[0], mrow[1]).reshape(-1, D)
    a = _matmul("ffn_up", h, w1, BF16)
    bb = _matmul("ffn_up", h, w3, BF16)
    u = _swiglu_gate(a, bb)
    y = _matmul("ffn_down", u, w2, F32).reshape(xt.shape)
    return _post_norm(xt, y, mrow[2], _rows_const(g), _rows_const(b), 0.5)


def _s5_discretize(lam_re, lam_im, log_dt, b_re, b_im):
    dt = jnp.exp(log_dt)[:, None]
    zr, zi = lam_re * dt, lam_im * dt
    er = jnp.exp(zr)
    lbr, lbi = er * jnp.cos(zi), er * jnp.sin(zi)
    dd = lam_re * lam_re + lam_im * lam_im
    qr = ((lbr - 1.0) * lam_re + lbi * lam_im) / dd
    qi = (lbi * lam_re - (lbr - 1.0) * lam_im) / dd
    bbr = qr[..., None] * b_re - qi[..., None] * b_im
    bbi = qr[..., None] * b_im + qi[..., None] * b_re
    return lbr, lbi, bbr, bbi


def _s5_cols(a):
    return a.reshape(a.shape[:-1] + (S5_N // S5_LB, S5_LB))


def _s5_group(su, p):
    b_, t_, _ = su.shape
    eye = jnp.eye(S5_G, dtype=F32)
    ys = []
    for d in range(2):
        lbr, lbi, bbr, bbi = _s5_discretize(p['s5_lam_re'][d], p['s5_lam_im'][d], p['s5_log_dt'][d],
                                            p['s5_b_re'][d], p['s5_b_im'][d])
        bre = jnp.einsum('gph,gk->ghkp', bbr, eye).reshape(S5_W, S5_N)
        bim = jnp.einsum('gph,gk->ghkp', bbi, eye).reshape(S5_W, S5_N)
        bmat = jnp.stack([_s5_cols(bre), _s5_cols(bim)], axis=2).reshape(S5_W, 2 * S5_N)
        cre = jnp.einsum('ghp,gk->kpgh', p['s5_c_re'][d], eye).reshape(S5_N, S5_W)
        cim = -jnp.einsum('ghp,gk->kpgh', p['s5_c_im'][d], eye).reshape(S5_N, S5_W)
        cmat = jnp.stack([cre.reshape(S5_N // S5_LB, S5_LB, S5_W), cim.reshape(S5_N // S5_LB, S5_LB, S5_W)],
                         axis=1).reshape(2 * S5_N, S5_W)
        ud = su if d == 0 else _flip_segments(su, 1)
        bu = _matmul("s5_in", ud.reshape(-1, S5_W), bmat, F32).reshape(b_, t_, 2 * S5_N)
        hs = _s5_scan(bu, lbr.reshape(1, S5_N), lbi.reshape(1, S5_N))
        y = _matmul("s5_out", hs.reshape(-1, 2 * S5_N), cmat, F32).reshape(b_, t_, S5_W)
        ys.append(y if d == 0 else _flip_segments(y, 1))
    return _s5_post(su, ys[0], ys[1], _rows_const(p['s5_d']), p['glu_w'], _rows_const(p['glu_b']))


def _gdn_group(qkv, z, ba, p):
    b_, t_, _ = qkv.shape
    nc = t_ // CHUNK
    qkvn = _gdn_pre(qkv, _pair_major(p['gdn_conv_w']))
    bg = _gdn_gates(ba, p['gdn_a_log'], p['gdn_dt_bias'])

    def gates(a):
        a = a.reshape(b_, t_, 2, GDN_H // 2, 2).transpose(0, 3, 4, 2, 1)
        return a.reshape(b_, GDN_H // 2, 2, 2, nc, 1, CHUNK)
    outs = _gdn_prep(qkvn, gates(bg[..., 12:24]), gates(bg[..., 0:12]))
    u, w, qg, kd, intra = [a.reshape(b_ * GDN_H * 2, t_, HD) for a in outs[:5]]
    o = _gdn_scan(u, w, qg, kd, intra, outs[5].reshape(b_ * GDN_H * 2, nc, 1, CHUNK)).reshape(b_, GDN_H, 2, t_, HD)
    o = (o[:, :, 0] + o[:, :, 1]).transpose(0, 2, 1, 3).reshape(b_, t_, GDN_W)
    return _gdn_post(o, z, _rows_const(jnp.tile(p['gdn_norm_w'], GDN_H)))


def _att_group(aq, ak, av, p):
    b_, t_, _ = aq.shape
    qn = _att_pre(aq, _rows_const(jnp.tile(p['q_norm_w'], ATT_W // HD)), "att_pre_q")
    kn = _att_pre(ak, _rows_const(jnp.tile(p['k_norm_w'], ATT_KW // HD)), "att_pre_k")
    q = qn.reshape(b_, t_, ATT_HKV, ATT_G, HD).transpose(0, 2, 3, 1, 4)
    k = kn.reshape(b_, t_, ATT_HKV, HD).transpose(0, 2, 1, 3)
    v = av.reshape(b_, t_, ATT_HKV, HD).transpose(0, 2, 1, 3)
    o_lat = _attention(q[:, :, :, T_CTX:], k, v, "att_lat", ATT_TQ)
    o_ctx = _attention(q[:, :, :, :T_CTX], k[:, :, :T_CTX], v[:, :, :T_CTX], "att_ctx", T_CTX)
    o = jnp.concatenate([o_ctx, o_lat], axis=3)
    return o.transpose(0, 3, 1, 2, 4).reshape(b_, t_, ATT_W)


def _pair_major(w):
    lead = w.shape[:-1]
    return w.reshape(lead + (3, GDN_H // 2, 2 * HD)).swapaxes(-3, -2).reshape(lead + (GDN_QKV,))


def _permute_w_in(w):
    return jnp.concatenate([_pair_major(w[:, :GDN_QKV]), w[:, GDN_QKV:1536], w[:, 1560:], w[:, 1536:1560],
                            jnp.zeros((D, IN_PAD - IN_COLS), w.dtype)], axis=1)


PROJ_CUTS = (0, 1152, 1536, 1920, 2048, 2176, 2432, 2560)


@jax.custom_vjp
def _split_proj(proj):
    return tuple(proj[..., a:b] for a, b in zip(PROJ_CUTS[:-1], PROJ_CUTS[1:]))


_split_proj.defvjp(lambda proj: (_split_proj(proj), None), lambda _, d: (jnp.concatenate(d, axis=-1),))


def _mixer_sublayer(xt, mrow, lw, p, g, b):
    h = _modulate(xt, mrow[3], mrow[4]).reshape(-1, D)
    proj = _matmul("mix_in", h, _permute_w_in(lw['w_in']), F32).reshape(xt.shape[:2] + (IN_PAD,))
    qkv, z, aq, ak, av, su, ba = _split_proj(proj)
    o_gdn = _gdn_group(qkv, z, ba, p)
    o_att = _att_group(aq, ak, av, p)
    o_s5 = _s5_group(su, p)
    cat = jnp.concatenate([o_gdn, o_att.astype(BF16), o_s5], axis=-1).reshape(-1, D)
    y = _matmul("mix_out", cat, lw['w_out'], F32).reshape(xt.shape)
    return _post_norm(xt, y, mrow[5], _rows_const(g), _rows_const(b), 1.0)


def _local_loss(x, mod, modc, big, small, ctx, target):
    xt = jnp.concatenate([ctx, x], axis=1)
    for l in range(DEPTH):
        mrow = [_rows_of(modc[l, k * D:(k + 1) * D], mod[l, :, k * D:(k + 1) * D]) for k in range(N_MOD)]
        p = {n: small[n][l] for n in small}
        lw = big[l]
        xt = _ffn_sublayer(xt, mrow[0:3], lw['ffn_w1'][0], lw['ffn_w3'][0], lw['ffn_w2'][0], p['ln_g'][0], p['ln_b'][0])
        xt = _mixer_sublayer(xt, mrow, lw, p, p['ln_g'][1], p['ln_b'][1])
        xt = _ffn_sublayer(xt, mrow[6:9], lw['ffn_w1'][1], lw['ffn_w3'][1], lw['ffn_w2'][1], p['ln_g'][2], p['ln_b'][2])
    part = _loss_rows(xt[:, T_CTX:], target)
    return (0.5 / D) * jnp.sum(part)


def _silu_plain(x):
    return x * jax.nn.sigmoid(x)


def _dsilu_plain(x):
    s = jax.nn.sigmoid(x)
    return s * (1.0 + x * (1.0 - s))


def _small_shapes():
    return {'ln_g': (DEPTH, 3, D), 'ln_b': (DEPTH, 3, D), 'gdn_conv_w': (DEPTH, 5, GDN_QKV), 'glu_w': (DEPTH, S5_W, S5_W),
            'gdn_a_log': (DEPTH, 2, GDN_H), 'gdn_dt_bias': (DEPTH, 2, GDN_H), 'gdn_norm_w': (DEPTH, HD),
            'q_norm_w': (DEPTH, HD), 'k_norm_w': (DEPTH, HD), 's5_lam_re': (DEPTH, 2, S5_G, S5_P),
            's5_lam_im': (DEPTH, 2, S5_G, S5_P), 's5_log_dt': (DEPTH, 2, S5_G),
            's5_b_re': (DEPTH, 2, S5_G, S5_P, S5_H), 's5_b_im': (DEPTH, 2, S5_G, S5_P, S5_H),
            's5_c_re': (DEPTH, 2, S5_G, S5_H, S5_P), 's5_c_im': (DEPTH, 2, S5_G, S5_H, S5_P),
            's5_d': (DEPTH, S5_W), 'glu_b': (DEPTH, S5_W)}


def _gather_small_sharded(gathered, name):
    if name == 'glu_w':
        return gathered.transpose(1, 0, 2, 3).reshape(DEPTH, S5_W, S5_W)
    lead = gathered.shape[1:-1]
    return jnp.moveaxis(gathered, 0, -2).reshape(lead + (-1,))


def _my_small_shard(full, name, me):
    if name == 'glu_w':
        return lax.dynamic_slice_in_dim(full, me * (S5_W // NDEV), S5_W // NDEV, axis=1)
    n = full.shape[-1] // NDEV
    return lax.dynamic_slice_in_dim(full, me * n, n, axis=full.ndim - 1)


def kernel(x, c, ctx, c_ctx, w_ada, b_ada, ln_g, ln_b, ffn_w1, ffn_w3, ffn_w2, w_in, w_out, gdn_conv_w, gdn_a_log, gdn_dt_bias, gdn_norm_w, q_norm_w, k_norm_w, s5_lam_re, s5_lam_im, s5_log_dt, s5_b_re, s5_b_im, s5_c_re, s5_c_im, s5_d, glu_w, glu_b, loss_target, m_c_ctx, m_w_ada, m_b_ada, m_ln_g, m_ln_b, m_ffn_w1, m_ffn_w3, m_ffn_w2, m_w_in, m_w_out, m_gdn_conv_w, m_gdn_a_log, m_gdn_dt_bias, m_gdn_norm_w, m_q_norm_w, m_k_norm_w, m_s5_lam_re, m_s5_lam_im, m_s5_log_dt, m_s5_b_re, m_s5_b_im, m_s5_c_re, m_s5_c_im, m_s5_d, m_glu_w, m_glu_b, v_c_ctx, v_w_ada, v_b_ada, v_ln_g, v_ln_b, v_ffn_w1, v_ffn_w3, v_ffn_w2, v_w_in, v_w_out, v_gdn_conv_w, v_gdn_a_log, v_gdn_dt_bias, v_gdn_norm_w, v_q_norm_w, v_k_norm_w, v_s5_lam_re, v_s5_lam_im, v_s5_log_dt, v_s5_b_re, v_s5_b_im, v_s5_c_re, v_s5_c_im, v_s5_d, v_glu_w, v_glu_b):
    a = dict(locals())
    me = _my_index()
    ada_cols = N_MOD * D // NDEV

    sc = _silu_plain(a['c'])
    scc = _silu_plain(a['c_ctx'])
    small_in = [sc] + [a[n] for n in SMALL_SHARDED]
    got = _all_gather(_pack_flat(small_in, 128, 8), "gather_small")
    parts = _unpack_flat(got, [t.shape for t in small_in], lead=(NDEV,))
    sc_all = parts[0].reshape(NDEV * B_LOC, D)
    small = {n: _gather_small_sharded(parts[1 + i], n) for i, n in enumerate(SMALL_SHARDED)}
    for n in SMALL_REPL:
        small[n] = a[n]

    nb = NDEV * B_LOC
    rows_pad = 8
    sc_rows = jnp.concatenate([sc_all, scc[None], jnp.zeros((rows_pad - 1, D), F32)], axis=0)
    mod_part = jnp.stack([_mm_call("ada_fwd", sc_rows, a['w_ada'][l], "nn", F32) for l in range(DEPTH)])
    mod_all = _all_gather(mod_part.reshape(DEPTH * (nb + rows_pad), ada_cols), "gather_mod")
    mod_all = mod_all.reshape(NDEV, DEPTH, nb + rows_pad, ada_cols).transpose(1, 2, 0, 3).reshape(DEPTH, nb + rows_pad, N_MOD * D)
    mod_all = mod_all + a['b_ada'][:, None, :]
    mod = lax.dynamic_slice_in_dim(mod_all, me * B_LOC, B_LOC, axis=1)
    modc = mod_all[:, nb]

    big = _full_from_gathered(_all_gather(_pack_big_shards(a, BF16), "gather_weights"))

    loss_part, grads = jax.value_and_grad(_local_loss, argnums=(0, 1, 2, 3, 4))(
        a['x'], mod, modc, big, small, a['ctx'], a['loss_target'])
    gx, gmod, gmodc, gbig, gsmall = grads

    gm_rows = jnp.concatenate([gmod, gmodc[:, None], jnp.zeros((DEPTH, rows_pad - B_LOC - 1, N_MOD * D), F32)], axis=1)
    gm_all = _all_gather(gm_rows.reshape(DEPTH * rows_pad, N_MOD * D), "gather_dmod").reshape(NDEV, DEPTH, rows_pad, N_MOD * D)
    gm_all = gm_all.transpose(1, 0, 2, 3).reshape(DEPTH, NDEV * rows_pad, N_MOD * D)
    g_b_ada = jnp.sum(gm_all, axis=1)
    sc_dev = jnp.concatenate([sc_all.reshape(NDEV, B_LOC, D), jnp.broadcast_to(scc[None, None], (NDEV, 1, D)),
                              jnp.zeros((NDEV, rows_pad - B_LOC - 1, D), F32)], axis=1).reshape(NDEV * rows_pad, D)
    gm_mine = lax.dynamic_slice_in_dim(gm_all, me * ada_cols, ada_cols, axis=2)
    g_w_ada = jnp.stack([_mm_call("ada_dw", sc_dev, gm_mine[l], "tn", F32) for l in range(DEPTH)])
    gmc = gm_mine.reshape(DEPTH, NDEV, rows_pad, ada_cols)[:, :, B_LOC].sum(axis=1)
    gmc = jnp.concatenate([gmc[:, None], jnp.zeros((DEPTH, 7, ada_cols), F32)], axis=1)
    dscc_part = sum(_mm_call("ada_dx", gmc[l], a['w_ada'][l], "nt", F32)[0] for l in range(DEPTH))

    small_names = SMALL_SHARDED + SMALL_REPL
    sums_in = [loss_part.reshape(1), dscc_part] + [gsmall[n] for n in small_names]
    tot = _sum_pieces(_all_gather(_pack_flat(sums_in, 128, 8), "gather_sums"), "sum_small")
    tparts = _unpack_flat(tot, [t.shape for t in sums_in])
    loss = tparts[0].reshape(())
    g_c_ctx = tparts[1] * _dsilu_plain(a['c_ctx'])
    g = {'c_ctx': g_c_ctx, 'b_ada': g_b_ada, 'w_ada': g_w_ada}
    for i, n in enumerate(small_names):
        g[n] = _my_small_shard(tparts[2 + i], n, me) if n in SMALL_SHARDED else tparts[2 + i]

    recv = _all_to_all(_pieces_from_full(gbig), "scatter_grads")
    gb, db, mb, vb = _adamw_call("adamw_big", _pack_big_shards(a, F32), recv,
                                 _pack_big_shards({n: a['m_' + n] for n in BIG}, F32),
                                 _pack_big_shards({n: a['v_' + n] for n in BIG}, F32), True)
    res = {'g': {}, 'd': {}, 'm': {}, 'v': {}}
    for key, packed in (('g', gb), ('d', db), ('m', mb), ('v', vb)):
        res[key].update(_unpack_big_shards(packed))

    shp = a['w_ada'].shape
    flat2 = lambda t: t.reshape(-1, shp[-1])
    ga, da, ma, va = _adamw_call("adamw_ada", flat2(a['w_ada']), flat2(g['w_ada']), flat2(a['m_w_ada']), flat2(a['v_w_ada']), False)
    for key, val in (('g', ga), ('d', da), ('m', ma), ('v', va)):
        res[key]['w_ada'] = val.reshape(shp)
    rest = [n for n in WEIGHTS if n not in BIG and n != 'w_ada']
    shapes = [a[n].shape for n in rest]
    pk = lambda d: _pack_flat([d[n] for n in rest], 128, 8)
    outs = _adamw_call("adamw_small", pk(a), pk(g), pk({n: a['m_' + n] for n in rest}), pk({n: a['v_' + n] for n in rest}), False)
    for key, val in zip(('g', 'd', 'm', 'v'), outs):
        for n, t in zip(rest, _unpack_flat(val, shapes)):
            res[key][n] = t

    out = [loss, gx]
    for key in ('g', 'd', 'm', 'v'):
        out += [res[key][n] for n in WEIGHTS]
    return tuple(out)
```

```python
import functools
import math

import numpy as np
import jax
import jax.numpy as jnp
from jax import lax
from jax.experimental import pallas as pl
from jax.experimental.pallas import tpu as pltpu

F32 = jnp.float32
BF16 = jnp.bfloat16
MESH = pl.DeviceIdType.MESH

NDEV = 8
D = 1024
DFF = 2816
DEPTH = 4
B_LOC = 4
T_CTX = 256
T_LAT = 2048
GRID_W = 64
N_MOD = 9
GDN_H = 6
HD = 64
GDN_QKV = 3 * GDN_H * HD
GDN_W = GDN_H * HD
ATT_HKV = 2
ATT_G = 3
ATT_W = ATT_HKV * ATT_G * HD
ATT_KW = ATT_HKV * HD
S5_G = 16
S5_H = 16
S5_P = 64
S5_W = S5_G * S5_H
S5_N = S5_G * S5_P
IN_COLS = 2456
IN_PAD = 2560
ROPE_THETA = 10000.0
ROPE_PAIRS = 16
ALPHA = (2.0 * 4) ** 0.25
EPS = 1e-6
CHUNK = 64
ADAM_LR, ADAM_B1, ADAM_B2, ADAM_EPS, ADAM_WD, ADAM_STEP = 0.001, 0.9, 0.999, 1e-08, 0.01, 10

TT = 256
ATT_TQ = 256
S5_LB = 256
VMEM_LIMIT = 56 * 1024 * 1024

WEIGHTS = ['c_ctx', 'w_ada', 'b_ada', 'ln_g', 'ln_b', 'ffn_w1', 'ffn_w3', 'ffn_w2', 'w_in', 'w_out', 'gdn_conv_w',
           'gdn_a_log', 'gdn_dt_bias', 'gdn_norm_w', 'q_norm_w', 'k_norm_w', 's5_lam_re', 's5_lam_im', 's5_log_dt',
           's5_b_re', 's5_b_im', 's5_c_re', 's5_c_im', 's5_d', 'glu_w', 'glu_b']
INPUTS = ['x', 'c', 'ctx'] + WEIGHTS + ['loss_target'] + ['m_' + n for n in WEIGHTS] + ['v_' + n for n in WEIGHTS]
BIG = ['ffn_w1', 'ffn_w3', 'ffn_w2', 'w_in', 'w_out']
SMALL_SHARDED = ['ln_g', 'ln_b', 'gdn_conv_w', 'glu_w']
SMALL_REPL = ['gdn_a_log', 'gdn_dt_bias', 'gdn_norm_w', 'q_norm_w', 'k_norm_w', 's5_lam_re', 's5_lam_im',
              's5_log_dt', 's5_b_re', 's5_b_im', 's5_c_re', 's5_c_im', 's5_d', 'glu_b']


def _cparams(sem=None):
    return pltpu.CompilerParams(dimension_semantics=sem, vmem_limit_bytes=VMEM_LIMIT)


def _ntok():
    return T_CTX + T_LAT


def _my_pos():
    return lax.axis_index("x"), lax.axis_index("y"), lax.axis_index("c")


def _my_index():
    x, y, c = _my_pos()
    return 4 * x + 2 * y + c


def _all_gather(shard, name):
    def body(x_ref, out_ref, send_sems, recv_sems, local_sem):
        x, y, c = _my_pos()
        me, sibling = (x, y, c), (x, y, 1 - c)
        chips = [(1 - x, y), (x, 1 - y), (1 - x, 1 - y)]

        def slab(px, py, pc):
            return out_ref.at[4 * px + 2 * py + pc]

        def copy(k, block, to, src=None):
            return pltpu.make_async_remote_copy(
                src_ref=slab(*block) if src is None else src, dst_ref=slab(*block),
                send_sem=send_sems.at[k], recv_sem=recv_sems.at[k], device_id=to, device_id_type=MESH)

        mine = pltpu.make_async_copy(x_ref, slab(*me), local_sem)
        mine.start()
        first = [copy(0, me, sibling, src=x_ref)]
        first += [copy(1 + j, me, (*chip, c), src=x_ref) for j, chip in enumerate(chips)]
        for cp in first:
            cp.start()
        passed = [copy(4 + j, (*chip, c), sibling) for j, chip in enumerate(chips)]
        for j, chip in enumerate(chips):
            copy(1 + j, (*chip, c), me).wait_recv()
            passed[j].start()
        copy(0, sibling, me).wait_recv()
        for j, chip in enumerate(chips):
            copy(4 + j, (*chip, 1 - c), me).wait_recv()
        for cp in first + passed:
            cp.wait_send()
        mine.wait()

    return pl.pallas_call(
        body, name=name,
        out_shape=jax.ShapeDtypeStruct((NDEV,) + shard.shape, shard.dtype),
        in_specs=[pl.BlockSpec(memory_space=pl.ANY)],
        out_specs=pl.BlockSpec(memory_space=pl.ANY),
        scratch_shapes=[pltpu.SemaphoreType.DMA((7,)), pltpu.SemaphoreType.DMA((7,)), pltpu.SemaphoreType.DMA],
    )(shard)


def _all_to_all(pieces, name):
    def body(x_ref, out_ref, send_sems, recv_sems, local_sem):
        x, y, c = _my_pos()
        me_i = 4 * x + 2 * y + c
        mine = pltpu.make_async_copy(x_ref.at[me_i], out_ref.at[me_i], local_sem)
        mine.start()
        sends, recvs = [], []
        for k in range(1, NDEV):
            px = 1 - x if (k >> 2) & 1 else x
            py = 1 - y if (k >> 1) & 1 else y
            pc = 1 - c if k & 1 else c
            peer_i = 4 * px + 2 * py + pc
            sends.append(pltpu.make_async_remote_copy(
                src_ref=x_ref.at[peer_i], dst_ref=out_ref.at[me_i], send_sem=send_sems.at[k - 1],
                recv_sem=recv_sems.at[k - 1], device_id=(px, py, pc), device_id_type=MESH))
            recvs.append(pltpu.make_async_remote_copy(
                src_ref=x_ref.at[peer_i], dst_ref=out_ref.at[peer_i], send_sem=send_sems.at[k - 1],
                recv_sem=recv_sems.at[k - 1], device_id=(px, py, pc), device_id_type=MESH))
        for cp in sends:
            cp.start()
        for cp in recvs:
            cp.wait_recv()
        for cp in sends:
            cp.wait_send()
        mine.wait()

    return pl.pallas_call(
        body, name=name,
        out_shape=jax.ShapeDtypeStruct(pieces.shape, pieces.dtype),
        in_specs=[pl.BlockSpec(memory_space=pl.ANY)],
        out_specs=pl.BlockSpec(memory_space=pl.ANY),
        scratch_shapes=[pltpu.SemaphoreType.DMA((7,)), pltpu.SemaphoreType.DMA((7,)), pltpu.SemaphoreType.DMA],
    )(pieces)


def _dims(ta, tb):
    return (((0 if ta else 1,), (1 if tb else 0,)), ((), ()))


def _raw_dot(a, b, ta, tb):
    return lax.dot_general(a, b, _dims(ta, tb), preferred_element_type=F32)


def _split2(x):
    hi = x.astype(BF16)
    return hi, (x - hi.astype(F32)).astype(BF16)


def _dot_impl(a, b, ta, tb, prec):
    if prec == "bf16":
        return _raw_dot(a.astype(BF16), b.astype(BF16), ta, tb)
    ka, kb = (0 if ta else 1), (1 if tb else 0)
    ah, al = _split2(a)
    if prec == "bx":
        bb = b.astype(BF16)
        return _raw_dot(jnp.concatenate([ah, al], axis=ka), jnp.concatenate([bb, bb], axis=kb), ta, tb)
    bh, bl = _split2(b)
    return _raw_dot(ah, bh, ta, tb) + (_raw_dot(ah, bl, ta, tb) + _raw_dot(al, bh, ta, tb))


@functools.lru_cache(maxsize=None)
def _mm_fn(ta, tb, prec):
    @jax.custom_vjp
    def mm(a, b):
        return _dot_impl(a, b, ta, tb, prec)

    def fwd(a, b):
        return mm(a, b), (a, b)

    def bwd(res, dc):
        a, b = res
        bprec = "f32" if prec == "f32" else "bf16"
        if prec == "bx":
            assert not ta
            return _mm_fn(False, not tb, "bx")(dc, b).astype(a.dtype), jnp.zeros_like(b)
        da = _mm_fn(tb, True, bprec)(b, dc) if ta else _mm_fn(False, not tb, bprec)(dc, b)
        db = _mm_fn(True, ta, bprec)(dc, a) if tb else _mm_fn(not ta, False, bprec)(a, dc)
        return da.astype(a.dtype), db.astype(b.dtype)

    mm.defvjp(fwd, bwd)
    return mm


def _mm(a, b, ta=False, tb=False, prec="bf16"):
    return _mm_fn(ta, tb, prec)(a, b)


@functools.lru_cache(maxsize=None)
def _shift_fn(k):
    @jax.custom_vjp
    def shift(x):
        n = x.shape[0]
        r = pltpu.roll(x, (-k) % n, 0)
        t = lax.broadcasted_iota(jnp.int32, x.shape, 0)
        ok = (t + k >= 0) & (t + k < n)
        return jnp.where(ok, r, 0.0)

    shift.defvjp(lambda x: (shift(x), None), lambda _, dy: (_shift_fn(-k)(dy),))
    return shift


def _sigmoid(x):
    return 1.0 / (1.0 + jnp.exp(-x))


@jax.custom_vjp
def _softplus(x):
    y = jnp.exp(-jnp.abs(x))
    u = 1.0 + y
    l1p = jnp.where(u == 1.0, y, jnp.log(u) * y / jnp.where(u == 1.0, 1.0, u - 1.0))
    return jnp.maximum(x, 0.0) + l1p


_softplus.defvjp(lambda x: (_softplus(x), x), lambda x, dy: (dy * _sigmoid(x),))


def _silu(x):
    return x * _sigmoid(x)


def _gelu_tanh(x):
    return 0.5 * x * (1.0 + jnp.tanh(math.sqrt(2.0 / math.pi) * (x + 0.044715 * (x * x * x))))


def _block_op(name, f, grid, in_specs, out_specs, out_shapes, diff, acc=None):
    n_in, n_out = len(in_specs), len(out_specs)
    acc = acc or [None] * n_in
    didx = [i for i in range(n_in) if diff[i]]
    sem = ("arbitrary",) * len(grid)

    def run_fwd(*xs):
        def body(*refs):
            outs = f(*[r[...] for r in refs[:n_in]])
            for r, o in zip(refs[n_in:], outs):
                r[...] = o.astype(r.dtype)
        return pl.pallas_call(body, name=name + "_fwd", grid=grid, in_specs=in_specs, out_specs=out_specs,
                              out_shape=out_shapes, compiler_params=_cparams(sem))(*xs)

    def run_bwd(xs, douts):
        def body(*refs):
            ins = [r[...] for r in refs[:n_in]]
            dos = [r[...] for r in refs[n_in:n_in + n_out]]

            def g(*dv):
                full = list(ins)
                for i, v in zip(didx, dv):
                    full[i] = v
                return tuple(f(*full))

            outs, vjp = jax.vjp(g, *[ins[i] for i in didx])
            dins = vjp(tuple(d.astype(o.dtype) for d, o in zip(dos, outs)))
            for r, i, dv in zip(refs[n_in + n_out:], didx, dins):
                dv = dv.astype(r.dtype)
                if acc[i] is None:
                    r[...] = dv
                else:
                    if acc[i] == 'last':
                        first = pl.program_id(len(grid) - 1) == 0
                    else:
                        first = functools.reduce(jnp.logical_and, [pl.program_id(a) == 0 for a in range(len(grid))])

                    @pl.when(first)
                    def _(r=r, dv=dv):
                        r[...] = dv

                    @pl.when(jnp.logical_not(first))
                    def _(r=r, dv=dv):
                        r[...] += dv
        return pl.pallas_call(
            body, name=name + "_bwd", grid=grid, in_specs=list(in_specs) + list(out_specs),
            out_specs=[in_specs[i] for i in didx],
            out_shape=[jax.ShapeDtypeStruct(xs[i].shape, xs[i].dtype) for i in didx],
            compiler_params=_cparams(sem))(*xs, *douts)

    @jax.custom_vjp
    def op(*xs):
        return tuple(run_fwd(*xs))

    def op_fwd(*xs):
        return tuple(run_fwd(*xs)), xs

    def op_bwd(xs, douts):
        dins = run_bwd(xs, douts)
        full = [jnp.zeros_like(x) for x in xs]
        for i, dv in zip(didx, dins):
            full[i] = dv
        return tuple(full)

    op.defvjp(op_fwd, op_bwd)
    return op


def _sds(shape, dtype):
    return jax.ShapeDtypeStruct(tuple(shape), dtype)


def _pick(n, cands):
    for c in cands:
        if n % c == 0:
            return c
    return n


def _mm_call(name, a, b, mode, out_dtype):
    if mode == "tn":
        m, k = a.shape
        n = b.shape[1]
        tm = _pick(m, (512, 256, 128, 64))
        tn = _pick(n, (1408, 1280, 1152, 1024, 512, 256, 128))
        steps = m // tm

        def body(a_ref, b_ref, o_ref, acc_ref):
            i = pl.program_id(1)

            @pl.when(i == 0)
            def _():
                acc_ref[...] = jnp.zeros_like(acc_ref)

            acc_ref[...] += _raw_dot(a_ref[...].astype(BF16), b_ref[...].astype(BF16), True, False)

            @pl.when(i == steps - 1)
            def _():
                o_ref[...] = acc_ref[...].astype(o_ref.dtype)

        return pl.pallas_call(
            body, name=name, grid=(n // tn, steps),
            in_specs=[pl.BlockSpec((tm, k), lambda j, i: (i, 0)), pl.BlockSpec((tm, tn), lambda j, i: (i, j))],
            out_specs=pl.BlockSpec((k, tn), lambda j, i: (0, j)),
            out_shape=_sds((k, n), out_dtype),
            scratch_shapes=[pltpu.VMEM((k, tn), F32)],
            compiler_params=_cparams(("arbitrary", "arbitrary")))(a, b)

    m, k = a.shape
    n = b.shape[1] if mode == "nn" else b.shape[0]
    tm = _pick(m, (512, 256, 128, 64))
    tn = _pick(n, (1408, 1280, 1152, 1024, 512, 256, 128))

    def body(a_ref, b_ref, o_ref):
        o_ref[...] = _raw_dot(a_ref[...].astype(BF16), b_ref[...].astype(BF16), False, mode == "nt").astype(o_ref.dtype)

    b_spec = pl.BlockSpec((k, tn), lambda j, i: (0, j)) if mode == "nn" else pl.BlockSpec((tn, k), lambda j, i: (j, 0))
    return pl.pallas_call(
        body, name=name, grid=(n // tn, m // tm),
        in_specs=[pl.BlockSpec((tm, k), lambda j, i: (i, 0)), b_spec],
        out_specs=pl.BlockSpec((tm, tn), lambda j, i: (i, j)),
        out_shape=_sds((m, n), out_dtype),
        compiler_params=_cparams(("arbitrary", "arbitrary")))(a, b)


def _matmul(name, a, w, out_dtype):
    @jax.custom_vjp
    def mm(a, w):
        return _mm_call(name + "_nn", a, w, "nn", out_dtype)

    def fwd(a, w):
        return mm(a, w), (a, w)

    def bwd(res, dy):
        a, w = res
        return (_mm_call(name + "_nt", dy, w, "nt", a.dtype), _mm_call(name + "_tn", a, dy, "tn", w.dtype))

    mm.defvjp(fwd, bwd)
    return mm(a, w)


def _tok(width):
    return pl.BlockSpec((None, TT, width), lambda b, t: (b, t, 0))


def _row(width):
    return pl.BlockSpec((None, None, 1, width), lambda b, t: (b, t, 0, 0))


def _const2(shape):
    return pl.BlockSpec(shape, lambda b, t: (0,) * len(shape))


def _tok_grid():
    return (B_LOC, _ntok() // TT)


def _modulate(x, shift, scale):
    def f(x, sh, sc):
        return ((x * (1.0 + sc) + sh),)
    op = _block_op("modulate", f, _tok_grid(), [_tok(D), _row(D), _row(D)], [_tok(D)],
                   [_sds(x.shape, BF16)], [True, True, True])
    return op(x, shift, scale)[0]


def _post_norm(x, y, gate, g, b, res_w):
    def f(x, y, gate, g, b):
        z = ALPHA * x + res_w * gate * y
        mu = jnp.mean(z, axis=-1, keepdims=True)
        zc = z - mu
        var = jnp.mean(zc * zc, axis=-1, keepdims=True)
        return (zc * lax.rsqrt(var + EPS) * g + b,)
    op = _block_op("post_norm", f, _tok_grid(), [_tok(D), _tok(D), _row(D), _row(D), _row(D)], [_tok(D)],
                   [_sds(x.shape, F32)], [True] * 5)
    return op(x, y, gate, g, b)[0]


def _swiglu_gate(a, b):
    m = a.shape[0]
    tm = _pick(m, (256, 128, 64))

    def f(a, b):
        a = a.astype(F32)
        return (_silu(a) * b.astype(F32),)
    spec = pl.BlockSpec((tm, DFF), lambda i: (i, 0))
    op = _block_op("swiglu_gate", f, (m // tm,), [spec, spec], [spec], [_sds(a.shape, BF16)], [True, True])
    return op(a, b)[0]


def _seg_ones(width):
    i = np.arange(width)
    return jnp.asarray((i[:, None] // HD) == (i[None, :] // HD), BF16)


def _rope_perm(width):
    p = np.zeros((width, width), np.float32)
    for j in range(width):
        if (j % 32) < 16:
            p[j + 16, j] = -1.0
        else:
            p[j - 16, j] = 1.0
    return jnp.asarray(p, BF16)


def _rope_tables(width):
    t = jnp.arange(T_LAT)
    pos = jnp.stack([t // GRID_W, t % GRID_W], axis=-1).astype(F32)
    inv_freq = ROPE_THETA ** (-jnp.arange(ROPE_PAIRS, dtype=F32) / ROPE_PAIRS)
    ang = pos[..., None] * inv_freq
    ang = jnp.broadcast_to(ang[:, :, None, :], (T_LAT, 2, 2, ROPE_PAIRS)).reshape(T_LAT, HD)
    ang = jnp.tile(ang, (1, width // HD))
    cos = jnp.concatenate([jnp.ones((T_CTX, width), F32), jnp.cos(ang)], axis=0)
    sin = jnp.concatenate([jnp.zeros((T_CTX, width), F32), jnp.sin(ang)], axis=0)
    return cos, sin


def _att_pre(x, w_row, name):
    width = x.shape[-1]
    cos, sin = _rope_tables(width)

    def f(x, w, cos, sin, seg, perm):
        ms = _mm(x * x, seg, prec="bx") * (1.0 / HD)
        xn = x * lax.rsqrt(ms + EPS) * w
        return (xn * cos + _mm(xn, perm, prec="bx") * sin,)
    tab = pl.BlockSpec((TT, width), lambda b, t: (t, 0))
    op = _block_op(name, f, _tok_grid(),
                   [_tok(width), _row(width), tab, tab, _const2((width, width)), _const2((width, width))],
                   [_tok(width)], [_sds(x.shape, F32)], [True, True, False, False, False, False])
    return op(x, w_row, cos, sin, _seg_ones(width), _rope_perm(width))[0]


def _attention(q, k, v, name, tq):
    b_, hk, g_, tq_all, _ = q.shape
    tk = k.shape[2]

    def f(q, k, v):
        outs = []
        for gi in range(g_):
            s = _mm(q[gi] * (HD ** -0.5), k, tb=True)
            m = lax.stop_gradient(jnp.max(s, axis=-1, keepdims=True))
            e = jnp.exp(s - m)
            outs.append(_mm(e, v) / jnp.sum(e, axis=-1, keepdims=True))
        return (jnp.stack(outs, axis=0),)
    qs = pl.BlockSpec((None, None, g_, tq, HD), lambda b, h, i: (b, h, 0, i, 0))
    ks = pl.BlockSpec((None, None, tk, HD), lambda b, h, i: (b, h, 0, 0))
    op = _block_op(name, f, (b_, hk, tq_all // tq), [qs, ks, ks], [qs], [_sds(q.shape, F32)],
                   [True, True, True], acc=[None, 'last', 'last'])
    return op(q, k, v)[0]


def _gdn_pre(qkv, conv_w):
    nt_c = GDN_QKV // 128
    flag = jnp.asarray((np.arange(nt_c) % 3 < 2).astype(np.float32)[:, None, None] * np.ones((1, 1, 128), np.float32))
    cw = jnp.broadcast_to(conv_w[None], (B_LOC,) + conv_w.shape)

    def f(x, cw, flag, seg):
        def conv(s):
            acc = cw[2:3, :] * s
            for j in (0, 1, 3, 4):
                acc = acc + cw[j:j + 1, :] * _shift_fn(j - 2)(s)
            return acc
        y = jnp.concatenate([conv(x[:T_CTX]), conv(x[T_CTX:])], axis=0)
        s = _silu(y)
        ss = _mm(s * s, seg, prec="bx")
        return (s * (flag * lax.rsqrt(ss + EPS) + (1.0 - flag)),)
    xs = pl.BlockSpec((None, _ntok(), 128), lambda b, j: (b, 0, j))
    op = _block_op("gdn_pre", f, (B_LOC, nt_c),
                   [xs, pl.BlockSpec((None, 5, 128), lambda b, j: (b, 0, j)),
                    pl.BlockSpec((None, 1, 128), lambda b, j: (j, 0, 0)), pl.BlockSpec((128, 128), lambda b, j: (0, 0))],
                   [xs], [_sds(qkv.shape, F32)], [True, True, False, False])
    return op(qkv, cw, flag, _seg_ones(128))[0]


def _gdn_gates(ba, a_log, dt_bias):
    pad = jnp.zeros((12,), F32)
    al = jnp.broadcast_to(jnp.concatenate([pad, a_log.reshape(12), jnp.zeros((104,), F32)])[None, None], (B_LOC, 1, 128))
    db = jnp.broadcast_to(jnp.concatenate([pad, dt_bias.reshape(12), jnp.zeros((104,), F32)])[None, None], (B_LOC, 1, 128))

    def f(x, al, db):
        lane = lax.broadcasted_iota(jnp.int32, x.shape, 1)
        return (jnp.where(lane < 12, _sigmoid(x), -jnp.exp(al) * _softplus(x + db)),)
    xs = pl.BlockSpec((None, _ntok(), 128), lambda b: (b, 0, 0))
    ps = pl.BlockSpec((None, 1, 128), lambda b: (b, 0, 0))
    op = _block_op("gdn_gates", f, (B_LOC,), [xs, ps, ps], [xs], [_sds(ba.shape, F32)], [True, True, True])
    return op(ba, al, db)[0]


@jax.custom_vjp
def _unit_lower_solve(lowers, rhss):
    return _solve_fwd(lowers, rhss)[0]


def _solve_fwd(lowers, rhss):
    n = lowers[0].shape[0]
    eye = (lax.broadcasted_iota(jnp.int32, (n, n), 0) == lax.broadcasted_iota(jnp.int32, (n, n), 1)).astype(F32)
    nks = [-l for l in lowers]
    invs = [eye + nk for nk in nks]
    for _ in range(int(math.log2(n)) - 1):
        nks = [_dot_impl(nk, nk, False, False, "f32") for nk in nks]
        invs = [inv + _dot_impl(inv, nk, False, False, "f32") for inv, nk in zip(invs, nks)]
    sols = tuple(_dot_impl(inv, rhs, False, False, "f32") for inv, rhs in zip(invs, rhss))
    return sols, (tuple(invs), sols)


def _solve_bwd(res, dsols):
    invs, sols = res
    drhss = tuple(_dot_impl(inv, d, True, False, "f32") for inv, d in zip(invs, dsols))
    return tuple(-_dot_impl(dr, s, False, True, "f32") for dr, s in zip(drhss, sols)), drhss


_unit_lower_solve.defvjp(_solve_fwd, _solve_bwd)


def _gdn_masks():
    ii, jj = np.arange(CHUNK)[:, None], np.arange(CHUNK)[None, :]
    fwd = [jj <= ii, jj < ii, ii <= jj]
    bwd = [jj >= ii, jj > ii, ii >= jj]
    return jnp.asarray(np.stack([np.stack(fwd), np.stack(bwd)]).astype(np.float32))


def _gdn_prep(qkv, g, beta):
    b_, t_, _ = qkv.shape
    nc = t_ // CHUNK
    cb = max(d for d in (1, 2, 3, 4, 6) if nc % d == 0)
    npair = GDN_H // 2

    def f(x, g, beta, masks):
        q2, k2, v2 = x[:, :2 * HD], x[:, 2 * HD:4 * HD], x[:, 4 * HD:]
        ii = lax.broadcasted_iota(jnp.int32, (CHUNK, CHUNK), 0)
        jj = lax.broadcasted_iota(jnp.int32, (CHUNK, CHUNK), 1)
        eye = (ii == jj).astype(F32)
        incl, strict, incl_t = masks[0] > 0.5, masks[1] > 0.5, masks[2] > 0.5
        items = [(hh, c) for hh in range(2) for c in range(cb)]
        qs, ks, vs, decays, kbs, rhss, qgs, kds, egs = [], [], [], [], [], [], [], [], []
        for hh, c in items:
            r = slice(c * CHUNK, (c + 1) * CHUNK)
            lanes = slice(hh * HD, (hh + 1) * HD)
            qc, kc, vc = q2[r, lanes] * (HD ** -0.5), k2[r, lanes], v2[r, lanes]
            g_row, b_row = g[hh, c], beta[hh, c]
            g_col = jnp.sum(eye * g_row, axis=1, keepdims=True)
            b_col = jnp.sum(eye * b_row, axis=1, keepdims=True)
            gc_col = jnp.sum(jnp.where(incl, g_row, 0.0), axis=1, keepdims=True)
            gc_row = jnp.sum(jnp.where(incl_t, g_col, 0.0), axis=0, keepdims=True)
            g_tot = jnp.sum(g_row, axis=1, keepdims=True)
            decays.append(jnp.where(incl, jnp.exp(jnp.where(incl, gc_col - gc_row, 0.0)), 0.0))
            kb = kc * b_col
            qs.append(qc)
            ks.append(kc)
            kbs.append(kb)
            rhss.append(jnp.concatenate([vc * b_col, kb * jnp.exp(gc_col)], axis=1))
            qgs.append(qc * jnp.exp(gc_col))
            kds.append(kc * jnp.exp(g_tot - gc_col))
            egs.append(jnp.broadcast_to(jnp.exp(g_tot), (1, CHUNK)))
        lowers = tuple(jnp.where(strict, _mm(kb, kc, tb=True) * dec, 0.0) for kb, kc, dec in zip(kbs, ks, decays))
        ins = [jnp.where(incl, _mm(qc, kc, tb=True) * dec, 0.0) for qc, kc, dec in zip(qs, ks, decays)]
        sols = _unit_lower_solve(lowers, tuple(rhss))
        us, ws = [s[:, :HD] for s in sols], [s[:, HD:] for s in sols]

        def heads(xs, joiner):
            return jnp.stack([joiner(xs[:cb]), joiner(xs[cb:])], axis=0)
        cat = lambda xs: jnp.concatenate(xs, axis=0)
        return (heads(us, cat), heads(ws, cat), heads(qgs, cat), heads(kds, cat), heads(ins, cat),
                heads(egs, lambda xs: jnp.stack(xs, axis=0)))

    xs = pl.BlockSpec((None, cb * CHUNK, 6 * HD), lambda b, p, i, d: (b, i, p))
    rs = pl.BlockSpec((None, None, 2, None, cb, 1, CHUNK), lambda b, p, i, d: (b, p, 0, d, i, 0, 0))
    ts = pl.BlockSpec((None, None, 2, None, cb * CHUNK, HD), lambda b, p, i, d: (b, p, 0, d, i, 0))
    ms = pl.BlockSpec((None, 3, CHUNK, CHUNK), lambda b, p, i, d: (d, 0, 0, 0))
    big = _sds((b_, npair, 2, 2, t_, HD), F32)
    op = _block_op("gdn_prep", f, (b_, npair, nc // cb, 2), [xs, rs, rs, ms],
                   [ts, ts, ts, ts, ts, rs], [big, big, big, big, big, _sds(g.shape, F32)],
                   [True, True, True, False], acc=['last', None, None, None])
    return op(qkv, g, beta, _gdn_masks())


def _gdn_scan_specs(t_, backward):
    seg_c = T_CTX // CHUNK
    nseg = t_ // T_CTX

    def seg_of(d, s):
        s = nseg - 1 - s if backward else s
        return jnp.where(d == 0, s, jnp.where(s == 0, 0, nseg - s))
    ts = pl.BlockSpec((None, GDN_H // 2, 2, None, T_CTX, HD), lambda b, d, s: (b, 0, 0, d, seg_of(d, s), 0))
    es = pl.BlockSpec((None, GDN_H // 2, 2, None, seg_c, 1, CHUNK), lambda b, d, s: (b, 0, 0, d, seg_of(d, s), 0, 0))
    return ts, es, seg_c, nseg


def _gdn_scan_call(u, w, qg, kd, intra, eg):
    b_, t_ = u.shape[0], u.shape[4]
    ts, es, seg_c, nseg = _gdn_scan_specs(t_, False)
    heads = [(p, hh) for p in range(GDN_H // 2) for hh in range(2)]

    def body(u_ref, w_ref, qg_ref, kd_ref, in_ref, eg_ref, o_ref, st_ref, state):
        @pl.when(pl.program_id(2) == 0)
        def _():
            state[...] = jnp.zeros_like(state)
        d = pl.program_id(1)
        for i in range(seg_c):
            c = jnp.where(d == 0, i, seg_c - 1 - i)
            rows = pl.ds(pl.multiple_of(c * CHUNK, CHUNK), CHUNK)
            sts = [state[n] for n in range(len(heads))]
            for n, (p, hh) in enumerate(heads):
                st_ref[p, hh, rows, :] = sts[n]
            sbs = [st.astype(BF16) for st in sts]
            ws = [_raw_dot(w_ref[p, hh, rows, :].astype(BF16), sbs[n], False, False) for n, (p, hh) in enumerate(heads)]
            qss = [_raw_dot(qg_ref[p, hh, rows, :].astype(BF16), sbs[n], False, False) for n, (p, hh) in enumerate(heads)]
            vbs = [(u_ref[p, hh, rows, :] - ws[n]).astype(BF16) for n, (p, hh) in enumerate(heads)]
            for n, (p, hh) in enumerate(heads):
                o_ref[p, hh, rows, :] = qss[n] + _raw_dot(in_ref[p, hh, rows, :].astype(BF16), vbs[n], False, False)
            kvs = [_raw_dot(kd_ref[p, hh, rows, :].astype(BF16), vbs[n], True, False) for n, (p, hh) in enumerate(heads)]
            for n, (p, hh) in enumerate(heads):
                e = eg_ref[p, hh, pl.ds(c, 1), :, :].reshape(1, CHUNK)
                state[n] = sts[n] * e + kvs[n]
    return pl.pallas_call(body, name="gdn_scan_fwd", grid=(b_, 2, nseg), in_specs=[ts] * 5 + [es], out_specs=[ts, ts],
                          out_shape=[_sds(u.shape, F32), _sds(u.shape, F32)],
                          scratch_shapes=[pltpu.VMEM((GDN_H, HD, HD), F32)],
                          compiler_params=_cparams(("arbitrary",) * 3))(u, w, qg, kd, intra, eg)


def _gdn_scan_bwd_call(u, w, qg, kd, intra, eg, states, do):
    b_, t_ = u.shape[0], u.shape[4]
    ts, es, seg_c, nseg = _gdn_scan_specs(t_, True)
    heads = [(p, hh) for p in range(GDN_H // 2) for hh in range(2)]

    def body(u_ref, w_ref, qg_ref, kd_ref, in_ref, eg_ref, st_ref, do_ref, du_ref, dw_ref, dqg_ref, dkd_ref, din_ref,
             deg_ref, dstate):
        @pl.when(pl.program_id(2) == 0)
        def _():
            dstate[...] = jnp.zeros_like(dstate)
        d = pl.program_id(1)
        hs = list(enumerate(heads))
        for i in range(seg_c):
            c = jnp.where(d == 0, seg_c - 1 - i, i)
            rows = pl.ds(pl.multiple_of(c * CHUNK, CHUNK), CHUNK)
            dss = [dstate[n] for n, _ in hs]
            sts = [st_ref[p, hh, rows, :] for _, (p, hh) in hs]
            sbs = [st.astype(BF16) for st in sts]
            dsbs = [ds.astype(BF16) for ds in dss]
            wbs = [w_ref[p, hh, rows, :].astype(BF16) for _, (p, hh) in hs]
            dobs = [do_ref[p, hh, rows, :].astype(BF16) for _, (p, hh) in hs]
            kdbs = [kd_ref[p, hh, rows, :].astype(BF16) for _, (p, hh) in hs]
            vbs = [(u_ref[p, hh, rows, :] - _raw_dot(wbs[n], sbs[n], False, False)).astype(BF16) for n, (p, hh) in hs]
            dvns = [_raw_dot(in_ref[p, hh, rows, :].astype(BF16), dobs[n], True, False)
                    + _raw_dot(kdbs[n], dsbs[n], False, False) for n, (p, hh) in hs]
            dvbs = [dvn.astype(BF16) for dvn in dvns]
            for n, (p, hh) in hs:
                din_ref[p, hh, rows, :] = _raw_dot(dobs[n], vbs[n], False, True)
                dqg_ref[p, hh, rows, :] = _raw_dot(dobs[n], sbs[n], False, True)
                dkd_ref[p, hh, rows, :] = _raw_dot(vbs[n], dsbs[n], False, True)
                du_ref[p, hh, rows, :] = dvns[n]
                dw_ref[p, hh, rows, :] = -_raw_dot(dvbs[n], sbs[n], False, True)
                deg_ref[p, hh, pl.ds(c, 1), :, :] = jnp.sum(sts[n] * dss[n], axis=0, keepdims=True).reshape(1, 1, CHUNK)
            upd = [_raw_dot(qg_ref[p, hh, rows, :].astype(BF16), dobs[n], True, False)
                   - _raw_dot(wbs[n], dvbs[n], True, False) for n, (p, hh) in hs]
            for n, (p, hh) in hs:
                e = eg_ref[p, hh, pl.ds(c, 1), :, :].reshape(1, CHUNK)
                dstate[n] = dss[n] * e + upd[n]
    big = _sds(u.shape, F32)
    return pl.pallas_call(body, name="gdn_scan_bwd", grid=(b_, 2, nseg), in_specs=[ts] * 5 + [es, ts, ts],
                          out_specs=[ts] * 5 + [es], out_shape=[big] * 5 + [_sds(eg.shape, F32)],
                          scratch_shapes=[pltpu.VMEM((GDN_H, HD, HD), F32)],
                          compiler_params=_cparams(("arbitrary",) * 3))(u, w, qg, kd, intra, eg, states, do)


@jax.custom_vjp
def _gdn_scan(u, w, qg, kd, intra, eg):
    return _gdn_scan_call(u, w, qg, kd, intra, eg)[0]


def _gdn_scan_f(u, w, qg, kd, intra, eg):
    o, states = _gdn_scan_call(u, w, qg, kd, intra, eg)
    return o, (u, w, qg, kd, intra, eg, states)


def _gdn_scan_b(res, do):
    return tuple(_gdn_scan_bwd_call(*res, do))


_gdn_scan.defvjp(_gdn_scan_f, _gdn_scan_b)


def _gdn_post(o, z, w_row):
    def f(o, z, w, seg):
        ms = _mm(o * o, seg, prec="bx") * (1.0 / HD)
        return (o * lax.rsqrt(ms + EPS) * w * _silu(z),)
    op = _block_op("gdn_post", f, _tok_grid(), [_tok(GDN_W), _tok(GDN_W), _row(GDN_W), _const2((GDN_W, GDN_W))],
                   [_tok(GDN_W)], [_sds(o.shape, BF16)], [True, True, True, False])
    return op(o, z, w_row, _seg_ones(GDN_W))[0]


def _s5_tables(ar, ai, rev):
    pr, pi = [ar], [ai]
    for _ in range(7):
        pr, pi = pr + [pr[-1] * ar - pi[-1] * ai], pi + [pr[-1] * ai + pi[-1] * ar]
    if rev:
        pr, pi = pr[::-1], pi[::-1]
    return jnp.concatenate(pr, axis=0), jnp.concatenate(pi, axis=0)


def _s5_scan_call(bu, ar, ai, rev, h=None):
    b_, t_, n2 = bu.shape
    nblk = n2 // (2 * S5_LB)
    tr, ti = _s5_tables(ar, ai, rev)
    tab = jnp.concatenate([tr.reshape(8, nblk, 1, S5_LB), ti.reshape(8, nblk, 1, S5_LB)], axis=2).reshape(8, n2)
    ntile = t_ // 8
    with_grad = h is not None

    def scan_tile(xr, xi, tabr, tabi, cr, ci):
        row = lax.broadcasted_iota(jnp.int32, xr.shape, 0)
        for k in (1, 2, 4):
            idx = (8 - k) if rev else (k - 1)
            akr, aki = tabr[idx:idx + 1, :], tabi[idx:idx + 1, :]
            sh = (8 - k) if rev else k
            sr, si = pltpu.roll(xr, sh, 0), pltpu.roll(xi, sh, 0)
            ok = (row < 8 - k) if rev else (row >= k)
            xr, xi = (xr + jnp.where(ok, akr * sr - aki * si, 0.0), xi + jnp.where(ok, akr * si + aki * sr, 0.0))
        return xr + tabr * cr - tabi * ci, xi + tabr * ci + tabi * cr

    def body(*refs):
        if with_grad:
            bu_ref, tab_ref, h_ref, o_ref, da_ref = refs
        else:
            bu_ref, tab_ref, o_ref = refs
        tabr, tabi = tab_ref[:, :S5_LB], tab_ref[:, S5_LB:]
        zero = jnp.zeros((1, S5_LB), F32)
        row = lax.broadcasted_iota(jnp.int32, (8, S5_LB), 0)

        def step(i, carry):
            cr, ci = carry[0], carry[1]
            tix = (ntile - 1 - i) if rev else i
            rows = pl.ds(pl.multiple_of(tix * 8, 8), 8)
            hr, hi = scan_tile(bu_ref[rows, :S5_LB], bu_ref[rows, S5_LB:], tabr, tabi, cr, ci)
            o_ref[rows, :S5_LB] = hr
            o_ref[rows, S5_LB:] = hi
            out = (hr[0:1, :], hi[0:1, :]) if rev else (hr[7:8, :], hi[7:8, :])
            if with_grad:
                prev = pl.ds(pl.multiple_of(jnp.maximum(tix - 1, 0) * 8, 8), 8)
                live = jnp.where(tix > 0, 1.0, 0.0)
                pr = jnp.where(row == 0, pltpu.roll(h_ref[prev, :S5_LB], 1, 0) * live, pltpu.roll(h_ref[rows, :S5_LB], 1, 0))
                pi = jnp.where(row == 0, pltpu.roll(h_ref[prev, S5_LB:], 1, 0) * live, pltpu.roll(h_ref[rows, S5_LB:], 1, 0))
                out = out + (carry[2] + hr * pr + hi * pi, carry[3] + hi * pr - hr * pi)
            return out
        init = (zero, zero) + ((jnp.zeros((8, S5_LB), F32),) * 2 if with_grad else ())
        fin = lax.fori_loop(0, ntile, step, init)
        if with_grad:
            da_ref[:, :S5_LB] = fin[2]
            da_ref[:, S5_LB:] = fin[3]
    xs = pl.BlockSpec((None, t_, 2 * S5_LB), lambda b, j: (b, 0, j))
    tb = pl.BlockSpec((8, 2 * S5_LB), lambda b, j: (0, j))
    if with_grad:
        return pl.pallas_call(body, name="s5_scan_bwd", grid=(b_, nblk), in_specs=[xs, tb, xs],
                              out_specs=[xs, pl.BlockSpec((None, 8, 2 * S5_LB), lambda b, j: (b, 0, j))],
                              out_shape=[_sds(bu.shape, F32), _sds((b_, 8, n2), F32)],
                              compiler_params=_cparams(("arbitrary", "arbitrary")))(bu, tab, h)
    return pl.pallas_call(body, name="s5_scan_fwd", grid=(b_, nblk), in_specs=[xs, tb], out_specs=xs,
                          out_shape=_sds(bu.shape, F32), compiler_params=_cparams(("arbitrary", "arbitrary")))(bu, tab)


@jax.custom_vjp
def _s5_scan(bu, ar, ai):
    return _s5_scan_call(bu, ar, ai, False)


def _s5_scan_f(bu, ar, ai):
    h = _s5_scan_call(bu, ar, ai, False)
    return h, (h, ar, ai)


def _s5_scan_b(res, dh):
    h, ar, ai = res
    lam, da = _s5_scan_call(dh, ar, -ai, True, h=h)
    nblk = da.shape[-1] // (2 * S5_LB)
    da = jnp.sum(da, axis=(0, 1)).reshape(nblk, 2, S5_LB)
    return lam, da[:, 0].reshape(1, -1), da[:, 1].reshape(1, -1)


_s5_scan.defvjp(_s5_scan_f, _s5_scan_b)


def _s5_post(u, y0, y1, d_row, glu_w, glu_b_row):
    def f(u, y0, y1, d, gw, gb):
        zz = _gelu_tanh(d * u + y0 + y1)
        return (zz * _sigmoid(_mm(zz, gw) + gb),)
    op = _block_op("s5_post", f, _tok_grid(),
                   [_tok(S5_W), _tok(S5_W), _tok(S5_W), _row(S5_W), _const2((S5_W, S5_W)), _row(S5_W)],
                   [_tok(S5_W)], [_sds(u.shape, BF16)], [True] * 6, acc=[None, None, None, None, 'all', None])
    return op(u, y0, y1, d_row, glu_w, glu_b_row)[0]


def _loss_rows(x, target):
    def f(x, t):
        e = x - t
        return (jnp.sum(e * e, axis=0, keepdims=True),)
    grid = (B_LOC, T_LAT // TT)
    op = _block_op("loss_rows", f, grid, [_tok(D), _tok(D)], [_row(D)], [_sds((B_LOC, T_LAT // TT, 1, D), F32)],
                   [True, False])
    return op(x, target)[0]


def _adamw_call(name, w, g, m, v, pieces):
    r, c = w.shape
    tr = _pick(r, (256, 128, 64, 32, 16, 8))
    c1, c2 = 1.0 - ADAM_B1 ** ADAM_STEP, 1.0 - ADAM_B2 ** ADAM_STEP

    def body(w_ref, g_ref, m_ref, v_ref, go_ref, d_ref, mo_ref, vo_ref):
        if pieces:
            g = g_ref[0].astype(F32)
            for i in range(1, NDEV):
                g = g + g_ref[i].astype(F32)
        else:
            g = g_ref[...]
        m = ADAM_B1 * m_ref[...] + (1.0 - ADAM_B1) * g
        v = ADAM_B2 * v_ref[...] + (1.0 - ADAM_B2) * (g * g)
        go_ref[...] = g
        mo_ref[...] = m
        vo_ref[...] = v
        d_ref[...] = -ADAM_LR * ((m / c1) / (jnp.sqrt(v / c2) + ADAM_EPS) + ADAM_WD * w_ref[...])
    spec = pl.BlockSpec((tr, c), lambda i: (i, 0))
    gspec = pl.BlockSpec((NDEV, tr, c), lambda i: (0, i, 0)) if pieces else spec
    return pl.pallas_call(body, name=name, grid=(r // tr,), in_specs=[spec, gspec, spec, spec], out_specs=[spec] * 4,
                          out_shape=[_sds((r, c), F32)] * 4, compiler_params=_cparams(("arbitrary",)))(w, g, m, v)


def _sum_pieces(x, name):
    _, r, c = x.shape
    tr = _pick(r, (512, 256, 128, 64, 32, 16, 8))

    def body(x_ref, o_ref):
        acc = x_ref[0]
        for i in range(1, NDEV):
            acc = acc + x_ref[i]
        o_ref[...] = acc
    return pl.pallas_call(body, name=name, grid=(r // tr,), in_specs=[pl.BlockSpec((NDEV, tr, c), lambda i: (0, i, 0))],
                          out_specs=pl.BlockSpec((tr, c), lambda i: (i, 0)), out_shape=_sds((r, c), F32),
                          compiler_params=_cparams(("arbitrary",)))(x)


def _pack_flat(arrs, lanes, row_mult):
    flat = jnp.concatenate([a.reshape(-1).astype(F32) for a in arrs])
    n = flat.shape[0]
    rows = -(-n // lanes)
    rows = -(-rows // row_mult) * row_mult
    return jnp.pad(flat, (0, rows * lanes - n)).reshape(rows, lanes)


def _unpack_flat(packed, shapes, lead=()):
    flat = packed.reshape(lead + (-1,))
    out, off = [], 0
    for s in shapes:
        n = int(np.prod(s))
        out.append(flat[..., off:off + n].reshape(lead + tuple(s)))
        off += n
    return out


FS_ROWS = DFF // NDEV
WI_ROWS = IN_COLS // NDEV
WI_PAD = -(-WI_ROWS // 16) * 16
WO_ROWS = D // NDEV


def _pad_rows(blk, rows, axis):
    pad = [(0, 0)] * blk.ndim
    pad[axis] = (0, rows - blk.shape[axis])
    return jnp.pad(blk, pad)


def _pack_big_shards(t, dtype):
    parts = []
    for l in range(DEPTH):
        for n in ('ffn_w1', 'ffn_w3', 'ffn_w2'):
            for i in range(2):
                parts.append(t[n][l, i].reshape(-1, D))
        parts.append(_pad_rows(t['w_in'][l].reshape(-1, D), WI_PAD, 0))
        parts.append(t['w_out'][l].reshape(-1, D))
    return jnp.concatenate(parts, axis=0).astype(dtype)


def _unpack_big_shards(p):
    out = {n: [] for n in BIG}
    off = 0
    for l in range(DEPTH):
        for n in ('ffn_w1', 'ffn_w3', 'ffn_w2'):
            pair = []
            for i in range(2):
                blk = p[off:off + FS_ROWS]
                off += FS_ROWS
                pair.append(blk.reshape(FS_ROWS, D) if n == 'ffn_w2' else blk.reshape(D, FS_ROWS))
            out[n].append(jnp.stack(pair))
        out['w_in'].append(p[off:off + WI_ROWS].reshape(D, WI_ROWS))
        off += WI_PAD
        out['w_out'].append(p[off:off + WO_ROWS].reshape(WO_ROWS, D))
        off += WO_ROWS
    return {n: jnp.stack(v) for n, v in out.items()}


def _full_from_gathered(g):
    layers, off = [], 0

    def cols(blk, n):
        return blk.reshape(NDEV, D, n).transpose(1, 0, 2).reshape(D, NDEV * n)
    for l in range(DEPTH):
        lw = {}
        for n in ('ffn_w1', 'ffn_w3', 'ffn_w2'):
            pair = []
            for i in range(2):
                blk = g[:, off:off + FS_ROWS]
                off += FS_ROWS
                pair.append(blk.reshape(DFF, D) if n == 'ffn_w2' else cols(blk, FS_ROWS))
            lw[n] = pair
        lw['w_in'] = cols(g[:, off:off + WI_ROWS], WI_ROWS)
        off += WI_PAD
        lw['w_out'] = g[:, off:off + WO_ROWS].reshape(D, D)
        off += WO_ROWS
        layers.append(lw)
    return layers


def _pieces_from_full(layers):
    def cols(wf, n):
        return wf.reshape(D, NDEV, n).transpose(1, 0, 2).reshape(NDEV, n, D)
    parts = []
    for lw in layers:
        for n in ('ffn_w1', 'ffn_w3', 'ffn_w2'):
            for i in range(2):
                parts.append(lw[n][i].reshape(NDEV, FS_ROWS, D) if n == 'ffn_w2' else cols(lw[n][i], FS_ROWS))
        parts.append(_pad_rows(cols(lw['w_in'], WI_ROWS), WI_PAD, 1))
        parts.append(lw['w_out'].reshape(NDEV, WO_ROWS, D))
    return jnp.concatenate(parts, axis=1)


def _flip_segments(a, axis):
    ctx, lat = lax.slice_in_dim(a, 0, T_CTX, axis=axis), lax.slice_in_dim(a, T_CTX, _ntok(), axis=axis)
    return jnp.concatenate([jnp.flip(ctx, axis), jnp.flip(lat, axis)], axis=axis)


def _rows_of(vec_ctx, vec_lat):
    w = vec_lat.shape[-1]
    ntc, ntl = T_CTX // TT, T_LAT // TT
    return jnp.concatenate([jnp.broadcast_to(vec_ctx[None, None, None, :], (B_LOC, ntc, 1, w)),
                            jnp.broadcast_to(vec_lat[:, None, None, :], (B_LOC, ntl, 1, w))], axis=1)


def _rows_const(vec):
    return jnp.broadcast_to(vec[None, None, None, :], (B_LOC, _ntok() // TT, 1, vec.shape[-1]))


def _ffn_sublayer(xt, mrow, w1, w3, w2, g, b):
    h = _modulate(xt, mrow[0], mrow[1]).reshape(-1, D)
    a = _matmul("ffn_up", h, w1, BF16)
    bb = _matmul("ffn_up", h, w3, BF16)
    u = _swiglu_gate(a, bb)
    y = _matmul("ffn_down", u, w2, F32).reshape(xt.shape)
    return _post_norm(xt, y, mrow[2], _rows_const(g), _rows_const(b), 0.5)


def _s5_discretize(lam_re, lam_im, log_dt, b_re, b_im):
    dt = jnp.exp(log_dt)[:, None]
    zr, zi = lam_re * dt, lam_im * dt
    er = jnp.exp(zr)
    lbr, lbi = er * jnp.cos(zi), er * jnp.sin(zi)
    dd = lam_re * lam_re + lam_im * lam_im
    qr = ((lbr - 1.0) * lam_re + lbi * lam_im) / dd
    qi = (lbi * lam_re - (lbr - 1.0) * lam_im) / dd
    bbr = qr[..., None] * b_re - qi[..., None] * b_im
    bbi = qr[..., None] * b_im + qi[..., None] * b_re
    return lbr, lbi, bbr, bbi


def _s5_cols(a):
    return a.reshape(a.shape[:-1] + (S5_N // S5_LB, S5_LB))


def _s5_group(su, p):
    b_, t_, _ = su.shape
    eye = jnp.eye(S5_G, dtype=F32)
    ys = []
    for d in range(2):
        lbr, lbi, bbr, bbi = _s5_discretize(p['s5_lam_re'][d], p['s5_lam_im'][d], p['s5_log_dt'][d],
                                            p['s5_b_re'][d], p['s5_b_im'][d])
        bre = jnp.einsum('gph,gk->ghkp', bbr, eye).reshape(S5_W, S5_N)
        bim = jnp.einsum('gph,gk->ghkp', bbi, eye).reshape(S5_W, S5_N)
        bmat = jnp.stack([_s5_cols(bre), _s5_cols(bim)], axis=2).reshape(S5_W, 2 * S5_N)
        cre = jnp.einsum('ghp,gk->kpgh', p['s5_c_re'][d], eye).reshape(S5_N, S5_W)
        cim = -jnp.einsum('ghp,gk->kpgh', p['s5_c_im'][d], eye).reshape(S5_N, S5_W)
        cmat = jnp.stack([cre.reshape(S5_N // S5_LB, S5_LB, S5_W), cim.reshape(S5_N // S5_LB, S5_LB, S5_W)],
                         axis=1).reshape(2 * S5_N, S5_W)
        ud = su if d == 0 else _flip_segments(su, 1)
        bu = _matmul("s5_in", ud.reshape(-1, S5_W), bmat, F32).reshape(b_, t_, 2 * S5_N)
        hs = _s5_scan(bu, lbr.reshape(1, S5_N), lbi.reshape(1, S5_N))
        y = _matmul("s5_out", hs.reshape(-1, 2 * S5_N), cmat, F32).reshape(b_, t_, S5_W)
        ys.append(y if d == 0 else _flip_segments(y, 1))
    return _s5_post(su, ys[0], ys[1], _rows_const(p['s5_d']), p['glu_w'], _rows_const(p['glu_b']))


def _gdn_group(qkv, z, ba, p):
    b_, t_, _ = qkv.shape
    nc = t_ // CHUNK
    qkvn = _gdn_pre(qkv, _pair_major(p['gdn_conv_w']))
    bg = _gdn_gates(ba, p['gdn_a_log'], p['gdn_dt_bias'])

    def gates(a):
        a = a.reshape(b_, t_, 2, GDN_H // 2, 2).transpose(0, 3, 4, 2, 1)
        return a.reshape(b_, GDN_H // 2, 2, 2, nc, 1, CHUNK)
    outs = _gdn_prep(qkvn, gates(bg[..., 12:24]), gates(bg[..., 0:12]))
    o = _gdn_scan(*outs).reshape(b_, GDN_H, 2, t_, HD)
    o = (o[:, :, 0] + o[:, :, 1]).transpose(0, 2, 1, 3).reshape(b_, t_, GDN_W)
    return _gdn_post(o, z, _rows_const(jnp.tile(p['gdn_norm_w'], GDN_H)))


def _att_group(aq, ak, av, p):
    b_, t_, _ = aq.shape
    qn = _att_pre(aq, _rows_const(jnp.tile(p['q_norm_w'], ATT_W // HD)), "att_pre_q")
    kn = _att_pre(ak, _rows_const(jnp.tile(p['k_norm_w'], ATT_KW // HD)), "att_pre_k")
    q = qn.reshape(b_, t_, ATT_HKV, ATT_G, HD).transpose(0, 2, 3, 1, 4)
    k = kn.reshape(b_, t_, ATT_HKV, HD).transpose(0, 2, 1, 3)
    v = av.reshape(b_, t_, ATT_HKV, HD).transpose(0, 2, 1, 3)
    o_lat = _attention(q[:, :, :, T_CTX:], k, v, "att_lat", ATT_TQ)
    o_ctx = _attention(q[:, :, :, :T_CTX], k[:, :, :T_CTX], v[:, :, :T_CTX], "att_ctx", T_CTX)
    o = jnp.concatenate([o_ctx, o_lat], axis=3)
    return o.transpose(0, 3, 1, 2, 4).reshape(b_, t_, ATT_W)


def _pair_major(w):
    lead = w.shape[:-1]
    return w.reshape(lead + (3, GDN_H // 2, 2 * HD)).swapaxes(-3, -2).reshape(lead + (GDN_QKV,))


def _permute_w_in(w):
    return jnp.concatenate([_pair_major(w[:, :GDN_QKV]), w[:, GDN_QKV:1536], w[:, 1560:], w[:, 1536:1560],
                            jnp.zeros((D, IN_PAD - IN_COLS), w.dtype)], axis=1)


PROJ_CUTS = (0, 1152, 1536, 1920, 2048, 2176, 2432, 2560)


@jax.custom_vjp
def _split_proj(proj):
    return tuple(proj[..., a:b] for a, b in zip(PROJ_CUTS[:-1], PROJ_CUTS[1:]))


_split_proj.defvjp(lambda proj: (_split_proj(proj), None), lambda _, d: (jnp.concatenate(d, axis=-1),))


def _mixer_sublayer(xt, mrow, lw, p, g, b):
    h = _modulate(xt, mrow[3], mrow[4]).reshape(-1, D)
    proj = _matmul("mix_in", h, _permute_w_in(lw['w_in']), F32).reshape(xt.shape[:2] + (IN_PAD,))
    qkv, z, aq, ak, av, su, ba = _split_proj(proj)
    o_gdn = _gdn_group(qkv, z, ba, p)
    o_att = _att_group(aq, ak, av, p)
    o_s5 = _s5_group(su, p)
    cat = jnp.concatenate([o_gdn, o_att.astype(BF16), o_s5], axis=-1).reshape(-1, D)
    y = _matmul("mix_out", cat, lw['w_out'], F32).reshape(xt.shape)
    return _post_norm(xt, y, mrow[5], _rows_const(g), _rows_const(b), 1.0)


def _local_loss(x, mod, modc, big, small, ctx, target):
    xt = jnp.concatenate([ctx, x], axis=1)
    for l in range(DEPTH):
        mrow = [_rows_of(modc[l, k * D:(k + 1) * D], mod[l, :, k * D:(k + 1) * D]) for k in range(N_MOD)]
        p = {n: small[n][l] for n in small}
        lw = big[l]
        xt = _ffn_sublayer(xt, mrow[0:3], lw['ffn_w1'][0], lw['ffn_w3'][0], lw['ffn_w2'][0], p['ln_g'][0], p['ln_b'][0])
        xt = _mixer_sublayer(xt, mrow, lw, p, p['ln_g'][1], p['ln_b'][1])
        xt = _ffn_sublayer(xt, mrow[6:9], lw['ffn_w1'][1], lw['ffn_w3'][1], lw['ffn_w2'][1], p['ln_g'][2], p['ln_b'][2])
    part = _loss_rows(xt[:, T_CTX:], target)
    return (0.5 / D) * jnp.sum(part)


def _silu_plain(x):
    return x * jax.nn.sigmoid(x)


def _dsilu_plain(x):
    s = jax.nn.sigmoid(x)
    return s * (1.0 + x * (1.0 - s))


def _small_shapes():
    return {'ln_g': (DEPTH, 3, D), 'ln_b': (DEPTH, 3, D), 'gdn_conv_w': (DEPTH, 5, GDN_QKV), 'glu_w': (DEPTH, S5_W, S5_W),
            'gdn_a_log': (DEPTH, 2, GDN_H), 'gdn_dt_bias': (DEPTH, 2, GDN_H), 'gdn_norm_w': (DEPTH, HD),
            'q_norm_w': (DEPTH, HD), 'k_norm_w': (DEPTH, HD), 's5_lam_re': (DEPTH, 2, S5_G, S5_P),
            's5_lam_im': (DEPTH, 2, S5_G, S5_P), 's5_log_dt': (DEPTH, 2, S5_G),
            's5_b_re': (DEPTH, 2, S5_G, S5_P, S5_H), 's5_b_im': (DEPTH, 2, S5_G, S5_P, S5_H),
            's5_c_re': (DEPTH, 2, S5_G, S5_H, S5_P), 's5_c_im': (DEPTH, 2, S5_G, S5_H, S5_P),
            's5_d': (DEPTH, S5_W), 'glu_b': (DEPTH, S5_W)}


def _gather_small_sharded(gathered, name):
    if name == 'glu_w':
        return gathered.transpose(1, 0, 2, 3).reshape(DEPTH, S5_W, S5_W)
    lead = gathered.shape[1:-1]
    return jnp.moveaxis(gathered, 0, -2).reshape(lead + (-1,))


def _my_small_shard(full, name, me):
    if name == 'glu_w':
        return lax.dynamic_slice_in_dim(full, me * (S5_W // NDEV), S5_W // NDEV, axis=1)
    n = full.shape[-1] // NDEV
    return lax.dynamic_slice_in_dim(full, me * n, n, axis=full.ndim - 1)


def kernel(x, c, ctx, c_ctx, w_ada, b_ada, ln_g, ln_b, ffn_w1, ffn_w3, ffn_w2, w_in, w_out, gdn_conv_w, gdn_a_log, gdn_dt_bias, gdn_norm_w, q_norm_w, k_norm_w, s5_lam_re, s5_lam_im, s5_log_dt, s5_b_re, s5_b_im, s5_c_re, s5_c_im, s5_d, glu_w, glu_b, loss_target, m_c_ctx, m_w_ada, m_b_ada, m_ln_g, m_ln_b, m_ffn_w1, m_ffn_w3, m_ffn_w2, m_w_in, m_w_out, m_gdn_conv_w, m_gdn_a_log, m_gdn_dt_bias, m_gdn_norm_w, m_q_norm_w, m_k_norm_w, m_s5_lam_re, m_s5_lam_im, m_s5_log_dt, m_s5_b_re, m_s5_b_im, m_s5_c_re, m_s5_c_im, m_s5_d, m_glu_w, m_glu_b, v_c_ctx, v_w_ada, v_b_ada, v_ln_g, v_ln_b, v_ffn_w1, v_ffn_w3, v_ffn_w2, v_w_in, v_w_out, v_gdn_conv_w, v_gdn_a_log, v_gdn_dt_bias, v_gdn_norm_w, v_q_norm_w, v_k_norm_w, v_s5_lam_re, v_s5_lam_im, v_s5_log_dt, v_s5_b_re, v_s5_b_im, v_s5_c_re, v_s5_c_im, v_s5_d, v_glu_w, v_glu_b):
    a = dict(locals())
    me = _my_index()
    ada_cols = N_MOD * D // NDEV

    sc = _silu_plain(a['c'])
    scc = _silu_plain(a['c_ctx'])
    small_in = [sc] + [a[n] for n in SMALL_SHARDED]
    got = _all_gather(_pack_flat(small_in, 128, 8), "gather_small")
    parts = _unpack_flat(got, [t.shape for t in small_in], lead=(NDEV,))
    sc_all = parts[0].reshape(NDEV * B_LOC, D)
    small = {n: _gather_small_sharded(parts[1 + i], n) for i, n in enumerate(SMALL_SHARDED)}
    for n in SMALL_REPL:
        small[n] = a[n]

    nb = NDEV * B_LOC
    rows_pad = 8
    sc_rows = jnp.concatenate([sc_all, scc[None], jnp.zeros((rows_pad - 1, D), F32)], axis=0)
    mod_part = jnp.stack([_mm_call("ada_fwd", sc_rows, a['w_ada'][l], "nn", F32) for l in range(DEPTH)])
    mod_all = _all_gather(mod_part.reshape(DEPTH * (nb + rows_pad), ada_cols), "gather_mod")
    mod_all = mod_all.reshape(NDEV, DEPTH, nb + rows_pad, ada_cols).transpose(1, 2, 0, 3).reshape(DEPTH, nb + rows_pad, N_MOD * D)
    mod_all = mod_all + a['b_ada'][:, None, :]
    mod = lax.dynamic_slice_in_dim(mod_all, me * B_LOC, B_LOC, axis=1)
    modc = mod_all[:, nb]

    big = _full_from_gathered(_all_gather(_pack_big_shards(a, BF16), "gather_weights"))

    loss_part, grads = jax.value_and_grad(_local_loss, argnums=(0, 1, 2, 3, 4))(
        a['x'], mod, modc, big, small, a['ctx'], a['loss_target'])
    gx, gmod, gmodc, gbig, gsmall = grads

    gm_rows = jnp.concatenate([gmod, gmodc[:, None], jnp.zeros((DEPTH, rows_pad - B_LOC - 1, N_MOD * D), F32)], axis=1)
    gm_all = _all_gather(gm_rows.reshape(DEPTH * rows_pad, N_MOD * D), "gather_dmod").reshape(NDEV, DEPTH, rows_pad, N_MOD * D)
    gm_all = gm_all.transpose(1, 0, 2, 3).reshape(DEPTH, NDEV * rows_pad, N_MOD * D)
    g_b_ada = jnp.sum(gm_all, axis=1)
    sc_dev = jnp.concatenate([sc_all.reshape(NDEV, B_LOC, D), jnp.broadcast_to(scc[None, None], (NDEV, 1, D)),
                              jnp.zeros((NDEV, rows_pad - B_LOC - 1, D), F32)], axis=1).reshape(NDEV * rows_pad, D)
    gm_mine = lax.dynamic_slice_in_dim(gm_all, me * ada_cols, ada_cols, axis=2)
    g_w_ada = jnp.stack([_mm_call("ada_dw", sc_dev, gm_mine[l], "tn", F32) for l in range(DEPTH)])
    gmc = gm_mine.reshape(DEPTH, NDEV, rows_pad, ada_cols)[:, :, B_LOC].sum(axis=1)
    gmc = jnp.concatenate([gmc[:, None], jnp.zeros((DEPTH, 7, ada_cols), F32)], axis=1)
    dscc_part = sum(_mm_call("ada_dx", gmc[l], a['w_ada'][l], "nt", F32)[0] for l in range(DEPTH))

    small_names = SMALL_SHARDED + SMALL_REPL
    sums_in = [loss_part.reshape(1), dscc_part] + [gsmall[n] for n in small_names]
    tot = _sum_pieces(_all_gather(_pack_flat(sums_in, 128, 512), "gather_sums"), "sum_small")
    tparts = _unpack_flat(tot, [t.shape for t in sums_in])
    loss = tparts[0].reshape(())
    g_c_ctx = tparts[1] * _dsilu_plain(a['c_ctx'])
    g = {'c_ctx': g_c_ctx, 'b_ada': g_b_ada, 'w_ada': g_w_ada}
    for i, n in enumerate(small_names):
        g[n] = _my_small_shard(tparts[2 + i], n, me) if n in SMALL_SHARDED else tparts[2 + i]

    recv = _all_to_all(_pieces_from_full(gbig), "scatter_grads")
    gb, db, mb, vb = _adamw_call("adamw_big", _pack_big_shards(a, F32), recv,
                                 _pack_big_shards({n: a['m_' + n] for n in BIG}, F32),
                                 _pack_big_shards({n: a['v_' + n] for n in BIG}, F32), True)
    res = {'g': {}, 'd': {}, 'm': {}, 'v': {}}
    for key, packed in (('g', gb), ('d', db), ('m', mb), ('v', vb)):
        res[key].update(_unpack_big_shards(packed))

    shp = a['w_ada'].shape
    flat2 = lambda t: t.reshape(-1, shp[-1])
    ga, da, ma, va = _adamw_call("adamw_ada", flat2(a['w_ada']), flat2(g['w_ada']), flat2(a['m_w_ada']), flat2(a['v_w_ada']), False)
    for key, val in (('g', ga), ('d', da), ('m', ma), ('v', va)):
        res[key]['w_ada'] = val.reshape(shp)
    rest = [n for n in WEIGHTS if n not in BIG and n != 'w_ada']
    shapes = [a[n].shape for n in rest]
    pk = lambda d: _pack_flat([d[n] for n in rest], 128, 256)
    outs = _adamw_call("adamw_small", pk(a), pk(g), pk({n: a['m_' + n] for n in rest}), pk({n: a['v_' + n] for n in rest}), False)
    for key, val in zip(('g', 'd', 'm', 'v'), outs):
        for n, t in zip(rest, _unpack_flat(val, shapes)):
            res[key][n] = t

    out = [loss, gx]
    for key in ('g', 'd', 'm', 'v'):
        out += [res[key][n] for n in WEIGHTS]
    return tuple(out)
```

```python
import functools
import math

import numpy as np
import jax
import jax.numpy as jnp
from jax import lax
from jax.experimental import pallas as pl
from jax.experimental.pallas import tpu as pltpu

F32 = jnp.float32
BF16 = jnp.bfloat16
MESH = pl.DeviceIdType.MESH

NDEV = 8
D = 1024
DFF = 2816
DEPTH = 4
B_LOC = 4
T_CTX = 256
T_LAT = 2048
GRID_W = 64
N_MOD = 9
GDN_H = 6
HD = 64
GDN_QKV = 3 * GDN_H * HD
GDN_W = GDN_H * HD
ATT_HKV = 2
ATT_G = 3
ATT_W = ATT_HKV * ATT_G * HD
ATT_KW = ATT_HKV * HD
S5_G = 16
S5_H = 16
S5_P = 64
S5_W = S5_G * S5_H
S5_N = S5_G * S5_P
IN_COLS = 2456
IN_PAD = 2560
ROPE_THETA = 10000.0
ROPE_PAIRS = 16
ALPHA = (2.0 * 4) ** 0.25
EPS = 1e-6
CHUNK = 64
ADAM_LR, ADAM_B1, ADAM_B2, ADAM_EPS, ADAM_WD, ADAM_STEP = 0.001, 0.9, 0.999, 1e-08, 0.01, 10

TT = 256
ATT_TQ = 256
S5_LB = 256
VMEM_LIMIT = 56 * 1024 * 1024

WEIGHTS = ['c_ctx', 'w_ada', 'b_ada', 'ln_g', 'ln_b', 'ffn_w1', 'ffn_w3', 'ffn_w2', 'w_in', 'w_out', 'gdn_conv_w',
           'gdn_a_log', 'gdn_dt_bias', 'gdn_norm_w', 'q_norm_w', 'k_norm_w', 's5_lam_re', 's5_lam_im', 's5_log_dt',
           's5_b_re', 's5_b_im', 's5_c_re', 's5_c_im', 's5_d', 'glu_w', 'glu_b']
INPUTS = ['x', 'c', 'ctx'] + WEIGHTS + ['loss_target'] + ['m_' + n for n in WEIGHTS] + ['v_' + n for n in WEIGHTS]
BIG = ['ffn_w1', 'ffn_w3', 'ffn_w2', 'w_in', 'w_out']
SMALL_SHARDED = ['ln_g', 'ln_b', 'gdn_conv_w', 'glu_w']
SMALL_REPL = ['gdn_a_log', 'gdn_dt_bias', 'gdn_norm_w', 'q_norm_w', 'k_norm_w', 's5_lam_re', 's5_lam_im',
              's5_log_dt', 's5_b_re', 's5_b_im', 's5_c_re', 's5_c_im', 's5_d', 'glu_b']


def _cparams(sem=None):
    return pltpu.CompilerParams(dimension_semantics=sem, vmem_limit_bytes=VMEM_LIMIT)


def _ntok():
    return T_CTX + T_LAT


def _my_pos():
    return lax.axis_index("x"), lax.axis_index("y"), lax.axis_index("c")


def _my_index():
    x, y, c = _my_pos()
    return 4 * x + 2 * y + c


def _all_gather(shard, name):
    def body(x_ref, out_ref, send_sems, recv_sems, local_sem):
        x, y, c = _my_pos()
        me, sibling = (x, y, c), (x, y, 1 - c)
        chips = [(1 - x, y), (x, 1 - y), (1 - x, 1 - y)]

        def slab(px, py, pc):
            return out_ref.at[4 * px + 2 * py + pc]

        def copy(k, block, to, src=None):
            return pltpu.make_async_remote_copy(
                src_ref=slab(*block) if src is None else src, dst_ref=slab(*block),
                send_sem=send_sems.at[k], recv_sem=recv_sems.at[k], device_id=to, device_id_type=MESH)

        mine = pltpu.make_async_copy(x_ref, slab(*me), local_sem)
        mine.start()
        first = [copy(0, me, sibling, src=x_ref)]
        first += [copy(1 + j, me, (*chip, c), src=x_ref) for j, chip in enumerate(chips)]
        for cp in first:
            cp.start()
        passed = [copy(4 + j, (*chip, c), sibling) for j, chip in enumerate(chips)]
        for j, chip in enumerate(chips):
            copy(1 + j, (*chip, c), me).wait_recv()
            passed[j].start()
        copy(0, sibling, me).wait_recv()
        for j, chip in enumerate(chips):
            copy(4 + j, (*chip, 1 - c), me).wait_recv()
        for cp in first + passed:
            cp.wait_send()
        mine.wait()

    return pl.pallas_call(
        body, name=name,
        out_shape=jax.ShapeDtypeStruct((NDEV,) + shard.shape, shard.dtype),
        in_specs=[pl.BlockSpec(memory_space=pl.ANY)],
        out_specs=pl.BlockSpec(memory_space=pl.ANY),
        scratch_shapes=[pltpu.SemaphoreType.DMA((7,)), pltpu.SemaphoreType.DMA((7,)), pltpu.SemaphoreType.DMA],
    )(shard)


def _all_to_all(pieces, name):
    def body(x_ref, out_ref, send_sems, recv_sems, local_sem):
        x, y, c = _my_pos()
        me_i = 4 * x + 2 * y + c
        mine = pltpu.make_async_copy(x_ref.at[me_i], out_ref.at[me_i], local_sem)
        mine.start()
        sends, recvs = [], []
        for k in range(1, NDEV):
            px = 1 - x if (k >> 2) & 1 else x
            py = 1 - y if (k >> 1) & 1 else y
            pc = 1 - c if k & 1 else c
            peer_i = 4 * px + 2 * py + pc
            sends.append(pltpu.make_async_remote_copy(
                src_ref=x_ref.at[peer_i], dst_ref=out_ref.at[me_i], send_sem=send_sems.at[k - 1],
                recv_sem=recv_sems.at[k - 1], device_id=(px, py, pc), device_id_type=MESH))
            recvs.append(pltpu.make_async_remote_copy(
                src_ref=x_ref.at[peer_i], dst_ref=out_ref.at[peer_i], send_sem=send_sems.at[k - 1],
                recv_sem=recv_sems.at[k - 1], device_id=(px, py, pc), device_id_type=MESH))
        for cp in sends:
            cp.start()
        for cp in recvs:
            cp.wait_recv()
        for cp in sends:
            cp.wait_send()
        mine.wait()

    return pl.pallas_call(
        body, name=name,
        out_shape=jax.ShapeDtypeStruct(pieces.shape, pieces.dtype),
        in_specs=[pl.BlockSpec(memory_space=pl.ANY)],
        out_specs=pl.BlockSpec(memory_space=pl.ANY),
        scratch_shapes=[pltpu.SemaphoreType.DMA((7,)), pltpu.SemaphoreType.DMA((7,)), pltpu.SemaphoreType.DMA],
    )(pieces)


def _dims(ta, tb):
    return (((0 if ta else 1,), (1 if tb else 0,)), ((), ()))


def _raw_dot(a, b, ta, tb):
    return lax.dot_general(a, b, _dims(ta, tb), preferred_element_type=F32)


def _split2(x):
    hi = x.astype(BF16)
    return hi, (x - hi.astype(F32)).astype(BF16)


def _dot_impl(a, b, ta, tb, prec):
    if prec == "bf16":
        return _raw_dot(a.astype(BF16), b.astype(BF16), ta, tb)
    ka, kb = (0 if ta else 1), (1 if tb else 0)
    ah, al = _split2(a)
    if prec == "bx":
        bb = b.astype(BF16)
        return _raw_dot(jnp.concatenate([ah, al], axis=ka), jnp.concatenate([bb, bb], axis=kb), ta, tb)
    bh, bl = _split2(b)
    return _raw_dot(ah, bh, ta, tb) + (_raw_dot(ah, bl, ta, tb) + _raw_dot(al, bh, ta, tb))


@functools.lru_cache(maxsize=None)
def _mm_fn(ta, tb, prec):
    @jax.custom_vjp
    def mm(a, b):
        return _dot_impl(a, b, ta, tb, prec)

    def fwd(a, b):
        return mm(a, b), (a, b)

    def bwd(res, dc):
        a, b = res
        bprec = "f32" if prec == "f32" else "bf16"
        if prec == "bx":
            assert not ta
            return _mm_fn(False, not tb, "bx")(dc, b).astype(a.dtype), jnp.zeros_like(b)
        da = _mm_fn(tb, True, bprec)(b, dc) if ta else _mm_fn(False, not tb, bprec)(dc, b)
        db = _mm_fn(True, ta, bprec)(dc, a) if tb else _mm_fn(not ta, False, bprec)(a, dc)
        return da.astype(a.dtype), db.astype(b.dtype)

    mm.defvjp(fwd, bwd)
    return mm


def _mm(a, b, ta=False, tb=False, prec="bf16"):
    return _mm_fn(ta, tb, prec)(a, b)


@functools.lru_cache(maxsize=None)
def _shift_fn(k):
    @jax.custom_vjp
    def shift(x):
        n = x.shape[0]
        r = pltpu.roll(x, (-k) % n, 0)
        t = lax.broadcasted_iota(jnp.int32, x.shape, 0)
        ok = (t + k >= 0) & (t + k < n)
        return jnp.where(ok, r, 0.0)

    shift.defvjp(lambda x: (shift(x), None), lambda _, dy: (_shift_fn(-k)(dy),))
    return shift


def _sigmoid(x):
    return 1.0 / (1.0 + jnp.exp(-x))


@jax.custom_vjp
def _softplus(x):
    y = jnp.exp(-jnp.abs(x))
    u = 1.0 + y
    l1p = jnp.where(u == 1.0, y, jnp.log(u) * y / jnp.where(u == 1.0, 1.0, u - 1.0))
    return jnp.maximum(x, 0.0) + l1p


_softplus.defvjp(lambda x: (_softplus(x), x), lambda x, dy: (dy * _sigmoid(x),))


def _silu(x):
    return x * _sigmoid(x)


def _gelu_tanh(x):
    return 0.5 * x * (1.0 + jnp.tanh(math.sqrt(2.0 / math.pi) * (x + 0.044715 * (x * x * x))))


def _block_op(name, f, grid, in_specs, out_specs, out_shapes, diff, acc=None, n_res=0, f_bwd=None):
    n_in, n_all = len(in_specs), len(out_specs)
    n_out = n_all - n_res
    acc = acc or [None] * n_in
    didx = [i for i in range(n_in) if diff[i]]
    sem = ("arbitrary",) * len(grid)
    fb = f_bwd or f

    def run_fwd(*xs):
        def body(*refs):
            outs = f(*[r[...] for r in refs[:n_in]])
            for r, o in zip(refs[n_in:], outs):
                r[...] = o.astype(r.dtype)
        return pl.pallas_call(body, name=name + "_fwd", grid=grid, in_specs=in_specs, out_specs=out_specs,
                              out_shape=out_shapes, compiler_params=_cparams(sem))(*xs)

    def run_bwd(xs, res, douts):
        def body(*refs):
            ins = [r[...] for r in refs[:n_in]]
            ress = [r[...] for r in refs[n_in:n_in + n_res]]
            dos = [r[...] for r in refs[n_in + n_res:n_in + n_all]]

            def g(*dv):
                full = list(ins)
                for i, v in zip(didx, dv):
                    full[i] = v
                return tuple(fb(*full, *ress))

            outs, vjp = jax.vjp(g, *[ins[i] for i in didx])
            dins = vjp(tuple(d.astype(o.dtype) for d, o in zip(dos, outs)))
            for r, i, dv in zip(refs[n_in + n_all:], didx, dins):
                dv = dv.astype(r.dtype)
                if acc[i] is None:
                    r[...] = dv
                else:
                    if acc[i] == 'last':
                        first = pl.program_id(len(grid) - 1) == 0
                    else:
                        first = functools.reduce(jnp.logical_and, [pl.program_id(a) == 0 for a in range(len(grid))])

                    @pl.when(first)
                    def _(r=r, dv=dv):
                        r[...] = dv

                    @pl.when(jnp.logical_not(first))
                    def _(r=r, dv=dv):
                        r[...] += dv
        return pl.pallas_call(
            body, name=name + "_bwd", grid=grid,
            in_specs=list(in_specs) + list(out_specs[n_out:]) + list(out_specs[:n_out]),
            out_specs=[in_specs[i] for i in didx],
            out_shape=[jax.ShapeDtypeStruct(xs[i].shape, xs[i].dtype) for i in didx],
            compiler_params=_cparams(sem))(*xs, *res, *douts)

    @jax.custom_vjp
    def op(*xs):
        return tuple(run_fwd(*xs))[:n_out]

    def op_fwd(*xs):
        outs = tuple(run_fwd(*xs))
        return outs[:n_out], (xs, outs[n_out:])

    def op_bwd(saved, douts):
        xs, res = saved
        dins = run_bwd(xs, res, douts)
        full = [jnp.zeros_like(x) for x in xs]
        for i, dv in zip(didx, dins):
            full[i] = dv
        return tuple(full)

    op.defvjp(op_fwd, op_bwd)
    return op


def _sds(shape, dtype):
    return jax.ShapeDtypeStruct(tuple(shape), dtype)


def _pick(n, cands):
    for c in cands:
        if n % c == 0:
            return c
    return n


def _mm_call(name, a, b, mode, out_dtype):
    if mode == "tn":
        m, k = a.shape
        n = b.shape[1]
        tm = _pick(m, (512, 256, 128, 64))
        tn = _pick(n, (1408, 1280, 1152, 1024, 512, 256, 128))
        steps = m // tm

        def body(a_ref, b_ref, o_ref, acc_ref):
            i = pl.program_id(1)

            @pl.when(i == 0)
            def _():
                acc_ref[...] = jnp.zeros_like(acc_ref)

            acc_ref[...] += _raw_dot(a_ref[...].astype(BF16), b_ref[...].astype(BF16), True, False)

            @pl.when(i == steps - 1)
            def _():
                o_ref[...] = acc_ref[...].astype(o_ref.dtype)

        return pl.pallas_call(
            body, name=name, grid=(n // tn, steps),
            in_specs=[pl.BlockSpec((tm, k), lambda j, i: (i, 0)), pl.BlockSpec((tm, tn), lambda j, i: (i, j))],
            out_specs=pl.BlockSpec((k, tn), lambda j, i: (0, j)),
            out_shape=_sds((k, n), out_dtype),
            scratch_shapes=[pltpu.VMEM((k, tn), F32)],
            compiler_params=_cparams(("arbitrary", "arbitrary")))(a, b)

    m, k = a.shape
    n = b.shape[1] if mode == "nn" else b.shape[0]
    tm = _pick(m, (512, 256, 128, 64))
    tn = _pick(n, (1408, 1280, 1152, 1024, 512, 256, 128))

    def body(a_ref, b_ref, o_ref):
        o_ref[...] = _raw_dot(a_ref[...].astype(BF16), b_ref[...].astype(BF16), False, mode == "nt").astype(o_ref.dtype)

    b_spec = pl.BlockSpec((k, tn), lambda j, i: (0, j)) if mode == "nn" else pl.BlockSpec((tn, k), lambda j, i: (j, 0))
    return pl.pallas_call(
        body, name=name, grid=(n // tn, m // tm),
        in_specs=[pl.BlockSpec((tm, k), lambda j, i: (i, 0)), b_spec],
        out_specs=pl.BlockSpec((tm, tn), lambda j, i: (i, j)),
        out_shape=_sds((m, n), out_dtype),
        compiler_params=_cparams(("arbitrary", "arbitrary")))(a, b)


def _matmul(name, a, w, out_dtype):
    @jax.custom_vjp
    def mm(a, w):
        return _mm_call(name + "_nn", a, w, "nn", out_dtype)

    def fwd(a, w):
        return mm(a, w), (a, w)

    def bwd(res, dy):
        a, w = res
        return (_mm_call(name + "_nt", dy, w, "nt", a.dtype), _mm_call(name + "_tn", a, dy, "tn", w.dtype))

    mm.defvjp(fwd, bwd)
    return mm(a, w)


def _matmul_t(name, a, wt, out_dtype):
    @jax.custom_vjp
    def mm(a, wt):
        return _mm_call(name + "_nt", a, wt, "nt", out_dtype)

    def fwd(a, wt):
        return mm(a, wt), (a, wt)

    def bwd(res, dy):
        a, wt = res
        return (_mm_call(name + "_nn", dy, wt, "nn", a.dtype), _mm_call(name + "_tn", dy, a, "tn", wt.dtype))

    mm.defvjp(fwd, bwd)
    return mm(a, wt)


def _tok(width):
    return pl.BlockSpec((None, TT, width), lambda b, t: (b, t, 0))


def _row(width):
    return pl.BlockSpec((None, None, 1, width), lambda b, t: (b, t, 0, 0))


def _const2(shape):
    return pl.BlockSpec(shape, lambda b, t: (0,) * len(shape))


def _tok_grid():
    return (B_LOC, _ntok() // TT)


def _modulate(x, shift, scale):
    def f(x, sh, sc):
        return ((x * (1.0 + sc) + sh),)
    op = _block_op("modulate", f, _tok_grid(), [_tok(D), _row(D), _row(D)], [_tok(D)],
                   [_sds(x.shape, BF16)], [True, True, True])
    return op(x, shift, scale)[0]


def _post_norm(x, y, gate, g, b, res_w):
    def f(x, y, gate, g, b):
        z = ALPHA * x + res_w * gate * y
        mu = jnp.mean(z, axis=-1, keepdims=True)
        zc = z - mu
        var = jnp.mean(zc * zc, axis=-1, keepdims=True)
        return (zc * lax.rsqrt(var + EPS) * g + b,)
    op = _block_op("post_norm", f, _tok_grid(), [_tok(D), _tok(D), _row(D), _row(D), _row(D)], [_tok(D)],
                   [_sds(x.shape, F32)], [True] * 5)
    return op(x, y, gate, g, b)[0]


def _swiglu_gate(a, b):
    m = a.shape[0]
    tm = _pick(m, (256, 128, 64))

    def f(a, b):
        a = a.astype(F32)
        return (_silu(a) * b.astype(F32),)
    spec = pl.BlockSpec((tm, DFF), lambda i: (i, 0))
    op = _block_op("swiglu_gate", f, (m // tm,), [spec, spec], [spec], [_sds(a.shape, BF16)], [True, True])
    return op(a, b)[0]


def _seg_ones(width):
    i = np.arange(width)
    return jnp.asarray((i[:, None] // HD) == (i[None, :] // HD), BF16)


def _rope_perm(width):
    p = np.zeros((width, width), np.float32)
    for j in range(width):
        if (j % 32) < 16:
            p[j + 16, j] = -1.0
        else:
            p[j - 16, j] = 1.0
    return jnp.asarray(p, BF16)


def _rope_tables(width):
    t = jnp.arange(T_LAT)
    pos = jnp.stack([t // GRID_W, t % GRID_W], axis=-1).astype(F32)
    inv_freq = ROPE_THETA ** (-jnp.arange(ROPE_PAIRS, dtype=F32) / ROPE_PAIRS)
    ang = pos[..., None] * inv_freq
    ang = jnp.broadcast_to(ang[:, :, None, :], (T_LAT, 2, 2, ROPE_PAIRS)).reshape(T_LAT, HD)
    ang = jnp.tile(ang, (1, width // HD))
    cos = jnp.concatenate([jnp.ones((T_CTX, width), F32), jnp.cos(ang)], axis=0)
    sin = jnp.concatenate([jnp.zeros((T_CTX, width), F32), jnp.sin(ang)], axis=0)
    return cos, sin


def _att_pre(x, w_row, name):
    width = x.shape[-1]
    cos, sin = _rope_tables(width)

    def f(x, w, cos, sin, seg, perm):
        ms = _mm(x * x, seg, prec="bx") * (1.0 / HD)
        xn = x * lax.rsqrt(ms + EPS) * w
        return (xn * cos + _mm(xn, perm, prec="bx") * sin,)
    tab = pl.BlockSpec((TT, width), lambda b, t: (t, 0))
    op = _block_op(name, f, _tok_grid(),
                   [_tok(width), _row(width), tab, tab, _const2((width, width)), _const2((width, width))],
                   [_tok(width)], [_sds(x.shape, F32)], [True, True, False, False, False, False])
    return op(x, w_row, cos, sin, _seg_ones(width), _rope_perm(width))[0]


def _attention(q, k, v, name, tq):
    b_, hk, g_, tq_all, _ = q.shape
    tk = k.shape[2]

    def f(q, k, v):
        outs = []
        for gi in range(g_):
            s = _mm(q[gi] * (HD ** -0.5), k, tb=True)
            m = lax.stop_gradient(jnp.max(s, axis=-1, keepdims=True))
            e = jnp.exp(s - m)
            outs.append(_mm(e, v) / jnp.sum(e, axis=-1, keepdims=True))
        return (jnp.stack(outs, axis=0),)
    qs = pl.BlockSpec((None, None, g_, tq, HD), lambda b, h, i: (b, h, 0, i, 0))
    ks = pl.BlockSpec((None, None, tk, HD), lambda b, h, i: (b, h, 0, 0))
    op = _block_op(name, f, (b_, hk, tq_all // tq), [qs, ks, ks], [qs], [_sds(q.shape, F32)],
                   [True, True, True], acc=[None, 'last', 'last'])
    return op(q, k, v)[0]


def _gdn_pre(qkv, conv_w):
    nt_c = GDN_QKV // 128
    flag = jnp.asarray((np.arange(nt_c) % 3 < 2).astype(np.float32)[:, None, None] * np.ones((1, 1, 128), np.float32))
    cw = jnp.broadcast_to(conv_w[None], (B_LOC,) + conv_w.shape)

    def f(x, cw, flag, seg):
        def conv(s):
            acc = cw[2:3, :] * s
            for j in (0, 1, 3, 4):
                acc = acc + cw[j:j + 1, :] * _shift_fn(j - 2)(s)
            return acc
        y = jnp.concatenate([conv(x[:T_CTX]), conv(x[T_CTX:])], axis=0)
        s = _silu(y)
        ss = _mm(s * s, seg, prec="bx")
        return (s * (flag * lax.rsqrt(ss + EPS) + (1.0 - flag)),)
    xs = pl.BlockSpec((None, _ntok(), 128), lambda b, j: (b, 0, j))
    op = _block_op("gdn_pre", f, (B_LOC, nt_c),
                   [xs, pl.BlockSpec((None, 5, 128), lambda b, j: (b, 0, j)),
                    pl.BlockSpec((None, 1, 128), lambda b, j: (j, 0, 0)), pl.BlockSpec((128, 128), lambda b, j: (0, 0))],
                   [xs], [_sds(qkv.shape, F32)], [True, True, False, False])
    return op(qkv, cw, flag, _seg_ones(128))[0]


def _gdn_gates(ba, a_log, dt_bias):
    pad = jnp.zeros((12,), F32)
    al = jnp.broadcast_to(jnp.concatenate([pad, a_log.reshape(12), jnp.zeros((104,), F32)])[None, None], (B_LOC, 1, 128))
    db = jnp.broadcast_to(jnp.concatenate([pad, dt_bias.reshape(12), jnp.zeros((104,), F32)])[None, None], (B_LOC, 1, 128))

    def f(x, al, db):
        lane = lax.broadcasted_iota(jnp.int32, x.shape, 1)
        return (jnp.where(lane < 12, _sigmoid(x), -jnp.exp(al) * _softplus(x + db)),)
    xs = pl.BlockSpec((None, _ntok(), 128), lambda b: (b, 0, 0))
    ps = pl.BlockSpec((None, 1, 128), lambda b: (b, 0, 0))
    op = _block_op("gdn_gates", f, (B_LOC,), [xs, ps, ps], [xs], [_sds(ba.shape, F32)], [True, True, True])
    return op(ba, al, db)[0]


def _unit_triangular_inverses(lowers):
    n = lowers[0].shape[0]
    eye = (lax.broadcasted_iota(jnp.int32, (n, n), 0) == lax.broadcasted_iota(jnp.int32, (n, n), 1)).astype(F32)
    nks = [-l for l in lowers]
    invs = [eye + nk for nk in nks]
    for _ in range(int(math.log2(n)) - 1):
        nks = [_dot_impl(nk, nk, False, False, "f32") for nk in nks]
        invs = [inv + _dot_impl(inv, nk, False, False, "f32") for inv, nk in zip(invs, nks)]
    return tuple(invs)


@jax.custom_vjp
def _solve_with_inverses(lowers, rhss, invs):
    return tuple(_dot_impl(inv, rhs, False, False, "f32") for inv, rhs in zip(invs, rhss))


def _solve_fwd(lowers, rhss, invs):
    sols = _solve_with_inverses(lowers, rhss, invs)
    return sols, (invs, sols)


def _solve_bwd(res, dsols):
    invs, sols = res
    drhss = tuple(_dot_impl(inv, d, True, False, "f32") for inv, d in zip(invs, dsols))
    dlowers = tuple(-_dot_impl(dr, s, False, True, "f32") for dr, s in zip(drhss, sols))
    return dlowers, drhss, tuple(jnp.zeros_like(inv) for inv in invs)


_solve_with_inverses.defvjp(_solve_fwd, _solve_bwd)


def _gdn_masks():
    ii, jj = np.arange(CHUNK)[:, None], np.arange(CHUNK)[None, :]
    fwd = [jj <= ii, jj < ii, ii <= jj]
    bwd = [jj >= ii, jj > ii, ii >= jj]
    return jnp.asarray(np.stack([np.stack(fwd), np.stack(bwd)]).astype(np.float32))


def _gdn_prep(qkv, g, beta):
    b_, t_, _ = qkv.shape
    nc = t_ // CHUNK
    cb = max(d for d in (1, 2, 3, 4, 6) if nc % d == 0)
    npair = GDN_H // 2

    def f(x, g, beta, masks, saved_inv=None):
        q2, k2, v2 = x[:, :2 * HD], x[:, 2 * HD:4 * HD], x[:, 4 * HD:]
        ii = lax.broadcasted_iota(jnp.int32, (CHUNK, CHUNK), 0)
        jj = lax.broadcasted_iota(jnp.int32, (CHUNK, CHUNK), 1)
        eye = (ii == jj).astype(F32)
        incl, strict, incl_t = masks[0] > 0.5, masks[1] > 0.5, masks[2] > 0.5
        items = [(hh, c) for hh in range(2) for c in range(cb)]
        qs, ks, vs, decays, kbs, rhss, qgs, kds, egs = [], [], [], [], [], [], [], [], []
        for hh, c in items:
            r = slice(c * CHUNK, (c + 1) * CHUNK)
            lanes = slice(hh * HD, (hh + 1) * HD)
            qc, kc, vc = q2[r, lanes] * (HD ** -0.5), k2[r, lanes], v2[r, lanes]
            g_row, b_row = g[hh, c], beta[hh, c]
            g_col = jnp.sum(eye * g_row, axis=1, keepdims=True)
            b_col = jnp.sum(eye * b_row, axis=1, keepdims=True)
            gc_col = jnp.sum(jnp.where(incl, g_row, 0.0), axis=1, keepdims=True)
            gc_row = jnp.sum(jnp.where(incl_t, g_col, 0.0), axis=0, keepdims=True)
            g_tot = jnp.sum(g_row, axis=1, keepdims=True)
            decays.append(jnp.where(incl, jnp.exp(jnp.where(incl, gc_col - gc_row, 0.0)), 0.0))
            kb = kc * b_col
            qs.append(qc)
            ks.append(kc)
            kbs.append(kb)
            rhss.append(jnp.concatenate([vc * b_col, kb * jnp.exp(gc_col)], axis=1))
            qgs.append(qc * jnp.exp(gc_col))
            kds.append(kc * jnp.exp(g_tot - gc_col))
            egs.append(jnp.broadcast_to(jnp.exp(g_tot), (1, CHUNK)))
        lowers = tuple(jnp.where(strict, _mm(kb, kc, tb=True) * dec, 0.0) for kb, kc, dec in zip(kbs, ks, decays))
        ins = [jnp.where(incl, _mm(qc, kc, tb=True) * dec, 0.0) for qc, kc, dec in zip(qs, ks, decays)]
        if saved_inv is None:
            invs = _unit_triangular_inverses(lowers)
        else:
            invs = tuple(saved_inv[hh, c * CHUNK:(c + 1) * CHUNK] for hh, c in items)
        sols = _solve_with_inverses(lowers, tuple(rhss), invs)
        us, ws = [s[:, :HD] for s in sols], [s[:, HD:] for s in sols]

        def heads(xs, joiner):
            return jnp.stack([joiner(xs[:cb]), joiner(xs[cb:])], axis=0)
        cat = lambda xs: jnp.concatenate(xs, axis=0)
        outs = (heads(us, cat), heads(ws, cat), heads(qgs, cat), heads(kds, cat), heads(ins, cat),
                heads(egs, lambda xs: jnp.stack(xs, axis=0)))
        return outs if saved_inv is not None else outs + (heads(list(invs), cat),)

    xs = pl.BlockSpec((None, cb * CHUNK, 6 * HD), lambda b, p, i, d: (b, i, p))
    rs = pl.BlockSpec((None, None, 2, None, cb, 1, CHUNK), lambda b, p, i, d: (b, p, 0, d, i, 0, 0))
    ts = pl.BlockSpec((None, None, 2, None, cb * CHUNK, HD), lambda b, p, i, d: (b, p, 0, d, i, 0))
    ms = pl.BlockSpec((None, 3, CHUNK, CHUNK), lambda b, p, i, d: (d, 0, 0, 0))
    big = _sds((b_, npair, 2, 2, t_, HD), F32)
    op = _block_op("gdn_prep", f, (b_, npair, nc // cb, 2), [xs, rs, rs, ms],
                   [ts, ts, ts, ts, ts, rs, ts], [big, big, big, big, big, _sds(g.shape, F32), big],
                   [True, True, True, False], acc=['last', None, None, None], n_res=1, f_bwd=f)
    return op(qkv, g, beta, _gdn_masks())


def _gdn_scan_specs(t_, backward):
    seg_c = T_CTX // CHUNK
    nseg = t_ // T_CTX

    def seg_of(d, s):
        s = nseg - 1 - s if backward else s
        return jnp.where(d == 0, s, jnp.where(s == 0, 0, nseg - s))
    ts = pl.BlockSpec((None, GDN_H // 2, 2, None, T_CTX, HD), lambda b, d, s: (b, 0, 0, d, seg_of(d, s), 0))
    es = pl.BlockSpec((None, GDN_H // 2, 2, None, seg_c, 1, CHUNK), lambda b, d, s: (b, 0, 0, d, seg_of(d, s), 0, 0))
    return ts, es, seg_c, nseg


def _gdn_scan_call(u, w, qg, kd, intra, eg):
    b_, t_ = u.shape[0], u.shape[4]
    ts, es, seg_c, nseg = _gdn_scan_specs(t_, False)
    heads = [(p, hh) for p in range(GDN_H // 2) for hh in range(2)]

    def body(u_ref, w_ref, qg_ref, kd_ref, in_ref, eg_ref, o_ref, st_ref, state):
        @pl.when(pl.program_id(2) == 0)
        def _():
            state[...] = jnp.zeros_like(state)
        d = pl.program_id(1)
        for i in range(seg_c):
            c = jnp.where(d == 0, i, seg_c - 1 - i)
            rows = pl.ds(pl.multiple_of(c * CHUNK, CHUNK), CHUNK)
            sts = [state[n] for n in range(len(heads))]
            for n, (p, hh) in enumerate(heads):
                st_ref[p, hh, rows, :] = sts[n]
            sbs = [st.astype(BF16) for st in sts]
            ws = [_raw_dot(w_ref[p, hh, rows, :].astype(BF16), sbs[n], False, False) for n, (p, hh) in enumerate(heads)]
            qss = [_raw_dot(qg_ref[p, hh, rows, :].astype(BF16), sbs[n], False, False) for n, (p, hh) in enumerate(heads)]
            vbs = [(u_ref[p, hh, rows, :] - ws[n]).astype(BF16) for n, (p, hh) in enumerate(heads)]
            for n, (p, hh) in enumerate(heads):
                o_ref[p, hh, rows, :] = qss[n] + _raw_dot(in_ref[p, hh, rows, :].astype(BF16), vbs[n], False, False)
            kvs = [_raw_dot(kd_ref[p, hh, rows, :].astype(BF16), vbs[n], True, False) for n, (p, hh) in enumerate(heads)]
            for n, (p, hh) in enumerate(heads):
                e = eg_ref[p, hh, pl.ds(c, 1), :, :].reshape(1, CHUNK)
                state[n] = sts[n] * e + kvs[n]
    return pl.pallas_call(body, name="gdn_scan_fwd", grid=(b_, 2, nseg), in_specs=[ts] * 5 + [es], out_specs=[ts, ts],
                          out_shape=[_sds(u.shape, F32), _sds(u.shape, F32)],
                          scratch_shapes=[pltpu.VMEM((GDN_H, HD, HD), F32)],
                          compiler_params=_cparams(("arbitrary",) * 3))(u, w, qg, kd, intra, eg)


def _gdn_scan_bwd_call(u, w, qg, kd, intra, eg, states, do):
    b_, t_ = u.shape[0], u.shape[4]
    ts, es, seg_c, nseg = _gdn_scan_specs(t_, True)
    heads = [(p, hh) for p in range(GDN_H // 2) for hh in range(2)]

    def body(u_ref, w_ref, qg_ref, kd_ref, in_ref, eg_ref, st_ref, do_ref, du_ref, dw_ref, dqg_ref, dkd_ref, din_ref,
             deg_ref, dstate):
        @pl.when(pl.program_id(2) == 0)
        def _():
            dstate[...] = jnp.zeros_like(dstate)
        d = pl.program_id(1)
        hs = list(enumerate(heads))
        for i in range(seg_c):
            c = jnp.where(d == 0, seg_c - 1 - i, i)
            rows = pl.ds(pl.multiple_of(c * CHUNK, CHUNK), CHUNK)
            dss = [dstate[n] for n, _ in hs]
            sts = [st_ref[p, hh, rows, :] for _, (p, hh) in hs]
            sbs = [st.astype(BF16) for st in sts]
            dsbs = [ds.astype(BF16) for ds in dss]
            wbs = [w_ref[p, hh, rows, :].astype(BF16) for _, (p, hh) in hs]
            dobs = [do_ref[p, hh, rows, :].astype(BF16) for _, (p, hh) in hs]
            kdbs = [kd_ref[p, hh, rows, :].astype(BF16) for _, (p, hh) in hs]
            vbs = [(u_ref[p, hh, rows, :] - _raw_dot(wbs[n], sbs[n], False, False)).astype(BF16) for n, (p, hh) in hs]
            dvns = [_raw_dot(in_ref[p, hh, rows, :].astype(BF16), dobs[n], True, False)
                    + _raw_dot(kdbs[n], dsbs[n], False, False) for n, (p, hh) in hs]
            dvbs = [dvn.astype(BF16) for dvn in dvns]
            for n, (p, hh) in hs:
                din_ref[p, hh, rows, :] = _raw_dot(dobs[n], vbs[n], False, True)
                dqg_ref[p, hh, rows, :] = _raw_dot(dobs[n], sbs[n], False, True)
                dkd_ref[p, hh, rows, :] = _raw_dot(vbs[n], dsbs[n], False, True)
                du_ref[p, hh, rows, :] = dvns[n]
                dw_ref[p, hh, rows, :] = -_raw_dot(dvbs[n], sbs[n], False, True)
                deg_ref[p, hh, pl.ds(c, 1), :, :] = jnp.sum(sts[n] * dss[n], axis=0, keepdims=True).reshape(1, 1, CHUNK)
            upd = [_raw_dot(qg_ref[p, hh, rows, :].astype(BF16), dobs[n], True, False)
                   - _raw_dot(wbs[n], dvbs[n], True, False) for n, (p, hh) in hs]
            for n, (p, hh) in hs:
                e = eg_ref[p, hh, pl.ds(c, 1), :, :].reshape(1, CHUNK)
                dstate[n] = dss[n] * e + upd[n]
    big = _sds(u.shape, F32)
    return pl.pallas_call(body, name="gdn_scan_bwd", grid=(b_, 2, nseg), in_specs=[ts] * 5 + [es, ts, ts],
                          out_specs=[ts] * 5 + [es], out_shape=[big] * 5 + [_sds(eg.shape, F32)],
                          scratch_shapes=[pltpu.VMEM((GDN_H, HD, HD), F32)],
                          compiler_params=_cparams(("arbitrary",) * 3))(u, w, qg, kd, intra, eg, states, do)


@jax.custom_vjp
def _gdn_scan(u, w, qg, kd, intra, eg):
    return _gdn_scan_call(u, w, qg, kd, intra, eg)[0]


def _gdn_scan_f(u, w, qg, kd, intra, eg):
    o, states = _gdn_scan_call(u, w, qg, kd, intra, eg)
    return o, (u, w, qg, kd, intra, eg, states)


def _gdn_scan_b(res, do):
    return tuple(_gdn_scan_bwd_call(*res, do))


_gdn_scan.defvjp(_gdn_scan_f, _gdn_scan_b)


def _gdn_post(o, z, w_row):
    def f(o, z, w, seg):
        ms = _mm(o * o, seg, prec="bx") * (1.0 / HD)
        return (o * lax.rsqrt(ms + EPS) * w * _silu(z),)
    op = _block_op("gdn_post", f, _tok_grid(), [_tok(GDN_W), _tok(GDN_W), _row(GDN_W), _const2((GDN_W, GDN_W))],
                   [_tok(GDN_W)], [_sds(o.shape, BF16)], [True, True, True, False])
    return op(o, z, w_row, _seg_ones(GDN_W))[0]


def _s5_tables(ar, ai, rev):
    pr, pi = [ar], [ai]
    for _ in range(7):
        pr, pi = pr + [pr[-1] * ar - pi[-1] * ai], pi + [pr[-1] * ai + pi[-1] * ar]
    if rev:
        pr, pi = pr[::-1], pi[::-1]
    return jnp.concatenate(pr, axis=0), jnp.concatenate(pi, axis=0)


def _s5_scan_call(bu, ar, ai, direction, h=None):
    b_, t_, n2 = bu.shape
    nblk = n2 // (2 * S5_LB)
    with_grad = h is not None
    rev = (direction == 1) != with_grad
    tr, ti = _s5_tables(ar, ai, rev)
    tab = jnp.concatenate([tr.reshape(8, nblk, 1, S5_LB), ti.reshape(8, nblk, 1, S5_LB)], axis=2).reshape(8, n2)
    ntile, ntc = t_ // 8, T_CTX // 8

    def path(j):
        return j if direction == 0 else jnp.where(j < ntc, ntc - 1 - j, ntile + ntc - 1 - j)

    def scan_tile(xr, xi, tabr, tabi, cr, ci):
        row = lax.broadcasted_iota(jnp.int32, xr.shape, 0)
        for k in (1, 2, 4):
            idx = (8 - k) if rev else (k - 1)
            akr, aki = tabr[idx:idx + 1, :], tabi[idx:idx + 1, :]
            sh = (8 - k) if rev else k
            sr, si = pltpu.roll(xr, sh, 0), pltpu.roll(xi, sh, 0)
            ok = (row < 8 - k) if rev else (row >= k)
            xr, xi = (xr + jnp.where(ok, akr * sr - aki * si, 0.0), xi + jnp.where(ok, akr * si + aki * sr, 0.0))
        return xr + tabr * cr - tabi * ci, xi + tabr * ci + tabi * cr

    def body(*refs):
        if with_grad:
            bu_ref, tab_ref, h_ref, o_ref, da_ref = refs
        else:
            bu_ref, tab_ref, o_ref = refs
        tabr, tabi = tab_ref[:, :S5_LB], tab_ref[:, S5_LB:]
        zero = jnp.zeros((1, S5_LB), F32)
        row = lax.broadcasted_iota(jnp.int32, (8, S5_LB), 0)

        def step(i, carry):
            cr, ci = carry[0], carry[1]
            j = (ntile - 1 - i) if with_grad else i
            rows = pl.ds(pl.multiple_of(path(j) * 8, 8), 8)
            hr, hi = scan_tile(bu_ref[rows, :S5_LB], bu_ref[rows, S5_LB:], tabr, tabi, cr, ci)
            o_ref[rows, :S5_LB] = hr
            o_ref[rows, S5_LB:] = hi
            out = (hr[0:1, :], hi[0:1, :]) if rev else (hr[7:8, :], hi[7:8, :])
            if with_grad:
                prev = pl.ds(pl.multiple_of(path(jnp.maximum(j - 1, 0)) * 8, 8), 8)
                live = jnp.where(j > 0, 1.0, 0.0)
                sh, edge = (1, 0) if direction == 0 else (7, 7)
                pr = jnp.where(row == edge, pltpu.roll(h_ref[prev, :S5_LB], sh, 0) * live, pltpu.roll(h_ref[rows, :S5_LB], sh, 0))
                pi = jnp.where(row == edge, pltpu.roll(h_ref[prev, S5_LB:], sh, 0) * live, pltpu.roll(h_ref[rows, S5_LB:], sh, 0))
                out = out + (carry[2] + hr * pr + hi * pi, carry[3] + hi * pr - hr * pi)
            return out
        init = (zero, zero) + ((jnp.zeros((8, S5_LB), F32),) * 2 if with_grad else ())
        fin = lax.fori_loop(0, ntile, step, init)
        if with_grad:
            da_ref[:, :S5_LB] = fin[2]
            da_ref[:, S5_LB:] = fin[3]
    xs = pl.BlockSpec((None, t_, 2 * S5_LB), lambda b, j: (b, 0, j))
    tb = pl.BlockSpec((8, 2 * S5_LB), lambda b, j: (0, j))
    if with_grad:
        return pl.pallas_call(body, name="s5_scan_bwd%d" % direction, grid=(b_, nblk), in_specs=[xs, tb, xs],
                              out_specs=[xs, pl.BlockSpec((None, 8, 2 * S5_LB), lambda b, j: (b, 0, j))],
                              out_shape=[_sds(bu.shape, F32), _sds((b_, 8, n2), F32)],
                              compiler_params=_cparams(("arbitrary", "arbitrary")))(bu, tab, h)
    return pl.pallas_call(body, name="s5_scan_fwd%d" % direction, grid=(b_, nblk), in_specs=[xs, tb], out_specs=xs,
                          out_shape=_sds(bu.shape, F32), compiler_params=_cparams(("arbitrary", "arbitrary")))(bu, tab)


@functools.lru_cache(maxsize=None)
def _s5_scan_fn(direction):
    @jax.custom_vjp
    def scan(bu, ar, ai):
        return _s5_scan_call(bu, ar, ai, direction)

    def fwd(bu, ar, ai):
        h = _s5_scan_call(bu, ar, ai, direction)
        return h, (h, ar, ai)

    def bwd(res, dh):
        h, ar, ai = res
        lam, da = _s5_scan_call(dh, ar, -ai, direction, h=h)
        nblk = da.shape[-1] // (2 * S5_LB)
        da = jnp.sum(da, axis=(0, 1)).reshape(nblk, 2, S5_LB)
        return lam, da[:, 0].reshape(1, -1), da[:, 1].reshape(1, -1)

    scan.defvjp(fwd, bwd)
    return scan


def _s5_post(u, y0, y1, d_row, glu_w, glu_b_row):
    def f(u, y0, y1, d, gw, gb):
        zz = _gelu_tanh(d * u + y0 + y1)
        return (zz * _sigmoid(_mm(zz, gw) + gb),)
    op = _block_op("s5_post", f, _tok_grid(),
                   [_tok(S5_W), _tok(S5_W), _tok(S5_W), _row(S5_W), _const2((S5_W, S5_W)), _row(S5_W)],
                   [_tok(S5_W)], [_sds(u.shape, BF16)], [True] * 6, acc=[None, None, None, None, 'all', None])
    return op(u, y0, y1, d_row, glu_w, glu_b_row)[0]


def _loss_rows(x, target):
    def f(x, t):
        e = x - t
        return (jnp.sum(e * e, axis=0, keepdims=True),)
    grid = (B_LOC, T_LAT // TT)
    op = _block_op("loss_rows", f, grid, [_tok(D), _tok(D)], [_row(D)], [_sds((B_LOC, T_LAT // TT, 1, D), F32)],
                   [True, False])
    return op(x, target)[0]


def _adamw_call(name, w, g, m, v, pieces):
    r, c = w.shape
    tr = _pick(r, (256, 128, 64, 32, 16, 8))
    c1, c2 = 1.0 - ADAM_B1 ** ADAM_STEP, 1.0 - ADAM_B2 ** ADAM_STEP

    def body(w_ref, g_ref, m_ref, v_ref, go_ref, d_ref, mo_ref, vo_ref):
        if pieces:
            g = g_ref[0].astype(F32)
            for i in range(1, NDEV):
                g = g + g_ref[i].astype(F32)
        else:
            g = g_ref[...]
        m = ADAM_B1 * m_ref[...] + (1.0 - ADAM_B1) * g
        v = ADAM_B2 * v_ref[...] + (1.0 - ADAM_B2) * (g * g)
        go_ref[...] = g
        mo_ref[...] = m
        vo_ref[...] = v
        d_ref[...] = -ADAM_LR * ((m / c1) / (jnp.sqrt(v / c2) + ADAM_EPS) + ADAM_WD * w_ref[...])
    spec = pl.BlockSpec((tr, c), lambda i: (i, 0))
    gspec = pl.BlockSpec((NDEV, tr, c), lambda i: (0, i, 0)) if pieces else spec
    return pl.pallas_call(body, name=name, grid=(r // tr,), in_specs=[spec, gspec, spec, spec], out_specs=[spec] * 4,
                          out_shape=[_sds((r, c), F32)] * 4, compiler_params=_cparams(("arbitrary",)))(w, g, m, v)


def _sum_pieces(x, name):
    _, r, c = x.shape
    tr = _pick(r, (512, 256, 128, 64, 32, 16, 8))

    def body(x_ref, o_ref):
        acc = x_ref[0]
        for i in range(1, NDEV):
            acc = acc + x_ref[i]
        o_ref[...] = acc
    return pl.pallas_call(body, name=name, grid=(r // tr,), in_specs=[pl.BlockSpec((NDEV, tr, c), lambda i: (0, i, 0))],
                          out_specs=pl.BlockSpec((tr, c), lambda i: (i, 0)), out_shape=_sds((r, c), F32),
                          compiler_params=_cparams(("arbitrary",)))(x)


def _pack_flat(arrs, lanes, row_mult):
    flat = jnp.concatenate([a.reshape(-1).astype(F32) for a in arrs])
    n = flat.shape[0]
    rows = -(-n // lanes)
    rows = -(-rows // row_mult) * row_mult
    return jnp.pad(flat, (0, rows * lanes - n)).reshape(rows, lanes)


def _unpack_flat(packed, shapes, lead=()):
    flat = packed.reshape(lead + (-1,))
    out, off = [], 0
    for s in shapes:
        n = int(np.prod(s))
        out.append(flat[..., off:off + n].reshape(lead + tuple(s)))
        off += n
    return out


FS_ROWS = DFF // NDEV
WI_ROWS = IN_COLS // NDEV
WI_PAD = -(-WI_ROWS // 16) * 16
WO_ROWS = D // NDEV


def _pad_rows(blk, rows, axis):
    pad = [(0, 0)] * blk.ndim
    pad[axis] = (0, rows - blk.shape[axis])
    return jnp.pad(blk, pad)


def _pack_big_shards(t, dtype):
    parts = []
    for l in range(DEPTH):
        for n in ('ffn_w1', 'ffn_w3', 'ffn_w2'):
            for i in range(2):
                parts.append(t[n][l, i] if n == 'ffn_w2' else t[n][l, i].T)
        parts.append(_pad_rows(t['w_in'][l].T, WI_PAD, 0))
        parts.append(t['w_out'][l])
    return jnp.concatenate(parts, axis=0).astype(dtype)


def _unpack_big_shards(p):
    out = {n: [] for n in BIG}
    off = 0
    for l in range(DEPTH):
        for n in ('ffn_w1', 'ffn_w3', 'ffn_w2'):
            pair = []
            for i in range(2):
                blk = p[off:off + FS_ROWS]
                off += FS_ROWS
                pair.append(blk if n == 'ffn_w2' else blk.T)
            out[n].append(jnp.stack(pair))
        out['w_in'].append(p[off:off + WI_ROWS].T)
        off += WI_PAD
        out['w_out'].append(p[off:off + WO_ROWS])
        off += WO_ROWS
    return {n: jnp.stack(v) for n, v in out.items()}


def _full_from_gathered(g):
    layers, off = [], 0
    for l in range(DEPTH):
        lw = {}
        for n in ('ffn_w1', 'ffn_w3', 'ffn_w2'):
            pair = []
            for i in range(2):
                pair.append(g[:, off:off + FS_ROWS].reshape(DFF, D))
                off += FS_ROWS
            lw[n] = pair
        lw['w_in'] = g[:, off:off + WI_ROWS].reshape(IN_COLS, D)
        off += WI_PAD
        lw['w_out'] = g[:, off:off + WO_ROWS].reshape(D, D)
        off += WO_ROWS
        layers.append(lw)
    return layers


def _pieces_from_full(layers):
    parts = []
    for lw in layers:
        for n in ('ffn_w1', 'ffn_w3', 'ffn_w2'):
            for i in range(2):
                parts.append(lw[n][i].reshape(NDEV, FS_ROWS, D))
        parts.append(_pad_rows(lw['w_in'].reshape(NDEV, WI_ROWS, D), WI_PAD, 1))
        parts.append(lw['w_out'].reshape(NDEV, WO_ROWS, D))
    return jnp.concatenate(parts, axis=1)


def _flip_segments(a, axis):
    ctx, lat = lax.slice_in_dim(a, 0, T_CTX, axis=axis), lax.slice_in_dim(a, T_CTX, _ntok(), axis=axis)
    return jnp.concatenate([jnp.flip(ctx, axis), jnp.flip(lat, axis)], axis=axis)


def _rows_of(vec_ctx, vec_lat):
    w = vec_lat.shape[-1]
    ntc, ntl = T_CTX // TT, T_LAT // TT
    return jnp.concatenate([jnp.broadcast_to(vec_ctx[None, None, None, :], (B_LOC, ntc, 1, w)),
                            jnp.broadcast_to(vec_lat[:, None, None, :], (B_LOC, ntl, 1, w))], axis=1)


def _rows_const(vec):
    return jnp.broadcast_to(vec[None, None, None, :], (B_LOC, _ntok() // TT, 1, vec.shape[-1]))


def _ffn_sublayer(xt, mrow, w1, w3, w2, g, b):
    h = _modulate(xt, mrow[0], mrow[1]).reshape(-1, D)
    a = _matmul_t("ffn_up", h, w1, BF16)
    bb = _matmul_t("ffn_up", h, w3, BF16)
    u = _swiglu_gate(a, bb)
    y = _matmul("ffn_down", u, w2, F32).reshape(xt.shape)
    return _post_norm(xt, y, mrow[2], _rows_const(g), _rows_const(b), 0.5)


def _s5_discretize(lam_re, lam_im, log_dt, b_re, b_im):
    dt = jnp.exp(log_dt)[:, None]
    zr, zi = lam_re * dt, lam_im * dt
    er = jnp.exp(zr)
    lbr, lbi = er * jnp.cos(zi), er * jnp.sin(zi)
    dd = lam_re * lam_re + lam_im * lam_im
    qr = ((lbr - 1.0) * lam_re + lbi * lam_im) / dd
    qi = (lbi * lam_re - (lbr - 1.0) * lam_im) / dd
    bbr = qr[..., None] * b_re - qi[..., None] * b_im
    bbi = qr[..., None] * b_im + qi[..., None] * b_re
    return lbr, lbi, bbr, bbi


def _s5_cols(a):
    return a.reshape(a.shape[:-1] + (S5_N // S5_LB, S5_LB))


def _s5_group(su, p):
    b_, t_, _ = su.shape
    eye = jnp.eye(S5_G, dtype=F32)
    ys = []
    for d in range(2):
        lbr, lbi, bbr, bbi = _s5_discretize(p['s5_lam_re'][d], p['s5_lam_im'][d], p['s5_log_dt'][d],
                                            p['s5_b_re'][d], p['s5_b_im'][d])
        bre = jnp.einsum('gph,gk->ghkp', bbr, eye).reshape(S5_W, S5_N)
        bim = jnp.einsum('gph,gk->ghkp', bbi, eye).reshape(S5_W, S5_N)
        bmat = jnp.stack([_s5_cols(bre), _s5_cols(bim)], axis=2).reshape(S5_W, 2 * S5_N)
        cre = jnp.einsum('ghp,gk->kpgh', p['s5_c_re'][d], eye).reshape(S5_N, S5_W)
        cim = -jnp.einsum('ghp,gk->kpgh', p['s5_c_im'][d], eye).reshape(S5_N, S5_W)
        cmat = jnp.stack([cre.reshape(S5_N // S5_LB, S5_LB, S5_W), cim.reshape(S5_N // S5_LB, S5_LB, S5_W)],
                         axis=1).reshape(2 * S5_N, S5_W)
        bu = _matmul("s5_in", su.reshape(-1, S5_W), bmat, F32).reshape(b_, t_, 2 * S5_N)
        hs = _s5_scan_fn(d)(bu, lbr.reshape(1, S5_N), lbi.reshape(1, S5_N))
        ys.append(_matmul("s5_out", hs.reshape(-1, 2 * S5_N), cmat, F32).reshape(b_, t_, S5_W))
    return _s5_post(su, ys[0], ys[1], _rows_const(p['s5_d']), p['glu_w'], _rows_const(p['glu_b']))


def _gdn_group(qkv, z, ba, p):
    b_, t_, _ = qkv.shape
    nc = t_ // CHUNK
    qkvn = _gdn_pre(qkv, _pair_major(p['gdn_conv_w']))
    bg = _gdn_gates(ba, p['gdn_a_log'], p['gdn_dt_bias'])

    def gates(a):
        a = a.reshape(b_, t_, 2, GDN_H // 2, 2).transpose(0, 3, 4, 2, 1)
        return a.reshape(b_, GDN_H // 2, 2, 2, nc, 1, CHUNK)
    outs = _gdn_prep(qkvn, gates(bg[..., 12:24]), gates(bg[..., 0:12]))
    o = _gdn_scan(*outs).reshape(b_, GDN_H, 2, t_, HD)
    o = (o[:, :, 0] + o[:, :, 1]).transpose(0, 2, 1, 3).reshape(b_, t_, GDN_W)
    return _gdn_post(o, z, _rows_const(jnp.tile(p['gdn_norm_w'], GDN_H)))


def _att_group(aq, ak, av, p):
    b_, t_, _ = aq.shape
    qn = _att_pre(aq, _rows_const(jnp.tile(p['q_norm_w'], ATT_W // HD)), "att_pre_q")
    kn = _att_pre(ak, _rows_const(jnp.tile(p['k_norm_w'], ATT_KW // HD)), "att_pre_k")
    q = qn.reshape(b_, t_, ATT_HKV, ATT_G, HD).transpose(0, 2, 3, 1, 4)
    k = kn.reshape(b_, t_, ATT_HKV, HD).transpose(0, 2, 1, 3)
    v = av.reshape(b_, t_, ATT_HKV, HD).transpose(0, 2, 1, 3)
    o_lat = _attention(q[:, :, :, T_CTX:], k, v, "att_lat", ATT_TQ)
    o_ctx = _attention(q[:, :, :, :T_CTX], k[:, :, :T_CTX], v[:, :, :T_CTX], "att_ctx", T_CTX)
    o = jnp.concatenate([o_ctx, o_lat], axis=3)
    return o.transpose(0, 3, 1, 2, 4).reshape(b_, t_, ATT_W)


def _pair_major(w):
    lead = w.shape[:-1]
    return w.reshape(lead + (3, GDN_H // 2, 2 * HD)).swapaxes(-3, -2).reshape(lead + (GDN_QKV,))


def _permute_w_in(wt):
    qkv = wt[:GDN_QKV].reshape(3, GDN_H // 2, 2 * HD, D).swapaxes(0, 1).reshape(GDN_QKV, D)
    return jnp.concatenate([qkv, wt[GDN_QKV:1536], wt[1560:], wt[1536:1560],
                            jnp.zeros((IN_PAD - IN_COLS, D), wt.dtype)], axis=0)


PROJ_CUTS = (0, 1152, 1536, 1920, 2048, 2176, 2432, 2560)


@jax.custom_vjp
def _split_proj(proj):
    return tuple(proj[..., a:b] for a, b in zip(PROJ_CUTS[:-1], PROJ_CUTS[1:]))


_split_proj.defvjp(lambda proj: (_split_proj(proj), None), lambda _, d: (jnp.concatenate(d, axis=-1),))


def _mixer_sublayer(xt, mrow, lw, p, g, b):
    h = _modulate(xt, mrow[3], mrow[4]).reshape(-1, D)
    proj = _matmul_t("mix_in", h, _permute_w_in(lw['w_in']), F32).reshape(xt.shape[:2] + (IN_PAD,))
    qkv, z, aq, ak, av, su, ba = _split_proj(proj)
    o_gdn = _gdn_group(qkv, z, ba, p)
    o_att = _att_group(aq, ak, av, p)
    o_s5 = _s5_group(su, p)
    cat = jnp.concatenate([o_gdn, o_att.astype(BF16), o_s5], axis=-1).reshape(-1, D)
    y = _matmul("mix_out", cat, lw['w_out'], F32).reshape(xt.shape)
    return _post_norm(xt, y, mrow[5], _rows_const(g), _rows_const(b), 1.0)


def _local_loss(x, mod, modc, big, small, ctx, target):
    xt = jnp.concatenate([ctx, x], axis=1)
    for l in range(DEPTH):
        mrow = [_rows_of(modc[l, k * D:(k + 1) * D], mod[l, :, k * D:(k + 1) * D]) for k in range(N_MOD)]
        p = {n: small[n][l] for n in small}
        lw = big[l]
        xt = _ffn_sublayer(xt, mrow[0:3], lw['ffn_w1'][0], lw['ffn_w3'][0], lw['ffn_w2'][0], p['ln_g'][0], p['ln_b'][0])
        xt = _mixer_sublayer(xt, mrow, lw, p, p['ln_g'][1], p['ln_b'][1])
        xt = _ffn_sublayer(xt, mrow[6:9], lw['ffn_w1'][1], lw['ffn_w3'][1], lw['ffn_w2'][1], p['ln_g'][2], p['ln_b'][2])
    part = _loss_rows(xt[:, T_CTX:], target)
    return (0.5 / D) * jnp.sum(part)


def _silu_plain(x):
    return x * jax.nn.sigmoid(x)


def _dsilu_plain(x):
    s = jax.nn.sigmoid(x)
    return s * (1.0 + x * (1.0 - s))


def _small_shapes():
    return {'ln_g': (DEPTH, 3, D), 'ln_b': (DEPTH, 3, D), 'gdn_conv_w': (DEPTH, 5, GDN_QKV), 'glu_w': (DEPTH, S5_W, S5_W),
            'gdn_a_log': (DEPTH, 2, GDN_H), 'gdn_dt_bias': (DEPTH, 2, GDN_H), 'gdn_norm_w': (DEPTH, HD),
            'q_norm_w': (DEPTH, HD), 'k_norm_w': (DEPTH, HD), 's5_lam_re': (DEPTH, 2, S5_G, S5_P),
            's5_lam_im': (DEPTH, 2, S5_G, S5_P), 's5_log_dt': (DEPTH, 2, S5_G),
            's5_b_re': (DEPTH, 2, S5_G, S5_P, S5_H), 's5_b_im': (DEPTH, 2, S5_G, S5_P, S5_H),
            's5_c_re': (DEPTH, 2, S5_G, S5_H, S5_P), 's5_c_im': (DEPTH, 2, S5_G, S5_H, S5_P),
            's5_d': (DEPTH, S5_W), 'glu_b': (DEPTH, S5_W)}


def _gather_small_sharded(gathered, name):
    if name == 'glu_w':
        return gathered.transpose(1, 0, 2, 3).reshape(DEPTH, S5_W, S5_W)
    lead = gathered.shape[1:-1]
    return jnp.moveaxis(gathered, 0, -2).reshape(lead + (-1,))


def _my_small_shard(full, name, me):
    if name == 'glu_w':
        return lax.dynamic_slice_in_dim(full, me * (S5_W // NDEV), S5_W // NDEV, axis=1)
    n = full.shape[-1] // NDEV
    return lax.dynamic_slice_in_dim(full, me * n, n, axis=full.ndim - 1)


def kernel(x, c, ctx, c_ctx, w_ada, b_ada, ln_g, ln_b, ffn_w1, ffn_w3, ffn_w2, w_in, w_out, gdn_conv_w, gdn_a_log, gdn_dt_bias, gdn_norm_w, q_norm_w, k_norm_w, s5_lam_re, s5_lam_im, s5_log_dt, s5_b_re, s5_b_im, s5_c_re, s5_c_im, s5_d, glu_w, glu_b, loss_target, m_c_ctx, m_w_ada, m_b_ada, m_ln_g, m_ln_b, m_ffn_w1, m_ffn_w3, m_ffn_w2, m_w_in, m_w_out, m_gdn_conv_w, m_gdn_a_log, m_gdn_dt_bias, m_gdn_norm_w, m_q_norm_w, m_k_norm_w, m_s5_lam_re, m_s5_lam_im, m_s5_log_dt, m_s5_b_re, m_s5_b_im, m_s5_c_re, m_s5_c_im, m_s5_d, m_glu_w, m_glu_b, v_c_ctx, v_w_ada, v_b_ada, v_ln_g, v_ln_b, v_ffn_w1, v_ffn_w3, v_ffn_w2, v_w_in, v_w_out, v_gdn_conv_w, v_gdn_a_log, v_gdn_dt_bias, v_gdn_norm_w, v_q_norm_w, v_k_norm_w, v_s5_lam_re, v_s5_lam_im, v_s5_log_dt, v_s5_b_re, v_s5_b_im, v_s5_c_re, v_s5_c_im, v_s5_d, v_glu_w, v_glu_b):
    a = dict(locals())
    me = _my_index()
    ada_cols = N_MOD * D // NDEV

    sc = _silu_plain(a['c'])
    scc = _silu_plain(a['c_ctx'])
    small_in = [sc] + [a[n] for n in SMALL_SHARDED]
    got = _all_gather(_pack_flat(small_in, 128, 8), "gather_small")
    parts = _unpack_flat(got, [t.shape for t in small_in], lead=(NDEV,))
    sc_all = parts[0].reshape(NDEV * B_LOC, D)
    small = {n: _gather_small_sharded(parts[1 + i], n) for i, n in enumerate(SMALL_SHARDED)}
    for n in SMALL_REPL:
        small[n] = a[n]

    nb = NDEV * B_LOC
    rows_pad = 8
    sc_rows = jnp.concatenate([sc_all, scc[None], jnp.zeros((rows_pad - 1, D), F32)], axis=0)
    mod_part = jnp.stack([_mm_call("ada_fwd", sc_rows, a['w_ada'][l], "nn", F32) for l in range(DEPTH)])
    mod_all = _all_gather(mod_part.reshape(DEPTH * (nb + rows_pad), ada_cols), "gather_mod")
    mod_all = mod_all.reshape(NDEV, DEPTH, nb + rows_pad, ada_cols).transpose(1, 2, 0, 3).reshape(DEPTH, nb + rows_pad, N_MOD * D)
    mod_all = mod_all + a['b_ada'][:, None, :]
    mod = lax.dynamic_slice_in_dim(mod_all, me * B_LOC, B_LOC, axis=1)
    modc = mod_all[:, nb]

    big = _full_from_gathered(_all_gather(_pack_big_shards(a, BF16), "gather_weights"))

    loss_part, grads = jax.value_and_grad(_local_loss, argnums=(0, 1, 2, 3, 4))(
        a['x'], mod, modc, big, small, a['ctx'], a['loss_target'])
    gx, gmod, gmodc, gbig, gsmall = grads

    gm_rows = jnp.concatenate([gmod, gmodc[:, None], jnp.zeros((DEPTH, rows_pad - B_LOC - 1, N_MOD * D), F32)], axis=1)
    gm_all = _all_gather(gm_rows.reshape(DEPTH * rows_pad, N_MOD * D), "gather_dmod").reshape(NDEV, DEPTH, rows_pad, N_MOD * D)
    gm_all = gm_all.transpose(1, 0, 2, 3).reshape(DEPTH, NDEV * rows_pad, N_MOD * D)
    g_b_ada = jnp.sum(gm_all, axis=1)
    sc_dev = jnp.concatenate([sc_all.reshape(NDEV, B_LOC, D), jnp.broadcast_to(scc[None, None], (NDEV, 1, D)),
                              jnp.zeros((NDEV, rows_pad - B_LOC - 1, D), F32)], axis=1).reshape(NDEV * rows_pad, D)
    gm_mine = lax.dynamic_slice_in_dim(gm_all, me * ada_cols, ada_cols, axis=2)
    g_w_ada = jnp.stack([_mm_call("ada_dw", sc_dev, gm_mine[l], "tn", F32) for l in range(DEPTH)])
    gmc = gm_mine.reshape(DEPTH, NDEV, rows_pad, ada_cols)[:, :, B_LOC].sum(axis=1)
    gmc = jnp.concatenate([gmc[:, None], jnp.zeros((DEPTH, 7, ada_cols), F32)], axis=1)
    dscc_part = sum(_mm_call("ada_dx", gmc[l], a['w_ada'][l], "nt", F32)[0] for l in range(DEPTH))

    small_names = SMALL_SHARDED + SMALL_REPL
    sums_in = [loss_part.reshape(1), dscc_part] + [gsmall[n] for n in small_names]
    tot = _sum_pieces(_all_gather(_pack_flat(sums_in, 128, 512), "gather_sums"), "sum_small")
    tparts = _unpack_flat(tot, [t.shape for t in sums_in])
    loss = tparts[0].reshape(())
    g_c_ctx = tparts[1] * _dsilu_plain(a['c_ctx'])
    g = {'c_ctx': g_c_ctx, 'b_ada': g_b_ada, 'w_ada': g_w_ada}
    for i, n in enumerate(small_names):
        g[n] = _my_small_shard(tparts[2 + i], n, me) if n in SMALL_SHARDED else tparts[2 + i]

    recv = _all_to_all(_pieces_from_full(gbig), "scatter_grads")
    gb, db, mb, vb = _adamw_call("adamw_big", _pack_big_shards(a, F32), recv,
                                 _pack_big_shards({n: a['m_' + n] for n in BIG}, F32),
                                 _pack_big_shards({n: a['v_' + n] for n in BIG}, F32), True)
    res = {'g': {}, 'd': {}, 'm': {}, 'v': {}}
    for key, packed in (('g', gb), ('d', db), ('m', mb), ('v', vb)):
        res[key].update(_unpack_big_shards(packed))

    shp = a['w_ada'].shape
    flat2 = lambda t: t.reshape(-1, shp[-1])
    ga, da, ma, va = _adamw_call("adamw_ada", flat2(a['w_ada']), flat2(g['w_ada']), flat2(a['m_w_ada']), flat2(a['v_w_ada']), False)
    for key, val in (('g', ga), ('d', da), ('m', ma), ('v', va)):
        res[key]['w_ada'] = val.reshape(shp)
    rest = [n for n in WEIGHTS if n not in BIG and n != 'w_ada']
    shapes = [a[n].shape for n in rest]
    pk = lambda d: _pack_flat([d[n] for n in rest], 128, 256)
    outs = _adamw_call("adamw_small", pk(a), pk(g), pk({n: a['m_' + n] for n in rest}), pk({n: a['v_' + n] for n in rest}), False)
    for key, val in zip(('g', 'd', 'm', 'v'), outs):
        for n, t in zip(rest, _unpack_flat(val, shapes)):
            res[key][n] = t

    out = [loss, gx]
    for key in ('g', 'd', 'm', 'v'):
        out += [res[key][n] for n in WEIGHTS]
    return tuple(out)
```

```python
import functools
import math

import numpy as np
import jax
import jax.numpy as jnp
from jax import lax
from jax.experimental import pallas as pl
from jax.experimental.pallas import tpu as pltpu

F32 = jnp.float32
BF16 = jnp.bfloat16
MESH = pl.DeviceIdType.MESH

NDEV = 8
D = 1024
DFF = 2816
DEPTH = 4
B_LOC = 4
T_CTX = 256
T_LAT = 2048
GRID_W = 64
N_MOD = 9
GDN_H = 6
HD = 64
GDN_QKV = 3 * GDN_H * HD
GDN_W = GDN_H * HD
ATT_HKV = 2
ATT_G = 3
ATT_W = ATT_HKV * ATT_G * HD
ATT_KW = ATT_HKV * HD
S5_G = 16
S5_H = 16
S5_P = 64
S5_W = S5_G * S5_H
S5_N = S5_G * S5_P
IN_COLS = 2456
IN_PAD = 2560
ROPE_THETA = 10000.0
ROPE_PAIRS = 16
ALPHA = (2.0 * 4) ** 0.25
EPS = 1e-6
CHUNK = 64
ADAM_LR, ADAM_B1, ADAM_B2, ADAM_EPS, ADAM_WD, ADAM_STEP = 0.001, 0.9, 0.999, 1e-08, 0.01, 10

TT = 256
ATT_TQ = 256
S5_LB = 256
VMEM_LIMIT = 56 * 1024 * 1024

WEIGHTS = ['c_ctx', 'w_ada', 'b_ada', 'ln_g', 'ln_b', 'ffn_w1', 'ffn_w3', 'ffn_w2', 'w_in', 'w_out', 'gdn_conv_w',
           'gdn_a_log', 'gdn_dt_bias', 'gdn_norm_w', 'q_norm_w', 'k_norm_w', 's5_lam_re', 's5_lam_im', 's5_log_dt',
           's5_b_re', 's5_b_im', 's5_c_re', 's5_c_im', 's5_d', 'glu_w', 'glu_b']
INPUTS = ['x', 'c', 'ctx'] + WEIGHTS + ['loss_target'] + ['m_' + n for n in WEIGHTS] + ['v_' + n for n in WEIGHTS]
BIG = ['ffn_w1', 'ffn_w3', 'ffn_w2', 'w_in', 'w_out']
SMALL_SHARDED = ['ln_g', 'ln_b', 'gdn_conv_w', 'glu_w']
SMALL_REPL = ['gdn_a_log', 'gdn_dt_bias', 'gdn_norm_w', 'q_norm_w', 'k_norm_w', 's5_lam_re', 's5_lam_im',
              's5_log_dt', 's5_b_re', 's5_b_im', 's5_c_re', 's5_c_im', 's5_d', 'glu_b']


def _cparams(sem=None):
    return pltpu.CompilerParams(dimension_semantics=sem, vmem_limit_bytes=VMEM_LIMIT)


def _ntok():
    return T_CTX + T_LAT


def _my_pos():
    return lax.axis_index("x"), lax.axis_index("y"), lax.axis_index("c")


def _my_index():
    x, y, c = _my_pos()
    return 4 * x + 2 * y + c


def _all_gather(shard, name):
    def body(x_ref, out_ref, send_sems, recv_sems, local_sem):
        x, y, c = _my_pos()
        me, sibling = (x, y, c), (x, y, 1 - c)
        chips = [(1 - x, y), (x, 1 - y), (1 - x, 1 - y)]

        def slab(px, py, pc):
            return out_ref.at[4 * px + 2 * py + pc]

        def copy(k, block, to, src=None):
            return pltpu.make_async_remote_copy(
                src_ref=slab(*block) if src is None else src, dst_ref=slab(*block),
                send_sem=send_sems.at[k], recv_sem=recv_sems.at[k], device_id=to, device_id_type=MESH)

        mine = pltpu.make_async_copy(x_ref, slab(*me), local_sem)
        mine.start()
        first = [copy(0, me, sibling, src=x_ref)]
        first += [copy(1 + j, me, (*chip, c), src=x_ref) for j, chip in enumerate(chips)]
        for cp in first:
            cp.start()
        passed = [copy(4 + j, (*chip, c), sibling) for j, chip in enumerate(chips)]
        for j, chip in enumerate(chips):
            copy(1 + j, (*chip, c), me).wait_recv()
            passed[j].start()
        copy(0, sibling, me).wait_recv()
        for j, chip in enumerate(chips):
            copy(4 + j, (*chip, 1 - c), me).wait_recv()
        for cp in first + passed:
            cp.wait_send()
        mine.wait()

    return pl.pallas_call(
        body, name=name,
        out_shape=jax.ShapeDtypeStruct((NDEV,) + shard.shape, shard.dtype),
        in_specs=[pl.BlockSpec(memory_space=pl.ANY)],
        out_specs=pl.BlockSpec(memory_space=pl.ANY),
        scratch_shapes=[pltpu.SemaphoreType.DMA((7,)), pltpu.SemaphoreType.DMA((7,)), pltpu.SemaphoreType.DMA],
    )(shard)


def _all_to_all(pieces, name):
    def body(x_ref, out_ref, send_sems, recv_sems, local_sem):
        x, y, c = _my_pos()
        me_i = 4 * x + 2 * y + c
        mine = pltpu.make_async_copy(x_ref.at[me_i], out_ref.at[me_i], local_sem)
        mine.start()
        sends, recvs = [], []
        for k in range(1, NDEV):
            px = 1 - x if (k >> 2) & 1 else x
            py = 1 - y if (k >> 1) & 1 else y
            pc = 1 - c if k & 1 else c
            peer_i = 4 * px + 2 * py + pc
            sends.append(pltpu.make_async_remote_copy(
                src_ref=x_ref.at[peer_i], dst_ref=out_ref.at[me_i], send_sem=send_sems.at[k - 1],
                recv_sem=recv_sems.at[k - 1], device_id=(px, py, pc), device_id_type=MESH))
            recvs.append(pltpu.make_async_remote_copy(
                src_ref=x_ref.at[peer_i], dst_ref=out_ref.at[peer_i], send_sem=send_sems.at[k - 1],
                recv_sem=recv_sems.at[k - 1], device_id=(px, py, pc), device_id_type=MESH))
        for cp in sends:
            cp.start()
        for cp in recvs:
            cp.wait_recv()
        for cp in sends:
            cp.wait_send()
        mine.wait()

    return pl.pallas_call(
        body, name=name,
        out_shape=jax.ShapeDtypeStruct(pieces.shape, pieces.dtype),
        in_specs=[pl.BlockSpec(memory_space=pl.ANY)],
        out_specs=pl.BlockSpec(memory_space=pl.ANY),
        scratch_shapes=[pltpu.SemaphoreType.DMA((7,)), pltpu.SemaphoreType.DMA((7,)), pltpu.SemaphoreType.DMA],
    )(pieces)


def _dims(ta, tb):
    return (((0 if ta else 1,), (1 if tb else 0,)), ((), ()))


def _raw_dot(a, b, ta, tb):
    return lax.dot_general(a, b, _dims(ta, tb), preferred_element_type=F32)


def _split2(x):
    hi = x.astype(BF16)
    return hi, (x - hi.astype(F32)).astype(BF16)


def _split3(x):
    hi = x.astype(BF16)
    r = x - hi.astype(F32)
    mid = r.astype(BF16)
    return hi, mid, (r - mid.astype(F32)).astype(BF16)


def _dot_impl(a, b, ta, tb, prec):
    if prec == "bf16":
        return _raw_dot(a.astype(BF16), b.astype(BF16), ta, tb)
    if prec == "bx3":
        bb = b.astype(BF16)
        h, m, l = _split3(a)
        return _raw_dot(h, bb, ta, tb) + (_raw_dot(m, bb, ta, tb) + _raw_dot(l, bb, ta, tb))
    if prec == "ax3":
        ab = a.astype(BF16)
        h, m, l = _split3(b)
        return _raw_dot(ab, h, ta, tb) + (_raw_dot(ab, m, ta, tb) + _raw_dot(ab, l, ta, tb))
    ka, kb = (0 if ta else 1), (1 if tb else 0)
    ah, al = _split2(a)
    if prec == "bx":
        bb = b.astype(BF16)
        return _raw_dot(jnp.concatenate([ah, al], axis=ka), jnp.concatenate([bb, bb], axis=kb), ta, tb)
    bh, bl = _split2(b)
    return _raw_dot(ah, bh, ta, tb) + (_raw_dot(ah, bl, ta, tb) + _raw_dot(al, bh, ta, tb))


@functools.lru_cache(maxsize=None)
def _mm_fn(ta, tb, prec):
    @jax.custom_vjp
    def mm(a, b):
        return _dot_impl(a, b, ta, tb, prec)

    def fwd(a, b):
        return mm(a, b), (a, b)

    def bwd(res, dc):
        a, b = res
        bprec = "f32" if prec == "f32" else "bf16"
        if prec in ("bx", "bx3"):
            assert not ta
            return _mm_fn(False, not tb, prec)(dc, b).astype(a.dtype), jnp.zeros_like(b)
        if prec == "ax3":
            assert not tb
            return jnp.zeros_like(a), _mm_fn(not ta, False, prec)(a, dc).astype(b.dtype)
        da = _mm_fn(tb, True, bprec)(b, dc) if ta else _mm_fn(False, not tb, bprec)(dc, b)
        db = _mm_fn(True, ta, bprec)(dc, a) if tb else _mm_fn(not ta, False, bprec)(a, dc)
        return da.astype(a.dtype), db.astype(b.dtype)

    mm.defvjp(fwd, bwd)
    return mm


def _mm(a, b, ta=False, tb=False, prec="bf16"):
    return _mm_fn(ta, tb, prec)(a, b)


@functools.lru_cache(maxsize=None)
def _shift_fn(k):
    @jax.custom_vjp
    def shift(x):
        n = x.shape[0]
        r = pltpu.roll(x, (-k) % n, 0)
        t = lax.broadcasted_iota(jnp.int32, x.shape, 0)
        ok = (t + k >= 0) & (t + k < n)
        return jnp.where(ok, r, 0.0)

    shift.defvjp(lambda x: (shift(x), None), lambda _, dy: (_shift_fn(-k)(dy),))
    return shift


def _sigmoid(x):
    return 1.0 / (1.0 + jnp.exp(-x))


@jax.custom_vjp
def _softplus(x):
    y = jnp.exp(-jnp.abs(x))
    u = 1.0 + y
    l1p = jnp.where(u == 1.0, y, jnp.log(u) * y / jnp.where(u == 1.0, 1.0, u - 1.0))
    return jnp.maximum(x, 0.0) + l1p


_softplus.defvjp(lambda x: (_softplus(x), x), lambda x, dy: (dy * _sigmoid(x),))


def _silu(x):
    return x * _sigmoid(x)


def _gelu_tanh(x):
    return 0.5 * x * (1.0 + jnp.tanh(math.sqrt(2.0 / math.pi) * (x + 0.044715 * (x * x * x))))


def _block_op(name, f, grid, in_specs, out_specs, out_shapes, diff, acc=None, n_res=0, f_bwd=None):
    n_in, n_all = len(in_specs), len(out_specs)
    n_out = n_all - n_res
    acc = acc or [None] * n_in
    didx = [i for i in range(n_in) if diff[i]]
    sem = ("arbitrary",) * len(grid)
    fb = f_bwd or f

    def run_fwd(*xs):
        def body(*refs):
            outs = f(*[r[...] for r in refs[:n_in]])
            for r, o in zip(refs[n_in:], outs):
                r[...] = o.astype(r.dtype)
        return pl.pallas_call(body, name=name + "_fwd", grid=grid, in_specs=in_specs, out_specs=out_specs,
                              out_shape=out_shapes, compiler_params=_cparams(sem))(*xs)

    def run_bwd(xs, res, douts):
        def body(*refs):
            ins = [r[...] for r in refs[:n_in]]
            ress = [r[...] for r in refs[n_in:n_in + n_res]]
            dos = [r[...] for r in refs[n_in + n_res:n_in + n_all]]

            def g(*dv):
                full = list(ins)
                for i, v in zip(didx, dv):
                    full[i] = v
                return tuple(fb(*full, *ress))

            outs, vjp = jax.vjp(g, *[ins[i] for i in didx])
            dins = vjp(tuple(d.astype(o.dtype) for d, o in zip(dos, outs)))
            for r, i, dv in zip(refs[n_in + n_all:], didx, dins):
                dv = dv.astype(r.dtype)
                if acc[i] is None:
                    r[...] = dv
                else:
                    if acc[i] == 'last':
                        first = pl.program_id(len(grid) - 1) == 0
                    else:
                        first = functools.reduce(jnp.logical_and, [pl.program_id(a) == 0 for a in range(len(grid))])

                    @pl.when(first)
                    def _(r=r, dv=dv):
                        r[...] = dv

                    @pl.when(jnp.logical_not(first))
                    def _(r=r, dv=dv):
                        r[...] += dv
        return pl.pallas_call(
            body, name=name + "_bwd", grid=grid,
            in_specs=list(in_specs) + list(out_specs[n_out:]) + list(out_specs[:n_out]),
            out_specs=[in_specs[i] for i in didx],
            out_shape=[jax.ShapeDtypeStruct(xs[i].shape, xs[i].dtype) for i in didx],
            compiler_params=_cparams(sem))(*xs, *res, *douts)

    @jax.custom_vjp
    def op(*xs):
        return tuple(run_fwd(*xs))[:n_out]

    def op_fwd(*xs):
        outs = tuple(run_fwd(*xs))
        return outs[:n_out], (xs, outs[n_out:])

    def op_bwd(saved, douts):
        xs, res = saved
        dins = run_bwd(xs, res, douts)
        full = [jnp.zeros_like(x) for x in xs]
        for i, dv in zip(didx, dins):
            full[i] = dv
        return tuple(full)

    op.defvjp(op_fwd, op_bwd)
    return op


def _sds(shape, dtype):
    return jax.ShapeDtypeStruct(tuple(shape), dtype)


def _pick(n, cands):
    for c in cands:
        if n % c == 0:
            return c
    return n


def _mm_call(name, a, b, mode, out_dtype):
    if mode == "tn":
        m, k = a.shape
        n = b.shape[1]
        tm = _pick(m, (512, 256, 128, 64))
        tn = _pick(n, (1408, 1280, 1152, 1024, 512, 256, 128))
        steps = m // tm

        def body(a_ref, b_ref, o_ref, acc_ref):
            i = pl.program_id(1)

            @pl.when(i == 0)
            def _():
                acc_ref[...] = jnp.zeros_like(acc_ref)

            acc_ref[...] += _raw_dot(a_ref[...].astype(BF16), b_ref[...].astype(BF16), True, False)

            @pl.when(i == steps - 1)
            def _():
                o_ref[...] = acc_ref[...].astype(o_ref.dtype)

        return pl.pallas_call(
            body, name=name, grid=(n // tn, steps),
            in_specs=[pl.BlockSpec((tm, k), lambda j, i: (i, 0)), pl.BlockSpec((tm, tn), lambda j, i: (i, j))],
            out_specs=pl.BlockSpec((k, tn), lambda j, i: (0, j)),
            out_shape=_sds((k, n), out_dtype),
            scratch_shapes=[pltpu.VMEM((k, tn), F32)],
            compiler_params=_cparams(("arbitrary", "arbitrary")))(a, b)

    m, k = a.shape
    n = b.shape[1] if mode == "nn" else b.shape[0]
    tm = _pick(m, (512, 256, 128, 64))
    tn = _pick(n, (1408, 1280, 1152, 1024, 512, 256, 128))

    def body(a_ref, b_ref, o_ref):
        o_ref[...] = _raw_dot(a_ref[...].astype(BF16), b_ref[...].astype(BF16), False, mode == "nt").astype(o_ref.dtype)

    b_spec = pl.BlockSpec((k, tn), lambda j, i: (0, j)) if mode == "nn" else pl.BlockSpec((tn, k), lambda j, i: (j, 0))
    return pl.pallas_call(
        body, name=name, grid=(n // tn, m // tm),
        in_specs=[pl.BlockSpec((tm, k), lambda j, i: (i, 0)), b_spec],
        out_specs=pl.BlockSpec((tm, tn), lambda j, i: (i, j)),
        out_shape=_sds((m, n), out_dtype),
        compiler_params=_cparams(("arbitrary", "arbitrary")))(a, b)


def _matmul(name, a, w, out_dtype):
    @jax.custom_vjp
    def mm(a, w):
        return _mm_call(name + "_nn", a, w, "nn", out_dtype)

    def fwd(a, w):
        return mm(a, w), (a, w)

    def bwd(res, dy):
        a, w = res
        return (_mm_call(name + "_nt", dy, w, "nt", a.dtype), _mm_call(name + "_tn", a, dy, "tn", w.dtype))

    mm.defvjp(fwd, bwd)
    return mm(a, w)


def _matmul_t(name, a, wt, out_dtype):
    @jax.custom_vjp
    def mm(a, wt):
        return _mm_call(name + "_nt", a, wt, "nt", out_dtype)

    def fwd(a, wt):
        return mm(a, wt), (a, wt)

    def bwd(res, dy):
        a, wt = res
        return (_mm_call(name + "_nn", dy, wt, "nn", a.dtype), _mm_call(name + "_tn", dy, a, "tn", wt.dtype))

    mm.defvjp(fwd, bwd)
    return mm(a, wt)


def _tok(width):
    return pl.BlockSpec((None, TT, width), lambda b, t: (b, t, 0))


def _row(width):
    return pl.BlockSpec((None, None, 1, width), lambda b, t: (b, t, 0, 0))


def _const2(shape):
    return pl.BlockSpec(shape, lambda b, t: (0,) * len(shape))


def _tok_grid():
    return (B_LOC, _ntok() // TT)


def _modulate(x, shift, scale):
    def f(x, sh, sc):
        return ((x * (1.0 + sc) + sh),)
    op = _block_op("modulate", f, _tok_grid(), [_tok(D), _row(D), _row(D)], [_tok(D)],
                   [_sds(x.shape, BF16)], [True, True, True])
    return op(x, shift, scale)[0]


def _post_norm(x, y, gate, g, b, res_w):
    def f(x, y, gate, g, b):
        z = ALPHA * x + res_w * gate * y
        mu = jnp.mean(z, axis=-1, keepdims=True)
        zc = z - mu
        var = jnp.mean(zc * zc, axis=-1, keepdims=True)
        return (zc * lax.rsqrt(var + EPS) * g + b,)
    op = _block_op("post_norm", f, _tok_grid(), [_tok(D), _tok(D), _row(D), _row(D), _row(D)], [_tok(D)],
                   [_sds(x.shape, F32)], [True] * 5)
    return op(x, y, gate, g, b)[0]


def _swiglu_gate(a, b):
    m = a.shape[0]
    tm = _pick(m, (256, 128, 64))

    def f(a, b):
        a = a.astype(F32)
        return (_silu(a) * b.astype(F32),)
    spec = pl.BlockSpec((tm, DFF), lambda i: (i, 0))
    op = _block_op("swiglu_gate", f, (m // tm,), [spec, spec], [spec], [_sds(a.shape, BF16)], [True, True])
    return op(a, b)[0]


def _seg_ones(width):
    i = np.arange(width)
    return jnp.asarray((i[:, None] // HD) == (i[None, :] // HD), BF16)


def _rope_perm(width):
    p = np.zeros((width, width), np.float32)
    for j in range(width):
        if (j % 32) < 16:
            p[j + 16, j] = -1.0
        else:
            p[j - 16, j] = 1.0
    return jnp.asarray(p, BF16)


def _rope_tables(width):
    t = jnp.arange(T_LAT)
    pos = jnp.stack([t // GRID_W, t % GRID_W], axis=-1).astype(F32)
    inv_freq = ROPE_THETA ** (-jnp.arange(ROPE_PAIRS, dtype=F32) / ROPE_PAIRS)
    ang = pos[..., None] * inv_freq
    ang = jnp.broadcast_to(ang[:, :, None, :], (T_LAT, 2, 2, ROPE_PAIRS)).reshape(T_LAT, HD)
    ang = jnp.tile(ang, (1, width // HD))
    cos = jnp.concatenate([jnp.ones((T_CTX, width), F32), jnp.cos(ang)], axis=0)
    sin = jnp.concatenate([jnp.zeros((T_CTX, width), F32), jnp.sin(ang)], axis=0)
    return cos, sin


def _att_pre(x, w_row, name):
    width = x.shape[-1]
    cos, sin = _rope_tables(width)

    def f(x, w, cos, sin, seg, perm):
        ms = _mm(x * x, seg, prec="bx") * (1.0 / HD)
        xn = x * lax.rsqrt(ms + EPS) * w
        return (xn * cos + _mm(xn, perm, prec="bx") * sin,)
    tab = pl.BlockSpec((TT, width), lambda b, t: (t, 0))
    op = _block_op(name, f, _tok_grid(),
                   [_tok(width), _row(width), tab, tab, _const2((width, width)), _const2((width, width))],
                   [_tok(width)], [_sds(x.shape, F32)], [True, True, False, False, False, False])
    return op(x, w_row, cos, sin, _seg_ones(width), _rope_perm(width))[0]


def _attention(q, k, v, name, tq):
    b_, hk, g_, tq_all, _ = q.shape
    tk = k.shape[2]

    def f(q, k, v):
        outs = []
        for gi in range(g_):
            s = _mm(q[gi] * (HD ** -0.5), k, tb=True)
            m = lax.stop_gradient(jnp.max(s, axis=-1, keepdims=True))
            e = jnp.exp(s - m)
            outs.append(_mm(e, v) / jnp.sum(e, axis=-1, keepdims=True))
        return (jnp.stack(outs, axis=0),)
    qs = pl.BlockSpec((None, None, g_, tq, HD), lambda b, h, i: (b, h, 0, i, 0))
    ks = pl.BlockSpec((None, None, tk, HD), lambda b, h, i: (b, h, 0, 0))
    op = _block_op(name, f, (b_, hk, tq_all // tq), [qs, ks, ks], [qs], [_sds(q.shape, F32)],
                   [True, True, True], acc=[None, 'last', 'last'])
    return op(q, k, v)[0]


def _gdn_pre(qkv, conv_w):
    nt_c = GDN_QKV // 128
    flag = jnp.asarray((np.arange(nt_c) % 3 < 2).astype(np.float32)[:, None, None] * np.ones((1, 1, 128), np.float32))
    cw = jnp.broadcast_to(conv_w[None], (B_LOC,) + conv_w.shape)

    def f(x, cw, flag, seg):
        def conv(s):
            acc = cw[2:3, :] * s
            for j in (0, 1, 3, 4):
                acc = acc + cw[j:j + 1, :] * _shift_fn(j - 2)(s)
            return acc
        y = jnp.concatenate([conv(x[:T_CTX]), conv(x[T_CTX:])], axis=0)
        s = _silu(y)
        ss = _mm(s * s, seg, prec="bx")
        return (s * (flag * lax.rsqrt(ss + EPS) + (1.0 - flag)),)
    xs = pl.BlockSpec((None, _ntok(), 128), lambda b, j: (b, 0, j))
    op = _block_op("gdn_pre", f, (B_LOC, nt_c),
                   [xs, pl.BlockSpec((None, 5, 128), lambda b, j: (b, 0, j)),
                    pl.BlockSpec((None, 1, 128), lambda b, j: (j, 0, 0)), pl.BlockSpec((128, 128), lambda b, j: (0, 0))],
                   [xs], [_sds(qkv.shape, F32)], [True, True, False, False])
    return op(qkv, cw, flag, _seg_ones(128))[0]


def _gdn_gates(ba, a_log, dt_bias):
    pad = jnp.zeros((12,), F32)
    al = jnp.broadcast_to(jnp.concatenate([pad, a_log.reshape(12), jnp.zeros((104,), F32)])[None, None], (B_LOC, 1, 128))
    db = jnp.broadcast_to(jnp.concatenate([pad, dt_bias.reshape(12), jnp.zeros((104,), F32)])[None, None], (B_LOC, 1, 128))

    def f(x, al, db):
        lane = lax.broadcasted_iota(jnp.int32, x.shape, 1)
        return (jnp.where(lane < 12, _sigmoid(x), -jnp.exp(al) * _softplus(x + db)),)
    xs = pl.BlockSpec((None, _ntok(), 128), lambda b: (b, 0, 0))
    ps = pl.BlockSpec((None, 1, 128), lambda b: (b, 0, 0))
    op = _block_op("gdn_gates", f, (B_LOC,), [xs, ps, ps], [xs], [_sds(ba.shape, F32)], [True, True, True])
    return op(ba, al, db)[0]


def _unit_triangular_inverses(lowers):
    n = lowers[0].shape[0]
    eye = (lax.broadcasted_iota(jnp.int32, (n, n), 0) == lax.broadcasted_iota(jnp.int32, (n, n), 1)).astype(F32)
    nks = [-l for l in lowers]
    invs = [eye + nk for nk in nks]
    for _ in range(int(math.log2(n)) - 1):
        nks = [_dot_impl(nk, nk, False, False, "f32") for nk in nks]
        invs = [inv + _dot_impl(inv, nk, False, False, "f32") for inv, nk in zip(invs, nks)]
    return tuple(invs)


@jax.custom_vjp
def _solve_with_inverses(lowers, rhss, invs):
    return tuple(_dot_impl(inv, rhs, False, False, "f32") for inv, rhs in zip(invs, rhss))


def _solve_fwd(lowers, rhss, invs):
    sols = _solve_with_inverses(lowers, rhss, invs)
    return sols, (invs, sols)


def _solve_bwd(res, dsols):
    invs, sols = res
    drhss = tuple(_dot_impl(inv, d, True, False, "f32") for inv, d in zip(invs, dsols))
    dlowers = tuple(-_dot_impl(dr, s, False, True, "f32") for dr, s in zip(drhss, sols))
    return dlowers, drhss, tuple(jnp.zeros_like(inv) for inv in invs)


_solve_with_inverses.defvjp(_solve_fwd, _solve_bwd)


def _gdn_masks():
    ii, jj = np.arange(CHUNK)[:, None], np.arange(CHUNK)[None, :]
    fwd = [jj <= ii, jj < ii, ii <= jj]
    bwd = [jj >= ii, jj > ii, ii >= jj]
    return jnp.asarray(np.stack([np.stack(fwd), np.stack(bwd)]).astype(np.float32))


def _gdn_prep(qkv, g, beta):
    b_, t_, _ = qkv.shape
    nc = t_ // CHUNK
    cb = max(d for d in (1, 2, 3, 4, 6) if nc % d == 0)
    npair = GDN_H // 2

    def f(x, g, beta, masks, saved_inv=None):
        q2, k2, v2 = x[:, :2 * HD], x[:, 2 * HD:4 * HD], x[:, 4 * HD:]
        ii = lax.broadcasted_iota(jnp.int32, (CHUNK, CHUNK), 0)
        jj = lax.broadcasted_iota(jnp.int32, (CHUNK, CHUNK), 1)
        eye = (ii == jj).astype(F32)
        incl, strict, incl_t = masks[0] > 0.5, masks[1] > 0.5, masks[2] > 0.5
        items = [(hh, c) for hh in range(2) for c in range(cb)]
        def sl(a, hh, c):
            return a[c * CHUNK:(c + 1) * CHUNK, hh * HD:(hh + 1) * HD]
        qs = [sl(q2, hh, c) * (HD ** -0.5) for hh, c in items]
        ks = [sl(k2, hh, c) for hh, c in items]
        vs = [sl(v2, hh, c) for hh, c in items]
        ones = jnp.ones((CHUNK, CHUNK), F32)
        lane_sum = lambda m: _mm(m, ones, prec="bx3")
        row_sum = lambda m: _mm(ones, m, prec="ax3")
        g_rows = [jnp.broadcast_to(g[hh, c], (CHUNK, CHUNK)) for hh, c in items]
        b_rows = [jnp.broadcast_to(beta[hh, c], (CHUNK, CHUNK)) for hh, c in items]
        g_cols = [lane_sum(eye * gr) for gr in g_rows]
        b_cols = [lane_sum(eye * br) for br in b_rows]
        gc_cols = [lane_sum(jnp.where(incl, gr, 0.0)) for gr in g_rows]
        gc_rows = [row_sum(jnp.where(incl_t, gc, 0.0)) for gc in g_cols]
        g_tots = [lane_sum(gr) for gr in g_rows]
        decays = [jnp.where(incl, jnp.exp(jnp.where(incl, gcc - gcr, 0.0)), 0.0) for gcc, gcr in zip(gc_cols, gc_rows)]
        e_cols = [jnp.exp(gcc) for gcc in gc_cols]
        kbs = [kc * bc for kc, bc in zip(ks, b_cols)]
        rhss = [jnp.concatenate([vc * bc, kb * ec], axis=1) for vc, bc, kb, ec in zip(vs, b_cols, kbs, e_cols)]
        qgs = [qc * ec for qc, ec in zip(qs, e_cols)]
        kds = [kc * jnp.exp(gt - gcc) for kc, gt, gcc in zip(ks, g_tots, gc_cols)]
        egs = [jnp.exp(gt)[0:1, :] for gt in g_tots]
        lowers = tuple(jnp.where(strict, _mm(kb, kc, tb=True) * dec, 0.0) for kb, kc, dec in zip(kbs, ks, decays))
        ins = [jnp.where(incl, _mm(qc, kc, tb=True) * dec, 0.0) for qc, kc, dec in zip(qs, ks, decays)]
        if saved_inv is None:
            invs = _unit_triangular_inverses(lowers)
        else:
            invs = tuple(saved_inv[hh, c * CHUNK:(c + 1) * CHUNK] for hh, c in items)
        sols = _solve_with_inverses(lowers, tuple(rhss), invs)
        us, ws = [s[:, :HD] for s in sols], [s[:, HD:] for s in sols]

        def heads(xs, joiner):
            return jnp.stack([joiner(xs[:cb]), joiner(xs[cb:])], axis=0)
        cat = lambda xs: jnp.concatenate(xs, axis=0)
        outs = (heads(us, cat), heads(ws, cat), heads(qgs, cat), heads(kds, cat), heads(ins, cat),
                heads(egs, lambda xs: jnp.stack(xs, axis=0)))
        return outs if saved_inv is not None else outs + (heads(list(invs), cat),)

    xs = pl.BlockSpec((None, cb * CHUNK, 6 * HD), lambda b, p, i, d: (b, i, p))
    rs = pl.BlockSpec((None, None, 2, None, cb, 1, CHUNK), lambda b, p, i, d: (b, p, 0, d, i, 0, 0))
    ts = pl.BlockSpec((None, None, 2, None, cb * CHUNK, HD), lambda b, p, i, d: (b, p, 0, d, i, 0))
    ms = pl.BlockSpec((None, 3, CHUNK, CHUNK), lambda b, p, i, d: (d, 0, 0, 0))
    big = _sds((b_, npair, 2, 2, t_, HD), F32)
    op = _block_op("gdn_prep", f, (b_, npair, nc // cb, 2), [xs, rs, rs, ms],
                   [ts, ts, ts, ts, ts, rs, ts], [big, big, big, big, big, _sds(g.shape, F32), big],
                   [True, True, True, False], acc=['last', None, None, None], n_res=1, f_bwd=f)
    return op(qkv, g, beta, _gdn_masks())


def _gdn_scan_specs(t_, backward):
    seg_c = T_CTX // CHUNK
    nseg = t_ // T_CTX

    def seg_of(d, s):
        s = nseg - 1 - s if backward else s
        return jnp.where(d == 0, s, jnp.where(s == 0, 0, nseg - s))
    ts = pl.BlockSpec((None, GDN_H // 2, 2, None, T_CTX, HD), lambda b, d, s: (b, 0, 0, d, seg_of(d, s), 0))
    es = pl.BlockSpec((None, GDN_H // 2, 2, None, seg_c, 1, CHUNK), lambda b, d, s: (b, 0, 0, d, seg_of(d, s), 0, 0))
    return ts, es, seg_c, nseg


def _gdn_scan_call(u, w, qg, kd, intra, eg):
    b_, t_ = u.shape[0], u.shape[4]
    ts, es, seg_c, nseg = _gdn_scan_specs(t_, False)
    heads = [(p, hh) for p in range(GDN_H // 2) for hh in range(2)]

    def body(u_ref, w_ref, qg_ref, kd_ref, in_ref, eg_ref, o_ref, st_ref, state):
        @pl.when(pl.program_id(2) == 0)
        def _():
            state[...] = jnp.zeros_like(state)
        d = pl.program_id(1)
        for i in range(seg_c):
            c = jnp.where(d == 0, i, seg_c - 1 - i)
            rows = pl.ds(pl.multiple_of(c * CHUNK, CHUNK), CHUNK)
            sts = [state[n] for n in range(len(heads))]
            for n, (p, hh) in enumerate(heads):
                st_ref[p, hh, rows, :] = sts[n]
            sbs = [st.astype(BF16) for st in sts]
            ws = [_raw_dot(w_ref[p, hh, rows, :].astype(BF16), sbs[n], False, False) for n, (p, hh) in enumerate(heads)]
            qss = [_raw_dot(qg_ref[p, hh, rows, :].astype(BF16), sbs[n], False, False) for n, (p, hh) in enumerate(heads)]
            vbs = [(u_ref[p, hh, rows, :] - ws[n]).astype(BF16) for n, (p, hh) in enumerate(heads)]
            for n, (p, hh) in enumerate(heads):
                o_ref[p, hh, rows, :] = qss[n] + _raw_dot(in_ref[p, hh, rows, :].astype(BF16), vbs[n], False, False)
            kvs = [_raw_dot(kd_ref[p, hh, rows, :].astype(BF16), vbs[n], True, False) for n, (p, hh) in enumerate(heads)]
            for n, (p, hh) in enumerate(heads):
                e = eg_ref[p, hh, pl.ds(c, 1), :, :].reshape(1, CHUNK)
                state[n] = sts[n] * e + kvs[n]
    return pl.pallas_call(body, name="gdn_scan_fwd", grid=(b_, 2, nseg), in_specs=[ts] * 5 + [es], out_specs=[ts, ts],
                          out_shape=[_sds(u.shape, F32), _sds(u.shape, F32)],
                          scratch_shapes=[pltpu.VMEM((GDN_H, HD, HD), F32)],
                          compiler_params=_cparams(("arbitrary",) * 3))(u, w, qg, kd, intra, eg)


def _gdn_scan_bwd_call(u, w, qg, kd, intra, eg, states, do):
    b_, t_ = u.shape[0], u.shape[4]
    ts, es, seg_c, nseg = _gdn_scan_specs(t_, True)
    heads = [(p, hh) for p in range(GDN_H // 2) for hh in range(2)]

    def body(u_ref, w_ref, qg_ref, kd_ref, in_ref, eg_ref, st_ref, do_ref, du_ref, dw_ref, dqg_ref, dkd_ref, din_ref,
             deg_ref, dstate):
        @pl.when(pl.program_id(2) == 0)
        def _():
            dstate[...] = jnp.zeros_like(dstate)
        d = pl.program_id(1)
        hs = list(enumerate(heads))
        for i in range(seg_c):
            c = jnp.where(d == 0, seg_c - 1 - i, i)
            rows = pl.ds(pl.multiple_of(c * CHUNK, CHUNK), CHUNK)
            dss = [dstate[n] for n, _ in hs]
            sts = [st_ref[p, hh, rows, :] for _, (p, hh) in hs]
            sbs = [st.astype(BF16) for st in sts]
            dsbs = [ds.astype(BF16) for ds in dss]
            wbs = [w_ref[p, hh, rows, :].astype(BF16) for _, (p, hh) in hs]
            dobs = [do_ref[p, hh, rows, :].astype(BF16) for _, (p, hh) in hs]
            kdbs = [kd_ref[p, hh, rows, :].astype(BF16) for _, (p, hh) in hs]
            vbs = [(u_ref[p, hh, rows, :] - _raw_dot(wbs[n], sbs[n], False, False)).astype(BF16) for n, (p, hh) in hs]
            dvns = [_raw_dot(in_ref[p, hh, rows, :].astype(BF16), dobs[n], True, False)
                    + _raw_dot(kdbs[n], dsbs[n], False, False) for n, (p, hh) in hs]
            dvbs = [dvn.astype(BF16) for dvn in dvns]
            for n, (p, hh) in hs:
                din_ref[p, hh, rows, :] = _raw_dot(dobs[n], vbs[n], False, True)
                dqg_ref[p, hh, rows, :] = _raw_dot(dobs[n], sbs[n], False, True)
                dkd_ref[p, hh, rows, :] = _raw_dot(vbs[n], dsbs[n], False, True)
                du_ref[p, hh, rows, :] = dvns[n]
                dw_ref[p, hh, rows, :] = -_raw_dot(dvbs[n], sbs[n], False, True)
                deg_ref[p, hh, pl.ds(c, 1), :, :] = jnp.sum(sts[n] * dss[n], axis=0, keepdims=True).reshape(1, 1, CHUNK)
            upd = [_raw_dot(qg_ref[p, hh, rows, :].astype(BF16), dobs[n], True, False)
                   - _raw_dot(wbs[n], dvbs[n], True, False) for n, (p, hh) in hs]
            for n, (p, hh) in hs:
                e = eg_ref[p, hh, pl.ds(c, 1), :, :].reshape(1, CHUNK)
                dstate[n] = dss[n] * e + upd[n]
    big = _sds(u.shape, F32)
    return pl.pallas_call(body, name="gdn_scan_bwd", grid=(b_, 2, nseg), in_specs=[ts] * 5 + [es, ts, ts],
                          out_specs=[ts] * 5 + [es], out_shape=[big] * 5 + [_sds(eg.shape, F32)],
                          scratch_shapes=[pltpu.VMEM((GDN_H, HD, HD), F32)],
                          compiler_params=_cparams(("arbitrary",) * 3))(u, w, qg, kd, intra, eg, states, do)


@jax.custom_vjp
def _gdn_scan(u, w, qg, kd, intra, eg):
    return _gdn_scan_call(u, w, qg, kd, intra, eg)[0]


def _gdn_scan_f(u, w, qg, kd, intra, eg):
    o, states = _gdn_scan_call(u, w, qg, kd, intra, eg)
    return o, (u, w, qg, kd, intra, eg, states)


def _gdn_scan_b(res, do):
    return tuple(_gdn_scan_bwd_call(*res, do))


_gdn_scan.defvjp(_gdn_scan_f, _gdn_scan_b)


def _gdn_post(o, z, w_row):
    def f(o, z, w, seg):
        ms = _mm(o * o, seg, prec="bx") * (1.0 / HD)
        return (o * lax.rsqrt(ms + EPS) * w * _silu(z),)
    op = _block_op("gdn_post", f, _tok_grid(), [_tok(GDN_W), _tok(GDN_W), _row(GDN_W), _const2((GDN_W, GDN_W))],
                   [_tok(GDN_W)], [_sds(o.shape, BF16)], [True, True, True, False])
    return op(o, z, w_row, _seg_ones(GDN_W))[0]


def _s5_tables(ar, ai, rev):
    pr, pi = [ar], [ai]
    for _ in range(7):
        pr, pi = pr + [pr[-1] * ar - pi[-1] * ai], pi + [pr[-1] * ai + pi[-1] * ar]
    if rev:
        pr, pi = pr[::-1], pi[::-1]
    return jnp.concatenate(pr, axis=0), jnp.concatenate(pi, axis=0)


def _s5_scan_call(bu, ar, ai, direction, h=None):
    b_, t_, n2 = bu.shape
    nblk = n2 // (2 * S5_LB)
    with_grad = h is not None
    rev = (direction == 1) != with_grad
    tr, ti = _s5_tables(ar, ai, rev)
    tab = jnp.concatenate([tr.reshape(8, nblk, 1, S5_LB), ti.reshape(8, nblk, 1, S5_LB)], axis=2).reshape(8, n2)
    ntile, ntc = t_ // 8, T_CTX // 8

    def path(j):
        return j if direction == 0 else jnp.where(j < ntc, ntc - 1 - j, ntile + ntc - 1 - j)

    def scan_tile(xr, xi, tabr, tabi, cr, ci):
        row = lax.broadcasted_iota(jnp.int32, xr.shape, 0)
        for k in (1, 2, 4):
            idx = (8 - k) if rev else (k - 1)
            akr, aki = tabr[idx:idx + 1, :], tabi[idx:idx + 1, :]
            sh = (8 - k) if rev else k
            sr, si = pltpu.roll(xr, sh, 0), pltpu.roll(xi, sh, 0)
            ok = (row < 8 - k) if rev else (row >= k)
            xr, xi = (xr + jnp.where(ok, akr * sr - aki * si, 0.0), xi + jnp.where(ok, akr * si + aki * sr, 0.0))
        return xr + tabr * cr - tabi * ci, xi + tabr * ci + tabi * cr

    def body(*refs):
        if with_grad:
            bu_ref, tab_ref, h_ref, o_ref, da_ref = refs
        else:
            bu_ref, tab_ref, o_ref = refs
        tabr, tabi = tab_ref[:, :S5_LB], tab_ref[:, S5_LB:]
        zero = jnp.zeros((1, S5_LB), F32)
        row = lax.broadcasted_iota(jnp.int32, (8, S5_LB), 0)

        def step(i, carry):
            cr, ci = carry[0], carry[1]
            j = (ntile - 1 - i) if with_grad else i
            rows = pl.ds(pl.multiple_of(path(j) * 8, 8), 8)
            hr, hi = scan_tile(bu_ref[rows, :S5_LB], bu_ref[rows, S5_LB:], tabr, tabi, cr, ci)
            o_ref[rows, :S5_LB] = hr
            o_ref[rows, S5_LB:] = hi
            out = (hr[0:1, :], hi[0:1, :]) if rev else (hr[7:8, :], hi[7:8, :])
            if with_grad:
                prev = pl.ds(pl.multiple_of(path(jnp.maximum(j - 1, 0)) * 8, 8), 8)
                live = jnp.where(j > 0, 1.0, 0.0)
                sh, edge = (1, 0) if direction == 0 else (7, 7)
                pr = jnp.where(row == edge, pltpu.roll(h_ref[prev, :S5_LB], sh, 0) * live, pltpu.roll(h_ref[rows, :S5_LB], sh, 0))
                pi = jnp.where(row == edge, pltpu.roll(h_ref[prev, S5_LB:], sh, 0) * live, pltpu.roll(h_ref[rows, S5_LB:], sh, 0))
                out = out + (carry[2] + hr * pr + hi * pi, carry[3] + hi * pr - hr * pi)
            return out
        init = (zero, zero) + ((jnp.zeros((8, S5_LB), F32),) * 2 if with_grad else ())
        fin = lax.fori_loop(0, ntile, step, init)
        if with_grad:
            da_ref[:, :S5_LB] = fin[2]
            da_ref[:, S5_LB:] = fin[3]
    xs = pl.BlockSpec((None, t_, 2 * S5_LB), lambda b, j: (b, 0, j))
    tb = pl.BlockSpec((8, 2 * S5_LB), lambda b, j: (0, j))
    if with_grad:
        return pl.pallas_call(body, name="s5_scan_bwd%d" % direction, grid=(b_, nblk), in_specs=[xs, tb, xs],
                              out_specs=[xs, pl.BlockSpec((None, 8, 2 * S5_LB), lambda b, j: (b, 0, j))],
                              out_shape=[_sds(bu.shape, F32), _sds((b_, 8, n2), F32)],
                              compiler_params=_cparams(("arbitrary", "arbitrary")))(bu, tab, h)
    return pl.pallas_call(body, name="s5_scan_fwd%d" % direction, grid=(b_, nblk), in_specs=[xs, tb], out_specs=xs,
                          out_shape=_sds(bu.shape, F32), compiler_params=_cparams(("arbitrary", "arbitrary")))(bu, tab)


@functools.lru_cache(maxsize=None)
def _s5_scan_fn(direction):
    @jax.custom_vjp
    def scan(bu, ar, ai):
        return _s5_scan_call(bu, ar, ai, direction)

    def fwd(bu, ar, ai):
        h = _s5_scan_call(bu, ar, ai, direction)
        return h, (h, ar, ai)

    def bwd(res, dh):
        h, ar, ai = res
        lam, da = _s5_scan_call(dh, ar, -ai, direction, h=h)
        nblk = da.shape[-1] // (2 * S5_LB)
        da = jnp.sum(da, axis=(0, 1)).reshape(nblk, 2, S5_LB)
        return lam, da[:, 0].reshape(1, -1), da[:, 1].reshape(1, -1)

    scan.defvjp(fwd, bwd)
    return scan


def _s5_post(u, y0, y1, d_row, glu_w, glu_b_row):
    def f(u, y0, y1, d, gw, gb):
        zz = _gelu_tanh(d * u + y0 + y1)
        return (zz * _sigmoid(_mm(zz, gw) + gb),)
    op = _block_op("s5_post", f, _tok_grid(),
                   [_tok(S5_W), _tok(S5_W), _tok(S5_W), _row(S5_W), _const2((S5_W, S5_W)), _row(S5_W)],
                   [_tok(S5_W)], [_sds(u.shape, BF16)], [True] * 6, acc=[None, None, None, None, 'all', None])
    return op(u, y0, y1, d_row, glu_w, glu_b_row)[0]


def _loss_rows(x, target):
    def f(x, t):
        e = x - t
        return (jnp.sum(e * e, axis=0, keepdims=True),)
    grid = (B_LOC, T_LAT // TT)
    op = _block_op("loss_rows", f, grid, [_tok(D), _tok(D)], [_row(D)], [_sds((B_LOC, T_LAT // TT, 1, D), F32)],
                   [True, False])
    return op(x, target)[0]


def _adamw_call(name, w, g, m, v, pieces):
    r, c = w.shape
    tr = _pick(r, (256, 128, 64, 32, 16, 8))
    c1, c2 = 1.0 - ADAM_B1 ** ADAM_STEP, 1.0 - ADAM_B2 ** ADAM_STEP

    def body(w_ref, g_ref, m_ref, v_ref, go_ref, d_ref, mo_ref, vo_ref):
        if pieces:
            g = g_ref[0].astype(F32)
            for i in range(1, NDEV):
                g = g + g_ref[i].astype(F32)
        else:
            g = g_ref[...]
        m = ADAM_B1 * m_ref[...] + (1.0 - ADAM_B1) * g
        v = ADAM_B2 * v_ref[...] + (1.0 - ADAM_B2) * (g * g)
        go_ref[...] = g
        mo_ref[...] = m
        vo_ref[...] = v
        d_ref[...] = -ADAM_LR * ((m / c1) / (jnp.sqrt(v / c2) + ADAM_EPS) + ADAM_WD * w_ref[...])
    spec = pl.BlockSpec((tr, c), lambda i: (i, 0))
    gspec = pl.BlockSpec((NDEV, tr, c), lambda i: (0, i, 0)) if pieces else spec
    return pl.pallas_call(body, name=name, grid=(r // tr,), in_specs=[spec, gspec, spec, spec], out_specs=[spec] * 4,
                          out_shape=[_sds((r, c), F32)] * 4, compiler_params=_cparams(("arbitrary",)))(w, g, m, v)


def _sum_pieces(x, name):
    _, r, c = x.shape
    tr = _pick(r, (512, 256, 128, 64, 32, 16, 8))

    def body(x_ref, o_ref):
        acc = x_ref[0]
        for i in range(1, NDEV):
            acc = acc + x_ref[i]
        o_ref[...] = acc
    return pl.pallas_call(body, name=name, grid=(r // tr,), in_specs=[pl.BlockSpec((NDEV, tr, c), lambda i: (0, i, 0))],
                          out_specs=pl.BlockSpec((tr, c), lambda i: (i, 0)), out_shape=_sds((r, c), F32),
                          compiler_params=_cparams(("arbitrary",)))(x)


def _pack_flat(arrs, lanes, row_mult):
    flat = jnp.concatenate([a.reshape(-1).astype(F32) for a in arrs])
    n = flat.shape[0]
    rows = -(-n // lanes)
    rows = -(-rows // row_mult) * row_mult
    return jnp.pad(flat, (0, rows * lanes - n)).reshape(rows, lanes)


def _unpack_flat(packed, shapes, lead=()):
    flat = packed.reshape(lead + (-1,))
    out, off = [], 0
    for s in shapes:
        n = int(np.prod(s))
        out.append(flat[..., off:off + n].reshape(lead + tuple(s)))
        off += n
    return out


FS_ROWS = DFF // NDEV
WI_ROWS = IN_COLS // NDEV
WI_PAD = -(-WI_ROWS // 16) * 16
WO_ROWS = D // NDEV


def _pad_rows(blk, rows, axis):
    pad = [(0, 0)] * blk.ndim
    pad[axis] = (0, rows - blk.shape[axis])
    return jnp.pad(blk, pad)


def _pack_big_shards(t, dtype):
    parts = []
    for l in range(DEPTH):
        for n in ('ffn_w1', 'ffn_w3', 'ffn_w2'):
            for i in range(2):
                parts.append(t[n][l, i] if n == 'ffn_w2' else t[n][l, i].T)
        parts.append(_pad_rows(t['w_in'][l].T, WI_PAD, 0))
        parts.append(t['w_out'][l])
    return jnp.concatenate(parts, axis=0).astype(dtype)


def _unpack_big_shards(p):
    out = {n: [] for n in BIG}
    off = 0
    for l in range(DEPTH):
        for n in ('ffn_w1', 'ffn_w3', 'ffn_w2'):
            pair = []
            for i in range(2):
                blk = p[off:off + FS_ROWS]
                off += FS_ROWS
                pair.append(blk if n == 'ffn_w2' else blk.T)
            out[n].append(jnp.stack(pair))
        out['w_in'].append(p[off:off + WI_ROWS].T)
        off += WI_PAD
        out['w_out'].append(p[off:off + WO_ROWS])
        off += WO_ROWS
    return {n: jnp.stack(v) for n, v in out.items()}


def _full_from_gathered(g):
    layers, off = [], 0
    for l in range(DEPTH):
        lw = {}
        for n in ('ffn_w1', 'ffn_w3', 'ffn_w2'):
            pair = []
            for i in range(2):
                pair.append(g[:, off:off + FS_ROWS].reshape(DFF, D))
                off += FS_ROWS
            lw[n] = pair
        lw['w_in'] = g[:, off:off + WI_ROWS].reshape(IN_COLS, D)
        off += WI_PAD
        lw['w_out'] = g[:, off:off + WO_ROWS].reshape(D, D)
        off += WO_ROWS
        layers.append(lw)
    return layers


def _pieces_from_full(layers):
    parts = []
    for lw in layers:
        for n in ('ffn_w1', 'ffn_w3', 'ffn_w2'):
            for i in range(2):
                parts.append(lw[n][i].reshape(NDEV, FS_ROWS, D))
        parts.append(_pad_rows(lw['w_in'].reshape(NDEV, WI_ROWS, D), WI_PAD, 1))
        parts.append(lw['w_out'].reshape(NDEV, WO_ROWS, D))
    return jnp.concatenate(parts, axis=1)


def _flip_segments(a, axis):
    ctx, lat = lax.slice_in_dim(a, 0, T_CTX, axis=axis), lax.slice_in_dim(a, T_CTX, _ntok(), axis=axis)
    return jnp.concatenate([jnp.flip(ctx, axis), jnp.flip(lat, axis)], axis=axis)


def _rows_of(vec_ctx, vec_lat):
    w = vec_lat.shape[-1]
    ntc, ntl = T_CTX // TT, T_LAT // TT
    return jnp.concatenate([jnp.broadcast_to(vec_ctx[None, None, None, :], (B_LOC, ntc, 1, w)),
                            jnp.broadcast_to(vec_lat[:, None, None, :], (B_LOC, ntl, 1, w))], axis=1)


def _rows_const(vec):
    return jnp.broadcast_to(vec[None, None, None, :], (B_LOC, _ntok() // TT, 1, vec.shape[-1]))


def _ffn_sublayer(xt, mrow, w1, w3, w2, g, b):
    h = _modulate(xt, mrow[0], mrow[1]).reshape(-1, D)
    a = _matmul_t("ffn_up", h, w1, BF16)
    bb = _matmul_t("ffn_up", h, w3, BF16)
    u = _swiglu_gate(a, bb)
    y = _matmul("ffn_down", u, w2, F32).reshape(xt.shape)
    return _post_norm(xt, y, mrow[2], _rows_const(g), _rows_const(b), 0.5)


def _s5_discretize(lam_re, lam_im, log_dt, b_re, b_im):
    dt = jnp.exp(log_dt)[:, None]
    zr, zi = lam_re * dt, lam_im * dt
    er = jnp.exp(zr)
    lbr, lbi = er * jnp.cos(zi), er * jnp.sin(zi)
    dd = lam_re * lam_re + lam_im * lam_im
    qr = ((lbr - 1.0) * lam_re + lbi * lam_im) / dd
    qi = (lbi * lam_re - (lbr - 1.0) * lam_im) / dd
    bbr = qr[..., None] * b_re - qi[..., None] * b_im
    bbi = qr[..., None] * b_im + qi[..., None] * b_re
    return lbr, lbi, bbr, bbi


def _s5_cols(a):
    return a.reshape(a.shape[:-1] + (S5_N // S5_LB, S5_LB))


def _s5_group(su, p):
    b_, t_, _ = su.shape
    eye = jnp.eye(S5_G, dtype=F32)
    ys = []
    for d in range(2):
        lbr, lbi, bbr, bbi = _s5_discretize(p['s5_lam_re'][d], p['s5_lam_im'][d], p['s5_log_dt'][d],
                                            p['s5_b_re'][d], p['s5_b_im'][d])
        bre = jnp.einsum('gph,gk->ghkp', bbr, eye).reshape(S5_W, S5_N)
        bim = jnp.einsum('gph,gk->ghkp', bbi, eye).reshape(S5_W, S5_N)
        bmat = jnp.stack([_s5_cols(bre), _s5_cols(bim)], axis=2).reshape(S5_W, 2 * S5_N)
        cre = jnp.einsum('ghp,gk->kpgh', p['s5_c_re'][d], eye).reshape(S5_N, S5_W)
        cim = -jnp.einsum('ghp,gk->kpgh', p['s5_c_im'][d], eye).reshape(S5_N, S5_W)
        cmat = jnp.stack([cre.reshape(S5_N // S5_LB, S5_LB, S5_W), cim.reshape(S5_N // S5_LB, S5_LB, S5_W)],
                         axis=1).reshape(2 * S5_N, S5_W)
        bu = _matmul("s5_in", su.reshape(-1, S5_W), bmat, F32).reshape(b_, t_, 2 * S5_N)
        hs = _s5_scan_fn(d)(bu, lbr.reshape(1, S5_N), lbi.reshape(1, S5_N))
        ys.append(_matmul("s5_out", hs.reshape(-1, 2 * S5_N), cmat, F32).reshape(b_, t_, S5_W))
    return _s5_post(su, ys[0], ys[1], _rows_const(p['s5_d']), p['glu_w'], _rows_const(p['glu_b']))


def _gdn_group(qkv, z, ba, p):
    b_, t_, _ = qkv.shape
    nc = t_ // CHUNK
    qkvn = _gdn_pre(qkv, _pair_major(p['gdn_conv_w']))
    bg = _gdn_gates(ba, p['gdn_a_log'], p['gdn_dt_bias'])

    def gates(a):
        a = a.reshape(b_, t_, 2, GDN_H // 2, 2).transpose(0, 3, 4, 2, 1)
        return a.reshape(b_, GDN_H // 2, 2, 2, nc, 1, CHUNK)
    outs = _gdn_prep(qkvn, gates(bg[..., 12:24]), gates(bg[..., 0:12]))
    o = _gdn_scan(*outs).reshape(b_, GDN_H, 2, t_, HD)
    o = (o[:, :, 0] + o[:, :, 1]).transpose(0, 2, 1, 3).reshape(b_, t_, GDN_W)
    return _gdn_post(o, z, _rows_const(jnp.tile(p['gdn_norm_w'], GDN_H)))


def _att_group(aq, ak, av, p):
    b_, t_, _ = aq.shape
    qn = _att_pre(aq, _rows_const(jnp.tile(p['q_norm_w'], ATT_W // HD)), "att_pre_q")
    kn = _att_pre(ak, _rows_const(jnp.tile(p['k_norm_w'], ATT_KW // HD)), "att_pre_k")
    q = qn.reshape(b_, t_, ATT_HKV, ATT_G, HD).transpose(0, 2, 3, 1, 4)
    k = kn.reshape(b_, t_, ATT_HKV, HD).transpose(0, 2, 1, 3)
    v = av.reshape(b_, t_, ATT_HKV, HD).transpose(0, 2, 1, 3)
    o_lat = _attention(q[:, :, :, T_CTX:], k, v, "att_lat", ATT_TQ)
    o_ctx = _attention(q[:, :, :, :T_CTX], k[:, :, :T_CTX], v[:, :, :T_CTX], "att_ctx", T_CTX)
    o = jnp.concatenate([o_ctx, o_lat], axis=3)
    return o.transpose(0, 3, 1, 2, 4).reshape(b_, t_, ATT_W)


def _pair_major(w):
    lead = w.shape[:-1]
    return w.reshape(lead + (3, GDN_H // 2, 2 * HD)).swapaxes(-3, -2).reshape(lead + (GDN_QKV,))


def _permute_w_in(wt):
    qkv = wt[:GDN_QKV].reshape(3, GDN_H // 2, 2 * HD, D).swapaxes(0, 1).reshape(GDN_QKV, D)
    return jnp.concatenate([qkv, wt[GDN_QKV:1536], wt[1560:], wt[1536:1560],
                            jnp.zeros((IN_PAD - IN_COLS, D), wt.dtype)], axis=0)


PROJ_CUTS = (0, 1152, 1536, 1920, 2048, 2176, 2432, 2560)


@jax.custom_vjp
def _split_proj(proj):
    return tuple(proj[..., a:b] for a, b in zip(PROJ_CUTS[:-1], PROJ_CUTS[1:]))


_split_proj.defvjp(lambda proj: (_split_proj(proj), None), lambda _, d: (jnp.concatenate(d, axis=-1),))


def _mixer_sublayer(xt, mrow, lw, p, g, b):
    h = _modulate(xt, mrow[3], mrow[4]).reshape(-1, D)
    proj = _matmul_t("mix_in", h, _permute_w_in(lw['w_in']), F32).reshape(xt.shape[:2] + (IN_PAD,))
    qkv, z, aq, ak, av, su, ba = _split_proj(proj)
    o_gdn = _gdn_group(qkv, z, ba, p)
    o_att = _att_group(aq, ak, av, p)
    o_s5 = _s5_group(su, p)
    cat = jnp.concatenate([o_gdn, o_att.astype(BF16), o_s5], axis=-1).reshape(-1, D)
    y = _matmul("mix_out", cat, lw['w_out'], F32).reshape(xt.shape)
    return _post_norm(xt, y, mrow[5], _rows_const(g), _rows_const(b), 1.0)


def _local_loss(x, mod, modc, big, small, ctx, target):
    xt = jnp.concatenate([ctx, x], axis=1)
    for l in range(DEPTH):
        mrow = [_rows_of(modc[l, k * D:(k + 1) * D], mod[l, :, k * D:(k + 1) * D]) for k in range(N_MOD)]
        p = {n: small[n][l] for n in small}
        lw = big[l]
        xt = _ffn_sublayer(xt, mrow[0:3], lw['ffn_w1'][0], lw['ffn_w3'][0], lw['ffn_w2'][0], p['ln_g'][0], p['ln_b'][0])
        xt = _mixer_sublayer(xt, mrow, lw, p, p['ln_g'][1], p['ln_b'][1])
        xt = _ffn_sublayer(xt, mrow[6:9], lw['ffn_w1'][1], lw['ffn_w3'][1], lw['ffn_w2'][1], p['ln_g'][2], p['ln_b'][2])
    part = _loss_rows(xt[:, T_CTX:], target)
    return (0.5 / D) * jnp.sum(part)


def _silu_plain(x):
    return x * jax.nn.sigmoid(x)


def _dsilu_plain(x):
    s = jax.nn.sigmoid(x)
    return s * (1.0 + x * (1.0 - s))


def _small_shapes():
    return {'ln_g': (DEPTH, 3, D), 'ln_b': (DEPTH, 3, D), 'gdn_conv_w': (DEPTH, 5, GDN_QKV), 'glu_w': (DEPTH, S5_W, S5_W),
            'gdn_a_log': (DEPTH, 2, GDN_H), 'gdn_dt_bias': (DEPTH, 2, GDN_H), 'gdn_norm_w': (DEPTH, HD),
            'q_norm_w': (DEPTH, HD), 'k_norm_w': (DEPTH, HD), 's5_lam_re': (DEPTH, 2, S5_G, S5_P),
            's5_lam_im': (DEPTH, 2, S5_G, S5_P), 's5_log_dt': (DEPTH, 2, S5_G),
            's5_b_re': (DEPTH, 2, S5_G, S5_P, S5_H), 's5_b_im': (DEPTH, 2, S5_G, S5_P, S5_H),
            's5_c_re': (DEPTH, 2, S5_G, S5_H, S5_P), 's5_c_im': (DEPTH, 2, S5_G, S5_H, S5_P),
            's5_d': (DEPTH, S5_W), 'glu_b': (DEPTH, S5_W)}


def _gather_small_sharded(gathered, name):
    if name == 'glu_w':
        return gathered.transpose(1, 0, 2, 3).reshape(DEPTH, S5_W, S5_W)
    lead = gathered.shape[1:-1]
    return jnp.moveaxis(gathered, 0, -2).reshape(lead + (-1,))


def _my_small_shard(full, name, me):
    if name == 'glu_w':
        return lax.dynamic_slice_in_dim(full, me * (S5_W // NDEV), S5_W // NDEV, axis=1)
    n = full.shape[-1] // NDEV
    return lax.dynamic_slice_in_dim(full, me * n, n, axis=full.ndim - 1)


def kernel(x, c, ctx, c_ctx, w_ada, b_ada, ln_g, ln_b, ffn_w1, ffn_w3, ffn_w2, w_in, w_out, gdn_conv_w, gdn_a_log, gdn_dt_bias, gdn_norm_w, q_norm_w, k_norm_w, s5_lam_re, s5_lam_im, s5_log_dt, s5_b_re, s5_b_im, s5_c_re, s5_c_im, s5_d, glu_w, glu_b, loss_target, m_c_ctx, m_w_ada, m_b_ada, m_ln_g, m_ln_b, m_ffn_w1, m_ffn_w3, m_ffn_w2, m_w_in, m_w_out, m_gdn_conv_w, m_gdn_a_log, m_gdn_dt_bias, m_gdn_norm_w, m_q_norm_w, m_k_norm_w, m_s5_lam_re, m_s5_lam_im, m_s5_log_dt, m_s5_b_re, m_s5_b_im, m_s5_c_re, m_s5_c_im, m_s5_d, m_glu_w, m_glu_b, v_c_ctx, v_w_ada, v_b_ada, v_ln_g, v_ln_b, v_ffn_w1, v_ffn_w3, v_ffn_w2, v_w_in, v_w_out, v_gdn_conv_w, v_gdn_a_log, v_gdn_dt_bias, v_gdn_norm_w, v_q_norm_w, v_k_norm_w, v_s5_lam_re, v_s5_lam_im, v_s5_log_dt, v_s5_b_re, v_s5_b_im, v_s5_c_re, v_s5_c_im, v_s5_d, v_glu_w, v_glu_b):
    a = dict(locals())
    me = _my_index()
    ada_cols = N_MOD * D // NDEV

    sc = _silu_plain(a['c'])
    scc = _silu_plain(a['c_ctx'])
    small_in = [sc] + [a[n] for n in SMALL_SHARDED]
    got = _all_gather(_pack_flat(small_in, 128, 8), "gather_small")
    parts = _unpack_flat(got, [t.shape for t in small_in], lead=(NDEV,))
    sc_all = parts[0].reshape(NDEV * B_LOC, D)
    small = {n: _gather_small_sharded(parts[1 + i], n) for i, n in enumerate(SMALL_SHARDED)}
    for n in SMALL_REPL:
        small[n] = a[n]

    nb = NDEV * B_LOC
    rows_pad = 8
    sc_rows = jnp.concatenate([sc_all, scc[None], jnp.zeros((rows_pad - 1, D), F32)], axis=0)
    mod_part = jnp.stack([_mm_call("ada_fwd", sc_rows, a['w_ada'][l], "nn", F32) for l in range(DEPTH)])
    mod_all = _all_gather(mod_part.reshape(DEPTH * (nb + rows_pad), ada_cols), "gather_mod")
    mod_all = mod_all.reshape(NDEV, DEPTH, nb + rows_pad, ada_cols).transpose(1, 2, 0, 3).reshape(DEPTH, nb + rows_pad, N_MOD * D)
    mod_all = mod_all + a['b_ada'][:, None, :]
    mod = lax.dynamic_slice_in_dim(mod_all, me * B_LOC, B_LOC, axis=1)
    modc = mod_all[:, nb]

    big = _full_from_gathered(_all_gather(_pack_big_shards(a, BF16), "gather_weights"))

    loss_part, grads = jax.value_and_grad(_local_loss, argnums=(0, 1, 2, 3, 4))(
        a['x'], mod, modc, big, small, a['ctx'], a['loss_target'])
    gx, gmod, gmodc, gbig, gsmall = grads

    gm_rows = jnp.concatenate([gmod, gmodc[:, None], jnp.zeros((DEPTH, rows_pad - B_LOC - 1, N_MOD * D), F32)], axis=1)
    gm_all = _all_gather(gm_rows.reshape(DEPTH * rows_pad, N_MOD * D), "gather_dmod").reshape(NDEV, DEPTH, rows_pad, N_MOD * D)
    gm_all = gm_all.transpose(1, 0, 2, 3).reshape(DEPTH, NDEV * rows_pad, N_MOD * D)
    g_b_ada = jnp.sum(gm_all, axis=1)
    sc_dev = jnp.concatenate([sc_all.reshape(NDEV, B_LOC, D), jnp.broadcast_to(scc[None, None], (NDEV, 1, D)),
                              jnp.zeros((NDEV, rows_pad - B_LOC - 1, D), F32)], axis=1).reshape(NDEV * rows_pad, D)
    gm_mine = lax.dynamic_slice_in_dim(gm_all, me * ada_cols, ada_cols, axis=2)
    g_w_ada = jnp.stack([_mm_call("ada_dw", sc_dev, gm_mine[l], "tn", F32) for l in range(DEPTH)])
    gmc = gm_mine.reshape(DEPTH, NDEV, rows_pad, ada_cols)[:, :, B_LOC].sum(axis=1)
    gmc = jnp.concatenate([gmc[:, None], jnp.zeros((DEPTH, 7, ada_cols), F32)], axis=1)
    dscc_part = sum(_mm_call("ada_dx", gmc[l], a['w_ada'][l], "nt", F32)[0] for l in range(DEPTH))

    small_names = SMALL_SHARDED + SMALL_REPL
    sums_in = [loss_part.reshape(1), dscc_part] + [gsmall[n] for n in small_names]
    tot = _sum_pieces(_all_gather(_pack_flat(sums_in, 128, 512), "gather_sums"), "sum_small")
    tparts = _unpack_flat(tot, [t.shape for t in sums_in])
    loss = tparts[0].reshape(())
    g_c_ctx = tparts[1] * _dsilu_plain(a['c_ctx'])
    g = {'c_ctx': g_c_ctx, 'b_ada': g_b_ada, 'w_ada': g_w_ada}
    for i, n in enumerate(small_names):
        g[n] = _my_small_shard(tparts[2 + i], n, me) if n in SMALL_SHARDED else tparts[2 + i]

    recv = _all_to_all(_pieces_from_full(gbig), "scatter_grads")
    gb, db, mb, vb = _adamw_call("adamw_big", _pack_big_shards(a, F32), recv,
                                 _pack_big_shards({n: a['m_' + n] for n in BIG}, F32),
                                 _pack_big_shards({n: a['v_' + n] for n in BIG}, F32), True)
    res = {'g': {}, 'd': {}, 'm': {}, 'v': {}}
    for key, packed in (('g', gb), ('d', db), ('m', mb), ('v', vb)):
        res[key].update(_unpack_big_shards(packed))

    shp = a['w_ada'].shape
    flat2 = lambda t: t.reshape(-1, shp[-1])
    ga, da, ma, va = _adamw_call("adamw_ada", flat2(a['w_ada']), flat2(g['w_ada']), flat2(a['m_w_ada']), flat2(a['v_w_ada']), False)
    for key, val in (('g', ga), ('d', da), ('m', ma), ('v', va)):
        res[key]['w_ada'] = val.reshape(shp)
    rest = [n for n in WEIGHTS if n not in BIG and n != 'w_ada']
    shapes = [a[n].shape for n in rest]
    pk = lambda d: _pack_flat([d[n] for n in rest], 128, 256)
    outs = _adamw_call("adamw_small", pk(a), pk(g), pk({n: a['m_' + n] for n in rest}), pk({n: a['v_' + n] for n in rest}), False)
    for key, val in zip(('g', 'd', 'm', 'v'), outs):
        for n, t in zip(rest, _unpack_flat(val, shapes)):
            res[key][n] = t

    out = [loss, gx]
    for key in ('g', 'd', 'm', 'v'):
        out += [res[key][n] for n in WEIGHTS]
    return tuple(out)
```

```python
import functools
import math

import numpy as np
import jax
import jax.numpy as jnp
from jax import lax
from jax.experimental import pallas as pl
from jax.experimental.pallas import tpu as pltpu

F32 = jnp.float32
BF16 = jnp.bfloat16
MESH = pl.DeviceIdType.MESH

NDEV = 8
D = 1024
DFF = 2816
DEPTH = 4
B_LOC = 4
T_CTX = 256
T_LAT = 2048
GRID_W = 64
N_MOD = 9
GDN_H = 6
HD = 64
GDN_QKV = 3 * GDN_H * HD
GDN_W = GDN_H * HD
ATT_HKV = 2
ATT_G = 3
ATT_W = ATT_HKV * ATT_G * HD
ATT_KW = ATT_HKV * HD
S5_G = 16
S5_H = 16
S5_P = 64
S5_W = S5_G * S5_H
S5_N = S5_G * S5_P
IN_COLS = 2456
IN_PAD = 2560
ROPE_THETA = 10000.0
ROPE_PAIRS = 16
ALPHA = (2.0 * 4) ** 0.25
EPS = 1e-6
CHUNK = 64
ADAM_LR, ADAM_B1, ADAM_B2, ADAM_EPS, ADAM_WD, ADAM_STEP = 0.001, 0.9, 0.999, 1e-08, 0.01, 10

TT = 256
ATT_TQ = 256
S5_LB = 256
VMEM_LIMIT = 56 * 1024 * 1024

WEIGHTS = ['c_ctx', 'w_ada', 'b_ada', 'ln_g', 'ln_b', 'ffn_w1', 'ffn_w3', 'ffn_w2', 'w_in', 'w_out', 'gdn_conv_w',
           'gdn_a_log', 'gdn_dt_bias', 'gdn_norm_w', 'q_norm_w', 'k_norm_w', 's5_lam_re', 's5_lam_im', 's5_log_dt',
           's5_b_re', 's5_b_im', 's5_c_re', 's5_c_im', 's5_d', 'glu_w', 'glu_b']
INPUTS = ['x', 'c', 'ctx'] + WEIGHTS + ['loss_target'] + ['m_' + n for n in WEIGHTS] + ['v_' + n for n in WEIGHTS]
BIG = ['ffn_w1', 'ffn_w3', 'ffn_w2', 'w_in', 'w_out']
SMALL_SHARDED = ['ln_g', 'ln_b', 'gdn_conv_w', 'glu_w']
SMALL_REPL = ['gdn_a_log', 'gdn_dt_bias', 'gdn_norm_w', 'q_norm_w', 'k_norm_w', 's5_lam_re', 's5_lam_im',
              's5_log_dt', 's5_b_re', 's5_b_im', 's5_c_re', 's5_c_im', 's5_d', 'glu_b']


def _cparams(sem=None):
    return pltpu.CompilerParams(dimension_semantics=sem, vmem_limit_bytes=VMEM_LIMIT)


def _ntok():
    return T_CTX + T_LAT


def _my_pos():
    return lax.axis_index("x"), lax.axis_index("y"), lax.axis_index("c")


def _my_index():
    x, y, c = _my_pos()
    return 4 * x + 2 * y + c


def _all_gather(shard, name):
    def body(x_ref, out_ref, send_sems, recv_sems, local_sem):
        x, y, c = _my_pos()
        me, sibling = (x, y, c), (x, y, 1 - c)
        chips = [(1 - x, y), (x, 1 - y), (1 - x, 1 - y)]

        def slab(px, py, pc):
            return out_ref.at[4 * px + 2 * py + pc]

        def copy(k, block, to, src=None):
            return pltpu.make_async_remote_copy(
                src_ref=slab(*block) if src is None else src, dst_ref=slab(*block),
                send_sem=send_sems.at[k], recv_sem=recv_sems.at[k], device_id=to, device_id_type=MESH)

        mine = pltpu.make_async_copy(x_ref, slab(*me), local_sem)
        mine.start()
        first = [copy(0, me, sibling, src=x_ref)]
        first += [copy(1 + j, me, (*chip, c), src=x_ref) for j, chip in enumerate(chips)]
        for cp in first:
            cp.start()
        passed = [copy(4 + j, (*chip, c), sibling) for j, chip in enumerate(chips)]
        for j, chip in enumerate(chips):
            copy(1 + j, (*chip, c), me).wait_recv()
            passed[j].start()
        copy(0, sibling, me).wait_recv()
        for j, chip in enumerate(chips):
            copy(4 + j, (*chip, 1 - c), me).wait_recv()
        for cp in first + passed:
            cp.wait_send()
        mine.wait()

    return pl.pallas_call(
        body, name=name,
        out_shape=jax.ShapeDtypeStruct((NDEV,) + shard.shape, shard.dtype),
        in_specs=[pl.BlockSpec(memory_space=pl.ANY)],
        out_specs=pl.BlockSpec(memory_space=pl.ANY),
        scratch_shapes=[pltpu.SemaphoreType.DMA((7,)), pltpu.SemaphoreType.DMA((7,)), pltpu.SemaphoreType.DMA],
    )(shard)


def _all_to_all(pieces, name):
    def body(x_ref, out_ref, send_sems, recv_sems, local_sem):
        x, y, c = _my_pos()
        me_i = 4 * x + 2 * y + c
        mine = pltpu.make_async_copy(x_ref.at[me_i], out_ref.at[me_i], local_sem)
        mine.start()
        sends, recvs = [], []
        for k in range(1, NDEV):
            px = 1 - x if (k >> 2) & 1 else x
            py = 1 - y if (k >> 1) & 1 else y
            pc = 1 - c if k & 1 else c
            peer_i = 4 * px + 2 * py + pc
            sends.append(pltpu.make_async_remote_copy(
                src_ref=x_ref.at[peer_i], dst_ref=out_ref.at[me_i], send_sem=send_sems.at[k - 1],
                recv_sem=recv_sems.at[k - 1], device_id=(px, py, pc), device_id_type=MESH))
            recvs.append(pltpu.make_async_remote_copy(
                src_ref=x_ref.at[peer_i], dst_ref=out_ref.at[peer_i], send_sem=send_sems.at[k - 1],
                recv_sem=recv_sems.at[k - 1], device_id=(px, py, pc), device_id_type=MESH))
        for cp in sends:
            cp.start()
        for cp in recvs:
            cp.wait_recv()
        for cp in sends:
            cp.wait_send()
        mine.wait()

    return pl.pallas_call(
        body, name=name,
        out_shape=jax.ShapeDtypeStruct(pieces.shape, pieces.dtype),
        in_specs=[pl.BlockSpec(memory_space=pl.ANY)],
        out_specs=pl.BlockSpec(memory_space=pl.ANY),
        scratch_shapes=[pltpu.SemaphoreType.DMA((7,)), pltpu.SemaphoreType.DMA((7,)), pltpu.SemaphoreType.DMA],
    )(pieces)


def _dims(ta, tb):
    return (((0 if ta else 1,), (1 if tb else 0,)), ((), ()))


def _raw_dot(a, b, ta, tb):
    return lax.dot_general(a, b, _dims(ta, tb), preferred_element_type=F32)


def _split2(x):
    hi = x.astype(BF16)
    return hi, (x - hi.astype(F32)).astype(BF16)


def _split3(x):
    hi = x.astype(BF16)
    r = x - hi.astype(F32)
    mid = r.astype(BF16)
    return hi, mid, (r - mid.astype(F32)).astype(BF16)


def _dot_impl(a, b, ta, tb, prec):
    if prec == "bf16":
        return _raw_dot(a.astype(BF16), b.astype(BF16), ta, tb)
    if prec == "bx3":
        bb = b.astype(BF16)
        h, m, l = _split3(a)
        return _raw_dot(h, bb, ta, tb) + (_raw_dot(m, bb, ta, tb) + _raw_dot(l, bb, ta, tb))
    if prec == "ax3":
        ab = a.astype(BF16)
        h, m, l = _split3(b)
        return _raw_dot(ab, h, ta, tb) + (_raw_dot(ab, m, ta, tb) + _raw_dot(ab, l, ta, tb))
    ka, kb = (0 if ta else 1), (1 if tb else 0)
    ah, al = _split2(a)
    if prec == "bx":
        bb = b.astype(BF16)
        return _raw_dot(jnp.concatenate([ah, al], axis=ka), jnp.concatenate([bb, bb], axis=kb), ta, tb)
    bh, bl = _split2(b)
    return _raw_dot(ah, bh, ta, tb) + (_raw_dot(ah, bl, ta, tb) + _raw_dot(al, bh, ta, tb))


@functools.lru_cache(maxsize=None)
def _mm_fn(ta, tb, prec):
    @jax.custom_vjp
    def mm(a, b):
        return _dot_impl(a, b, ta, tb, prec)

    def fwd(a, b):
        return mm(a, b), (a, b)

    def bwd(res, dc):
        a, b = res
        bprec = "f32" if prec == "f32" else "bf16"
        if prec in ("bx", "bx3"):
            assert not ta
            return _mm_fn(False, not tb, prec)(dc, b).astype(a.dtype), jnp.zeros_like(b)
        if prec == "ax3":
            assert not tb
            return jnp.zeros_like(a), _mm_fn(not ta, False, prec)(a, dc).astype(b.dtype)
        da = _mm_fn(tb, True, bprec)(b, dc) if ta else _mm_fn(False, not tb, bprec)(dc, b)
        db = _mm_fn(True, ta, bprec)(dc, a) if tb else _mm_fn(not ta, False, bprec)(a, dc)
        return da.astype(a.dtype), db.astype(b.dtype)

    mm.defvjp(fwd, bwd)
    return mm


def _mm(a, b, ta=False, tb=False, prec="bf16"):
    return _mm_fn(ta, tb, prec)(a, b)


@functools.lru_cache(maxsize=None)
def _shift_fn(k):
    @jax.custom_vjp
    def shift(x):
        n = x.shape[0]
        r = pltpu.roll(x, (-k) % n, 0)
        t = lax.broadcasted_iota(jnp.int32, x.shape, 0)
        ok = (t + k >= 0) & (t + k < n)
        return jnp.where(ok, r, 0.0)

    shift.defvjp(lambda x: (shift(x), None), lambda _, dy: (_shift_fn(-k)(dy),))
    return shift


def _sigmoid(x):
    return 1.0 / (1.0 + jnp.exp(-x))


@jax.custom_vjp
def _softplus(x):
    y = jnp.exp(-jnp.abs(x))
    u = 1.0 + y
    l1p = jnp.where(u == 1.0, y, jnp.log(u) * y / jnp.where(u == 1.0, 1.0, u - 1.0))
    return jnp.maximum(x, 0.0) + l1p


_softplus.defvjp(lambda x: (_softplus(x), x), lambda x, dy: (dy * _sigmoid(x),))


def _silu(x):
    return x * _sigmoid(x)


def _gelu_tanh(x):
    return 0.5 * x * (1.0 + jnp.tanh(math.sqrt(2.0 / math.pi) * (x + 0.044715 * (x * x * x))))


def _block_op(name, f, grid, in_specs, out_specs, out_shapes, diff, acc=None, n_res=0, f_bwd=None):
    n_in, n_all = len(in_specs), len(out_specs)
    n_out = n_all - n_res
    acc = acc or [None] * n_in
    didx = [i for i in range(n_in) if diff[i]]
    sem = ("arbitrary",) * len(grid)
    fb = f_bwd or f

    def run_fwd(*xs):
        def body(*refs):
            outs = f(*[r[...] for r in refs[:n_in]])
            for r, o in zip(refs[n_in:], outs):
                r[...] = o.astype(r.dtype)
        return pl.pallas_call(body, name=name + "_fwd", grid=grid, in_specs=in_specs, out_specs=out_specs,
                              out_shape=out_shapes, compiler_params=_cparams(sem))(*xs)

    def run_bwd(xs, res, douts):
        def body(*refs):
            ins = [r[...] for r in refs[:n_in]]
            ress = [r[...] for r in refs[n_in:n_in + n_res]]
            dos = [r[...] for r in refs[n_in + n_res:n_in + n_all]]

            def g(*dv):
                full = list(ins)
                for i, v in zip(didx, dv):
                    full[i] = v
                return tuple(fb(*full, *ress))

            outs, vjp = jax.vjp(g, *[ins[i] for i in didx])
            dins = vjp(tuple(d.astype(o.dtype) for d, o in zip(dos, outs)))
            for r, i, dv in zip(refs[n_in + n_all:], didx, dins):
                dv = dv.astype(r.dtype)
                if acc[i] is None:
                    r[...] = dv
                else:
                    if acc[i] == 'last':
                        first = pl.program_id(len(grid) - 1) == 0
                    else:
                        first = functools.reduce(jnp.logical_and, [pl.program_id(a) == 0 for a in range(len(grid))])

                    @pl.when(first)
                    def _(r=r, dv=dv):
                        r[...] = dv

                    @pl.when(jnp.logical_not(first))
                    def _(r=r, dv=dv):
                        r[...] += dv
        return pl.pallas_call(
            body, name=name + "_bwd", grid=grid,
            in_specs=list(in_specs) + list(out_specs[n_out:]) + list(out_specs[:n_out]),
            out_specs=[in_specs[i] for i in didx],
            out_shape=[jax.ShapeDtypeStruct(xs[i].shape, xs[i].dtype) for i in didx],
            compiler_params=_cparams(sem))(*xs, *res, *douts)

    @jax.custom_vjp
    def op(*xs):
        return tuple(run_fwd(*xs))[:n_out]

    def op_fwd(*xs):
        outs = tuple(run_fwd(*xs))
        return outs[:n_out], (xs, outs[n_out:])

    def op_bwd(saved, douts):
        xs, res = saved
        dins = run_bwd(xs, res, douts)
        full = [jnp.zeros_like(x) for x in xs]
        for i, dv in zip(didx, dins):
            full[i] = dv
        return tuple(full)

    op.defvjp(op_fwd, op_bwd)
    op.run_bwd = run_bwd
    return op


def _sds(shape, dtype):
    return jax.ShapeDtypeStruct(tuple(shape), dtype)


def _pick(n, cands):
    for c in cands:
        if n % c == 0:
            return c
    return n


def _mm_call(name, a, b, mode, out_dtype):
    if mode == "tn":
        m, k = a.shape
        n = b.shape[1]
        tm = _pick(m, (512, 256, 128, 64))
        tn = _pick(n, (1408, 1280, 1152, 1024, 512, 256, 128))
        steps = m // tm

        def body(a_ref, b_ref, o_ref, acc_ref):
            i = pl.program_id(1)

            @pl.when(i == 0)
            def _():
                acc_ref[...] = jnp.zeros_like(acc_ref)

            acc_ref[...] += _raw_dot(a_ref[...].astype(BF16), b_ref[...].astype(BF16), True, False)

            @pl.when(i == steps - 1)
            def _():
                o_ref[...] = acc_ref[...].astype(o_ref.dtype)

        return pl.pallas_call(
            body, name=name, grid=(n // tn, steps),
            in_specs=[pl.BlockSpec((tm, k), lambda j, i: (i, 0)), pl.BlockSpec((tm, tn), lambda j, i: (i, j))],
            out_specs=pl.BlockSpec((k, tn), lambda j, i: (0, j)),
            out_shape=_sds((k, n), out_dtype),
            scratch_shapes=[pltpu.VMEM((k, tn), F32)],
            compiler_params=_cparams(("arbitrary", "arbitrary")))(a, b)

    m, k = a.shape
    n = b.shape[1] if mode == "nn" else b.shape[0]
    tm = _pick(m, (512, 256, 128, 64))
    tn = _pick(n, (1408, 1280, 1152, 1024, 512, 256, 128))

    def body(a_ref, b_ref, o_ref):
        o_ref[...] = _raw_dot(a_ref[...].astype(BF16), b_ref[...].astype(BF16), False, mode == "nt").astype(o_ref.dtype)

    b_spec = pl.BlockSpec((k, tn), lambda j, i: (0, j)) if mode == "nn" else pl.BlockSpec((tn, k), lambda j, i: (j, 0))
    return pl.pallas_call(
        body, name=name, grid=(n // tn, m // tm),
        in_specs=[pl.BlockSpec((tm, k), lambda j, i: (i, 0)), b_spec],
        out_specs=pl.BlockSpec((tm, tn), lambda j, i: (i, j)),
        out_shape=_sds((m, n), out_dtype),
        compiler_params=_cparams(("arbitrary", "arbitrary")))(a, b)


def _matmul(name, a, w, out_dtype):
    @jax.custom_vjp
    def mm(a, w):
        return _mm_call(name + "_nn", a, w, "nn", out_dtype)

    def fwd(a, w):
        return mm(a, w), (a, w)

    def bwd(res, dy):
        a, w = res
        return (_mm_call(name + "_nt", dy, w, "nt", a.dtype), _mm_call(name + "_tn", a, dy, "tn", w.dtype))

    mm.defvjp(fwd, bwd)
    return mm(a, w)


def _matmul_t(name, a, wt, out_dtype):
    @jax.custom_vjp
    def mm(a, wt):
        return _mm_call(name + "_nt", a, wt, "nt", out_dtype)

    def fwd(a, wt):
        return mm(a, wt), (a, wt)

    def bwd(res, dy):
        a, wt = res
        return (_mm_call(name + "_nn", dy, wt, "nn", a.dtype), _mm_call(name + "_tn", dy, a, "tn", wt.dtype))

    mm.defvjp(fwd, bwd)
    return mm(a, wt)


def _tok(width):
    return pl.BlockSpec((None, TT, width), lambda b, t: (b, t, 0))


def _row(width):
    return pl.BlockSpec((None, None, 1, width), lambda b, t: (b, t, 0, 0))


def _const2(shape):
    return pl.BlockSpec(shape, lambda b, t: (0,) * len(shape))


def _tok_grid():
    return (B_LOC, _ntok() // TT)


def _modulate(x, shift, scale):
    def f(x, sh, sc):
        return ((x * (1.0 + sc) + sh),)
    op = _block_op("modulate", f, _tok_grid(), [_tok(D), _row(D), _row(D)], [_tok(D)],
                   [_sds(x.shape, BF16)], [True, True, True])
    return op(x, shift, scale)[0]


def _post_norm(x, y, gate, g, b, res_w):
    def f(x, y, gate, g, b):
        z = ALPHA * x + res_w * gate * y
        mu = jnp.mean(z, axis=-1, keepdims=True)
        zc = z - mu
        var = jnp.mean(zc * zc, axis=-1, keepdims=True)
        return (zc * lax.rsqrt(var + EPS) * g + b,)
    op = _block_op("post_norm", f, _tok_grid(), [_tok(D), _tok(D), _row(D), _row(D), _row(D)], [_tok(D)],
                   [_sds(x.shape, F32)], [True] * 5)
    return op(x, y, gate, g, b)[0]


def _swiglu_gate_op(m):
    tm = _pick(m, (256, 128, 64))

    def f(a, b):
        a = a.astype(F32)
        return (_silu(a) * b.astype(F32),)
    spec = pl.BlockSpec((tm, DFF), lambda i: (i, 0))
    return _block_op("swiglu_gate", f, (m // tm,), [spec, spec], [spec], [_sds((m, DFF), BF16)], [True, True])


def _ffn_up_call(h, w1t, w3t):
    m, k = h.shape
    n = w1t.shape[0]
    tm, tn = _pick(m, (512, 256, 128, 64)), _pick(n, (1408, 1024, 512, 256, 128))

    def body(h_ref, w1_ref, w3_ref, a_ref, b_ref, u_ref):
        hb = h_ref[...]
        a = _raw_dot(hb, w1_ref[...], False, True).astype(BF16)
        b = _raw_dot(hb, w3_ref[...], False, True).astype(BF16)
        a_ref[...] = a
        b_ref[...] = b
        u_ref[...] = (_silu(a.astype(F32)) * b.astype(F32)).astype(BF16)
    ws = pl.BlockSpec((tn, k), lambda j, i: (j, 0))
    os_ = pl.BlockSpec((tm, tn), lambda j, i: (i, j))
    return pl.pallas_call(body, name="ffn_up_gate", grid=(n // tn, m // tm),
                          in_specs=[pl.BlockSpec((tm, k), lambda j, i: (i, 0)), ws, ws], out_specs=[os_] * 3,
                          out_shape=[_sds((m, n), BF16)] * 3, compiler_params=_cparams(("arbitrary",) * 2))(h, w1t, w3t)


def _ffn_dh_call(da, db, w1t, w3t):
    m, n = da.shape
    k = w1t.shape[1]
    tm = _pick(m, (512, 256, 128, 64))

    def body(da_ref, db_ref, w1_ref, w3_ref, o_ref):
        o_ref[...] = (_raw_dot(da_ref[...], w1_ref[...], False, False)
                      + _raw_dot(db_ref[...], w3_ref[...], False, False)).astype(o_ref.dtype)
    xs = pl.BlockSpec((tm, n), lambda i: (i, 0))
    ws = pl.BlockSpec((n, k), lambda i: (0, 0))
    return pl.pallas_call(body, name="ffn_up_dh", grid=(m // tm,), in_specs=[xs, xs, ws, ws],
                          out_specs=pl.BlockSpec((tm, k), lambda i: (i, 0)), out_shape=_sds((m, k), BF16),
                          compiler_params=_cparams(("arbitrary",)))(da, db, w1t, w3t)


@jax.custom_vjp
def _ffn_up(h, w1t, w3t):
    return _ffn_up_call(h, w1t, w3t)[2]


def _ffn_up_f(h, w1t, w3t):
    a, b, u = _ffn_up_call(h, w1t, w3t)
    return u, (h, w1t, w3t, a, b)


def _ffn_up_b(res, du):
    h, w1t, w3t, a, b = res
    da, db = _swiglu_gate_op(h.shape[0]).run_bwd((a, b), (), (du,))
    return (_ffn_dh_call(da, db, w1t, w3t), _mm_call("ffn_up_tn", da, h, "tn", w1t.dtype),
            _mm_call("ffn_up_tn", db, h, "tn", w3t.dtype))


_ffn_up.defvjp(_ffn_up_f, _ffn_up_b)


def _seg_ones(width):
    i = np.arange(width)
    return jnp.asarray((i[:, None] // HD) == (i[None, :] // HD), BF16)


def _rope_perm(width):
    p = np.zeros((width, width), np.float32)
    for j in range(width):
        if (j % 32) < 16:
            p[j + 16, j] = -1.0
        else:
            p[j - 16, j] = 1.0
    return jnp.asarray(p, BF16)


def _rope_tables(width):
    t = jnp.arange(T_LAT)
    pos = jnp.stack([t // GRID_W, t % GRID_W], axis=-1).astype(F32)
    inv_freq = ROPE_THETA ** (-jnp.arange(ROPE_PAIRS, dtype=F32) / ROPE_PAIRS)
    ang = pos[..., None] * inv_freq
    ang = jnp.broadcast_to(ang[:, :, None, :], (T_LAT, 2, 2, ROPE_PAIRS)).reshape(T_LAT, HD)
    ang = jnp.tile(ang, (1, width // HD))
    cos = jnp.concatenate([jnp.ones((T_CTX, width), F32), jnp.cos(ang)], axis=0)
    sin = jnp.concatenate([jnp.zeros((T_CTX, width), F32), jnp.sin(ang)], axis=0)
    return cos, sin


def _att_pre(x, w_row, name):
    width = x.shape[-1]
    cos, sin = _rope_tables(width)

    def f(x, w, cos, sin, seg, perm):
        ms = _mm(x * x, seg, prec="bx") * (1.0 / HD)
        xn = x * lax.rsqrt(ms + EPS) * w
        return (xn * cos + _mm(xn, perm, prec="bx") * sin,)
    tab = pl.BlockSpec((TT, width), lambda b, t: (t, 0))
    op = _block_op(name, f, _tok_grid(),
                   [_tok(width), _row(width), tab, tab, _const2((width, width)), _const2((width, width))],
                   [_tok(width)], [_sds(x.shape, F32)], [True, True, False, False, False, False])
    return op(x, w_row, cos, sin, _seg_ones(width), _rope_perm(width))[0]


def _attention(q, k, v, name, tq):
    b_, hk, g_, tq_all, _ = q.shape
    tk = k.shape[2]

    def f(q, k, v):
        outs = []
        for gi in range(g_):
            s = _mm(q[gi] * (HD ** -0.5), k, tb=True)
            m = lax.stop_gradient(jnp.max(s, axis=-1, keepdims=True))
            e = jnp.exp(s - m)
            outs.append(_mm(e, v) / jnp.sum(e, axis=-1, keepdims=True))
        return (jnp.stack(outs, axis=0),)
    qs = pl.BlockSpec((None, None, g_, tq, HD), lambda b, h, i: (b, h, 0, i, 0))
    ks = pl.BlockSpec((None, None, tk, HD), lambda b, h, i: (b, h, 0, 0))
    op = _block_op(name, f, (b_, hk, tq_all // tq), [qs, ks, ks], [qs], [_sds(q.shape, F32)],
                   [True, True, True], acc=[None, 'last', 'last'])
    return op(q, k, v)[0]


def _gdn_pre(qkv, conv_w):
    nt_c = GDN_QKV // 128
    flag = jnp.asarray((np.arange(nt_c) % 3 < 2).astype(np.float32)[:, None, None] * np.ones((1, 1, 128), np.float32))
    cw = jnp.broadcast_to(conv_w[None], (B_LOC,) + conv_w.shape)

    def f(x, cw, flag, seg):
        def conv(s):
            acc = cw[2:3, :] * s
            for j in (0, 1, 3, 4):
                acc = acc + cw[j:j + 1, :] * _shift_fn(j - 2)(s)
            return acc
        y = jnp.concatenate([conv(x[:T_CTX]), conv(x[T_CTX:])], axis=0)
        s = _silu(y)
        ss = _mm(s * s, seg, prec="bx")
        return (s * (flag * lax.rsqrt(ss + EPS) + (1.0 - flag)),)
    xs = pl.BlockSpec((None, _ntok(), 128), lambda b, j: (b, 0, j))
    op = _block_op("gdn_pre", f, (B_LOC, nt_c),
                   [xs, pl.BlockSpec((None, 5, 128), lambda b, j: (b, 0, j)),
                    pl.BlockSpec((None, 1, 128), lambda b, j: (j, 0, 0)), pl.BlockSpec((128, 128), lambda b, j: (0, 0))],
                   [xs], [_sds(qkv.shape, F32)], [True, True, False, False])
    return op(qkv, cw, flag, _seg_ones(128))[0]


def _gdn_gates(ba, a_log, dt_bias):
    pad = jnp.zeros((12,), F32)
    al = jnp.broadcast_to(jnp.concatenate([pad, a_log.reshape(12), jnp.zeros((104,), F32)])[None, None], (B_LOC, 1, 128))
    db = jnp.broadcast_to(jnp.concatenate([pad, dt_bias.reshape(12), jnp.zeros((104,), F32)])[None, None], (B_LOC, 1, 128))

    def f(x, al, db):
        lane = lax.broadcasted_iota(jnp.int32, x.shape, 1)
        return (jnp.where(lane < 12, _sigmoid(x), -jnp.exp(al) * _softplus(x + db)),)
    xs = pl.BlockSpec((None, _ntok(), 128), lambda b: (b, 0, 0))
    ps = pl.BlockSpec((None, 1, 128), lambda b: (b, 0, 0))
    op = _block_op("gdn_gates", f, (B_LOC,), [xs, ps, ps], [xs], [_sds(ba.shape, F32)], [True, True, True])
    return op(ba, al, db)[0]


def _unit_triangular_inverses(lowers):
    n = lowers[0].shape[0]
    eye = (lax.broadcasted_iota(jnp.int32, (n, n), 0) == lax.broadcasted_iota(jnp.int32, (n, n), 1)).astype(F32)
    nks = [-l for l in lowers]
    invs = [eye + nk for nk in nks]
    for _ in range(int(math.log2(n)) - 1):
        nks = [_dot_impl(nk, nk, False, False, "f32") for nk in nks]
        invs = [inv + _dot_impl(inv, nk, False, False, "f32") for inv, nk in zip(invs, nks)]
    return tuple(invs)


@jax.custom_vjp
def _solve_with_inverses(lowers, rhss, invs):
    return tuple(_dot_impl(inv, rhs, False, False, "f32") for inv, rhs in zip(invs, rhss))


def _solve_fwd(lowers, rhss, invs):
    sols = _solve_with_inverses(lowers, rhss, invs)
    return sols, (invs, sols)


def _solve_bwd(res, dsols):
    invs, sols = res
    drhss = tuple(_dot_impl(inv, d, True, False, "f32") for inv, d in zip(invs, dsols))
    dlowers = tuple(-_dot_impl(dr, s, False, True, "f32") for dr, s in zip(drhss, sols))
    return dlowers, drhss, tuple(jnp.zeros_like(inv) for inv in invs)


_solve_with_inverses.defvjp(_solve_fwd, _solve_bwd)


def _gdn_masks():
    ii, jj = np.arange(CHUNK)[:, None], np.arange(CHUNK)[None, :]
    fwd = [jj <= ii, jj < ii, ii <= jj]
    bwd = [jj >= ii, jj > ii, ii >= jj]
    return jnp.asarray(np.stack([np.stack(fwd), np.stack(bwd)]).astype(np.float32))


def _gdn_prep(qkv, g, beta):
    b_, t_, _ = qkv.shape
    nc = t_ // CHUNK
    cb = max(d for d in (1, 2, 3, 4, 6) if nc % d == 0)
    npair = GDN_H // 2

    def f(x, g, beta, masks, saved_inv=None):
        q2, k2, v2 = x[:, :2 * HD], x[:, 2 * HD:4 * HD], x[:, 4 * HD:]
        ii = lax.broadcasted_iota(jnp.int32, (CHUNK, CHUNK), 0)
        jj = lax.broadcasted_iota(jnp.int32, (CHUNK, CHUNK), 1)
        eye = (ii == jj).astype(F32)
        incl, strict, incl_t = masks[0] > 0.5, masks[1] > 0.5, masks[2] > 0.5
        items = [(hh, c) for hh in range(2) for c in range(cb)]
        def sl(a, hh, c):
            return a[c * CHUNK:(c + 1) * CHUNK, hh * HD:(hh + 1) * HD]
        qs = [sl(q2, hh, c) * (HD ** -0.5) for hh, c in items]
        ks = [sl(k2, hh, c) for hh, c in items]
        vs = [sl(v2, hh, c) for hh, c in items]
        ones = jnp.ones((CHUNK, CHUNK), F32)
        lane_sum = lambda m: _mm(m, ones, prec="bx3")
        row_sum = lambda m: _mm(ones, m, prec="ax3")
        g_rows = [jnp.broadcast_to(g[hh, c], (CHUNK, CHUNK)) for hh, c in items]
        b_rows = [jnp.broadcast_to(beta[hh, c], (CHUNK, CHUNK)) for hh, c in items]
        g_cols = [lane_sum(eye * gr) for gr in g_rows]
        b_cols = [lane_sum(eye * br) for br in b_rows]
        gc_cols = [lane_sum(jnp.where(incl, gr, 0.0)) for gr in g_rows]
        gc_rows = [row_sum(jnp.where(incl_t, gc, 0.0)) for gc in g_cols]
        g_tots = [lane_sum(gr) for gr in g_rows]
        decays = [jnp.where(incl, jnp.exp(jnp.where(incl, gcc - gcr, 0.0)), 0.0) for gcc, gcr in zip(gc_cols, gc_rows)]
        e_cols = [jnp.exp(gcc) for gcc in gc_cols]
        kbs = [kc * bc for kc, bc in zip(ks, b_cols)]
        rhss = [jnp.concatenate([vc * bc, kb * ec], axis=1) for vc, bc, kb, ec in zip(vs, b_cols, kbs, e_cols)]
        qgs = [qc * ec for qc, ec in zip(qs, e_cols)]
        kds = [kc * jnp.exp(gt - gcc) for kc, gt, gcc in zip(ks, g_tots, gc_cols)]
        egs = [jnp.exp(gt)[0:1, :] for gt in g_tots]
        lowers = tuple(jnp.where(strict, _mm(kb, kc, tb=True) * dec, 0.0) for kb, kc, dec in zip(kbs, ks, decays))
        ins = [jnp.where(incl, _mm(qc, kc, tb=True) * dec, 0.0) for qc, kc, dec in zip(qs, ks, decays)]
        if saved_inv is None:
            invs = _unit_triangular_inverses(lowers)
        else:
            invs = tuple(saved_inv[hh, c * CHUNK:(c + 1) * CHUNK] for hh, c in items)
        sols = _solve_with_inverses(lowers, tuple(rhss), invs)
        us, ws = [s[:, :HD] for s in sols], [s[:, HD:] for s in sols]

        def heads(xs, joiner):
            return jnp.stack([joiner(xs[:cb]), joiner(xs[cb:])], axis=0)
        cat = lambda xs: jnp.concatenate(xs, axis=0)
        outs = (heads(us, cat), heads(ws, cat), heads(qgs, cat), heads(kds, cat), heads(ins, cat),
                heads(egs, lambda xs: jnp.stack(xs, axis=0)))
        return outs if saved_inv is not None else outs + (heads(list(invs), cat),)

    xs = pl.BlockSpec((None, cb * CHUNK, 6 * HD), lambda b, p, i, d: (b, i, p))
    rs = pl.BlockSpec((None, None, 2, None, cb, 1, CHUNK), lambda b, p, i, d: (b, p, 0, d, i, 0, 0))
    ts = pl.BlockSpec((None, None, 2, None, cb * CHUNK, HD), lambda b, p, i, d: (b, p, 0, d, i, 0))
    ms = pl.BlockSpec((None, 3, CHUNK, CHUNK), lambda b, p, i, d: (d, 0, 0, 0))
    big = _sds((b_, npair, 2, 2, t_, HD), F32)
    op = _block_op("gdn_prep", f, (b_, npair, nc // cb, 2), [xs, rs, rs, ms],
                   [ts, ts, ts, ts, ts, rs, ts], [big, big, big, big, big, _sds(g.shape, F32), big],
                   [True, True, True, False], acc=['last', None, None, None], n_res=1, f_bwd=f)
    return op(qkv, g, beta, _gdn_masks())


def _gdn_scan_specs(t_, backward):
    seg_c = T_CTX // CHUNK
    nseg = t_ // T_CTX

    def seg_of(d, s):
        s = nseg - 1 - s if backward else s
        return jnp.where(d == 0, s, jnp.where(s == 0, 0, nseg - s))
    ts = pl.BlockSpec((None, GDN_H // 2, 2, None, T_CTX, HD), lambda b, d, s: (b, 0, 0, d, seg_of(d, s), 0))
    es = pl.BlockSpec((None, GDN_H // 2, 2, None, seg_c, 1, CHUNK), lambda b, d, s: (b, 0, 0, d, seg_of(d, s), 0, 0))
    return ts, es, seg_c, nseg


def _gdn_scan_call(u, w, qg, kd, intra, eg):
    b_, t_ = u.shape[0], u.shape[4]
    ts, es, seg_c, nseg = _gdn_scan_specs(t_, False)
    heads = [(p, hh) for p in range(GDN_H // 2) for hh in range(2)]

    def body(u_ref, w_ref, qg_ref, kd_ref, in_ref, eg_ref, o_ref, st_ref, state):
        @pl.when(pl.program_id(2) == 0)
        def _():
            state[...] = jnp.zeros_like(state)
        d = pl.program_id(1)
        for i in range(seg_c):
            c = jnp.where(d == 0, i, seg_c - 1 - i)
            rows = pl.ds(pl.multiple_of(c * CHUNK, CHUNK), CHUNK)
            sts = [state[n] for n in range(len(heads))]
            for n, (p, hh) in enumerate(heads):
                st_ref[p, hh, rows, :] = sts[n]
            sbs = [st.astype(BF16) for st in sts]
            ws = [_raw_dot(w_ref[p, hh, rows, :].astype(BF16), sbs[n], False, False) for n, (p, hh) in enumerate(heads)]
            qss = [_raw_dot(qg_ref[p, hh, rows, :].astype(BF16), sbs[n], False, False) for n, (p, hh) in enumerate(heads)]
            vbs = [(u_ref[p, hh, rows, :] - ws[n]).astype(BF16) for n, (p, hh) in enumerate(heads)]
            for n, (p, hh) in enumerate(heads):
                o_ref[p, hh, rows, :] = qss[n] + _raw_dot(in_ref[p, hh, rows, :].astype(BF16), vbs[n], False, False)
            kvs = [_raw_dot(kd_ref[p, hh, rows, :].astype(BF16), vbs[n], True, False) for n, (p, hh) in enumerate(heads)]
            for n, (p, hh) in enumerate(heads):
                e = eg_ref[p, hh, pl.ds(c, 1), :, :].reshape(1, CHUNK)
                state[n] = sts[n] * e + kvs[n]
    return pl.pallas_call(body, name="gdn_scan_fwd", grid=(b_, 2, nseg), in_specs=[ts] * 5 + [es], out_specs=[ts, ts],
                          out_shape=[_sds(u.shape, F32), _sds(u.shape, F32)],
                          scratch_shapes=[pltpu.VMEM((GDN_H, HD, HD), F32)],
                          compiler_params=_cparams(("arbitrary",) * 3))(u, w, qg, kd, intra, eg)


def _gdn_scan_bwd_call(u, w, qg, kd, intra, eg, states, do):
    b_, t_ = u.shape[0], u.shape[4]
    ts, es, seg_c, nseg = _gdn_scan_specs(t_, True)
    heads = [(p, hh) for p in range(GDN_H // 2) for hh in range(2)]

    def body(u_ref, w_ref, qg_ref, kd_ref, in_ref, eg_ref, st_ref, do_ref, du_ref, dw_ref, dqg_ref, dkd_ref, din_ref,
             deg_ref, dstate):
        @pl.when(pl.program_id(2) == 0)
        def _():
            dstate[...] = jnp.zeros_like(dstate)
        d = pl.program_id(1)
        hs = list(enumerate(heads))
        for i in range(seg_c):
            c = jnp.where(d == 0, seg_c - 1 - i, i)
            rows = pl.ds(pl.multiple_of(c * CHUNK, CHUNK), CHUNK)
            dss = [dstate[n] for n, _ in hs]
            sts = [st_ref[p, hh, rows, :] for _, (p, hh) in hs]
            sbs = [st.astype(BF16) for st in sts]
            dsbs = [ds.astype(BF16) for ds in dss]
            wbs = [w_ref[p, hh, rows, :].astype(BF16) for _, (p, hh) in hs]
            dobs = [do_ref[p, hh, rows, :].astype(BF16) for _, (p, hh) in hs]
            kdbs = [kd_ref[p, hh, rows, :].astype(BF16) for _, (p, hh) in hs]
            vbs = [(u_ref[p, hh, rows, :] - _raw_dot(wbs[n], sbs[n], False, False)).astype(BF16) for n, (p, hh) in hs]
            dvns = [_raw_dot(in_ref[p, hh, rows, :].astype(BF16), dobs[n], True, False)
                    + _raw_dot(kdbs[n], dsbs[n], False, False) for n, (p, hh) in hs]
            dvbs = [dvn.astype(BF16) for dvn in dvns]
            for n, (p, hh) in hs:
                din_ref[p, hh, rows, :] = _raw_dot(dobs[n], vbs[n], False, True)
                dqg_ref[p, hh, rows, :] = _raw_dot(dobs[n], sbs[n], False, True)
                dkd_ref[p, hh, rows, :] = _raw_dot(vbs[n], dsbs[n], False, True)
                du_ref[p, hh, rows, :] = dvns[n]
                dw_ref[p, hh, rows, :] = -_raw_dot(dvbs[n], sbs[n], False, True)
                deg_ref[p, hh, pl.ds(c, 1), :, :] = jnp.sum(sts[n] * dss[n], axis=0, keepdims=True).reshape(1, 1, CHUNK)
            upd = [_raw_dot(qg_ref[p, hh, rows, :].astype(BF16), dobs[n], True, False)
                   - _raw_dot(wbs[n], dvbs[n], True, False) for n, (p, hh) in hs]
            for n, (p, hh) in hs:
                e = eg_ref[p, hh, pl.ds(c, 1), :, :].reshape(1, CHUNK)
                dstate[n] = dss[n] * e + upd[n]
    big = _sds(u.shape, F32)
    return pl.pallas_call(body, name="gdn_scan_bwd", grid=(b_, 2, nseg), in_specs=[ts] * 5 + [es, ts, ts],
                          out_specs=[ts] * 5 + [es], out_shape=[big] * 5 + [_sds(eg.shape, F32)],
                          scratch_shapes=[pltpu.VMEM((GDN_H, HD, HD), F32)],
                          compiler_params=_cparams(("arbitrary",) * 3))(u, w, qg, kd, intra, eg, states, do)


@jax.custom_vjp
def _gdn_scan(u, w, qg, kd, intra, eg):
    return _gdn_scan_call(u, w, qg, kd, intra, eg)[0]


def _gdn_scan_f(u, w, qg, kd, intra, eg):
    o, states = _gdn_scan_call(u, w, qg, kd, intra, eg)
    return o, (u, w, qg, kd, intra, eg, states)


def _gdn_scan_b(res, do):
    return tuple(_gdn_scan_bwd_call(*res, do))


_gdn_scan.defvjp(_gdn_scan_f, _gdn_scan_b)


def _gdn_post(o, z, w_row):
    def f(o, z, w, seg):
        ms = _mm(o * o, seg, prec="bx") * (1.0 / HD)
        return (o * lax.rsqrt(ms + EPS) * w * _silu(z),)
    op = _block_op("gdn_post", f, _tok_grid(), [_tok(GDN_W), _tok(GDN_W), _row(GDN_W), _const2((GDN_W, GDN_W))],
                   [_tok(GDN_W)], [_sds(o.shape, BF16)], [True, True, True, False])
    return op(o, z, w_row, _seg_ones(GDN_W))[0]


def _s5_tables(ar, ai, rev):
    pr, pi = [ar], [ai]
    for _ in range(7):
        pr, pi = pr + [pr[-1] * ar - pi[-1] * ai], pi + [pr[-1] * ai + pi[-1] * ar]
    if rev:
        pr, pi = pr[::-1], pi[::-1]
    return jnp.concatenate(pr, axis=0), jnp.concatenate(pi, axis=0)


def _s5_scan_call(bu, ar, ai, direction, h=None):
    b_, t_, n2 = bu.shape
    nblk = n2 // (2 * S5_LB)
    with_grad = h is not None
    rev = (direction == 1) != with_grad
    tr, ti = _s5_tables(ar, ai, rev)
    tab = jnp.concatenate([tr.reshape(8, nblk, 1, S5_LB), ti.reshape(8, nblk, 1, S5_LB)], axis=2).reshape(8, n2)
    ntile, ntc = t_ // 8, T_CTX // 8

    def path(j):
        return j if direction == 0 else jnp.where(j < ntc, ntc - 1 - j, ntile + ntc - 1 - j)

    def scan_tile(xr, xi, tabr, tabi, cr, ci):
        row = lax.broadcasted_iota(jnp.int32, xr.shape, 0)
        for k in (1, 2, 4):
            idx = (8 - k) if rev else (k - 1)
            akr, aki = tabr[idx:idx + 1, :], tabi[idx:idx + 1, :]
            sh = (8 - k) if rev else k
            sr, si = pltpu.roll(xr, sh, 0), pltpu.roll(xi, sh, 0)
            ok = (row < 8 - k) if rev else (row >= k)
            xr, xi = (xr + jnp.where(ok, akr * sr - aki * si, 0.0), xi + jnp.where(ok, akr * si + aki * sr, 0.0))
        return xr + tabr * cr - tabi * ci, xi + tabr * ci + tabi * cr

    def body(*refs):
        if with_grad:
            bu_ref, tab_ref, h_ref, o_ref, da_ref = refs
        else:
            bu_ref, tab_ref, o_ref = refs
        tabr, tabi = tab_ref[:, :S5_LB], tab_ref[:, S5_LB:]
        zero = jnp.zeros((1, S5_LB), F32)
        row = lax.broadcasted_iota(jnp.int32, (8, S5_LB), 0)

        def step(i, carry):
            cr, ci = carry[0], carry[1]
            j = (ntile - 1 - i) if with_grad else i
            rows = pl.ds(pl.multiple_of(path(j) * 8, 8), 8)
            hr, hi = scan_tile(bu_ref[rows, :S5_LB], bu_ref[rows, S5_LB:], tabr, tabi, cr, ci)
            o_ref[rows, :S5_LB] = hr
            o_ref[rows, S5_LB:] = hi
            out = (hr[0:1, :], hi[0:1, :]) if rev else (hr[7:8, :], hi[7:8, :])
            if with_grad:
                prev = pl.ds(pl.multiple_of(path(jnp.maximum(j - 1, 0)) * 8, 8), 8)
                live = jnp.where(j > 0, 1.0, 0.0)
                sh, edge = (1, 0) if direction == 0 else (7, 7)
                pr = jnp.where(row == edge, pltpu.roll(h_ref[prev, :S5_LB], sh, 0) * live, pltpu.roll(h_ref[rows, :S5_LB], sh, 0))
                pi = jnp.where(row == edge, pltpu.roll(h_ref[prev, S5_LB:], sh, 0) * live, pltpu.roll(h_ref[rows, S5_LB:], sh, 0))
                out = out + (carry[2] + hr * pr + hi * pi, carry[3] + hi * pr - hr * pi)
            return out
        init = (zero, zero) + ((jnp.zeros((8, S5_LB), F32),) * 2 if with_grad else ())
        fin = lax.fori_loop(0, ntile, step, init)
        if with_grad:
            da_ref[:, :S5_LB] = fin[2]
            da_ref[:, S5_LB:] = fin[3]
    xs = pl.BlockSpec((None, t_, 2 * S5_LB), lambda b, j: (b, 0, j))
    tb = pl.BlockSpec((8, 2 * S5_LB), lambda b, j: (0, j))
    if with_grad:
        return pl.pallas_call(body, name="s5_scan_bwd%d" % direction, grid=(b_, nblk), in_specs=[xs, tb, xs],
                              out_specs=[xs, pl.BlockSpec((None, 8, 2 * S5_LB), lambda b, j: (b, 0, j))],
                              out_shape=[_sds(bu.shape, F32), _sds((b_, 8, n2), F32)],
                              compiler_params=_cparams(("arbitrary", "arbitrary")))(bu, tab, h)
    return pl.pallas_call(body, name="s5_scan_fwd%d" % direction, grid=(b_, nblk), in_specs=[xs, tb], out_specs=xs,
                          out_shape=_sds(bu.shape, F32), compiler_params=_cparams(("arbitrary", "arbitrary")))(bu, tab)


@functools.lru_cache(maxsize=None)
def _s5_scan_fn(direction):
    @jax.custom_vjp
    def scan(bu, ar, ai):
        return _s5_scan_call(bu, ar, ai, direction)

    def fwd(bu, ar, ai):
        h = _s5_scan_call(bu, ar, ai, direction)
        return h, (h, ar, ai)

    def bwd(res, dh):
        h, ar, ai = res
        lam, da = _s5_scan_call(dh, ar, -ai, direction, h=h)
        nblk = da.shape[-1] // (2 * S5_LB)
        da = jnp.sum(da, axis=(0, 1)).reshape(nblk, 2, S5_LB)
        return lam, da[:, 0].reshape(1, -1), da[:, 1].reshape(1, -1)

    scan.defvjp(fwd, bwd)
    return scan


def _s5_post(u, y0, y1, d_row, glu_w, glu_b_row):
    def f(u, y0, y1, d, gw, gb):
        zz = _gelu_tanh(d * u + y0 + y1)
        return (zz * _sigmoid(_mm(zz, gw) + gb),)
    op = _block_op("s5_post", f, _tok_grid(),
                   [_tok(S5_W), _tok(S5_W), _tok(S5_W), _row(S5_W), _const2((S5_W, S5_W)), _row(S5_W)],
                   [_tok(S5_W)], [_sds(u.shape, BF16)], [True] * 6, acc=[None, None, None, None, 'all', None])
    return op(u, y0, y1, d_row, glu_w, glu_b_row)[0]


def _loss_rows(x, target):
    def f(x, t):
        e = x - t
        return (jnp.sum(e * e, axis=0, keepdims=True),)
    grid = (B_LOC, T_LAT // TT)
    op = _block_op("loss_rows", f, grid, [_tok(D), _tok(D)], [_row(D)], [_sds((B_LOC, T_LAT // TT, 1, D), F32)],
                   [True, False])
    return op(x, target)[0]


def _adamw_call(name, w, g, m, v, pieces):
    r, c = w.shape
    tr = _pick(r, (256, 128, 64, 32, 16, 8))
    c1, c2 = 1.0 - ADAM_B1 ** ADAM_STEP, 1.0 - ADAM_B2 ** ADAM_STEP

    def body(w_ref, g_ref, m_ref, v_ref, go_ref, d_ref, mo_ref, vo_ref):
        if pieces:
            g = g_ref[0].astype(F32)
            for i in range(1, NDEV):
                g = g + g_ref[i].astype(F32)
        else:
            g = g_ref[...]
        m = ADAM_B1 * m_ref[...] + (1.0 - ADAM_B1) * g
        v = ADAM_B2 * v_ref[...] + (1.0 - ADAM_B2) * (g * g)
        go_ref[...] = g
        mo_ref[...] = m
        vo_ref[...] = v
        d_ref[...] = -ADAM_LR * ((m / c1) / (jnp.sqrt(v / c2) + ADAM_EPS) + ADAM_WD * w_ref[...])
    spec = pl.BlockSpec((tr, c), lambda i: (i, 0))
    gspec = pl.BlockSpec((NDEV, tr, c), lambda i: (0, i, 0)) if pieces else spec
    return pl.pallas_call(body, name=name, grid=(r // tr,), in_specs=[spec, gspec, spec, spec], out_specs=[spec] * 4,
                          out_shape=[_sds((r, c), F32)] * 4, compiler_params=_cparams(("arbitrary",)))(w, g, m, v)


def _sum_pieces(x, name):
    _, r, c = x.shape
    tr = _pick(r, (512, 256, 128, 64, 32, 16, 8))

    def body(x_ref, o_ref):
        acc = x_ref[0]
        for i in range(1, NDEV):
            acc = acc + x_ref[i]
        o_ref[...] = acc
    return pl.pallas_call(body, name=name, grid=(r // tr,), in_specs=[pl.BlockSpec((NDEV, tr, c), lambda i: (0, i, 0))],
                          out_specs=pl.BlockSpec((tr, c), lambda i: (i, 0)), out_shape=_sds((r, c), F32),
                          compiler_params=_cparams(("arbitrary",)))(x)


def _pack_flat(arrs, lanes, row_mult):
    flat = jnp.concatenate([a.reshape(-1).astype(F32) for a in arrs])
    n = flat.shape[0]
    rows = -(-n // lanes)
    rows = -(-rows // row_mult) * row_mult
    return jnp.pad(flat, (0, rows * lanes - n)).reshape(rows, lanes)


def _unpack_flat(packed, shapes, lead=()):
    flat = packed.reshape(lead + (-1,))
    out, off = [], 0
    for s in shapes:
        n = int(np.prod(s))
        out.append(flat[..., off:off + n].reshape(lead + tuple(s)))
        off += n
    return out


FS_ROWS = DFF // NDEV
WI_ROWS = IN_COLS // NDEV
WI_PAD = -(-WI_ROWS // 16) * 16
WO_ROWS = D // NDEV


def _pad_rows(blk, rows, axis):
    pad = [(0, 0)] * blk.ndim
    pad[axis] = (0, rows - blk.shape[axis])
    return jnp.pad(blk, pad)


def _pack_big_shards(t, dtype):
    parts = []
    for l in range(DEPTH):
        for n in ('ffn_w1', 'ffn_w3', 'ffn_w2'):
            for i in range(2):
                parts.append(t[n][l, i] if n == 'ffn_w2' else t[n][l, i].T)
        parts.append(_pad_rows(t['w_in'][l].T, WI_PAD, 0))
        parts.append(t['w_out'][l])
    return jnp.concatenate(parts, axis=0).astype(dtype)


def _unpack_big_shards(p):
    out = {n: [] for n in BIG}
    off = 0
    for l in range(DEPTH):
        for n in ('ffn_w1', 'ffn_w3', 'ffn_w2'):
            pair = []
            for i in range(2):
                blk = p[off:off + FS_ROWS]
                off += FS_ROWS
                pair.append(blk if n == 'ffn_w2' else blk.T)
            out[n].append(jnp.stack(pair))
        out['w_in'].append(p[off:off + WI_ROWS].T)
        off += WI_PAD
        out['w_out'].append(p[off:off + WO_ROWS])
        off += WO_ROWS
    return {n: jnp.stack(v) for n, v in out.items()}


def _full_from_gathered(g):
    layers, off = [], 0
    for l in range(DEPTH):
        lw = {}
        for n in ('ffn_w1', 'ffn_w3', 'ffn_w2'):
            pair = []
            for i in range(2):
                pair.append(g[:, off:off + FS_ROWS].reshape(DFF, D))
                off += FS_ROWS
            lw[n] = pair
        lw['w_in'] = g[:, off:off + WI_ROWS].reshape(IN_COLS, D)
        off += WI_PAD
        lw['w_out'] = g[:, off:off + WO_ROWS].reshape(D, D)
        off += WO_ROWS
        layers.append(lw)
    return layers


def _pieces_from_full(layers):
    parts = []
    for lw in layers:
        for n in ('ffn_w1', 'ffn_w3', 'ffn_w2'):
            for i in range(2):
                parts.append(lw[n][i].reshape(NDEV, FS_ROWS, D))
        parts.append(_pad_rows(lw['w_in'].reshape(NDEV, WI_ROWS, D), WI_PAD, 1))
        parts.append(lw['w_out'].reshape(NDEV, WO_ROWS, D))
    return jnp.concatenate(parts, axis=1)


def _flip_segments(a, axis):
    ctx, lat = lax.slice_in_dim(a, 0, T_CTX, axis=axis), lax.slice_in_dim(a, T_CTX, _ntok(), axis=axis)
    return jnp.concatenate([jnp.flip(ctx, axis), jnp.flip(lat, axis)], axis=axis)


def _rows_of(vec_ctx, vec_lat):
    w = vec_lat.shape[-1]
    ntc, ntl = T_CTX // TT, T_LAT // TT
    return jnp.concatenate([jnp.broadcast_to(vec_ctx[None, None, None, :], (B_LOC, ntc, 1, w)),
                            jnp.broadcast_to(vec_lat[:, None, None, :], (B_LOC, ntl, 1, w))], axis=1)


def _rows_const(vec):
    return jnp.broadcast_to(vec[None, None, None, :], (B_LOC, _ntok() // TT, 1, vec.shape[-1]))


def _ffn_sublayer(xt, mrow, w1, w3, w2, g, b):
    h = _modulate(xt, mrow[0], mrow[1]).reshape(-1, D)
    u = _ffn_up(h, w1, w3)
    y = _matmul("ffn_down", u, w2, F32).reshape(xt.shape)
    return _post_norm(xt, y, mrow[2], _rows_const(g), _rows_const(b), 0.5)


def _s5_discretize(lam_re, lam_im, log_dt, b_re, b_im):
    dt = jnp.exp(log_dt)[:, None]
    zr, zi = lam_re * dt, lam_im * dt
    er = jnp.exp(zr)
    lbr, lbi = er * jnp.cos(zi), er * jnp.sin(zi)
    dd = lam_re * lam_re + lam_im * lam_im
    qr = ((lbr - 1.0) * lam_re + lbi * lam_im) / dd
    qi = (lbi * lam_re - (lbr - 1.0) * lam_im) / dd
    bbr = qr[..., None] * b_re - qi[..., None] * b_im
    bbi = qr[..., None] * b_im + qi[..., None] * b_re
    return lbr, lbi, bbr, bbi


def _s5_cols(a):
    return a.reshape(a.shape[:-1] + (S5_N // S5_LB, S5_LB))


def _s5_group(su, p):
    b_, t_, _ = su.shape
    eye = jnp.eye(S5_G, dtype=F32)
    ys = []
    for d in range(2):
        lbr, lbi, bbr, bbi = _s5_discretize(p['s5_lam_re'][d], p['s5_lam_im'][d], p['s5_log_dt'][d],
                                            p['s5_b_re'][d], p['s5_b_im'][d])
        bre = jnp.einsum('gph,gk->ghkp', bbr, eye).reshape(S5_W, S5_N)
        bim = jnp.einsum('gph,gk->ghkp', bbi, eye).reshape(S5_W, S5_N)
        bmat = jnp.stack([_s5_cols(bre), _s5_cols(bim)], axis=2).reshape(S5_W, 2 * S5_N)
        cre = jnp.einsum('ghp,gk->kpgh', p['s5_c_re'][d], eye).reshape(S5_N, S5_W)
        cim = -jnp.einsum('ghp,gk->kpgh', p['s5_c_im'][d], eye).reshape(S5_N, S5_W)
        cmat = jnp.stack([cre.reshape(S5_N // S5_LB, S5_LB, S5_W), cim.reshape(S5_N // S5_LB, S5_LB, S5_W)],
                         axis=1).reshape(2 * S5_N, S5_W)
        bu = _matmul("s5_in", su.reshape(-1, S5_W), bmat, F32).reshape(b_, t_, 2 * S5_N)
        hs = _s5_scan_fn(d)(bu, lbr.reshape(1, S5_N), lbi.reshape(1, S5_N))
        ys.append(_matmul("s5_out", hs.reshape(-1, 2 * S5_N), cmat, F32).reshape(b_, t_, S5_W))
    return _s5_post(su, ys[0], ys[1], _rows_const(p['s5_d']), p['glu_w'], _rows_const(p['glu_b']))


def _gdn_group(qkv, z, ba, p):
    b_, t_, _ = qkv.shape
    nc = t_ // CHUNK
    qkvn = _gdn_pre(qkv, _pair_major(p['gdn_conv_w']))
    bg = _gdn_gates(ba, p['gdn_a_log'], p['gdn_dt_bias'])

    def gates(a):
        a = a.reshape(b_, t_, 2, GDN_H // 2, 2).transpose(0, 3, 4, 2, 1)
        return a.reshape(b_, GDN_H // 2, 2, 2, nc, 1, CHUNK)
    outs = _gdn_prep(qkvn, gates(bg[..., 12:24]), gates(bg[..., 0:12]))
    o = _gdn_scan(*outs).reshape(b_, GDN_H, 2, t_, HD)
    o = (o[:, :, 0] + o[:, :, 1]).transpose(0, 2, 1, 3).reshape(b_, t_, GDN_W)
    return _gdn_post(o, z, _rows_const(jnp.tile(p['gdn_norm_w'], GDN_H)))


def _att_group(aq, ak, av, p):
    b_, t_, _ = aq.shape
    qn = _att_pre(aq, _rows_const(jnp.tile(p['q_norm_w'], ATT_W // HD)), "att_pre_q")
    kn = _att_pre(ak, _rows_const(jnp.tile(p['k_norm_w'], ATT_KW // HD)), "att_pre_k")
    q = qn.reshape(b_, t_, ATT_HKV, ATT_G, HD).transpose(0, 2, 3, 1, 4)
    k = kn.reshape(b_, t_, ATT_HKV, HD).transpose(0, 2, 1, 3)
    v = av.reshape(b_, t_, ATT_HKV, HD).transpose(0, 2, 1, 3)
    o_lat = _attention(q[:, :, :, T_CTX:], k, v, "att_lat", ATT_TQ)
    o_ctx = _attention(q[:, :, :, :T_CTX], k[:, :, :T_CTX], v[:, :, :T_CTX], "att_ctx", T_CTX)
    o = jnp.concatenate([o_ctx, o_lat], axis=3)
    return o.transpose(0, 3, 1, 2, 4).reshape(b_, t_, ATT_W)


def _pair_major(w):
    lead = w.shape[:-1]
    return w.reshape(lead + (3, GDN_H // 2, 2 * HD)).swapaxes(-3, -2).reshape(lead + (GDN_QKV,))


def _permute_w_in(wt):
    qkv = wt[:GDN_QKV].reshape(3, GDN_H // 2, 2 * HD, D).swapaxes(0, 1).reshape(GDN_QKV, D)
    return jnp.concatenate([qkv, wt[GDN_QKV:1536], wt[1560:], wt[1536:1560],
                            jnp.zeros((IN_PAD - IN_COLS, D), wt.dtype)], axis=0)


PROJ_CUTS = (0, 1152, 1536, 1920, 2048, 2176, 2432, 2560)


@jax.custom_vjp
def _split_proj(proj):
    return tuple(proj[..., a:b] for a, b in zip(PROJ_CUTS[:-1], PROJ_CUTS[1:]))


_split_proj.defvjp(lambda proj: (_split_proj(proj), None), lambda _, d: (jnp.concatenate(d, axis=-1),))


def _mixer_sublayer(xt, mrow, lw, p, g, b):
    h = _modulate(xt, mrow[3], mrow[4]).reshape(-1, D)
    proj = _matmul_t("mix_in", h, _permute_w_in(lw['w_in']), F32).reshape(xt.shape[:2] + (IN_PAD,))
    qkv, z, aq, ak, av, su, ba = _split_proj(proj)
    o_gdn = _gdn_group(qkv, z, ba, p)
    o_att = _att_group(aq, ak, av, p)
    o_s5 = _s5_group(su, p)
    cat = jnp.concatenate([o_gdn, o_att.astype(BF16), o_s5], axis=-1).reshape(-1, D)
    y = _matmul("mix_out", cat, lw['w_out'], F32).reshape(xt.shape)
    return _post_norm(xt, y, mrow[5], _rows_const(g), _rows_const(b), 1.0)


def _local_loss(x, mod, modc, big, small, ctx, target):
    xt = jnp.concatenate([ctx, x], axis=1)
    for l in range(DEPTH):
        mrow = [_rows_of(modc[l, k * D:(k + 1) * D], mod[l, :, k * D:(k + 1) * D]) for k in range(N_MOD)]
        p = {n: small[n][l] for n in small}
        lw = big[l]
        xt = _ffn_sublayer(xt, mrow[0:3], lw['ffn_w1'][0], lw['ffn_w3'][0], lw['ffn_w2'][0], p['ln_g'][0], p['ln_b'][0])
        xt = _mixer_sublayer(xt, mrow, lw, p, p['ln_g'][1], p['ln_b'][1])
        xt = _ffn_sublayer(xt, mrow[6:9], lw['ffn_w1'][1], lw['ffn_w3'][1], lw['ffn_w2'][1], p['ln_g'][2], p['ln_b'][2])
    part = _loss_rows(xt[:, T_CTX:], target)
    return (0.5 / D) * jnp.sum(part)


def _silu_plain(x):
    return x * jax.nn.sigmoid(x)


def _dsilu_plain(x):
    s = jax.nn.sigmoid(x)
    return s * (1.0 + x * (1.0 - s))


def _small_shapes():
    return {'ln_g': (DEPTH, 3, D), 'ln_b': (DEPTH, 3, D), 'gdn_conv_w': (DEPTH, 5, GDN_QKV), 'glu_w': (DEPTH, S5_W, S5_W),
            'gdn_a_log': (DEPTH, 2, GDN_H), 'gdn_dt_bias': (DEPTH, 2, GDN_H), 'gdn_norm_w': (DEPTH, HD),
            'q_norm_w': (DEPTH, HD), 'k_norm_w': (DEPTH, HD), 's5_lam_re': (DEPTH, 2, S5_G, S5_P),
            's5_lam_im': (DEPTH, 2, S5_G, S5_P), 's5_log_dt': (DEPTH, 2, S5_G),
            's5_b_re': (DEPTH, 2, S5_G, S5_P, S5_H), 's5_b_im': (DEPTH, 2, S5_G, S5_P, S5_H),
            's5_c_re': (DEPTH, 2, S5_G, S5_H, S5_P), 's5_c_im': (DEPTH, 2, S5_G, S5_H, S5_P),
            's5_d': (DEPTH, S5_W), 'glu_b': (DEPTH, S5_W)}


def _gather_small_sharded(gathered, name):
    if name == 'glu_w':
        return gathered.transpose(1, 0, 2, 3).reshape(DEPTH, S5_W, S5_W)
    lead = gathered.shape[1:-1]
    return jnp.moveaxis(gathered, 0, -2).reshape(lead + (-1,))


def _my_small_shard(full, name, me):
    if name == 'glu_w':
        return lax.dynamic_slice_in_dim(full, me * (S5_W // NDEV), S5_W // NDEV, axis=1)
    n = full.shape[-1] // NDEV
    return lax.dynamic_slice_in_dim(full, me * n, n, axis=full.ndim - 1)


def kernel(x, c, ctx, c_ctx, w_ada, b_ada, ln_g, ln_b, ffn_w1, ffn_w3, ffn_w2, w_in, w_out, gdn_conv_w, gdn_a_log, gdn_dt_bias, gdn_norm_w, q_norm_w, k_norm_w, s5_lam_re, s5_lam_im, s5_log_dt, s5_b_re, s5_b_im, s5_c_re, s5_c_im, s5_d, glu_w, glu_b, loss_target, m_c_ctx, m_w_ada, m_b_ada, m_ln_g, m_ln_b, m_ffn_w1, m_ffn_w3, m_ffn_w2, m_w_in, m_w_out, m_gdn_conv_w, m_gdn_a_log, m_gdn_dt_bias, m_gdn_norm_w, m_q_norm_w, m_k_norm_w, m_s5_lam_re, m_s5_lam_im, m_s5_log_dt, m_s5_b_re, m_s5_b_im, m_s5_c_re, m_s5_c_im, m_s5_d, m_glu_w, m_glu_b, v_c_ctx, v_w_ada, v_b_ada, v_ln_g, v_ln_b, v_ffn_w1, v_ffn_w3, v_ffn_w2, v_w_in, v_w_out, v_gdn_conv_w, v_gdn_a_log, v_gdn_dt_bias, v_gdn_norm_w, v_q_norm_w, v_k_norm_w, v_s5_lam_re, v_s5_lam_im, v_s5_log_dt, v_s5_b_re, v_s5_b_im, v_s5_c_re, v_s5_c_im, v_s5_d, v_glu_w, v_glu_b):
    a = dict(locals())
    me = _my_index()
    ada_cols = N_MOD * D // NDEV

    sc = _silu_plain(a['c'])
    scc = _silu_plain(a['c_ctx'])
    small_in = [sc] + [a[n] for n in SMALL_SHARDED]
    got = _all_gather(_pack_flat(small_in, 128, 8), "gather_small")
    parts = _unpack_flat(got, [t.shape for t in small_in], lead=(NDEV,))
    sc_all = parts[0].reshape(NDEV * B_LOC, D)
    small = {n: _gather_small_sharded(parts[1 + i], n) for i, n in enumerate(SMALL_SHARDED)}
    for n in SMALL_REPL:
        small[n] = a[n]

    nb = NDEV * B_LOC
    rows_pad = 8
    sc_rows = jnp.concatenate([sc_all, scc[None], jnp.zeros((rows_pad - 1, D), F32)], axis=0)
    mod_part = jnp.stack([_mm_call("ada_fwd", sc_rows, a['w_ada'][l], "nn", F32) for l in range(DEPTH)])
    mod_all = _all_gather(mod_part.reshape(DEPTH * (nb + rows_pad), ada_cols), "gather_mod")
    mod_all = mod_all.reshape(NDEV, DEPTH, nb + rows_pad, ada_cols).transpose(1, 2, 0, 3).reshape(DEPTH, nb + rows_pad, N_MOD * D)
    mod_all = mod_all + a['b_ada'][:, None, :]
    mod = lax.dynamic_slice_in_dim(mod_all, me * B_LOC, B_LOC, axis=1)
    modc = mod_all[:, nb]

    big = _full_from_gathered(_all_gather(_pack_big_shards(a, BF16), "gather_weights"))

    loss_part, grads = jax.value_and_grad(_local_loss, argnums=(0, 1, 2, 3, 4))(
        a['x'], mod, modc, big, small, a['ctx'], a['loss_target'])
    gx, gmod, gmodc, gbig, gsmall = grads

    gm_rows = jnp.concatenate([gmod, gmodc[:, None], jnp.zeros((DEPTH, rows_pad - B_LOC - 1, N_MOD * D), F32)], axis=1)
    gm_all = _all_gather(gm_rows.reshape(DEPTH * rows_pad, N_MOD * D), "gather_dmod").reshape(NDEV, DEPTH, rows_pad, N_MOD * D)
    gm_all = gm_all.transpose(1, 0, 2, 3).reshape(DEPTH, NDEV * rows_pad, N_MOD * D)
    g_b_ada = jnp.sum(gm_all, axis=1)
    sc_dev = jnp.concatenate([sc_all.reshape(NDEV, B_LOC, D), jnp.broadcast_to(scc[None, None], (NDEV, 1, D)),
                              jnp.zeros((NDEV, rows_pad - B_LOC - 1, D), F32)], axis=1).reshape(NDEV * rows_pad, D)
    gm_mine = lax.dynamic_slice_in_dim(gm_all, me * ada_cols, ada_cols, axis=2)
    g_w_ada = jnp.stack([_mm_call("ada_dw", sc_dev, gm_mine[l], "tn", F32) for l in range(DEPTH)])
    gmc = gm_mine.reshape(DEPTH, NDEV, rows_pad, ada_cols)[:, :, B_LOC].sum(axis=1)
    gmc = jnp.concatenate([gmc[:, None], jnp.zeros((DEPTH, 7, ada_cols), F32)], axis=1)
    dscc_part = sum(_mm_call("ada_dx", gmc[l], a['w_ada'][l], "nt", F32)[0] for l in range(DEPTH))

    small_names = SMALL_SHARDED + SMALL_REPL
    sums_in = [loss_part.reshape(1), dscc_part] + [gsmall[n] for n in small_names]
    tot = _sum_pieces(_all_gather(_pack_flat(sums_in, 128, 512), "gather_sums"), "sum_small")
    tparts = _unpack_flat(tot, [t.shape for t in sums_in])
    loss = tparts[0].reshape(())
    g_c_ctx = tparts[1] * _dsilu_plain(a['c_ctx'])
    g = {'c_ctx': g_c_ctx, 'b_ada': g_b_ada, 'w_ada': g_w_ada}
    for i, n in enumerate(small_names):
        g[n] = _my_small_shard(tparts[2 + i], n, me) if n in SMALL_SHARDED else tparts[2 + i]

    recv = _all_to_all(_pieces_from_full(gbig), "scatter_grads")
    gb, db, mb, vb = _adamw_call("adamw_big", _pack_big_shards(a, F32), recv,
                                 _pack_big_shards({n: a['m_' + n] for n in BIG}, F32),
                                 _pack_big_shards({n: a['v_' + n] for n in BIG}, F32), True)
    res = {'g': {}, 'd': {}, 'm': {}, 'v': {}}
    for key, packed in (('g', gb), ('d', db), ('m', mb), ('v', vb)):
        res[key].update(_unpack_big_shards(packed))

    shp = a['w_ada'].shape
    flat2 = lambda t: t.reshape(-1, shp[-1])
    ga, da, ma, va = _adamw_call("adamw_ada", flat2(a['w_ada']), flat2(g['w_ada']), flat2(a['m_w_ada']), flat2(a['v_w_ada']), False)
    for key, val in (('g', ga), ('d', da), ('m', ma), ('v', va)):
        res[key]['w_ada'] = val.reshape(shp)
    rest = [n for n in WEIGHTS if n not in BIG and n != 'w_ada']
    shapes = [a[n].shape for n in rest]
    pk = lambda d: _pack_flat([d[n] for n in rest], 128, 256)
    outs = _adamw_call("adamw_small", pk(a), pk(g), pk({n: a['m_' + n] for n in rest}), pk({n: a['v_' + n] for n in rest}), False)
    for key, val in zip(('g', 'd', 'm', 'v'), outs):
        for n, t in zip(rest, _unpack_flat(val, shapes)):
            res[key][n] = t

    out = [loss, gx]
    for key in ('g', 'd', 'm', 'v'):
        out += [res[key][n] for n in WEIGHTS]
    return tuple(out)
```

```python
import functools
import math

import numpy as np
import jax
import jax.numpy as jnp
from jax import lax
from jax.experimental import pallas as pl
from jax.experimental.pallas import tpu as pltpu

F32 = jnp.float32
BF16 = jnp.bfloat16
MESH = pl.DeviceIdType.MESH

NDEV = 8
D = 1024
DFF = 2816
DEPTH = 4
B_LOC = 4
T_CTX = 256
T_LAT = 2048
GRID_W = 64
N_MOD = 9
GDN_H = 6
HD = 64
GDN_QKV = 3 * GDN_H * HD
GDN_W = GDN_H * HD
ATT_HKV = 2
ATT_G = 3
ATT_W = ATT_HKV * ATT_G * HD
ATT_KW = ATT_HKV * HD
S5_G = 16
S5_H = 16
S5_P = 64
S5_W = S5_G * S5_H
S5_N = S5_G * S5_P
IN_COLS = 2456
IN_PAD = 2560
ROPE_THETA = 10000.0
ROPE_PAIRS = 16
ALPHA = (2.0 * 4) ** 0.25
EPS = 1e-6
CHUNK = 64
ADAM_LR, ADAM_B1, ADAM_B2, ADAM_EPS, ADAM_WD, ADAM_STEP = 0.001, 0.9, 0.999, 1e-08, 0.01, 10

TT = 256
ATT_TQ = 256
S5_LB = 256
VMEM_LIMIT = 56 * 1024 * 1024

WEIGHTS = ['c_ctx', 'w_ada', 'b_ada', 'ln_g', 'ln_b', 'ffn_w1', 'ffn_w3', 'ffn_w2', 'w_in', 'w_out', 'gdn_conv_w',
           'gdn_a_log', 'gdn_dt_bias', 'gdn_norm_w', 'q_norm_w', 'k_norm_w', 's5_lam_re', 's5_lam_im', 's5_log_dt',
           's5_b_re', 's5_b_im', 's5_c_re', 's5_c_im', 's5_d', 'glu_w', 'glu_b']
INPUTS = ['x', 'c', 'ctx'] + WEIGHTS + ['loss_target'] + ['m_' + n for n in WEIGHTS] + ['v_' + n for n in WEIGHTS]
BIG = ['ffn_w1', 'ffn_w3', 'ffn_w2', 'w_in', 'w_out']
SMALL_SHARDED = ['ln_g', 'ln_b', 'gdn_conv_w', 'glu_w']
SMALL_REPL = ['gdn_a_log', 'gdn_dt_bias', 'gdn_norm_w', 'q_norm_w', 'k_norm_w', 's5_lam_re', 's5_lam_im',
              's5_log_dt', 's5_b_re', 's5_b_im', 's5_c_re', 's5_c_im', 's5_d', 'glu_b']


def _cparams(sem=None):
    return pltpu.CompilerParams(dimension_semantics=sem, vmem_limit_bytes=VMEM_LIMIT)


def _ntok():
    return T_CTX + T_LAT


def _my_pos():
    return lax.axis_index("x"), lax.axis_index("y"), lax.axis_index("c")


def _my_index():
    x, y, c = _my_pos()
    return 4 * x + 2 * y + c


def _all_gather(shard, name):
    def body(x_ref, out_ref, send_sems, recv_sems, local_sem):
        x, y, c = _my_pos()
        me, sibling = (x, y, c), (x, y, 1 - c)
        chips = [(1 - x, y), (x, 1 - y), (1 - x, 1 - y)]

        def slab(px, py, pc):
            return out_ref.at[4 * px + 2 * py + pc]

        def copy(k, block, to, src=None):
            return pltpu.make_async_remote_copy(
                src_ref=slab(*block) if src is None else src, dst_ref=slab(*block),
                send_sem=send_sems.at[k], recv_sem=recv_sems.at[k], device_id=to, device_id_type=MESH)

        mine = pltpu.make_async_copy(x_ref, slab(*me), local_sem)
        mine.start()
        first = [copy(0, me, sibling, src=x_ref)]
        first += [copy(1 + j, me, (*chip, c), src=x_ref) for j, chip in enumerate(chips)]
        for cp in first:
            cp.start()
        passed = [copy(4 + j, (*chip, c), sibling) for j, chip in enumerate(chips)]
        for j, chip in enumerate(chips):
            copy(1 + j, (*chip, c), me).wait_recv()
            passed[j].start()
        copy(0, sibling, me).wait_recv()
        for j, chip in enumerate(chips):
            copy(4 + j, (*chip, 1 - c), me).wait_recv()
        for cp in first + passed:
            cp.wait_send()
        mine.wait()

    return pl.pallas_call(
        body, name=name,
        out_shape=jax.ShapeDtypeStruct((NDEV,) + shard.shape, shard.dtype),
        in_specs=[pl.BlockSpec(memory_space=pl.ANY)],
        out_specs=pl.BlockSpec(memory_space=pl.ANY),
        scratch_shapes=[pltpu.SemaphoreType.DMA((7,)), pltpu.SemaphoreType.DMA((7,)), pltpu.SemaphoreType.DMA],
    )(shard)


def _all_to_all(pieces, name):
    def body(x_ref, out_ref, send_sems, recv_sems, local_sem):
        x, y, c = _my_pos()
        me_i = 4 * x + 2 * y + c
        mine = pltpu.make_async_copy(x_ref.at[me_i], out_ref.at[me_i], local_sem)
        mine.start()
        sends, recvs = [], []
        for k in range(1, NDEV):
            px = 1 - x if (k >> 2) & 1 else x
            py = 1 - y if (k >> 1) & 1 else y
            pc = 1 - c if k & 1 else c
            peer_i = 4 * px + 2 * py + pc
            sends.append(pltpu.make_async_remote_copy(
                src_ref=x_ref.at[peer_i], dst_ref=out_ref.at[me_i], send_sem=send_sems.at[k - 1],
                recv_sem=recv_sems.at[k - 1], device_id=(px, py, pc), device_id_type=MESH))
            recvs.append(pltpu.make_async_remote_copy(
                src_ref=x_ref.at[peer_i], dst_ref=out_ref.at[peer_i], send_sem=send_sems.at[k - 1],
                recv_sem=recv_sems.at[k - 1], device_id=(px, py, pc), device_id_type=MESH))
        for cp in sends:
            cp.start()
        for cp in recvs:
            cp.wait_recv()
        for cp in sends:
            cp.wait_send()
        mine.wait()

    return pl.pallas_call(
        body, name=name,
        out_shape=jax.ShapeDtypeStruct(pieces.shape, pieces.dtype),
        in_specs=[pl.BlockSpec(memory_space=pl.ANY)],
        out_specs=pl.BlockSpec(memory_space=pl.ANY),
        scratch_shapes=[pltpu.SemaphoreType.DMA((7,)), pltpu.SemaphoreType.DMA((7,)), pltpu.SemaphoreType.DMA],
    )(pieces)


def _scatter_pair(pieces, name):
    def body(x_ref, out_ref, send_sems, recv_sems):
        x, y, c = _my_pos()
        copies = [pltpu.make_async_remote_copy(
            src_ref=x_ref.at[2 * q + (1 - c)], dst_ref=out_ref.at[q], send_sem=send_sems.at[q],
            recv_sem=recv_sems.at[q], device_id=(x, y, 1 - c), device_id_type=MESH) for q in range(4)]
        for cp in copies:
            cp.start()
        for cp in copies:
            cp.wait_recv()
        for cp in copies:
            cp.wait_send()

    return pl.pallas_call(
        body, name=name, out_shape=jax.ShapeDtypeStruct((4,) + pieces.shape[1:], pieces.dtype),
        in_specs=[pl.BlockSpec(memory_space=pl.ANY)], out_specs=pl.BlockSpec(memory_space=pl.ANY),
        scratch_shapes=[pltpu.SemaphoreType.DMA((4,)), pltpu.SemaphoreType.DMA((4,))],
    )(pieces)


def _pair_sum(pieces, from_sibling, name):
    _, r, c_ = pieces.shape
    tr = _pick(r, (512, 256, 128, 64, 32, 16))

    def body(p_ref, s_ref, o_ref):
        c = lax.axis_index("c")
        o_ref[...] = (p_ref[c].astype(F32) + s_ref[...].astype(F32)).astype(o_ref.dtype)
    return pl.pallas_call(
        body, name=name, grid=(4, r // tr),
        in_specs=[pl.BlockSpec((None, 2, tr, c_), lambda q, i: (q, 0, i, 0)), pl.BlockSpec((None, tr, c_), lambda q, i: (q, i, 0))],
        out_specs=pl.BlockSpec((None, tr, c_), lambda q, i: (q, i, 0)), out_shape=_sds((4, r, c_), pieces.dtype),
        compiler_params=_cparams(("arbitrary",) * 2))(pieces.reshape(4, 2, r, c_), from_sibling)


def _scatter_chip(sums, name):
    def body(x_ref, out_ref, send_sems, recv_sems, local_sem):
        x, y, c = _my_pos()
        myq = 2 * x + y
        mine = pltpu.make_async_copy(x_ref.at[myq], out_ref.at[myq], local_sem)
        mine.start()
        sends, recvs = [], []
        for j, (qx, qy) in enumerate([(1 - x, y), (x, 1 - y), (1 - x, 1 - y)]):
            q = 2 * qx + qy
            sends.append(pltpu.make_async_remote_copy(
                src_ref=x_ref.at[q], dst_ref=out_ref.at[myq], send_sem=send_sems.at[j], recv_sem=recv_sems.at[j],
                device_id=(qx, qy, c), device_id_type=MESH))
            recvs.append(pltpu.make_async_remote_copy(
                src_ref=x_ref.at[q], dst_ref=out_ref.at[q], send_sem=send_sems.at[j], recv_sem=recv_sems.at[j],
                device_id=(qx, qy, c), device_id_type=MESH))
        for cp in sends:
            cp.start()
        for cp in recvs:
            cp.wait_recv()
        for cp in sends:
            cp.wait_send()
        mine.wait()

    return pl.pallas_call(
        body, name=name, out_shape=jax.ShapeDtypeStruct(sums.shape, sums.dtype),
        in_specs=[pl.BlockSpec(memory_space=pl.ANY)], out_specs=pl.BlockSpec(memory_space=pl.ANY),
        scratch_shapes=[pltpu.SemaphoreType.DMA((3,)), pltpu.SemaphoreType.DMA((3,)), pltpu.SemaphoreType.DMA],
    )(sums)


def _dims(ta, tb):
    return (((0 if ta else 1,), (1 if tb else 0,)), ((), ()))


def _raw_dot(a, b, ta, tb):
    return lax.dot_general(a, b, _dims(ta, tb), preferred_element_type=F32)


def _split2(x):
    hi = x.astype(BF16)
    return hi, (x - hi.astype(F32)).astype(BF16)


def _split3(x):
    hi = x.astype(BF16)
    r = x - hi.astype(F32)
    mid = r.astype(BF16)
    return hi, mid, (r - mid.astype(F32)).astype(BF16)


def _dot_impl(a, b, ta, tb, prec):
    if prec == "bf16":
        return _raw_dot(a.astype(BF16), b.astype(BF16), ta, tb)
    if prec == "bx3":
        bb = b.astype(BF16)
        h, m, l = _split3(a)
        return _raw_dot(h, bb, ta, tb) + (_raw_dot(m, bb, ta, tb) + _raw_dot(l, bb, ta, tb))
    if prec == "ax3":
        ab = a.astype(BF16)
        h, m, l = _split3(b)
        return _raw_dot(ab, h, ta, tb) + (_raw_dot(ab, m, ta, tb) + _raw_dot(ab, l, ta, tb))
    ka, kb = (0 if ta else 1), (1 if tb else 0)
    ah, al = _split2(a)
    if prec == "bx":
        bb = b.astype(BF16)
        return _raw_dot(jnp.concatenate([ah, al], axis=ka), jnp.concatenate([bb, bb], axis=kb), ta, tb)
    bh, bl = _split2(b)
    return _raw_dot(ah, bh, ta, tb) + (_raw_dot(ah, bl, ta, tb) + _raw_dot(al, bh, ta, tb))


@functools.lru_cache(maxsize=None)
def _mm_fn(ta, tb, prec):
    @jax.custom_vjp
    def mm(a, b):
        return _dot_impl(a, b, ta, tb, prec)

    def fwd(a, b):
        return mm(a, b), (a, b)

    def bwd(res, dc):
        a, b = res
        bprec = "f32" if prec == "f32" else "bf16"
        if prec in ("bx", "bx3"):
            assert not ta
            return _mm_fn(False, not tb, prec)(dc, b).astype(a.dtype), jnp.zeros_like(b)
        if prec == "ax3":
            assert not tb
            return jnp.zeros_like(a), _mm_fn(not ta, False, prec)(a, dc).astype(b.dtype)
        da = _mm_fn(tb, True, bprec)(b, dc) if ta else _mm_fn(False, not tb, bprec)(dc, b)
        db = _mm_fn(True, ta, bprec)(dc, a) if tb else _mm_fn(not ta, False, bprec)(a, dc)
        return da.astype(a.dtype), db.astype(b.dtype)

    mm.defvjp(fwd, bwd)
    return mm


def _mm(a, b, ta=False, tb=False, prec="bf16"):
    return _mm_fn(ta, tb, prec)(a, b)


@functools.lru_cache(maxsize=None)
def _shift_fn(k):
    @jax.custom_vjp
    def shift(x):
        n = x.shape[0]
        r = pltpu.roll(x, (-k) % n, 0)
        t = lax.broadcasted_iota(jnp.int32, x.shape, 0)
        ok = (t + k >= 0) & (t + k < n)
        return jnp.where(ok, r, 0.0)

    shift.defvjp(lambda x: (shift(x), None), lambda _, dy: (_shift_fn(-k)(dy),))
    return shift


def _sigmoid(x):
    return 1.0 / (1.0 + jnp.exp(-x))


@jax.custom_vjp
def _softplus(x):
    y = jnp.exp(-jnp.abs(x))
    u = 1.0 + y
    l1p = jnp.where(u == 1.0, y, jnp.log(u) * y / jnp.where(u == 1.0, 1.0, u - 1.0))
    return jnp.maximum(x, 0.0) + l1p


_softplus.defvjp(lambda x: (_softplus(x), x), lambda x, dy: (dy * _sigmoid(x),))


def _silu(x):
    return x * _sigmoid(x)


def _gelu_tanh(x):
    return 0.5 * x * (1.0 + jnp.tanh(math.sqrt(2.0 / math.pi) * (x + 0.044715 * (x * x * x))))


def _block_op(name, f, grid, in_specs, out_specs, out_shapes, diff, acc=None, n_res=0, f_bwd=None):
    n_in, n_all = len(in_specs), len(out_specs)
    n_out = n_all - n_res
    acc = acc or [None] * n_in
    didx = [i for i in range(n_in) if diff[i]]
    sem = ("arbitrary",) * len(grid)
    fb = f_bwd or f

    def run_fwd(*xs):
        def body(*refs):
            outs = f(*[r[...] for r in refs[:n_in]])
            for r, o in zip(refs[n_in:], outs):
                r[...] = o.astype(r.dtype)
        return pl.pallas_call(body, name=name + "_fwd", grid=grid, in_specs=in_specs, out_specs=out_specs,
                              out_shape=out_shapes, compiler_params=_cparams(sem))(*xs)

    def run_bwd(xs, res, douts):
        def body(*refs):
            ins = [r[...] for r in refs[:n_in]]
            ress = [r[...] for r in refs[n_in:n_in + n_res]]
            dos = [r[...] for r in refs[n_in + n_res:n_in + n_all]]

            def g(*dv):
                full = list(ins)
                for i, v in zip(didx, dv):
                    full[i] = v
                return tuple(fb(*full, *ress))

            outs, vjp = jax.vjp(g, *[ins[i] for i in didx])
            dins = vjp(tuple(d.astype(o.dtype) for d, o in zip(dos, outs)))
            for r, i, dv in zip(refs[n_in + n_all:], didx, dins):
                dv = dv.astype(r.dtype)
                if acc[i] is None:
                    r[...] = dv
                else:
                    if acc[i] == 'last':
                        first = pl.program_id(len(grid) - 1) == 0
                    else:
                        first = functools.reduce(jnp.logical_and, [pl.program_id(a) == 0 for a in range(len(grid))])

                    @pl.when(first)
                    def _(r=r, dv=dv):
                        r[...] = dv

                    @pl.when(jnp.logical_not(first))
                    def _(r=r, dv=dv):
                        r[...] += dv
        return pl.pallas_call(
            body, name=name + "_bwd", grid=grid,
            in_specs=list(in_specs) + list(out_specs[n_out:]) + list(out_specs[:n_out]),
            out_specs=[in_specs[i] for i in didx],
            out_shape=[jax.ShapeDtypeStruct(xs[i].shape, xs[i].dtype) for i in didx],
            compiler_params=_cparams(sem))(*xs, *res, *douts)

    @jax.custom_vjp
    def op(*xs):
        return tuple(run_fwd(*xs))[:n_out]

    def op_fwd(*xs):
        outs = tuple(run_fwd(*xs))
        return outs[:n_out], (xs, outs[n_out:])

    def op_bwd(saved, douts):
        xs, res = saved
        dins = run_bwd(xs, res, douts)
        full = [jnp.zeros_like(x) for x in xs]
        for i, dv in zip(didx, dins):
            full[i] = dv
        return tuple(full)

    op.defvjp(op_fwd, op_bwd)
    op.run_bwd = run_bwd
    return op


def _sds(shape, dtype):
    return jax.ShapeDtypeStruct(tuple(shape), dtype)


def _pick(n, cands):
    for c in cands:
        if n % c == 0:
            return c
    return n


def _mm_call(name, a, b, mode, out_dtype):
    if mode == "tn":
        m, k = a.shape
        n = b.shape[1]
        tm = _pick(m, (512, 256, 128, 64))
        tn = _pick(n, (1408, 1280, 1152, 1024, 512, 256, 128))
        steps = m // tm

        def body(a_ref, b_ref, o_ref, acc_ref):
            i = pl.program_id(1)

            @pl.when(i == 0)
            def _():
                acc_ref[...] = jnp.zeros_like(acc_ref)

            acc_ref[...] += _raw_dot(a_ref[...].astype(BF16), b_ref[...].astype(BF16), True, False)

            @pl.when(i == steps - 1)
            def _():
                o_ref[...] = acc_ref[...].astype(o_ref.dtype)

        return pl.pallas_call(
            body, name=name, grid=(n // tn, steps),
            in_specs=[pl.BlockSpec((tm, k), lambda j, i: (i, 0)), pl.BlockSpec((tm, tn), lambda j, i: (i, j))],
            out_specs=pl.BlockSpec((k, tn), lambda j, i: (0, j)),
            out_shape=_sds((k, n), out_dtype),
            scratch_shapes=[pltpu.VMEM((k, tn), F32)],
            compiler_params=_cparams(("arbitrary", "arbitrary")))(a, b)

    m, k = a.shape
    n = b.shape[1] if mode == "nn" else b.shape[0]
    tm = _pick(m, (512, 256, 128, 64))
    tn = _pick(n, (1408, 1280, 1152, 1024, 512, 256, 128))

    def body(a_ref, b_ref, o_ref):
        o_ref[...] = _raw_dot(a_ref[...].astype(BF16), b_ref[...].astype(BF16), False, mode == "nt").astype(o_ref.dtype)

    b_spec = pl.BlockSpec((k, tn), lambda j, i: (0, j)) if mode == "nn" else pl.BlockSpec((tn, k), lambda j, i: (j, 0))
    return pl.pallas_call(
        body, name=name, grid=(n // tn, m // tm),
        in_specs=[pl.BlockSpec((tm, k), lambda j, i: (i, 0)), b_spec],
        out_specs=pl.BlockSpec((tm, tn), lambda j, i: (i, j)),
        out_shape=_sds((m, n), out_dtype),
        compiler_params=_cparams(("arbitrary", "arbitrary")))(a, b)


def _matmul(name, a, w, out_dtype):
    @jax.custom_vjp
    def mm(a, w):
        return _mm_call(name + "_nn", a, w, "nn", out_dtype)

    def fwd(a, w):
        return mm(a, w), (a, w)

    def bwd(res, dy):
        a, w = res
        return (_mm_call(name + "_nt", dy, w, "nt", a.dtype), _mm_call(name + "_tn", a, dy, "tn", w.dtype))

    mm.defvjp(fwd, bwd)
    return mm(a, w)


def _matmul_t(name, a, wt, out_dtype):
    @jax.custom_vjp
    def mm(a, wt):
        return _mm_call(name + "_nt", a, wt, "nt", out_dtype)

    def fwd(a, wt):
        return mm(a, wt), (a, wt)

    def bwd(res, dy):
        a, wt = res
        return (_mm_call(name + "_nn", dy, wt, "nn", a.dtype), _mm_call(name + "_tn", dy, a, "tn", wt.dtype))

    mm.defvjp(fwd, bwd)
    return mm(a, wt)


def _tok(width):
    return pl.BlockSpec((None, TT, width), lambda b, t: (b, t, 0))


def _row(width):
    return pl.BlockSpec((None, None, 1, width), lambda b, t: (b, t, 0, 0))


def _const2(shape):
    return pl.BlockSpec(shape, lambda b, t: (0,) * len(shape))


def _tok_grid():
    return (B_LOC, _ntok() // TT)


def _modulate(x, shift, scale):
    def f(x, sh, sc):
        return ((x * (1.0 + sc) + sh),)
    op = _block_op("modulate", f, _tok_grid(), [_tok(D), _row(D), _row(D)], [_tok(D)],
                   [_sds(x.shape, BF16)], [True, True, True])
    return op(x, shift, scale)[0]


def _post_norm(x, y, gate, g, b, res_w):
    def f(x, y, gate, g, b):
        z = ALPHA * x + res_w * gate * y
        mu = jnp.mean(z, axis=-1, keepdims=True)
        zc = z - mu
        var = jnp.mean(zc * zc, axis=-1, keepdims=True)
        return (zc * lax.rsqrt(var + EPS) * g + b,)
    op = _block_op("post_norm", f, _tok_grid(), [_tok(D), _tok(D), _row(D), _row(D), _row(D)], [_tok(D)],
                   [_sds(x.shape, F32)], [True] * 5)
    return op(x, y, gate, g, b)[0]


def _swiglu_gate_op(m):
    tm = _pick(m, (256, 128, 64))

    def f(a, b):
        a = a.astype(F32)
        return (_silu(a) * b.astype(F32),)
    spec = pl.BlockSpec((tm, DFF), lambda i: (i, 0))
    return _block_op("swiglu_gate", f, (m // tm,), [spec, spec], [spec], [_sds((m, DFF), BF16)], [True, True])


def _ffn_up_call(h, w1t, w3t):
    m, k = h.shape
    n = w1t.shape[0]
    tm, tn = _pick(m, (512, 256, 128, 64)), _pick(n, (1408, 1024, 512, 256, 128))

    def body(h_ref, w1_ref, w3_ref, a_ref, b_ref, u_ref):
        hb = h_ref[...]
        a = _raw_dot(hb, w1_ref[...], False, True).astype(BF16)
        b = _raw_dot(hb, w3_ref[...], False, True).astype(BF16)
        a_ref[...] = a
        b_ref[...] = b
        u_ref[...] = (_silu(a.astype(F32)) * b.astype(F32)).astype(BF16)
    ws = pl.BlockSpec((tn, k), lambda j, i: (j, 0))
    os_ = pl.BlockSpec((tm, tn), lambda j, i: (i, j))
    return pl.pallas_call(body, name="ffn_up_gate", grid=(n // tn, m // tm),
                          in_specs=[pl.BlockSpec((tm, k), lambda j, i: (i, 0)), ws, ws], out_specs=[os_] * 3,
                          out_shape=[_sds((m, n), BF16)] * 3, compiler_params=_cparams(("arbitrary",) * 2))(h, w1t, w3t)


def _ffn_dh_call(da, db, w1t, w3t):
    m, n = da.shape
    k = w1t.shape[1]
    tm = _pick(m, (512, 256, 128, 64))

    def body(da_ref, db_ref, w1_ref, w3_ref, o_ref):
        o_ref[...] = (_raw_dot(da_ref[...], w1_ref[...], False, False)
                      + _raw_dot(db_ref[...], w3_ref[...], False, False)).astype(o_ref.dtype)
    xs = pl.BlockSpec((tm, n), lambda i: (i, 0))
    ws = pl.BlockSpec((n, k), lambda i: (0, 0))
    return pl.pallas_call(body, name="ffn_up_dh", grid=(m // tm,), in_specs=[xs, xs, ws, ws],
                          out_specs=pl.BlockSpec((tm, k), lambda i: (i, 0)), out_shape=_sds((m, k), BF16),
                          compiler_params=_cparams(("arbitrary",)))(da, db, w1t, w3t)


@jax.custom_vjp
def _ffn_up(h, w1t, w3t):
    return _ffn_up_call(h, w1t, w3t)[2]


def _ffn_up_f(h, w1t, w3t):
    a, b, u = _ffn_up_call(h, w1t, w3t)
    return u, (h, w1t, w3t, a, b)


def _ffn_up_b(res, du):
    h, w1t, w3t, a, b = res
    da, db = _swiglu_gate_op(h.shape[0]).run_bwd((a, b), (), (du,))
    return (_ffn_dh_call(da, db, w1t, w3t), _mm_call("ffn_up_tn", da, h, "tn", w1t.dtype),
            _mm_call("ffn_up_tn", db, h, "tn", w3t.dtype))


_ffn_up.defvjp(_ffn_up_f, _ffn_up_b)


def _seg_ones(width):
    i = np.arange(width)
    return jnp.asarray((i[:, None] // HD) == (i[None, :] // HD), BF16)


def _rope_perm(width):
    p = np.zeros((width, width), np.float32)
    for j in range(width):
        if (j % 32) < 16:
            p[j + 16, j] = -1.0
        else:
            p[j - 16, j] = 1.0
    return jnp.asarray(p, BF16)


def _rope_tables(width):
    t = jnp.arange(T_LAT)
    pos = jnp.stack([t // GRID_W, t % GRID_W], axis=-1).astype(F32)
    inv_freq = ROPE_THETA ** (-jnp.arange(ROPE_PAIRS, dtype=F32) / ROPE_PAIRS)
    ang = pos[..., None] * inv_freq
    ang = jnp.broadcast_to(ang[:, :, None, :], (T_LAT, 2, 2, ROPE_PAIRS)).reshape(T_LAT, HD)
    ang = jnp.tile(ang, (1, width // HD))
    cos = jnp.concatenate([jnp.ones((T_CTX, width), F32), jnp.cos(ang)], axis=0)
    sin = jnp.concatenate([jnp.zeros((T_CTX, width), F32), jnp.sin(ang)], axis=0)
    return cos, sin


def _att_pre(x, w_row, name):
    width = x.shape[-1]
    cos, sin = _rope_tables(width)

    def f(x, w, cos, sin, seg, perm):
        ms = _mm(x * x, seg, prec="bx") * (1.0 / HD)
        xn = x * lax.rsqrt(ms + EPS) * w
        return (xn * cos + _mm(xn, perm, prec="bx") * sin,)
    tab = pl.BlockSpec((TT, width), lambda b, t: (t, 0))
    op = _block_op(name, f, _tok_grid(),
                   [_tok(width), _row(width), tab, tab, _const2((width, width)), _const2((width, width))],
                   [_tok(width)], [_sds(x.shape, F32)], [True, True, False, False, False, False])
    return op(x, w_row, cos, sin, _seg_ones(width), _rope_perm(width))[0]


def _attention(q, k, v, name, tq):
    b_, hk, g_, tq_all, _ = q.shape
    tk = k.shape[2]

    def f(q, k, v):
        outs = []
        for gi in range(g_):
            s = _mm(q[gi] * (HD ** -0.5), k, tb=True)
            m = lax.stop_gradient(jnp.max(s, axis=-1, keepdims=True))
            e = jnp.exp(s - m)
            outs.append(_mm(e, v) / jnp.sum(e, axis=-1, keepdims=True))
        return (jnp.stack(outs, axis=0),)
    qs = pl.BlockSpec((None, None, g_, tq, HD), lambda b, h, i: (b, h, 0, i, 0))
    ks = pl.BlockSpec((None, None, tk, HD), lambda b, h, i: (b, h, 0, 0))
    op = _block_op(name, f, (b_, hk, tq_all // tq), [qs, ks, ks], [qs], [_sds(q.shape, F32)],
                   [True, True, True], acc=[None, 'last', 'last'])
    return op(q, k, v)[0]


def _gdn_pre(qkv, conv_w):
    nt_c = GDN_QKV // 128
    flag = jnp.asarray((np.arange(nt_c) % 3 < 2).astype(np.float32)[:, None, None] * np.ones((1, 1, 128), np.float32))
    cw = jnp.broadcast_to(conv_w[None], (B_LOC,) + conv_w.shape)

    def f(x, cw, flag, seg):
        def conv(s):
            acc = cw[2:3, :] * s
            for j in (0, 1, 3, 4):
                acc = acc + cw[j:j + 1, :] * _shift_fn(j - 2)(s)
            return acc
        y = jnp.concatenate([conv(x[:T_CTX]), conv(x[T_CTX:])], axis=0)
        s = _silu(y)
        ss = _mm(s * s, seg, prec="bx")
        return (s * (flag * lax.rsqrt(ss + EPS) + (1.0 - flag)),)
    xs = pl.BlockSpec((None, _ntok(), 128), lambda b, j: (b, 0, j))
    op = _block_op("gdn_pre", f, (B_LOC, nt_c),
                   [xs, pl.BlockSpec((None, 5, 128), lambda b, j: (b, 0, j)),
                    pl.BlockSpec((None, 1, 128), lambda b, j: (j, 0, 0)), pl.BlockSpec((128, 128), lambda b, j: (0, 0))],
                   [xs], [_sds(qkv.shape, F32)], [True, True, False, False])
    return op(qkv, cw, flag, _seg_ones(128))[0]


def _gdn_gates(ba, a_log, dt_bias):
    pad = jnp.zeros((12,), F32)
    al = jnp.broadcast_to(jnp.concatenate([pad, a_log.reshape(12), jnp.zeros((104,), F32)])[None, None], (B_LOC, 1, 128))
    db = jnp.broadcast_to(jnp.concatenate([pad, dt_bias.reshape(12), jnp.zeros((104,), F32)])[None, None], (B_LOC, 1, 128))

    def f(x, al, db):
        lane = lax.broadcasted_iota(jnp.int32, x.shape, 1)
        return (jnp.where(lane < 12, _sigmoid(x), -jnp.exp(al) * _softplus(x + db)),)
    xs = pl.BlockSpec((None, _ntok(), 128), lambda b: (b, 0, 0))
    ps = pl.BlockSpec((None, 1, 128), lambda b: (b, 0, 0))
    op = _block_op("gdn_gates", f, (B_LOC,), [xs, ps, ps], [xs], [_sds(ba.shape, F32)], [True, True, True])
    return op(ba, al, db)[0]


def _unit_triangular_inverses(lowers):
    n = lowers[0].shape[0]
    eye = (lax.broadcasted_iota(jnp.int32, (n, n), 0) == lax.broadcasted_iota(jnp.int32, (n, n), 1)).astype(F32)
    nks = [-l for l in lowers]
    invs = [eye + nk for nk in nks]
    for _ in range(int(math.log2(n)) - 1):
        nks = [_dot_impl(nk, nk, False, False, "f32") for nk in nks]
        invs = [inv + _dot_impl(inv, nk, False, False, "f32") for inv, nk in zip(invs, nks)]
    return tuple(invs)


@jax.custom_vjp
def _solve_with_inverses(lowers, rhss, invs):
    return tuple(_dot_impl(inv, rhs, False, False, "f32") for inv, rhs in zip(invs, rhss))


def _solve_fwd(lowers, rhss, invs):
    sols = _solve_with_inverses(lowers, rhss, invs)
    return sols, (invs, sols)


def _solve_bwd(res, dsols):
    invs, sols = res
    drhss = tuple(_dot_impl(inv, d, True, False, "f32") for inv, d in zip(invs, dsols))
    dlowers = tuple(-_dot_impl(dr, s, False, True, "f32") for dr, s in zip(drhss, sols))
    return dlowers, drhss, tuple(jnp.zeros_like(inv) for inv in invs)


_solve_with_inverses.defvjp(_solve_fwd, _solve_bwd)


def _gdn_masks():
    ii, jj = np.arange(CHUNK)[:, None], np.arange(CHUNK)[None, :]
    fwd = [jj <= ii, jj < ii, ii <= jj]
    bwd = [jj >= ii, jj > ii, ii >= jj]
    return jnp.asarray(np.stack([np.stack(fwd), np.stack(bwd)]).astype(np.float32))


def _gdn_prep(qkv, g, beta):
    b_, t_, _ = qkv.shape
    nc = t_ // CHUNK
    cb = max(d for d in (1, 2, 3, 4, 6) if nc % d == 0)
    npair = GDN_H // 2

    def f(x, g, beta, masks, saved_inv=None):
        q2, k2, v2 = x[:, :2 * HD], x[:, 2 * HD:4 * HD], x[:, 4 * HD:]
        ii = lax.broadcasted_iota(jnp.int32, (CHUNK, CHUNK), 0)
        jj = lax.broadcasted_iota(jnp.int32, (CHUNK, CHUNK), 1)
        eye = (ii == jj).astype(F32)
        incl, strict, incl_t = masks[0] > 0.5, masks[1] > 0.5, masks[2] > 0.5
        items = [(hh, c) for hh in range(2) for c in range(cb)]
        def sl(a, hh, c):
            return a[c * CHUNK:(c + 1) * CHUNK, hh * HD:(hh + 1) * HD]
        qs = [sl(q2, hh, c) * (HD ** -0.5) for hh, c in items]
        ks = [sl(k2, hh, c) for hh, c in items]
        vs = [sl(v2, hh, c) for hh, c in items]
        ones = jnp.ones((CHUNK, CHUNK), F32)
        lane_sum = lambda m: _mm(m, ones, prec="bx3")
        row_sum = lambda m: _mm(ones, m, prec="ax3")
        g_rows = [jnp.broadcast_to(g[hh, c], (CHUNK, CHUNK)) for hh, c in items]
        b_rows = [jnp.broadcast_to(beta[hh, c], (CHUNK, CHUNK)) for hh, c in items]
        g_cols = [lane_sum(eye * gr) for gr in g_rows]
        b_cols = [lane_sum(eye * br) for br in b_rows]
        gc_cols = [lane_sum(jnp.where(incl, gr, 0.0)) for gr in g_rows]
        gc_rows = [row_sum(jnp.where(incl_t, gc, 0.0)) for gc in g_cols]
        g_tots = [lane_sum(gr) for gr in g_rows]
        decays = [jnp.where(incl, jnp.exp(jnp.where(incl, gcc - gcr, 0.0)), 0.0) for gcc, gcr in zip(gc_cols, gc_rows)]
        e_cols = [jnp.exp(gcc) for gcc in gc_cols]
        kbs = [kc * bc for kc, bc in zip(ks, b_cols)]
        rhss = [jnp.concatenate([vc * bc, kb * ec], axis=1) for vc, bc, kb, ec in zip(vs, b_cols, kbs, e_cols)]
        qgs = [qc * ec for qc, ec in zip(qs, e_cols)]
        kds = [kc * jnp.exp(gt - gcc) for kc, gt, gcc in zip(ks, g_tots, gc_cols)]
        egs = [jnp.exp(gt)[0:1, :] for gt in g_tots]
        lowers = tuple(jnp.where(strict, _mm(kb, kc, tb=True) * dec, 0.0) for kb, kc, dec in zip(kbs, ks, decays))
        ins = [jnp.where(incl, _mm(qc, kc, tb=True) * dec, 0.0) for qc, kc, dec in zip(qs, ks, decays)]
        if saved_inv is None:
            invs = _unit_triangular_inverses(lowers)
        else:
            invs = tuple(saved_inv[hh, c * CHUNK:(c + 1) * CHUNK] for hh, c in items)
        sols = _solve_with_inverses(lowers, tuple(rhss), invs)
        us, ws = [s[:, :HD] for s in sols], [s[:, HD:] for s in sols]

        def heads(xs, joiner):
            return jnp.stack([joiner(xs[:cb]), joiner(xs[cb:])], axis=0)
        cat = lambda xs: jnp.concatenate(xs, axis=0)
        outs = (heads(us, cat), heads(ws, cat), heads(qgs, cat), heads(kds, cat), heads(ins, cat),
                heads(egs, lambda xs: jnp.stack(xs, axis=0)))
        return outs if saved_inv is not None else outs + (heads(list(invs), cat),)

    xs = pl.BlockSpec((None, cb * CHUNK, 6 * HD), lambda b, p, i, d: (b, i, p))
    rs = pl.BlockSpec((None, None, 2, None, cb, 1, CHUNK), lambda b, p, i, d: (b, p, 0, d, i, 0, 0))
    ts = pl.BlockSpec((None, None, 2, None, cb * CHUNK, HD), lambda b, p, i, d: (b, p, 0, d, i, 0))
    ms = pl.BlockSpec((None, 3, CHUNK, CHUNK), lambda b, p, i, d: (d, 0, 0, 0))
    big = _sds((b_, npair, 2, 2, t_, HD), F32)
    op = _block_op("gdn_prep", f, (b_, npair, nc // cb, 2), [xs, rs, rs, ms],
                   [ts, ts, ts, ts, ts, rs, ts], [big, big, big, big, big, _sds(g.shape, F32), big],
                   [True, True, True, False], acc=['last', None, None, None], n_res=1, f_bwd=f)
    return op(qkv, g, beta, _gdn_masks())


def _gdn_scan_specs(t_, backward):
    seg_c = T_CTX // CHUNK
    nseg = t_ // T_CTX

    def seg_of(d, s):
        s = nseg - 1 - s if backward else s
        return jnp.where(d == 0, s, jnp.where(s == 0, 0, nseg - s))
    ts = pl.BlockSpec((None, GDN_H // 2, 2, None, T_CTX, HD), lambda b, d, s: (b, 0, 0, d, seg_of(d, s), 0))
    es = pl.BlockSpec((None, GDN_H // 2, 2, None, seg_c, 1, CHUNK), lambda b, d, s: (b, 0, 0, d, seg_of(d, s), 0, 0))
    return ts, es, seg_c, nseg


def _gdn_scan_call(u, w, qg, kd, intra, eg):
    b_, t_ = u.shape[0], u.shape[4]
    ts, es, seg_c, nseg = _gdn_scan_specs(t_, False)
    heads = [(p, hh) for p in range(GDN_H // 2) for hh in range(2)]

    def body(u_ref, w_ref, qg_ref, kd_ref, in_ref, eg_ref, o_ref, st_ref, state):
        @pl.when(pl.program_id(2) == 0)
        def _():
            state[...] = jnp.zeros_like(state)
        d = pl.program_id(1)
        for i in range(seg_c):
            c = jnp.where(d == 0, i, seg_c - 1 - i)
            rows = pl.ds(pl.multiple_of(c * CHUNK, CHUNK), CHUNK)
            sts = [state[n] for n in range(len(heads))]
            for n, (p, hh) in enumerate(heads):
                st_ref[p, hh, rows, :] = sts[n]
            sbs = [st.astype(BF16) for st in sts]
            ws = [_raw_dot(w_ref[p, hh, rows, :].astype(BF16), sbs[n], False, False) for n, (p, hh) in enumerate(heads)]
            qss = [_raw_dot(qg_ref[p, hh, rows, :].astype(BF16), sbs[n], False, False) for n, (p, hh) in enumerate(heads)]
            vbs = [(u_ref[p, hh, rows, :] - ws[n]).astype(BF16) for n, (p, hh) in enumerate(heads)]
            for n, (p, hh) in enumerate(heads):
                o_ref[p, hh, rows, :] = qss[n] + _raw_dot(in_ref[p, hh, rows, :].astype(BF16), vbs[n], False, False)
            kvs = [_raw_dot(kd_ref[p, hh, rows, :].astype(BF16), vbs[n], True, False) for n, (p, hh) in enumerate(heads)]
            for n, (p, hh) in enumerate(heads):
                e = eg_ref[p, hh, pl.ds(c, 1), :, :].reshape(1, CHUNK)
                state[n] = sts[n] * e + kvs[n]
    return pl.pallas_call(body, name="gdn_scan_fwd", grid=(b_, 2, nseg), in_specs=[ts] * 5 + [es], out_specs=[ts, ts],
                          out_shape=[_sds(u.shape, F32), _sds(u.shape, F32)],
                          scratch_shapes=[pltpu.VMEM((GDN_H, HD, HD), F32)],
                          compiler_params=_cparams(("arbitrary",) * 3))(u, w, qg, kd, intra, eg)


def _gdn_scan_bwd_call(u, w, qg, kd, intra, eg, states, do):
    b_, t_ = u.shape[0], u.shape[4]
    ts, es, seg_c, nseg = _gdn_scan_specs(t_, True)
    heads = [(p, hh) for p in range(GDN_H // 2) for hh in range(2)]

    def body(u_ref, w_ref, qg_ref, kd_ref, in_ref, eg_ref, st_ref, do_ref, du_ref, dw_ref, dqg_ref, dkd_ref, din_ref,
             deg_ref, dstate):
        @pl.when(pl.program_id(2) == 0)
        def _():
            dstate[...] = jnp.zeros_like(dstate)
        d = pl.program_id(1)
        hs = list(enumerate(heads))
        for i in range(seg_c):
            c = jnp.where(d == 0, seg_c - 1 - i, i)
            rows = pl.ds(pl.multiple_of(c * CHUNK, CHUNK), CHUNK)
            dss = [dstate[n] for n, _ in hs]
            sts = [st_ref[p, hh, rows, :] for _, (p, hh) in hs]
            sbs = [st.astype(BF16) for st in sts]
            dsbs = [ds.astype(BF16) for ds in dss]
            wbs = [w_ref[p, hh, rows, :].astype(BF16) for _, (p, hh) in hs]
            dobs = [do_ref[p, hh, rows, :].astype(BF16) for _, (p, hh) in hs]
            kdbs = [kd_ref[p, hh, rows, :].astype(BF16) for _, (p, hh) in hs]
            vbs = [(u_ref[p, hh, rows, :] - _raw_dot(wbs[n], sbs[n], False, False)).astype(BF16) for n, (p, hh) in hs]
            dvns = [_raw_dot(in_ref[p, hh, rows, :].astype(BF16), dobs[n], True, False)
                    + _raw_dot(kdbs[n], dsbs[n], False, False) for n, (p, hh) in hs]
            dvbs = [dvn.astype(BF16) for dvn in dvns]
            for n, (p, hh) in hs:
                din_ref[p, hh, rows, :] = _raw_dot(dobs[n], vbs[n], False, True)
                dqg_ref[p, hh, rows, :] = _raw_dot(dobs[n], sbs[n], False, True)
                dkd_ref[p, hh, rows, :] = _raw_dot(vbs[n], dsbs[n], False, True)
                du_ref[p, hh, rows, :] = dvns[n]
                dw_ref[p, hh, rows, :] = -_raw_dot(dvbs[n], sbs[n], False, True)
                deg_ref[p, hh, pl.ds(c, 1), :, :] = jnp.sum(sts[n] * dss[n], axis=0, keepdims=True).reshape(1, 1, CHUNK)
            upd = [_raw_dot(qg_ref[p, hh, rows, :].astype(BF16), dobs[n], True, False)
                   - _raw_dot(wbs[n], dvbs[n], True, False) for n, (p, hh) in hs]
            for n, (p, hh) in hs:
                e = eg_ref[p, hh, pl.ds(c, 1), :, :].reshape(1, CHUNK)
                dstate[n] = dss[n] * e + upd[n]
    big = _sds(u.shape, F32)
    return pl.pallas_call(body, name="gdn_scan_bwd", grid=(b_, 2, nseg), in_specs=[ts] * 5 + [es, ts, ts],
                          out_specs=[ts] * 5 + [es], out_shape=[big] * 5 + [_sds(eg.shape, F32)],
                          scratch_shapes=[pltpu.VMEM((GDN_H, HD, HD), F32)],
                          compiler_params=_cparams(("arbitrary",) * 3))(u, w, qg, kd, intra, eg, states, do)


@jax.custom_vjp
def _gdn_scan(u, w, qg, kd, intra, eg):
    return _gdn_scan_call(u, w, qg, kd, intra, eg)[0]


def _gdn_scan_f(u, w, qg, kd, intra, eg):
    o, states = _gdn_scan_call(u, w, qg, kd, intra, eg)
    return o, (u, w, qg, kd, intra, eg, states)


def _gdn_scan_b(res, do):
    return tuple(_gdn_scan_bwd_call(*res, do))


_gdn_scan.defvjp(_gdn_scan_f, _gdn_scan_b)


def _gdn_post(o, z, w_row):
    def f(o, z, w, seg):
        ms = _mm(o * o, seg, prec="bx") * (1.0 / HD)
        return (o * lax.rsqrt(ms + EPS) * w * _silu(z),)
    op = _block_op("gdn_post", f, _tok_grid(), [_tok(GDN_W), _tok(GDN_W), _row(GDN_W), _const2((GDN_W, GDN_W))],
                   [_tok(GDN_W)], [_sds(o.shape, BF16)], [True, True, True, False])
    return op(o, z, w_row, _seg_ones(GDN_W))[0]


def _s5_tables(ar, ai, rev):
    pr, pi = [ar], [ai]
    for _ in range(7):
        pr, pi = pr + [pr[-1] * ar - pi[-1] * ai], pi + [pr[-1] * ai + pi[-1] * ar]
    if rev:
        pr, pi = pr[::-1], pi[::-1]
    return jnp.concatenate(pr, axis=0), jnp.concatenate(pi, axis=0)


def _s5_scan_call(bu, ar, ai, direction, h=None):
    b_, t_, n2 = bu.shape
    nblk = n2 // (2 * S5_LB)
    with_grad = h is not None
    rev = (direction == 1) != with_grad
    tr, ti = _s5_tables(ar, ai, rev)
    tab = jnp.concatenate([tr.reshape(8, nblk, 1, S5_LB), ti.reshape(8, nblk, 1, S5_LB)], axis=2).reshape(8, n2)
    ntile, ntc = t_ // 8, T_CTX // 8

    def path(j):
        return j if direction == 0 else jnp.where(j < ntc, ntc - 1 - j, ntile + ntc - 1 - j)

    def scan_tile(xr, xi, tabr, tabi, cr, ci):
        row = lax.broadcasted_iota(jnp.int32, xr.shape, 0)
        for k in (1, 2, 4):
            idx = (8 - k) if rev else (k - 1)
            akr, aki = tabr[idx:idx + 1, :], tabi[idx:idx + 1, :]
            sh = (8 - k) if rev else k
            sr, si = pltpu.roll(xr, sh, 0), pltpu.roll(xi, sh, 0)
            ok = (row < 8 - k) if rev else (row >= k)
            xr, xi = (xr + jnp.where(ok, akr * sr - aki * si, 0.0), xi + jnp.where(ok, akr * si + aki * sr, 0.0))
        return xr + tabr * cr - tabi * ci, xi + tabr * ci + tabi * cr

    def body(*refs):
        if with_grad:
            bu_ref, tab_ref, h_ref, o_ref, da_ref = refs
        else:
            bu_ref, tab_ref, o_ref = refs
        tabr, tabi = tab_ref[:, :S5_LB], tab_ref[:, S5_LB:]
        zero = jnp.zeros((1, S5_LB), F32)
        row = lax.broadcasted_iota(jnp.int32, (8, S5_LB), 0)

        def step(i, carry):
            cr, ci = carry[0], carry[1]
            j = (ntile - 1 - i) if with_grad else i
            rows = pl.ds(pl.multiple_of(path(j) * 8, 8), 8)
            hr, hi = scan_tile(bu_ref[rows, :S5_LB], bu_ref[rows, S5_LB:], tabr, tabi, cr, ci)
            o_ref[rows, :S5_LB] = hr
            o_ref[rows, S5_LB:] = hi
            out = (hr[0:1, :], hi[0:1, :]) if rev else (hr[7:8, :], hi[7:8, :])
            if with_grad:
                prev = pl.ds(pl.multiple_of(path(jnp.maximum(j - 1, 0)) * 8, 8), 8)
                live = jnp.where(j > 0, 1.0, 0.0)
                sh, edge = (1, 0) if direction == 0 else (7, 7)
                pr = jnp.where(row == edge, pltpu.roll(h_ref[prev, :S5_LB], sh, 0) * live, pltpu.roll(h_ref[rows, :S5_LB], sh, 0))
                pi = jnp.where(row == edge, pltpu.roll(h_ref[prev, S5_LB:], sh, 0) * live, pltpu.roll(h_ref[rows, S5_LB:], sh, 0))
                out = out + (carry[2] + hr * pr + hi * pi, carry[3] + hi * pr - hr * pi)
            return out
        init = (zero, zero) + ((jnp.zeros((8, S5_LB), F32),) * 2 if with_grad else ())
        fin = lax.fori_loop(0, ntile, step, init)
        if with_grad:
            da_ref[:, :S5_LB] = fin[2]
            da_ref[:, S5_LB:] = fin[3]
    xs = pl.BlockSpec((None, t_, 2 * S5_LB), lambda b, j: (b, 0, j))
    tb = pl.BlockSpec((8, 2 * S5_LB), lambda b, j: (0, j))
    if with_grad:
        return pl.pallas_call(body, name="s5_scan_bwd%d" % direction, grid=(b_, nblk), in_specs=[xs, tb, xs],
                              out_specs=[xs, pl.BlockSpec((None, 8, 2 * S5_LB), lambda b, j: (b, 0, j))],
                              out_shape=[_sds(bu.shape, F32), _sds((b_, 8, n2), F32)],
                              compiler_params=_cparams(("arbitrary", "arbitrary")))(bu, tab, h)
    return pl.pallas_call(body, name="s5_scan_fwd%d" % direction, grid=(b_, nblk), in_specs=[xs, tb], out_specs=xs,
                          out_shape=_sds(bu.shape, F32), compiler_params=_cparams(("arbitrary", "arbitrary")))(bu, tab)


@functools.lru_cache(maxsize=None)
def _s5_scan_fn(direction):
    @jax.custom_vjp
    def scan(bu, ar, ai):
        return _s5_scan_call(bu, ar, ai, direction)

    def fwd(bu, ar, ai):
        h = _s5_scan_call(bu, ar, ai, direction)
        return h, (h, ar, ai)

    def bwd(res, dh):
        h, ar, ai = res
        lam, da = _s5_scan_call(dh, ar, -ai, direction, h=h)
        nblk = da.shape[-1] // (2 * S5_LB)
        da = jnp.sum(da, axis=(0, 1)).reshape(nblk, 2, S5_LB)
        return lam, da[:, 0].reshape(1, -1), da[:, 1].reshape(1, -1)

    scan.defvjp(fwd, bwd)
    return scan


def _s5_post(u, y0, y1, d_row, glu_w, glu_b_row):
    def f(u, y0, y1, d, gw, gb):
        zz = _gelu_tanh(d * u + y0 + y1)
        return (zz * _sigmoid(_mm(zz, gw) + gb),)
    op = _block_op("s5_post", f, _tok_grid(),
                   [_tok(S5_W), _tok(S5_W), _tok(S5_W), _row(S5_W), _const2((S5_W, S5_W)), _row(S5_W)],
                   [_tok(S5_W)], [_sds(u.shape, BF16)], [True] * 6, acc=[None, None, None, None, 'all', None])
    return op(u, y0, y1, d_row, glu_w, glu_b_row)[0]


def _loss_rows(x, target):
    def f(x, t):
        e = x - t
        return (jnp.sum(e * e, axis=0, keepdims=True),)
    grid = (B_LOC, T_LAT // TT)
    op = _block_op("loss_rows", f, grid, [_tok(D), _tok(D)], [_row(D)], [_sds((B_LOC, T_LAT // TT, 1, D), F32)],
                   [True, False])
    return op(x, target)[0]


def _adamw_call(name, w, g, m, v, pieces):
    r, c = w.shape
    tr = _pick(r, (256, 128, 64, 32, 16, 8))
    c1, c2 = 1.0 - ADAM_B1 ** ADAM_STEP, 1.0 - ADAM_B2 ** ADAM_STEP

    def body(w_ref, g_ref, m_ref, v_ref, go_ref, d_ref, mo_ref, vo_ref):
        if pieces:
            g = g_ref[0].astype(F32)
            for i in range(1, g_ref.shape[0]):
                g = g + g_ref[i].astype(F32)
        else:
            g = g_ref[...]
        m = ADAM_B1 * m_ref[...] + (1.0 - ADAM_B1) * g
        v = ADAM_B2 * v_ref[...] + (1.0 - ADAM_B2) * (g * g)
        go_ref[...] = g
        mo_ref[...] = m
        vo_ref[...] = v
        d_ref[...] = -ADAM_LR * ((m / c1) / (jnp.sqrt(v / c2) + ADAM_EPS) + ADAM_WD * w_ref[...])
    spec = pl.BlockSpec((tr, c), lambda i: (i, 0))
    gspec = pl.BlockSpec((g.shape[0], tr, c), lambda i: (0, i, 0)) if pieces else spec
    return pl.pallas_call(body, name=name, grid=(r // tr,), in_specs=[spec, gspec, spec, spec], out_specs=[spec] * 4,
                          out_shape=[_sds((r, c), F32)] * 4, compiler_params=_cparams(("arbitrary",)))(w, g, m, v)


def _sum_pieces(x, name):
    _, r, c = x.shape
    tr = _pick(r, (512, 256, 128, 64, 32, 16, 8))

    def body(x_ref, o_ref):
        acc = x_ref[0]
        for i in range(1, NDEV):
            acc = acc + x_ref[i]
        o_ref[...] = acc
    return pl.pallas_call(body, name=name, grid=(r // tr,), in_specs=[pl.BlockSpec((NDEV, tr, c), lambda i: (0, i, 0))],
                          out_specs=pl.BlockSpec((tr, c), lambda i: (i, 0)), out_shape=_sds((r, c), F32),
                          compiler_params=_cparams(("arbitrary",)))(x)


def _pack_flat(arrs, lanes, row_mult):
    flat = jnp.concatenate([a.reshape(-1).astype(F32) for a in arrs])
    n = flat.shape[0]
    rows = -(-n // lanes)
    rows = -(-rows // row_mult) * row_mult
    return jnp.pad(flat, (0, rows * lanes - n)).reshape(rows, lanes)


def _unpack_flat(packed, shapes, lead=()):
    flat = packed.reshape(lead + (-1,))
    out, off = [], 0
    for s in shapes:
        n = int(np.prod(s))
        out.append(flat[..., off:off + n].reshape(lead + tuple(s)))
        off += n
    return out


FS_ROWS = DFF // NDEV
WI_ROWS = IN_COLS // NDEV
WI_PAD = -(-WI_ROWS // 16) * 16
WO_ROWS = D // NDEV


def _pad_rows(blk, rows, axis):
    pad = [(0, 0)] * blk.ndim
    pad[axis] = (0, rows - blk.shape[axis])
    return jnp.pad(blk, pad)


def _pack_big_shards(t, dtype):
    parts = []
    for l in range(DEPTH):
        for n in ('ffn_w1', 'ffn_w3', 'ffn_w2'):
            for i in range(2):
                parts.append(t[n][l, i] if n == 'ffn_w2' else t[n][l, i].T)
        parts.append(_pad_rows(t['w_in'][l].T, WI_PAD, 0))
        parts.append(t['w_out'][l])
    return jnp.concatenate(parts, axis=0).astype(dtype)


def _unpack_big_shards(p):
    out = {n: [] for n in BIG}
    off = 0
    for l in range(DEPTH):
        for n in ('ffn_w1', 'ffn_w3', 'ffn_w2'):
            pair = []
            for i in range(2):
                blk = p[off:off + FS_ROWS]
                off += FS_ROWS
                pair.append(blk if n == 'ffn_w2' else blk.T)
            out[n].append(jnp.stack(pair))
        out['w_in'].append(p[off:off + WI_ROWS].T)
        off += WI_PAD
        out['w_out'].append(p[off:off + WO_ROWS])
        off += WO_ROWS
    return {n: jnp.stack(v) for n, v in out.items()}


def _full_from_gathered(g):
    layers, off = [], 0
    for l in range(DEPTH):
        lw = {}
        for n in ('ffn_w1', 'ffn_w3', 'ffn_w2'):
            pair = []
            for i in range(2):
                pair.append(g[:, off:off + FS_ROWS].reshape(DFF, D))
                off += FS_ROWS
            lw[n] = pair
        lw['w_in'] = g[:, off:off + WI_ROWS].reshape(IN_COLS, D)
        off += WI_PAD
        lw['w_out'] = g[:, off:off + WO_ROWS].reshape(D, D)
        off += WO_ROWS
        layers.append(lw)
    return layers


def _pieces_from_full(layers):
    parts = []
    for lw in layers:
        for n in ('ffn_w1', 'ffn_w3', 'ffn_w2'):
            for i in range(2):
                parts.append(lw[n][i].reshape(NDEV, FS_ROWS, D))
        parts.append(_pad_rows(lw['w_in'].reshape(NDEV, WI_ROWS, D), WI_PAD, 1))
        parts.append(lw['w_out'].reshape(NDEV, WO_ROWS, D))
    return jnp.concatenate(parts, axis=1)


def _flip_segments(a, axis):
    ctx, lat = lax.slice_in_dim(a, 0, T_CTX, axis=axis), lax.slice_in_dim(a, T_CTX, _ntok(), axis=axis)
    return jnp.concatenate([jnp.flip(ctx, axis), jnp.flip(lat, axis)], axis=axis)


def _rows_of(vec_ctx, vec_lat):
    w = vec_lat.shape[-1]
    ntc, ntl = T_CTX // TT, T_LAT // TT
    return jnp.concatenate([jnp.broadcast_to(vec_ctx[None, None, None, :], (B_LOC, ntc, 1, w)),
                            jnp.broadcast_to(vec_lat[:, None, None, :], (B_LOC, ntl, 1, w))], axis=1)


def _rows_const(vec):
    return jnp.broadcast_to(vec[None, None, None, :], (B_LOC, _ntok() // TT, 1, vec.shape[-1]))


def _ffn_sublayer(xt, mrow, w1, w3, w2, g, b):
    h = _modulate(xt, mrow[0], mrow[1]).reshape(-1, D)
    u = _ffn_up(h, w1, w3)
    y = _matmul("ffn_down", u, w2, F32).reshape(xt.shape)
    return _post_norm(xt, y, mrow[2], _rows_const(g), _rows_const(b), 0.5)


def _s5_discretize(lam_re, lam_im, log_dt, b_re, b_im):
    dt = jnp.exp(log_dt)[:, None]
    zr, zi = lam_re * dt, lam_im * dt
    er = jnp.exp(zr)
    lbr, lbi = er * jnp.cos(zi), er * jnp.sin(zi)
    dd = lam_re * lam_re + lam_im * lam_im
    qr = ((lbr - 1.0) * lam_re + lbi * lam_im) / dd
    qi = (lbi * lam_re - (lbr - 1.0) * lam_im) / dd
    bbr = qr[..., None] * b_re - qi[..., None] * b_im
    bbi = qr[..., None] * b_im + qi[..., None] * b_re
    return lbr, lbi, bbr, bbi


def _s5_cols(a):
    return a.reshape(a.shape[:-1] + (S5_N // S5_LB, S5_LB))


def _s5_group(su, p):
    b_, t_, _ = su.shape
    eye = jnp.eye(S5_G, dtype=F32)
    ys = []
    for d in range(2):
        lbr, lbi, bbr, bbi = _s5_discretize(p['s5_lam_re'][d], p['s5_lam_im'][d], p['s5_log_dt'][d],
                                            p['s5_b_re'][d], p['s5_b_im'][d])
        bre = jnp.einsum('gph,gk->ghkp', bbr, eye).reshape(S5_W, S5_N)
        bim = jnp.einsum('gph,gk->ghkp', bbi, eye).reshape(S5_W, S5_N)
        bmat = jnp.stack([_s5_cols(bre), _s5_cols(bim)], axis=2).reshape(S5_W, 2 * S5_N)
        cre = jnp.einsum('ghp,gk->kpgh', p['s5_c_re'][d], eye).reshape(S5_N, S5_W)
        cim = -jnp.einsum('ghp,gk->kpgh', p['s5_c_im'][d], eye).reshape(S5_N, S5_W)
        cmat = jnp.stack([cre.reshape(S5_N // S5_LB, S5_LB, S5_W), cim.reshape(S5_N // S5_LB, S5_LB, S5_W)],
                         axis=1).reshape(2 * S5_N, S5_W)
        bu = _matmul("s5_in", su.reshape(-1, S5_W), bmat, F32).reshape(b_, t_, 2 * S5_N)
        hs = _s5_scan_fn(d)(bu, lbr.reshape(1, S5_N), lbi.reshape(1, S5_N))
        ys.append(_matmul("s5_out", hs.reshape(-1, 2 * S5_N), cmat, F32).reshape(b_, t_, S5_W))
    return _s5_post(su, ys[0], ys[1], _rows_const(p['s5_d']), p['glu_w'], _rows_const(p['glu_b']))


def _gdn_group(qkv, z, ba, p):
    b_, t_, _ = qkv.shape
    nc = t_ // CHUNK
    qkvn = _gdn_pre(qkv, _pair_major(p['gdn_conv_w']))
    bg = _gdn_gates(ba, p['gdn_a_log'], p['gdn_dt_bias'])

    def gates(a):
        a = a.reshape(b_, t_, 2, GDN_H // 2, 2).transpose(0, 3, 4, 2, 1)
        return a.reshape(b_, GDN_H // 2, 2, 2, nc, 1, CHUNK)
    outs = _gdn_prep(qkvn, gates(bg[..., 12:24]), gates(bg[..., 0:12]))
    o = _gdn_scan(*outs).reshape(b_, GDN_H, 2, t_, HD)
    o = (o[:, :, 0] + o[:, :, 1]).transpose(0, 2, 1, 3).reshape(b_, t_, GDN_W)
    return _gdn_post(o, z, _rows_const(jnp.tile(p['gdn_norm_w'], GDN_H)))


def _att_group(aq, ak, av, p):
    b_, t_, _ = aq.shape
    qn = _att_pre(aq, _rows_const(jnp.tile(p['q_norm_w'], ATT_W // HD)), "att_pre_q")
    kn = _att_pre(ak, _rows_const(jnp.tile(p['k_norm_w'], ATT_KW // HD)), "att_pre_k")
    q = qn.reshape(b_, t_, ATT_HKV, ATT_G, HD).transpose(0, 2, 3, 1, 4)
    k = kn.reshape(b_, t_, ATT_HKV, HD).transpose(0, 2, 1, 3)
    v = av.reshape(b_, t_, ATT_HKV, HD).transpose(0, 2, 1, 3)
    o_lat = _attention(q[:, :, :, T_CTX:], k, v, "att_lat", ATT_TQ)
    o_ctx = _attention(q[:, :, :, :T_CTX], k[:, :, :T_CTX], v[:, :, :T_CTX], "att_ctx", T_CTX)
    o = jnp.concatenate([o_ctx, o_lat], axis=3)
    return o.transpose(0, 3, 1, 2, 4).reshape(b_, t_, ATT_W)


def _pair_major(w):
    lead = w.shape[:-1]
    return w.reshape(lead + (3, GDN_H // 2, 2 * HD)).swapaxes(-3, -2).reshape(lead + (GDN_QKV,))


def _permute_w_in(wt):
    qkv = wt[:GDN_QKV].reshape(3, GDN_H // 2, 2 * HD, D).swapaxes(0, 1).reshape(GDN_QKV, D)
    return jnp.concatenate([qkv, wt[GDN_QKV:1536], wt[1560:], wt[1536:1560],
                            jnp.zeros((IN_PAD - IN_COLS, D), wt.dtype)], axis=0)


PROJ_CUTS = (0, 1152, 1536, 1920, 2048, 2176, 2432, 2560)


@jax.custom_vjp
def _split_proj(proj):
    return tuple(proj[..., a:b] for a, b in zip(PROJ_CUTS[:-1], PROJ_CUTS[1:]))


_split_proj.defvjp(lambda proj: (_split_proj(proj), None), lambda _, d: (jnp.concatenate(d, axis=-1),))


def _mixer_sublayer(xt, mrow, lw, p, g, b):
    h = _modulate(xt, mrow[3], mrow[4]).reshape(-1, D)
    proj = _matmul_t("mix_in", h, _permute_w_in(lw['w_in']), F32).reshape(xt.shape[:2] + (IN_PAD,))
    qkv, z, aq, ak, av, su, ba = _split_proj(proj)
    o_gdn = _gdn_group(qkv, z, ba, p)
    o_att = _att_group(aq, ak, av, p)
    o_s5 = _s5_group(su, p)
    cat = jnp.concatenate([o_gdn, o_att.astype(BF16), o_s5], axis=-1).reshape(-1, D)
    y = _matmul("mix_out", cat, lw['w_out'], F32).reshape(xt.shape)
    return _post_norm(xt, y, mrow[5], _rows_const(g), _rows_const(b), 1.0)


def _local_loss(x, mod, modc, big, small, ctx, target):
    xt = jnp.concatenate([ctx, x], axis=1)
    for l in range(DEPTH):
        mrow = [_rows_of(modc[l, k * D:(k + 1) * D], mod[l, :, k * D:(k + 1) * D]) for k in range(N_MOD)]
        p = {n: small[n][l] for n in small}
        lw = big[l]
        xt = _ffn_sublayer(xt, mrow[0:3], lw['ffn_w1'][0], lw['ffn_w3'][0], lw['ffn_w2'][0], p['ln_g'][0], p['ln_b'][0])
        xt = _mixer_sublayer(xt, mrow, lw, p, p['ln_g'][1], p['ln_b'][1])
        xt = _ffn_sublayer(xt, mrow[6:9], lw['ffn_w1'][1], lw['ffn_w3'][1], lw['ffn_w2'][1], p['ln_g'][2], p['ln_b'][2])
    part = _loss_rows(xt[:, T_CTX:], target)
    return (0.5 / D) * jnp.sum(part)


def _silu_plain(x):
    return x * jax.nn.sigmoid(x)


def _dsilu_plain(x):
    s = jax.nn.sigmoid(x)
    return s * (1.0 + x * (1.0 - s))


def _small_shapes():
    return {'ln_g': (DEPTH, 3, D), 'ln_b': (DEPTH, 3, D), 'gdn_conv_w': (DEPTH, 5, GDN_QKV), 'glu_w': (DEPTH, S5_W, S5_W),
            'gdn_a_log': (DEPTH, 2, GDN_H), 'gdn_dt_bias': (DEPTH, 2, GDN_H), 'gdn_norm_w': (DEPTH, HD),
            'q_norm_w': (DEPTH, HD), 'k_norm_w': (DEPTH, HD), 's5_lam_re': (DEPTH, 2, S5_G, S5_P),
            's5_lam_im': (DEPTH, 2, S5_G, S5_P), 's5_log_dt': (DEPTH, 2, S5_G),
            's5_b_re': (DEPTH, 2, S5_G, S5_P, S5_H), 's5_b_im': (DEPTH, 2, S5_G, S5_P, S5_H),
            's5_c_re': (DEPTH, 2, S5_G, S5_H, S5_P), 's5_c_im': (DEPTH, 2, S5_G, S5_H, S5_P),
            's5_d': (DEPTH, S5_W), 'glu_b': (DEPTH, S5_W)}


def _gather_small_sharded(gathered, name):
    if name == 'glu_w':
        return gathered.transpose(1, 0, 2, 3).reshape(DEPTH, S5_W, S5_W)
    lead = gathered.shape[1:-1]
    return jnp.moveaxis(gathered, 0, -2).reshape(lead + (-1,))


def _my_small_shard(full, name, me):
    if name == 'glu_w':
        return lax.dynamic_slice_in_dim(full, me * (S5_W // NDEV), S5_W // NDEV, axis=1)
    n = full.shape[-1] // NDEV
    return lax.dynamic_slice_in_dim(full, me * n, n, axis=full.ndim - 1)


def kernel(x, c, ctx, c_ctx, w_ada, b_ada, ln_g, ln_b, ffn_w1, ffn_w3, ffn_w2, w_in, w_out, gdn_conv_w, gdn_a_log, gdn_dt_bias, gdn_norm_w, q_norm_w, k_norm_w, s5_lam_re, s5_lam_im, s5_log_dt, s5_b_re, s5_b_im, s5_c_re, s5_c_im, s5_d, glu_w, glu_b, loss_target, m_c_ctx, m_w_ada, m_b_ada, m_ln_g, m_ln_b, m_ffn_w1, m_ffn_w3, m_ffn_w2, m_w_in, m_w_out, m_gdn_conv_w, m_gdn_a_log, m_gdn_dt_bias, m_gdn_norm_w, m_q_norm_w, m_k_norm_w, m_s5_lam_re, m_s5_lam_im, m_s5_log_dt, m_s5_b_re, m_s5_b_im, m_s5_c_re, m_s5_c_im, m_s5_d, m_glu_w, m_glu_b, v_c_ctx, v_w_ada, v_b_ada, v_ln_g, v_ln_b, v_ffn_w1, v_ffn_w3, v_ffn_w2, v_w_in, v_w_out, v_gdn_conv_w, v_gdn_a_log, v_gdn_dt_bias, v_gdn_norm_w, v_q_norm_w, v_k_norm_w, v_s5_lam_re, v_s5_lam_im, v_s5_log_dt, v_s5_b_re, v_s5_b_im, v_s5_c_re, v_s5_c_im, v_s5_d, v_glu_w, v_glu_b):
    a = dict(locals())
    me = _my_index()
    ada_cols = N_MOD * D // NDEV

    sc = _silu_plain(a['c'])
    scc = _silu_plain(a['c_ctx'])
    small_in = [sc] + [a[n] for n in SMALL_SHARDED]
    got = _all_gather(_pack_flat(small_in, 128, 8), "gather_small")
    parts = _unpack_flat(got, [t.shape for t in small_in], lead=(NDEV,))
    sc_all = parts[0].reshape(NDEV * B_LOC, D)
    small = {n: _gather_small_sharded(parts[1 + i], n) for i, n in enumerate(SMALL_SHARDED)}
    for n in SMALL_REPL:
        small[n] = a[n]

    nb = NDEV * B_LOC
    rows_pad = 8
    sc_rows = jnp.concatenate([sc_all, scc[None], jnp.zeros((rows_pad - 1, D), F32)], axis=0)
    mod_part = jnp.stack([_mm_call("ada_fwd", sc_rows, a['w_ada'][l], "nn", F32) for l in range(DEPTH)])
    mod_all = _all_gather(mod_part.reshape(DEPTH * (nb + rows_pad), ada_cols), "gather_mod")
    mod_all = mod_all.reshape(NDEV, DEPTH, nb + rows_pad, ada_cols).transpose(1, 2, 0, 3).reshape(DEPTH, nb + rows_pad, N_MOD * D)
    mod_all = mod_all + a['b_ada'][:, None, :]
    mod = lax.dynamic_slice_in_dim(mod_all, me * B_LOC, B_LOC, axis=1)
    modc = mod_all[:, nb]

    big = _full_from_gathered(_all_gather(_pack_big_shards(a, BF16), "gather_weights"))

    loss_part, grads = jax.value_and_grad(_local_loss, argnums=(0, 1, 2, 3, 4))(
        a['x'], mod, modc, big, small, a['ctx'], a['loss_target'])
    gx, gmod, gmodc, gbig, gsmall = grads

    gm_rows = jnp.concatenate([gmod, gmodc[:, None], jnp.zeros((DEPTH, rows_pad - B_LOC - 1, N_MOD * D), F32)], axis=1)
    gm_all = _all_gather(gm_rows.reshape(DEPTH * rows_pad, N_MOD * D), "gather_dmod").reshape(NDEV, DEPTH, rows_pad, N_MOD * D)
    gm_all = gm_all.transpose(1, 0, 2, 3).reshape(DEPTH, NDEV * rows_pad, N_MOD * D)
    g_b_ada = jnp.sum(gm_all, axis=1)
    sc_dev = jnp.concatenate([sc_all.reshape(NDEV, B_LOC, D), jnp.broadcast_to(scc[None, None], (NDEV, 1, D)),
                              jnp.zeros((NDEV, rows_pad - B_LOC - 1, D), F32)], axis=1).reshape(NDEV * rows_pad, D)
    gm_mine = lax.dynamic_slice_in_dim(gm_all, me * ada_cols, ada_cols, axis=2)
    g_w_ada = jnp.stack([_mm_call("ada_dw", sc_dev, gm_mine[l], "tn", F32) for l in range(DEPTH)])
    gmc = gm_mine.reshape(DEPTH, NDEV, rows_pad, ada_cols)[:, :, B_LOC].sum(axis=1)
    gmc = jnp.concatenate([gmc[:, None], jnp.zeros((DEPTH, 7, ada_cols), F32)], axis=1)
    dscc_part = sum(_mm_call("ada_dx", gmc[l], a['w_ada'][l], "nt", F32)[0] for l in range(DEPTH))

    small_names = SMALL_SHARDED + SMALL_REPL
    sums_in = [loss_part.reshape(1), dscc_part] + [gsmall[n] for n in small_names]
    tot = _sum_pieces(_all_gather(_pack_flat(sums_in, 128, 512), "gather_sums"), "sum_small")
    tparts = _unpack_flat(tot, [t.shape for t in sums_in])
    loss = tparts[0].reshape(())
    g_c_ctx = tparts[1] * _dsilu_plain(a['c_ctx'])
    g = {'c_ctx': g_c_ctx, 'b_ada': g_b_ada, 'w_ada': g_w_ada}
    for i, n in enumerate(small_names):
        g[n] = _my_small_shard(tparts[2 + i], n, me) if n in SMALL_SHARDED else tparts[2 + i]

    pieces = _pieces_from_full(gbig)
    recv = _scatter_chip(_pair_sum(pieces, _scatter_pair(pieces, "scatter_pair"), "scatter_add"), "scatter_chip")
    gb, db, mb, vb = _adamw_call("adamw_big", _pack_big_shards(a, F32), recv,
                                 _pack_big_shards({n: a['m_' + n] for n in BIG}, F32),
                                 _pack_big_shards({n: a['v_' + n] for n in BIG}, F32), True)
    res = {'g': {}, 'd': {}, 'm': {}, 'v': {}}
    for key, packed in (('g', gb), ('d', db), ('m', mb), ('v', vb)):
        res[key].update(_unpack_big_shards(packed))

    shp = a['w_ada'].shape
    flat2 = lambda t: t.reshape(-1, shp[-1])
    ga, da, ma, va = _adamw_call("adamw_ada", flat2(a['w_ada']), flat2(g['w_ada']), flat2(a['m_w_ada']), flat2(a['v_w_ada']), False)
    for key, val in (('g', ga), ('d', da), ('m', ma), ('v', va)):
        res[key]['w_ada'] = val.reshape(shp)
    rest = [n for n in WEIGHTS if n not in BIG and n != 'w_ada']
    shapes = [a[n].shape for n in rest]
    pk = lambda d: _pack_flat([d[n] for n in rest], 128, 256)
    outs = _adamw_call("adamw_small", pk(a), pk(g), pk({n: a['m_' + n] for n in rest}), pk({n: a['v_' + n] for n in rest}), False)
    for key, val in zip(('g', 'd', 'm', 'v'), outs):
        for n, t in zip(rest, _unpack_flat(val, shapes)):
            res[key][n] = t

    out = [loss, gx]
    for key in ('g', 'd', 'm', 'v'):
        out += [res[key][n] for n in WEIGHTS]
    return tuple(out)
```

```python
import functools
import math

import numpy as np
import jax
import jax.numpy as jnp
from jax import lax
from jax.experimental import pallas as pl
from jax.experimental.pallas import tpu as pltpu

F32 = jnp.float32
BF16 = jnp.bfloat16
MESH = pl.DeviceIdType.MESH

NDEV = 8
D = 1024
DFF = 2816
DEPTH = 4
B_LOC = 4
T_CTX = 256
T_LAT = 2048
GRID_W = 64
N_MOD = 9
GDN_H = 6
HD = 64
GDN_QKV = 3 * GDN_H * HD
GDN_W = GDN_H * HD
ATT_HKV = 2
ATT_G = 3
ATT_W = ATT_HKV * ATT_G * HD
ATT_KW = ATT_HKV * HD
S5_G = 16
S5_H = 16
S5_P = 64
S5_W = S5_G * S5_H
S5_N = S5_G * S5_P
IN_COLS = 2456
IN_PAD = 2560
ROPE_THETA = 10000.0
ROPE_PAIRS = 16
ALPHA = (2.0 * 4) ** 0.25
EPS = 1e-6
CHUNK = 64
ADAM_LR, ADAM_B1, ADAM_B2, ADAM_EPS, ADAM_WD, ADAM_STEP = 0.001, 0.9, 0.999, 1e-08, 0.01, 10

TT = 256
ATT_TQ = 512
S5_LB = 256
VMEM_LIMIT = 56 * 1024 * 1024

WEIGHTS = ['c_ctx', 'w_ada', 'b_ada', 'ln_g', 'ln_b', 'ffn_w1', 'ffn_w3', 'ffn_w2', 'w_in', 'w_out', 'gdn_conv_w',
           'gdn_a_log', 'gdn_dt_bias', 'gdn_norm_w', 'q_norm_w', 'k_norm_w', 's5_lam_re', 's5_lam_im', 's5_log_dt',
           's5_b_re', 's5_b_im', 's5_c_re', 's5_c_im', 's5_d', 'glu_w', 'glu_b']
INPUTS = ['x', 'c', 'ctx'] + WEIGHTS + ['loss_target'] + ['m_' + n for n in WEIGHTS] + ['v_' + n for n in WEIGHTS]
BIG = ['ffn_w1', 'ffn_w3', 'ffn_w2', 'w_in', 'w_out']
SMALL_SHARDED = ['ln_g', 'ln_b', 'gdn_conv_w', 'glu_w']
SMALL_REPL = ['gdn_a_log', 'gdn_dt_bias', 'gdn_norm_w', 'q_norm_w', 'k_norm_w', 's5_lam_re', 's5_lam_im',
              's5_log_dt', 's5_b_re', 's5_b_im', 's5_c_re', 's5_c_im', 's5_d', 'glu_b']


def _cparams(sem=None):
    return pltpu.CompilerParams(dimension_semantics=sem, vmem_limit_bytes=VMEM_LIMIT)


def _ntok():
    return T_CTX + T_LAT


def _my_pos():
    return lax.axis_index("x"), lax.axis_index("y"), lax.axis_index("c")


def _my_index():
    x, y, c = _my_pos()
    return 4 * x + 2 * y + c


def _all_gather(shard, name):
    def body(x_ref, out_ref, send_sems, recv_sems, local_sem):
        x, y, c = _my_pos()
        me, sibling = (x, y, c), (x, y, 1 - c)
        chips = [(1 - x, y), (x, 1 - y), (1 - x, 1 - y)]

        def slab(px, py, pc):
            return out_ref.at[4 * px + 2 * py + pc]

        def copy(k, block, to, src=None):
            return pltpu.make_async_remote_copy(
                src_ref=slab(*block) if src is None else src, dst_ref=slab(*block),
                send_sem=send_sems.at[k], recv_sem=recv_sems.at[k], device_id=to, device_id_type=MESH)

        mine = pltpu.make_async_copy(x_ref, slab(*me), local_sem)
        mine.start()
        first = [copy(0, me, sibling, src=x_ref)]
        first += [copy(1 + j, me, (*chip, c), src=x_ref) for j, chip in enumerate(chips)]
        for cp in first:
            cp.start()
        passed = [copy(4 + j, (*chip, c), sibling) for j, chip in enumerate(chips)]
        for j, chip in enumerate(chips):
            copy(1 + j, (*chip, c), me).wait_recv()
            passed[j].start()
        copy(0, sibling, me).wait_recv()
        for j, chip in enumerate(chips):
            copy(4 + j, (*chip, 1 - c), me).wait_recv()
        for cp in first + passed:
            cp.wait_send()
        mine.wait()

    return pl.pallas_call(
        body, name=name,
        out_shape=jax.ShapeDtypeStruct((NDEV,) + shard.shape, shard.dtype),
        in_specs=[pl.BlockSpec(memory_space=pl.ANY)],
        out_specs=pl.BlockSpec(memory_space=pl.ANY),
        scratch_shapes=[pltpu.SemaphoreType.DMA((7,)), pltpu.SemaphoreType.DMA((7,)), pltpu.SemaphoreType.DMA],
    )(shard)


def _all_to_all(pieces, name):
    def body(x_ref, out_ref, send_sems, recv_sems, local_sem):
        x, y, c = _my_pos()
        me_i = 4 * x + 2 * y + c
        mine = pltpu.make_async_copy(x_ref.at[me_i], out_ref.at[me_i], local_sem)
        mine.start()
        sends, recvs = [], []
        for k in range(1, NDEV):
            px = 1 - x if (k >> 2) & 1 else x
            py = 1 - y if (k >> 1) & 1 else y
            pc = 1 - c if k & 1 else c
            peer_i = 4 * px + 2 * py + pc
            sends.append(pltpu.make_async_remote_copy(
                src_ref=x_ref.at[peer_i], dst_ref=out_ref.at[me_i], send_sem=send_sems.at[k - 1],
                recv_sem=recv_sems.at[k - 1], device_id=(px, py, pc), device_id_type=MESH))
            recvs.append(pltpu.make_async_remote_copy(
                src_ref=x_ref.at[peer_i], dst_ref=out_ref.at[peer_i], send_sem=send_sems.at[k - 1],
                recv_sem=recv_sems.at[k - 1], device_id=(px, py, pc), device_id_type=MESH))
        for cp in sends:
            cp.start()
        for cp in recvs:
            cp.wait_recv()
        for cp in sends:
            cp.wait_send()
        mine.wait()

    return pl.pallas_call(
        body, name=name,
        out_shape=jax.ShapeDtypeStruct(pieces.shape, pieces.dtype),
        in_specs=[pl.BlockSpec(memory_space=pl.ANY)],
        out_specs=pl.BlockSpec(memory_space=pl.ANY),
        scratch_shapes=[pltpu.SemaphoreType.DMA((7,)), pltpu.SemaphoreType.DMA((7,)), pltpu.SemaphoreType.DMA],
    )(pieces)


def _scatter_pair(pieces, name):
    def body(x_ref, out_ref, send_sems, recv_sems):
        x, y, c = _my_pos()
        copies = [pltpu.make_async_remote_copy(
            src_ref=x_ref.at[2 * q + (1 - c)], dst_ref=out_ref.at[q], send_sem=send_sems.at[q],
            recv_sem=recv_sems.at[q], device_id=(x, y, 1 - c), device_id_type=MESH) for q in range(4)]
        for cp in copies:
            cp.start()
        for cp in copies:
            cp.wait_recv()
        for cp in copies:
            cp.wait_send()

    return pl.pallas_call(
        body, name=name, out_shape=jax.ShapeDtypeStruct((4,) + pieces.shape[1:], pieces.dtype),
        in_specs=[pl.BlockSpec(memory_space=pl.ANY)], out_specs=pl.BlockSpec(memory_space=pl.ANY),
        scratch_shapes=[pltpu.SemaphoreType.DMA((4,)), pltpu.SemaphoreType.DMA((4,))],
    )(pieces)


def _pair_sum(pieces, from_sibling, name):
    _, r, c_ = pieces.shape
    tr = _pick(r, (512, 256, 128, 64, 32, 16))

    def body(p_ref, s_ref, o_ref):
        c = lax.axis_index("c")
        o_ref[...] = (p_ref[c].astype(F32) + s_ref[...].astype(F32)).astype(o_ref.dtype)
    return pl.pallas_call(
        body, name=name, grid=(4, r // tr),
        in_specs=[pl.BlockSpec((None, 2, tr, c_), lambda q, i: (q, 0, i, 0)), pl.BlockSpec((None, tr, c_), lambda q, i: (q, i, 0))],
        out_specs=pl.BlockSpec((None, tr, c_), lambda q, i: (q, i, 0)), out_shape=_sds((4, r, c_), pieces.dtype),
        compiler_params=_cparams(("arbitrary",) * 2))(pieces.reshape(4, 2, r, c_), from_sibling)


def _scatter_chip(sums, name):
    def body(x_ref, out_ref, send_sems, recv_sems, local_sem):
        x, y, c = _my_pos()
        myq = 2 * x + y
        mine = pltpu.make_async_copy(x_ref.at[myq], out_ref.at[myq], local_sem)
        mine.start()
        sends, recvs = [], []
        for j, (qx, qy) in enumerate([(1 - x, y), (x, 1 - y), (1 - x, 1 - y)]):
            q = 2 * qx + qy
            sends.append(pltpu.make_async_remote_copy(
                src_ref=x_ref.at[q], dst_ref=out_ref.at[myq], send_sem=send_sems.at[j], recv_sem=recv_sems.at[j],
                device_id=(qx, qy, c), device_id_type=MESH))
            recvs.append(pltpu.make_async_remote_copy(
                src_ref=x_ref.at[q], dst_ref=out_ref.at[q], send_sem=send_sems.at[j], recv_sem=recv_sems.at[j],
                device_id=(qx, qy, c), device_id_type=MESH))
        for cp in sends:
            cp.start()
        for cp in recvs:
            cp.wait_recv()
        for cp in sends:
            cp.wait_send()
        mine.wait()

    return pl.pallas_call(
        body, name=name, out_shape=jax.ShapeDtypeStruct(sums.shape, sums.dtype),
        in_specs=[pl.BlockSpec(memory_space=pl.ANY)], out_specs=pl.BlockSpec(memory_space=pl.ANY),
        scratch_shapes=[pltpu.SemaphoreType.DMA((3,)), pltpu.SemaphoreType.DMA((3,)), pltpu.SemaphoreType.DMA],
    )(sums)


def _dims(ta, tb):
    return (((0 if ta else 1,), (1 if tb else 0,)), ((), ()))


def _raw_dot(a, b, ta, tb):
    return lax.dot_general(a, b, _dims(ta, tb), preferred_element_type=F32)


def _split2(x):
    hi = x.astype(BF16)
    return hi, (x - hi.astype(F32)).astype(BF16)


def _split3(x):
    hi = x.astype(BF16)
    r = x - hi.astype(F32)
    mid = r.astype(BF16)
    return hi, mid, (r - mid.astype(F32)).astype(BF16)


def _dot_impl(a, b, ta, tb, prec):
    if prec == "bf16":
        return _raw_dot(a.astype(BF16), b.astype(BF16), ta, tb)
    if prec == "bx3":
        bb = b.astype(BF16)
        h, m, l = _split3(a)
        return _raw_dot(h, bb, ta, tb) + (_raw_dot(m, bb, ta, tb) + _raw_dot(l, bb, ta, tb))
    if prec == "ax3":
        ab = a.astype(BF16)
        h, m, l = _split3(b)
        return _raw_dot(ab, h, ta, tb) + (_raw_dot(ab, m, ta, tb) + _raw_dot(ab, l, ta, tb))
    ka, kb = (0 if ta else 1), (1 if tb else 0)
    ah, al = _split2(a)
    if prec == "bx":
        bb = b.astype(BF16)
        return _raw_dot(jnp.concatenate([ah, al], axis=ka), jnp.concatenate([bb, bb], axis=kb), ta, tb)
    bh, bl = _split2(b)
    return _raw_dot(ah, bh, ta, tb) + (_raw_dot(ah, bl, ta, tb) + _raw_dot(al, bh, ta, tb))


@functools.lru_cache(maxsize=None)
def _mm_fn(ta, tb, prec):
    @jax.custom_vjp
    def mm(a, b):
        return _dot_impl(a, b, ta, tb, prec)

    def fwd(a, b):
        return mm(a, b), (a, b)

    def bwd(res, dc):
        a, b = res
        bprec = "f32" if prec == "f32" else "bf16"
        if prec in ("bx", "bx3"):
            assert not ta
            return _mm_fn(False, not tb, prec)(dc, b).astype(a.dtype), jnp.zeros_like(b)
        if prec == "ax3":
            assert not tb
            return jnp.zeros_like(a), _mm_fn(not ta, False, prec)(a, dc).astype(b.dtype)
        da = _mm_fn(tb, True, bprec)(b, dc) if ta else _mm_fn(False, not tb, bprec)(dc, b)
        db = _mm_fn(True, ta, bprec)(dc, a) if tb else _mm_fn(not ta, False, bprec)(a, dc)
        return da.astype(a.dtype), db.astype(b.dtype)

    mm.defvjp(fwd, bwd)
    return mm


def _mm(a, b, ta=False, tb=False, prec="bf16"):
    return _mm_fn(ta, tb, prec)(a, b)


@functools.lru_cache(maxsize=None)
def _shift_fn(k):
    @jax.custom_vjp
    def shift(x):
        n = x.shape[0]
        r = pltpu.roll(x, (-k) % n, 0)
        t = lax.broadcasted_iota(jnp.int32, x.shape, 0)
        ok = (t + k >= 0) & (t + k < n)
        return jnp.where(ok, r, 0.0)

    shift.defvjp(lambda x: (shift(x), None), lambda _, dy: (_shift_fn(-k)(dy),))
    return shift


def _sigmoid(x):
    return 1.0 / (1.0 + jnp.exp(-x))


@jax.custom_vjp
def _softplus(x):
    y = jnp.exp(-jnp.abs(x))
    u = 1.0 + y
    l1p = jnp.where(u == 1.0, y, jnp.log(u) * y / jnp.where(u == 1.0, 1.0, u - 1.0))
    return jnp.maximum(x, 0.0) + l1p


_softplus.defvjp(lambda x: (_softplus(x), x), lambda x, dy: (dy * _sigmoid(x),))


def _silu(x):
    return x * _sigmoid(x)


def _gelu_tanh(x):
    return 0.5 * x * (1.0 + jnp.tanh(math.sqrt(2.0 / math.pi) * (x + 0.044715 * (x * x * x))))


def _block_op(name, f, grid, in_specs, out_specs, out_shapes, diff, acc=None, n_res=0, f_bwd=None):
    n_in, n_all = len(in_specs), len(out_specs)
    n_out = n_all - n_res
    acc = acc or [None] * n_in
    didx = [i for i in range(n_in) if diff[i]]
    sem = ("arbitrary",) * len(grid)
    fb = f_bwd or f

    def run_fwd(*xs):
        def body(*refs):
            outs = f(*[r[...] for r in refs[:n_in]])
            for r, o in zip(refs[n_in:], outs):
                r[...] = o.astype(r.dtype)
        return pl.pallas_call(body, name=name + "_fwd", grid=grid, in_specs=in_specs, out_specs=out_specs,
                              out_shape=out_shapes, compiler_params=_cparams(sem))(*xs)

    def run_bwd(xs, res, douts):
        def body(*refs):
            ins = [r[...] for r in refs[:n_in]]
            ress = [r[...] for r in refs[n_in:n_in + n_res]]
            dos = [r[...] for r in refs[n_in + n_res:n_in + n_all]]

            def g(*dv):
                full = list(ins)
                for i, v in zip(didx, dv):
                    full[i] = v
                return tuple(fb(*full, *ress))

            outs, vjp = jax.vjp(g, *[ins[i] for i in didx])
            dins = vjp(tuple(d.astype(o.dtype) for d, o in zip(dos, outs)))
            for r, i, dv in zip(refs[n_in + n_all:], didx, dins):
                dv = dv.astype(r.dtype)
                if acc[i] is None:
                    r[...] = dv
                else:
                    if acc[i] == 'last':
                        first = pl.program_id(len(grid) - 1) == 0
                    else:
                        first = functools.reduce(jnp.logical_and, [pl.program_id(a) == 0 for a in range(len(grid))])

                    @pl.when(first)
                    def _(r=r, dv=dv):
                        r[...] = dv

                    @pl.when(jnp.logical_not(first))
                    def _(r=r, dv=dv):
                        r[...] += dv
        return pl.pallas_call(
            body, name=name + "_bwd", grid=grid,
            in_specs=list(in_specs) + list(out_specs[n_out:]) + list(out_specs[:n_out]),
            out_specs=[in_specs[i] for i in didx],
            out_shape=[jax.ShapeDtypeStruct(xs[i].shape, xs[i].dtype) for i in didx],
            compiler_params=_cparams(sem))(*xs, *res, *douts)

    @jax.custom_vjp
    def op(*xs):
        return tuple(run_fwd(*xs))[:n_out]

    def op_fwd(*xs):
        outs = tuple(run_fwd(*xs))
        return outs[:n_out], (xs, outs[n_out:])

    def op_bwd(saved, douts):
        xs, res = saved
        dins = run_bwd(xs, res, douts)
        full = [jnp.zeros_like(x) for x in xs]
        for i, dv in zip(didx, dins):
            full[i] = dv
        return tuple(full)

    op.defvjp(op_fwd, op_bwd)
    op.run_bwd = run_bwd
    return op


def _sds(shape, dtype):
    return jax.ShapeDtypeStruct(tuple(shape), dtype)


def _pick(n, cands):
    for c in cands:
        if n % c == 0:
            return c
    return n


def _mm_call(name, a, b, mode, out_dtype):
    if mode == "tn":
        m, k = a.shape
        n = b.shape[1]
        tm = _pick(m, (512, 256, 128, 64))
        tn = _pick(n, (1408, 1280, 1152, 1024, 512, 256, 128))
        steps = m // tm

        def body(a_ref, b_ref, o_ref, acc_ref):
            i = pl.program_id(1)

            @pl.when(i == 0)
            def _():
                acc_ref[...] = jnp.zeros_like(acc_ref)

            acc_ref[...] += _raw_dot(a_ref[...].astype(BF16), b_ref[...].astype(BF16), True, False)

            @pl.when(i == steps - 1)
            def _():
                o_ref[...] = acc_ref[...].astype(o_ref.dtype)

        return pl.pallas_call(
            body, name=name, grid=(n // tn, steps),
            in_specs=[pl.BlockSpec((tm, k), lambda j, i: (i, 0)), pl.BlockSpec((tm, tn), lambda j, i: (i, j))],
            out_specs=pl.BlockSpec((k, tn), lambda j, i: (0, j)),
            out_shape=_sds((k, n), out_dtype),
            scratch_shapes=[pltpu.VMEM((k, tn), F32)],
            compiler_params=_cparams(("arbitrary", "arbitrary")))(a, b)

    m, k = a.shape
    n = b.shape[1] if mode == "nn" else b.shape[0]
    tm = _pick(m, (512, 256, 128, 64))
    tn = _pick(n, (1408, 1280, 1152, 1024, 512, 256, 128))

    def body(a_ref, b_ref, o_ref):
        o_ref[...] = _raw_dot(a_ref[...].astype(BF16), b_ref[...].astype(BF16), False, mode == "nt").astype(o_ref.dtype)

    b_spec = pl.BlockSpec((k, tn), lambda j, i: (0, j)) if mode == "nn" else pl.BlockSpec((tn, k), lambda j, i: (j, 0))
    return pl.pallas_call(
        body, name=name, grid=(n // tn, m // tm),
        in_specs=[pl.BlockSpec((tm, k), lambda j, i: (i, 0)), b_spec],
        out_specs=pl.BlockSpec((tm, tn), lambda j, i: (i, j)),
        out_shape=_sds((m, n), out_dtype),
        compiler_params=_cparams(("arbitrary", "arbitrary")))(a, b)


def _matmul(name, a, w, out_dtype):
    @jax.custom_vjp
    def mm(a, w):
        return _mm_call(name + "_nn", a, w, "nn", out_dtype)

    def fwd(a, w):
        return mm(a, w), (a, w)

    def bwd(res, dy):
        a, w = res
        return (_mm_call(name + "_nt", dy, w, "nt", a.dtype), _mm_call(name + "_tn", a, dy, "tn", w.dtype))

    mm.defvjp(fwd, bwd)
    return mm(a, w)


def _matmul_t(name, a, wt, out_dtype):
    @jax.custom_vjp
    def mm(a, wt):
        return _mm_call(name + "_nt", a, wt, "nt", out_dtype)

    def fwd(a, wt):
        return mm(a, wt), (a, wt)

    def bwd(res, dy):
        a, wt = res
        return (_mm_call(name + "_nn", dy, wt, "nn", a.dtype), _mm_call(name + "_tn", dy, a, "tn", wt.dtype))

    mm.defvjp(fwd, bwd)
    return mm(a, wt)


def _tok(width):
    return pl.BlockSpec((None, TT, width), lambda b, t: (b, t, 0))


def _row(width):
    return pl.BlockSpec((None, None, 1, width), lambda b, t: (b, t, 0, 0))


def _const2(shape):
    return pl.BlockSpec(shape, lambda b, t: (0,) * len(shape))


def _tok_grid():
    return (B_LOC, _ntok() // TT)


def _modulate(x, shift, scale):
    def f(x, sh, sc):
        return ((x * (1.0 + sc) + sh),)
    op = _block_op("modulate", f, _tok_grid(), [_tok(D), _row(D), _row(D)], [_tok(D)],
                   [_sds(x.shape, BF16)], [True, True, True])
    return op(x, shift, scale)[0]


def _post_norm(x, y, gate, g, b, res_w):
    def f(x, y, gate, g, b):
        z = ALPHA * x + res_w * gate * y
        mu = jnp.mean(z, axis=-1, keepdims=True)
        zc = z - mu
        var = jnp.mean(zc * zc, axis=-1, keepdims=True)
        return (zc * lax.rsqrt(var + EPS) * g + b,)
    op = _block_op("post_norm", f, _tok_grid(), [_tok(D), _tok(D), _row(D), _row(D), _row(D)], [_tok(D)],
                   [_sds(x.shape, F32)], [True] * 5)
    return op(x, y, gate, g, b)[0]


def _swiglu_gate_op(m):
    tm = _pick(m, (256, 128, 64))

    def f(a, b):
        a = a.astype(F32)
        return (_silu(a) * b.astype(F32),)
    spec = pl.BlockSpec((tm, DFF), lambda i: (i, 0))
    return _block_op("swiglu_gate", f, (m // tm,), [spec, spec], [spec], [_sds((m, DFF), BF16)], [True, True])


def _ffn_up_call(h, w1t, w3t):
    m, k = h.shape
    n = w1t.shape[0]
    tm, tn = _pick(m, (512, 256, 128, 64)), _pick(n, (1408, 1024, 512, 256, 128))

    def body(h_ref, w1_ref, w3_ref, a_ref, b_ref, u_ref):
        hb = h_ref[...]
        a = _raw_dot(hb, w1_ref[...], False, True).astype(BF16)
        b = _raw_dot(hb, w3_ref[...], False, True).astype(BF16)
        a_ref[...] = a
        b_ref[...] = b
        u_ref[...] = (_silu(a.astype(F32)) * b.astype(F32)).astype(BF16)
    ws = pl.BlockSpec((tn, k), lambda j, i: (j, 0))
    os_ = pl.BlockSpec((tm, tn), lambda j, i: (i, j))
    return pl.pallas_call(body, name="ffn_up_gate", grid=(n // tn, m // tm),
                          in_specs=[pl.BlockSpec((tm, k), lambda j, i: (i, 0)), ws, ws], out_specs=[os_] * 3,
                          out_shape=[_sds((m, n), BF16)] * 3, compiler_params=_cparams(("arbitrary",) * 2))(h, w1t, w3t)


def _ffn_dh_call(da, db, w1t, w3t):
    m, n = da.shape
    k = w1t.shape[1]
    tm = _pick(m, (512, 256, 128, 64))

    def body(da_ref, db_ref, w1_ref, w3_ref, o_ref):
        o_ref[...] = (_raw_dot(da_ref[...], w1_ref[...], False, False)
                      + _raw_dot(db_ref[...], w3_ref[...], False, False)).astype(o_ref.dtype)
    xs = pl.BlockSpec((tm, n), lambda i: (i, 0))
    ws = pl.BlockSpec((n, k), lambda i: (0, 0))
    return pl.pallas_call(body, name="ffn_up_dh", grid=(m // tm,), in_specs=[xs, xs, ws, ws],
                          out_specs=pl.BlockSpec((tm, k), lambda i: (i, 0)), out_shape=_sds((m, k), BF16),
                          compiler_params=_cparams(("arbitrary",)))(da, db, w1t, w3t)


@jax.custom_vjp
def _ffn_up(h, w1t, w3t):
    return _ffn_up_call(h, w1t, w3t)[2]


def _ffn_up_f(h, w1t, w3t):
    a, b, u = _ffn_up_call(h, w1t, w3t)
    return u, (h, w1t, w3t, a, b)


def _ffn_up_b(res, du):
    h, w1t, w3t, a, b = res
    da, db = _swiglu_gate_op(h.shape[0]).run_bwd((a, b), (), (du,))
    return (_ffn_dh_call(da, db, w1t, w3t), _mm_call("ffn_up_tn", da, h, "tn", w1t.dtype),
            _mm_call("ffn_up_tn", db, h, "tn", w3t.dtype))


_ffn_up.defvjp(_ffn_up_f, _ffn_up_b)


def _seg_ones(width):
    i = np.arange(width)
    return jnp.asarray((i[:, None] // HD) == (i[None, :] // HD), BF16)


def _rope_perm(width):
    p = np.zeros((width, width), np.float32)
    for j in range(width):
        if (j % 32) < 16:
            p[j + 16, j] = -1.0
        else:
            p[j - 16, j] = 1.0
    return jnp.asarray(p, BF16)


def _rope_tables(width):
    t = jnp.arange(T_LAT)
    pos = jnp.stack([t // GRID_W, t % GRID_W], axis=-1).astype(F32)
    inv_freq = ROPE_THETA ** (-jnp.arange(ROPE_PAIRS, dtype=F32) / ROPE_PAIRS)
    ang = pos[..., None] * inv_freq
    ang = jnp.broadcast_to(ang[:, :, None, :], (T_LAT, 2, 2, ROPE_PAIRS)).reshape(T_LAT, HD)
    ang = jnp.tile(ang, (1, width // HD))
    cos = jnp.concatenate([jnp.ones((T_CTX, width), F32), jnp.cos(ang)], axis=0)
    sin = jnp.concatenate([jnp.zeros((T_CTX, width), F32), jnp.sin(ang)], axis=0)
    return cos, sin


def _att_pre(x, w_row, name):
    width = x.shape[-1]
    cos, sin = _rope_tables(width)

    def f(x, w, cos, sin, seg, perm):
        ms = _mm(x * x, seg, prec="bx") * (1.0 / HD)
        xn = x * lax.rsqrt(ms + EPS) * w
        return (xn * cos + _mm(xn, perm, prec="bx") * sin,)
    tab = pl.BlockSpec((TT, width), lambda b, t: (t, 0))
    op = _block_op(name, f, _tok_grid(),
                   [_tok(width), _row(width), tab, tab, _const2((width, width)), _const2((width, width))],
                   [_tok(width)], [_sds(x.shape, F32)], [True, True, False, False, False, False])
    return op(x, w_row, cos, sin, _seg_ones(width), _rope_perm(width))[0]


def _attention(q, k, v, name, tq):
    b_, hk, g_, tq_all, _ = q.shape
    tk = k.shape[2]

    def f(q, k, v):
        outs = []
        for gi in range(g_):
            s = _mm(q[gi] * (HD ** -0.5), k, tb=True)
            m = lax.stop_gradient(jnp.max(s, axis=-1, keepdims=True))
            e = jnp.exp(s - m)
            outs.append(_mm(e, v) / jnp.sum(e, axis=-1, keepdims=True))
        return (jnp.stack(outs, axis=0),)
    qs = pl.BlockSpec((None, None, g_, tq, HD), lambda b, h, i: (b, h, 0, i, 0))
    ks = pl.BlockSpec((None, None, tk, HD), lambda b, h, i: (b, h, 0, 0))
    op = _block_op(name, f, (b_, hk, tq_all // tq), [qs, ks, ks], [qs], [_sds(q.shape, F32)],
                   [True, True, True], acc=[None, 'last', 'last'])
    return op(q, k, v)[0]


def _gdn_pre(qkv, conv_w):
    nt_c = GDN_QKV // 128
    flag = jnp.asarray((np.arange(nt_c) % 3 < 2).astype(np.float32)[:, None, None] * np.ones((1, 1, 128), np.float32))
    cw = jnp.broadcast_to(conv_w[None], (B_LOC,) + conv_w.shape)

    def f(x, cw, flag, seg):
        def conv(s):
            acc = cw[2:3, :] * s
            for j in (0, 1, 3, 4):
                acc = acc + cw[j:j + 1, :] * _shift_fn(j - 2)(s)
            return acc
        y = jnp.concatenate([conv(x[:T_CTX]), conv(x[T_CTX:])], axis=0)
        s = _silu(y)
        ss = _mm(s * s, seg, prec="bx")
        return (s * (flag * lax.rsqrt(ss + EPS) + (1.0 - flag)),)
    xs = pl.BlockSpec((None, _ntok(), 128), lambda b, j: (b, 0, j))
    op = _block_op("gdn_pre", f, (B_LOC, nt_c),
                   [xs, pl.BlockSpec((None, 5, 128), lambda b, j: (b, 0, j)),
                    pl.BlockSpec((None, 1, 128), lambda b, j: (j, 0, 0)), pl.BlockSpec((128, 128), lambda b, j: (0, 0))],
                   [xs], [_sds(qkv.shape, F32)], [True, True, False, False])
    return op(qkv, cw, flag, _seg_ones(128))[0]


def _gdn_gates(ba, a_log, dt_bias):
    pad = jnp.zeros((12,), F32)
    al = jnp.broadcast_to(jnp.concatenate([pad, a_log.reshape(12), jnp.zeros((104,), F32)])[None, None], (B_LOC, 1, 128))
    db = jnp.broadcast_to(jnp.concatenate([pad, dt_bias.reshape(12), jnp.zeros((104,), F32)])[None, None], (B_LOC, 1, 128))

    def f(x, al, db):
        lane = lax.broadcasted_iota(jnp.int32, x.shape, 1)
        return (jnp.where(lane < 12, _sigmoid(x), -jnp.exp(al) * _softplus(x + db)),)
    xs = pl.BlockSpec((None, _ntok(), 128), lambda b: (b, 0, 0))
    ps = pl.BlockSpec((None, 1, 128), lambda b: (b, 0, 0))
    op = _block_op("gdn_gates", f, (B_LOC,), [xs, ps, ps], [xs], [_sds(ba.shape, F32)], [True, True, True])
    return op(ba, al, db)[0]


def _unit_triangular_inverses(lowers):
    n = lowers[0].shape[0]
    eye = (lax.broadcasted_iota(jnp.int32, (n, n), 0) == lax.broadcasted_iota(jnp.int32, (n, n), 1)).astype(F32)
    nks = [-l for l in lowers]
    invs = [eye + nk for nk in nks]
    for _ in range(int(math.log2(n)) - 1):
        nks = [_dot_impl(nk, nk, False, False, "f32") for nk in nks]
        invs = [inv + _dot_impl(inv, nk, False, False, "f32") for inv, nk in zip(invs, nks)]
    return tuple(invs)


@jax.custom_vjp
def _solve_with_inverses(lowers, rhss, invs):
    return tuple(_dot_impl(inv, rhs, False, False, "f32") for inv, rhs in zip(invs, rhss))


def _solve_fwd(lowers, rhss, invs):
    sols = _solve_with_inverses(lowers, rhss, invs)
    return sols, (invs, sols)


def _solve_bwd(res, dsols):
    invs, sols = res
    drhss = tuple(_dot_impl(inv, d, True, False, "f32") for inv, d in zip(invs, dsols))
    dlowers = tuple(-_dot_impl(dr, s, False, True, "f32") for dr, s in zip(drhss, sols))
    return dlowers, drhss, tuple(jnp.zeros_like(inv) for inv in invs)


_solve_with_inverses.defvjp(_solve_fwd, _solve_bwd)


def _gdn_masks():
    ii, jj = np.arange(CHUNK)[:, None], np.arange(CHUNK)[None, :]
    fwd = [jj <= ii, jj < ii, ii <= jj]
    bwd = [jj >= ii, jj > ii, ii >= jj]
    return jnp.asarray(np.stack([np.stack(fwd), np.stack(bwd)]).astype(np.float32))


def _gdn_prep(qkv, g, beta):
    b_, t_, _ = qkv.shape
    nc = t_ // CHUNK
    cb = max(d for d in (1, 2, 3, 4, 6) if nc % d == 0)
    npair = GDN_H // 2

    def f(x, g, beta, masks, saved_inv=None):
        q2, k2, v2 = x[:, :2 * HD], x[:, 2 * HD:4 * HD], x[:, 4 * HD:]
        ii = lax.broadcasted_iota(jnp.int32, (CHUNK, CHUNK), 0)
        jj = lax.broadcasted_iota(jnp.int32, (CHUNK, CHUNK), 1)
        eye = (ii == jj).astype(F32)
        incl, strict, incl_t = masks[0] > 0.5, masks[1] > 0.5, masks[2] > 0.5
        items = [(hh, c) for hh in range(2) for c in range(cb)]
        def sl(a, hh, c):
            return a[c * CHUNK:(c + 1) * CHUNK, hh * HD:(hh + 1) * HD]
        qs = [sl(q2, hh, c) * (HD ** -0.5) for hh, c in items]
        ks = [sl(k2, hh, c) for hh, c in items]
        vs = [sl(v2, hh, c) for hh, c in items]
        ones = jnp.ones((CHUNK, CHUNK), F32)
        lane_sum = lambda m: _mm(m, ones, prec="bx3")
        row_sum = lambda m: _mm(ones, m, prec="ax3")
        g_rows = [jnp.broadcast_to(g[hh, c], (CHUNK, CHUNK)) for hh, c in items]
        b_rows = [jnp.broadcast_to(beta[hh, c], (CHUNK, CHUNK)) for hh, c in items]
        g_cols = [lane_sum(eye * gr) for gr in g_rows]
        b_cols = [lane_sum(eye * br) for br in b_rows]
        gc_cols = [lane_sum(jnp.where(incl, gr, 0.0)) for gr in g_rows]
        gc_rows = [row_sum(jnp.where(incl_t, gc, 0.0)) for gc in g_cols]
        g_tots = [lane_sum(gr) for gr in g_rows]
        decays = [jnp.where(incl, jnp.exp(jnp.where(incl, gcc - gcr, 0.0)), 0.0) for gcc, gcr in zip(gc_cols, gc_rows)]
        e_cols = [jnp.exp(gcc) for gcc in gc_cols]
        kbs = [kc * bc for kc, bc in zip(ks, b_cols)]
        rhss = [jnp.concatenate([vc * bc, kb * ec], axis=1) for vc, bc, kb, ec in zip(vs, b_cols, kbs, e_cols)]
        qgs = [qc * ec for qc, ec in zip(qs, e_cols)]
        kds = [kc * jnp.exp(gt - gcc) for kc, gt, gcc in zip(ks, g_tots, gc_cols)]
        egs = [jnp.exp(gt)[0:1, :] for gt in g_tots]
        lowers = tuple(jnp.where(strict, _mm(kb, kc, tb=True) * dec, 0.0) for kb, kc, dec in zip(kbs, ks, decays))
        ins = [jnp.where(incl, _mm(qc, kc, tb=True) * dec, 0.0) for qc, kc, dec in zip(qs, ks, decays)]
        if saved_inv is None:
            invs = _unit_triangular_inverses(lowers)
        else:
            invs = tuple(saved_inv[hh, c * CHUNK:(c + 1) * CHUNK] for hh, c in items)
        sols = _solve_with_inverses(lowers, tuple(rhss), invs)
        us, ws = [s[:, :HD] for s in sols], [s[:, HD:] for s in sols]

        def heads(xs, joiner):
            return jnp.stack([joiner(xs[:cb]), joiner(xs[cb:])], axis=0)
        cat = lambda xs: jnp.concatenate(xs, axis=0)
        outs = (heads(us, cat), heads(ws, cat), heads(qgs, cat), heads(kds, cat), heads(ins, cat),
                heads(egs, lambda xs: jnp.stack(xs, axis=0)))
        return outs if saved_inv is not None else outs + (heads(list(invs), cat),)

    xs = pl.BlockSpec((None, cb * CHUNK, 6 * HD), lambda b, p, i, d: (b, i, p))
    rs = pl.BlockSpec((None, None, 2, None, cb, 1, CHUNK), lambda b, p, i, d: (b, p, 0, d, i, 0, 0))
    ts = pl.BlockSpec((None, None, 2, None, cb * CHUNK, HD), lambda b, p, i, d: (b, p, 0, d, i, 0))
    ms = pl.BlockSpec((None, 3, CHUNK, CHUNK), lambda b, p, i, d: (d, 0, 0, 0))
    big = _sds((b_, npair, 2, 2, t_, HD), F32)
    op = _block_op("gdn_prep", f, (b_, npair, nc // cb, 2), [xs, rs, rs, ms],
                   [ts, ts, ts, ts, ts, rs, ts], [big, big, big, big, big, _sds(g.shape, F32), big],
                   [True, True, True, False], acc=['last', None, None, None], n_res=1, f_bwd=f)
    return op(qkv, g, beta, _gdn_masks())


def _gdn_scan_specs(t_, backward):
    seg_c = T_CTX // CHUNK
    nseg = t_ // T_CTX

    def seg_of(d, s):
        s = nseg - 1 - s if backward else s
        return jnp.where(d == 0, s, jnp.where(s == 0, 0, nseg - s))
    ts = pl.BlockSpec((None, GDN_H // 2, 2, None, T_CTX, HD), lambda b, d, s: (b, 0, 0, d, seg_of(d, s), 0))
    es = pl.BlockSpec((None, GDN_H // 2, 2, None, seg_c, 1, CHUNK), lambda b, d, s: (b, 0, 0, d, seg_of(d, s), 0, 0))
    return ts, es, seg_c, nseg


def _gdn_scan_call(u, w, qg, kd, intra, eg):
    b_, t_ = u.shape[0], u.shape[4]
    ts, es, seg_c, nseg = _gdn_scan_specs(t_, False)
    heads = [(p, hh) for p in range(GDN_H // 2) for hh in range(2)]

    def body(u_ref, w_ref, qg_ref, kd_ref, in_ref, eg_ref, o_ref, st_ref, state):
        @pl.when(pl.program_id(2) == 0)
        def _():
            state[...] = jnp.zeros_like(state)
        d = pl.program_id(1)
        for i in range(seg_c):
            c = jnp.where(d == 0, i, seg_c - 1 - i)
            rows = pl.ds(pl.multiple_of(c * CHUNK, CHUNK), CHUNK)
            sts = [state[n] for n in range(len(heads))]
            for n, (p, hh) in enumerate(heads):
                st_ref[p, hh, rows, :] = sts[n]
            sbs = [st.astype(BF16) for st in sts]
            ws = [_raw_dot(w_ref[p, hh, rows, :].astype(BF16), sbs[n], False, False) for n, (p, hh) in enumerate(heads)]
            qss = [_raw_dot(qg_ref[p, hh, rows, :].astype(BF16), sbs[n], False, False) for n, (p, hh) in enumerate(heads)]
            vbs = [(u_ref[p, hh, rows, :] - ws[n]).astype(BF16) for n, (p, hh) in enumerate(heads)]
            for n, (p, hh) in enumerate(heads):
                o_ref[p, hh, rows, :] = qss[n] + _raw_dot(in_ref[p, hh, rows, :].astype(BF16), vbs[n], False, False)
            kvs = [_raw_dot(kd_ref[p, hh, rows, :].astype(BF16), vbs[n], True, False) for n, (p, hh) in enumerate(heads)]
            for n, (p, hh) in enumerate(heads):
                e = eg_ref[p, hh, pl.ds(c, 1), :, :].reshape(1, CHUNK)
                state[n] = sts[n] * e + kvs[n]
    return pl.pallas_call(body, name="gdn_scan_fwd", grid=(b_, 2, nseg), in_specs=[ts] * 5 + [es], out_specs=[ts, ts],
                          out_shape=[_sds(u.shape, F32), _sds(u.shape, F32)],
                          scratch_shapes=[pltpu.VMEM((GDN_H, HD, HD), F32)],
                          compiler_params=_cparams(("arbitrary",) * 3))(u, w, qg, kd, intra, eg)


def _gdn_scan_bwd_call(u, w, qg, kd, intra, eg, states, do):
    b_, t_ = u.shape[0], u.shape[4]
    ts, es, seg_c, nseg = _gdn_scan_specs(t_, True)
    heads = [(p, hh) for p in range(GDN_H // 2) for hh in range(2)]

    def body(u_ref, w_ref, qg_ref, kd_ref, in_ref, eg_ref, st_ref, do_ref, du_ref, dw_ref, dqg_ref, dkd_ref, din_ref,
             deg_ref, dstate):
        @pl.when(pl.program_id(2) == 0)
        def _():
            dstate[...] = jnp.zeros_like(dstate)
        d = pl.program_id(1)
        hs = list(enumerate(heads))
        for i in range(seg_c):
            c = jnp.where(d == 0, seg_c - 1 - i, i)
            rows = pl.ds(pl.multiple_of(c * CHUNK, CHUNK), CHUNK)
            dss = [dstate[n] for n, _ in hs]
            sts = [st_ref[p, hh, rows, :] for _, (p, hh) in hs]
            sbs = [st.astype(BF16) for st in sts]
            dsbs = [ds.astype(BF16) for ds in dss]
            wbs = [w_ref[p, hh, rows, :].astype(BF16) for _, (p, hh) in hs]
            dobs = [do_ref[p, hh, rows, :].astype(BF16) for _, (p, hh) in hs]
            kdbs = [kd_ref[p, hh, rows, :].astype(BF16) for _, (p, hh) in hs]
            vbs = [(u_ref[p, hh, rows, :] - _raw_dot(wbs[n], sbs[n], False, False)).astype(BF16) for n, (p, hh) in hs]
            dvns = [_raw_dot(in_ref[p, hh, rows, :].astype(BF16), dobs[n], True, False)
                    + _raw_dot(kdbs[n], dsbs[n], False, False) for n, (p, hh) in hs]
            dvbs = [dvn.astype(BF16) for dvn in dvns]
            for n, (p, hh) in hs:
                din_ref[p, hh, rows, :] = _raw_dot(dobs[n], vbs[n], False, True)
                dqg_ref[p, hh, rows, :] = _raw_dot(dobs[n], sbs[n], False, True)
                dkd_ref[p, hh, rows, :] = _raw_dot(vbs[n], dsbs[n], False, True)
                du_ref[p, hh, rows, :] = dvns[n]
                dw_ref[p, hh, rows, :] = -_raw_dot(dvbs[n], sbs[n], False, True)
                deg_ref[p, hh, pl.ds(c, 1), :, :] = jnp.sum(sts[n] * dss[n], axis=0, keepdims=True).reshape(1, 1, CHUNK)
            upd = [_raw_dot(qg_ref[p, hh, rows, :].astype(BF16), dobs[n], True, False)
                   - _raw_dot(wbs[n], dvbs[n], True, False) for n, (p, hh) in hs]
            for n, (p, hh) in hs:
                e = eg_ref[p, hh, pl.ds(c, 1), :, :].reshape(1, CHUNK)
                dstate[n] = dss[n] * e + upd[n]
    big = _sds(u.shape, F32)
    return pl.pallas_call(body, name="gdn_scan_bwd", grid=(b_, 2, nseg), in_specs=[ts] * 5 + [es, ts, ts],
                          out_specs=[ts] * 5 + [es], out_shape=[big] * 5 + [_sds(eg.shape, F32)],
                          scratch_shapes=[pltpu.VMEM((GDN_H, HD, HD), F32)],
                          compiler_params=_cparams(("arbitrary",) * 3))(u, w, qg, kd, intra, eg, states, do)


@jax.custom_vjp
def _gdn_scan(u, w, qg, kd, intra, eg):
    return _gdn_scan_call(u, w, qg, kd, intra, eg)[0]


def _gdn_scan_f(u, w, qg, kd, intra, eg):
    o, states = _gdn_scan_call(u, w, qg, kd, intra, eg)
    return o, (u, w, qg, kd, intra, eg, states)


def _gdn_scan_b(res, do):
    return tuple(_gdn_scan_bwd_call(*res, do))


_gdn_scan.defvjp(_gdn_scan_f, _gdn_scan_b)


def _gdn_post(o, z, w_row):
    def f(o, z, w, seg):
        ms = _mm(o * o, seg, prec="bx") * (1.0 / HD)
        return (o * lax.rsqrt(ms + EPS) * w * _silu(z),)
    op = _block_op("gdn_post", f, _tok_grid(), [_tok(GDN_W), _tok(GDN_W), _row(GDN_W), _const2((GDN_W, GDN_W))],
                   [_tok(GDN_W)], [_sds(o.shape, BF16)], [True, True, True, False])
    return op(o, z, w_row, _seg_ones(GDN_W))[0]


def _s5_tables(ar, ai, rev):
    pr, pi = [ar], [ai]
    for _ in range(7):
        pr, pi = pr + [pr[-1] * ar - pi[-1] * ai], pi + [pr[-1] * ai + pi[-1] * ar]
    if rev:
        pr, pi = pr[::-1], pi[::-1]
    return jnp.concatenate(pr, axis=0), jnp.concatenate(pi, axis=0)


def _s5_scan_call(bu, ar, ai, direction, h=None):
    b_, t_, n2 = bu.shape
    nblk = n2 // (2 * S5_LB)
    with_grad = h is not None
    rev = (direction == 1) != with_grad
    tr, ti = _s5_tables(ar, ai, rev)
    tab = jnp.concatenate([tr.reshape(8, nblk, 1, S5_LB), ti.reshape(8, nblk, 1, S5_LB)], axis=2).reshape(8, n2)
    ntile, ntc = t_ // 8, T_CTX // 8

    def path(j):
        return j if direction == 0 else jnp.where(j < ntc, ntc - 1 - j, ntile + ntc - 1 - j)

    def scan_tile(xr, xi, tabr, tabi, cr, ci):
        row = lax.broadcasted_iota(jnp.int32, xr.shape, 0)
        for k in (1, 2, 4):
            idx = (8 - k) if rev else (k - 1)
            akr, aki = tabr[idx:idx + 1, :], tabi[idx:idx + 1, :]
            sh = (8 - k) if rev else k
            sr, si = pltpu.roll(xr, sh, 0), pltpu.roll(xi, sh, 0)
            ok = (row < 8 - k) if rev else (row >= k)
            xr, xi = (xr + jnp.where(ok, akr * sr - aki * si, 0.0), xi + jnp.where(ok, akr * si + aki * sr, 0.0))
        return xr + tabr * cr - tabi * ci, xi + tabr * ci + tabi * cr

    def body(*refs):
        if with_grad:
            bu_ref, tab_ref, h_ref, o_ref, da_ref = refs
        else:
            bu_ref, tab_ref, o_ref = refs
        tabr, tabi = tab_ref[:, :S5_LB], tab_ref[:, S5_LB:]
        zero = jnp.zeros((1, S5_LB), F32)
        row = lax.broadcasted_iota(jnp.int32, (8, S5_LB), 0)

        def step(i, carry):
            cr, ci = carry[0], carry[1]
            j = (ntile - 1 - i) if with_grad else i
            rows = pl.ds(pl.multiple_of(path(j) * 8, 8), 8)
            hr, hi = scan_tile(bu_ref[rows, :S5_LB], bu_ref[rows, S5_LB:], tabr, tabi, cr, ci)
            o_ref[rows, :S5_LB] = hr
            o_ref[rows, S5_LB:] = hi
            out = (hr[0:1, :], hi[0:1, :]) if rev else (hr[7:8, :], hi[7:8, :])
            if with_grad:
                prev = pl.ds(pl.multiple_of(path(jnp.maximum(j - 1, 0)) * 8, 8), 8)
                live = jnp.where(j > 0, 1.0, 0.0)
                sh, edge = (1, 0) if direction == 0 else (7, 7)
                pr = jnp.where(row == edge, pltpu.roll(h_ref[prev, :S5_LB], sh, 0) * live, pltpu.roll(h_ref[rows, :S5_LB], sh, 0))
                pi = jnp.where(row == edge, pltpu.roll(h_ref[prev, S5_LB:], sh, 0) * live, pltpu.roll(h_ref[rows, S5_LB:], sh, 0))
                out = out + (carry[2] + hr * pr + hi * pi, carry[3] + hi * pr - hr * pi)
            return out
        init = (zero, zero) + ((jnp.zeros((8, S5_LB), F32),) * 2 if with_grad else ())
        fin = lax.fori_loop(0, ntile, step, init)
        if with_grad:
            da_ref[:, :S5_LB] = fin[2]
            da_ref[:, S5_LB:] = fin[3]
    xs = pl.BlockSpec((None, t_, 2 * S5_LB), lambda b, j: (b, 0, j))
    tb = pl.BlockSpec((8, 2 * S5_LB), lambda b, j: (0, j))
    if with_grad:
        return pl.pallas_call(body, name="s5_scan_bwd%d" % direction, grid=(b_, nblk), in_specs=[xs, tb, xs],
                              out_specs=[xs, pl.BlockSpec((None, 8, 2 * S5_LB), lambda b, j: (b, 0, j))],
                              out_shape=[_sds(bu.shape, F32), _sds((b_, 8, n2), F32)],
                              compiler_params=_cparams(("arbitrary", "arbitrary")))(bu, tab, h)
    return pl.pallas_call(body, name="s5_scan_fwd%d" % direction, grid=(b_, nblk), in_specs=[xs, tb], out_specs=xs,
                          out_shape=_sds(bu.shape, F32), compiler_params=_cparams(("arbitrary", "arbitrary")))(bu, tab)


@functools.lru_cache(maxsize=None)
def _s5_scan_fn(direction):
    @jax.custom_vjp
    def scan(bu, ar, ai):
        return _s5_scan_call(bu, ar, ai, direction)

    def fwd(bu, ar, ai):
        h = _s5_scan_call(bu, ar, ai, direction)
        return h, (h, ar, ai)

    def bwd(res, dh):
        h, ar, ai = res
        lam, da = _s5_scan_call(dh, ar, -ai, direction, h=h)
        nblk = da.shape[-1] // (2 * S5_LB)
        da = jnp.sum(da, axis=(0, 1)).reshape(nblk, 2, S5_LB)
        return lam, da[:, 0].reshape(1, -1), da[:, 1].reshape(1, -1)

    scan.defvjp(fwd, bwd)
    return scan


def _s5_post(u, y0, y1, d_row, glu_w, glu_b_row):
    def f(u, y0, y1, d, gw, gb):
        zz = _gelu_tanh(d * u + y0 + y1)
        return (zz * _sigmoid(_mm(zz, gw) + gb),)
    op = _block_op("s5_post", f, _tok_grid(),
                   [_tok(S5_W), _tok(S5_W), _tok(S5_W), _row(S5_W), _const2((S5_W, S5_W)), _row(S5_W)],
                   [_tok(S5_W)], [_sds(u.shape, BF16)], [True] * 6, acc=[None, None, None, None, 'all', None])
    return op(u, y0, y1, d_row, glu_w, glu_b_row)[0]


def _loss_rows(x, target):
    def f(x, t):
        e = x - t
        return (jnp.sum(e * e, axis=0, keepdims=True),)
    grid = (B_LOC, T_LAT // TT)
    op = _block_op("loss_rows", f, grid, [_tok(D), _tok(D)], [_row(D)], [_sds((B_LOC, T_LAT // TT, 1, D), F32)],
                   [True, False])
    return op(x, target)[0]


def _adamw_call(name, w, g, m, v, pieces):
    r, c = w.shape
    tr = _pick(r, (256, 128, 64, 32, 16, 8))
    c1, c2 = 1.0 - ADAM_B1 ** ADAM_STEP, 1.0 - ADAM_B2 ** ADAM_STEP

    def body(w_ref, g_ref, m_ref, v_ref, go_ref, d_ref, mo_ref, vo_ref):
        if pieces:
            g = g_ref[0].astype(F32)
            for i in range(1, g_ref.shape[0]):
                g = g + g_ref[i].astype(F32)
        else:
            g = g_ref[...]
        m = ADAM_B1 * m_ref[...] + (1.0 - ADAM_B1) * g
        v = ADAM_B2 * v_ref[...] + (1.0 - ADAM_B2) * (g * g)
        go_ref[...] = g
        mo_ref[...] = m
        vo_ref[...] = v
        d_ref[...] = -ADAM_LR * ((m / c1) / (jnp.sqrt(v / c2) + ADAM_EPS) + ADAM_WD * w_ref[...])
    spec = pl.BlockSpec((tr, c), lambda i: (i, 0))
    gspec = pl.BlockSpec((g.shape[0], tr, c), lambda i: (0, i, 0)) if pieces else spec
    return pl.pallas_call(body, name=name, grid=(r // tr,), in_specs=[spec, gspec, spec, spec], out_specs=[spec] * 4,
                          out_shape=[_sds((r, c), F32)] * 4, compiler_params=_cparams(("arbitrary",)))(w, g, m, v)


def _sum_pieces(x, name):
    _, r, c = x.shape
    tr = _pick(r, (512, 256, 128, 64, 32, 16, 8))

    def body(x_ref, o_ref):
        acc = x_ref[0]
        for i in range(1, NDEV):
            acc = acc + x_ref[i]
        o_ref[...] = acc
    return pl.pallas_call(body, name=name, grid=(r // tr,), in_specs=[pl.BlockSpec((NDEV, tr, c), lambda i: (0, i, 0))],
                          out_specs=pl.BlockSpec((tr, c), lambda i: (i, 0)), out_shape=_sds((r, c), F32),
                          compiler_params=_cparams(("arbitrary",)))(x)


def _pack_flat(arrs, lanes, row_mult):
    flat = jnp.concatenate([a.reshape(-1).astype(F32) for a in arrs])
    n = flat.shape[0]
    rows = -(-n // lanes)
    rows = -(-rows // row_mult) * row_mult
    return jnp.pad(flat, (0, rows * lanes - n)).reshape(rows, lanes)


def _unpack_flat(packed, shapes, lead=()):
    flat = packed.reshape(lead + (-1,))
    out, off = [], 0
    for s in shapes:
        n = int(np.prod(s))
        out.append(flat[..., off:off + n].reshape(lead + tuple(s)))
        off += n
    return out


FS_ROWS = DFF // NDEV
WI_ROWS = IN_COLS // NDEV
WI_PAD = -(-WI_ROWS // 16) * 16
WO_ROWS = D // NDEV


def _pad_rows(blk, rows, axis):
    pad = [(0, 0)] * blk.ndim
    pad[axis] = (0, rows - blk.shape[axis])
    return jnp.pad(blk, pad)


def _pack_big_shards(t, dtype):
    parts = []
    for l in range(DEPTH):
        for n in ('ffn_w1', 'ffn_w3', 'ffn_w2'):
            for i in range(2):
                parts.append(t[n][l, i] if n == 'ffn_w2' else t[n][l, i].T)
        parts.append(_pad_rows(t['w_in'][l].T, WI_PAD, 0))
        parts.append(t['w_out'][l])
    return jnp.concatenate(parts, axis=0).astype(dtype)


def _unpack_big_shards(p):
    out = {n: [] for n in BIG}
    off = 0
    for l in range(DEPTH):
        for n in ('ffn_w1', 'ffn_w3', 'ffn_w2'):
            pair = []
            for i in range(2):
                blk = p[off:off + FS_ROWS]
                off += FS_ROWS
                pair.append(blk if n == 'ffn_w2' else blk.T)
            out[n].append(jnp.stack(pair))
        out['w_in'].append(p[off:off + WI_ROWS].T)
        off += WI_PAD
        out['w_out'].append(p[off:off + WO_ROWS])
        off += WO_ROWS
    return {n: jnp.stack(v) for n, v in out.items()}


def _full_from_gathered(g):
    layers, off = [], 0
    for l in range(DEPTH):
        lw = {}
        for n in ('ffn_w1', 'ffn_w3', 'ffn_w2'):
            pair = []
            for i in range(2):
                pair.append(g[:, off:off + FS_ROWS].reshape(DFF, D))
                off += FS_ROWS
            lw[n] = pair
        lw['w_in'] = g[:, off:off + WI_ROWS].reshape(IN_COLS, D)
        off += WI_PAD
        lw['w_out'] = g[:, off:off + WO_ROWS].reshape(D, D)
        off += WO_ROWS
        layers.append(lw)
    return layers


def _pieces_from_full(layers):
    parts = []
    for lw in layers:
        for n in ('ffn_w1', 'ffn_w3', 'ffn_w2'):
            for i in range(2):
                parts.append(lw[n][i].reshape(NDEV, FS_ROWS, D))
        parts.append(_pad_rows(lw['w_in'].reshape(NDEV, WI_ROWS, D), WI_PAD, 1))
        parts.append(lw['w_out'].reshape(NDEV, WO_ROWS, D))
    return jnp.concatenate(parts, axis=1)


def _flip_segments(a, axis):
    ctx, lat = lax.slice_in_dim(a, 0, T_CTX, axis=axis), lax.slice_in_dim(a, T_CTX, _ntok(), axis=axis)
    return jnp.concatenate([jnp.flip(ctx, axis), jnp.flip(lat, axis)], axis=axis)


def _rows_of(vec_ctx, vec_lat):
    w = vec_lat.shape[-1]
    ntc, ntl = T_CTX // TT, T_LAT // TT
    return jnp.concatenate([jnp.broadcast_to(vec_ctx[None, None, None, :], (B_LOC, ntc, 1, w)),
                            jnp.broadcast_to(vec_lat[:, None, None, :], (B_LOC, ntl, 1, w))], axis=1)


def _rows_const(vec):
    return jnp.broadcast_to(vec[None, None, None, :], (B_LOC, _ntok() // TT, 1, vec.shape[-1]))


def _ffn_sublayer(xt, mrow, w1, w3, w2, g, b):
    h = _modulate(xt, mrow[0], mrow[1]).reshape(-1, D)
    u = _ffn_up(h, w1, w3)
    y = _matmul("ffn_down", u, w2, F32).reshape(xt.shape)
    return _post_norm(xt, y, mrow[2], _rows_const(g), _rows_const(b), 0.5)


def _s5_discretize(lam_re, lam_im, log_dt, b_re, b_im):
    dt = jnp.exp(log_dt)[:, None]
    zr, zi = lam_re * dt, lam_im * dt
    er = jnp.exp(zr)
    lbr, lbi = er * jnp.cos(zi), er * jnp.sin(zi)
    dd = lam_re * lam_re + lam_im * lam_im
    qr = ((lbr - 1.0) * lam_re + lbi * lam_im) / dd
    qi = (lbi * lam_re - (lbr - 1.0) * lam_im) / dd
    bbr = qr[..., None] * b_re - qi[..., None] * b_im
    bbi = qr[..., None] * b_im + qi[..., None] * b_re
    return lbr, lbi, bbr, bbi


def _s5_cols(a):
    return a.reshape(a.shape[:-1] + (S5_N // S5_LB, S5_LB))


def _s5_group(su, p):
    b_, t_, _ = su.shape
    eye = jnp.eye(S5_G, dtype=F32)
    ys = []
    for d in range(2):
        lbr, lbi, bbr, bbi = _s5_discretize(p['s5_lam_re'][d], p['s5_lam_im'][d], p['s5_log_dt'][d],
                                            p['s5_b_re'][d], p['s5_b_im'][d])
        bre = jnp.einsum('gph,gk->ghkp', bbr, eye).reshape(S5_W, S5_N)
        bim = jnp.einsum('gph,gk->ghkp', bbi, eye).reshape(S5_W, S5_N)
        bmat = jnp.stack([_s5_cols(bre), _s5_cols(bim)], axis=2).reshape(S5_W, 2 * S5_N)
        cre = jnp.einsum('ghp,gk->kpgh', p['s5_c_re'][d], eye).reshape(S5_N, S5_W)
        cim = -jnp.einsum('ghp,gk->kpgh', p['s5_c_im'][d], eye).reshape(S5_N, S5_W)
        cmat = jnp.stack([cre.reshape(S5_N // S5_LB, S5_LB, S5_W), cim.reshape(S5_N // S5_LB, S5_LB, S5_W)],
                         axis=1).reshape(2 * S5_N, S5_W)
        bu = _matmul("s5_in", su.reshape(-1, S5_W), bmat, F32).reshape(b_, t_, 2 * S5_N)
        hs = _s5_scan_fn(d)(bu, lbr.reshape(1, S5_N), lbi.reshape(1, S5_N))
        ys.append(_matmul("s5_out", hs.reshape(-1, 2 * S5_N), cmat, F32).reshape(b_, t_, S5_W))
    return _s5_post(su, ys[0], ys[1], _rows_const(p['s5_d']), p['glu_w'], _rows_const(p['glu_b']))


def _gdn_group(qkv, z, ba, p):
    b_, t_, _ = qkv.shape
    nc = t_ // CHUNK
    qkvn = _gdn_pre(qkv, _pair_major(p['gdn_conv_w']))
    bg = _gdn_gates(ba, p['gdn_a_log'], p['gdn_dt_bias'])

    def gates(a):
        a = a.reshape(b_, t_, 2, GDN_H // 2, 2).transpose(0, 3, 4, 2, 1)
        return a.reshape(b_, GDN_H // 2, 2, 2, nc, 1, CHUNK)
    outs = _gdn_prep(qkvn, gates(bg[..., 12:24]), gates(bg[..., 0:12]))
    o = _gdn_scan(*outs).reshape(b_, GDN_H, 2, t_, HD)
    o = (o[:, :, 0] + o[:, :, 1]).transpose(0, 2, 1, 3).reshape(b_, t_, GDN_W)
    return _gdn_post(o, z, _rows_const(jnp.tile(p['gdn_norm_w'], GDN_H)))


def _att_group(aq, ak, av, p):
    b_, t_, _ = aq.shape
    qn = _att_pre(aq, _rows_const(jnp.tile(p['q_norm_w'], ATT_W // HD)), "att_pre_q")
    kn = _att_pre(ak, _rows_const(jnp.tile(p['k_norm_w'], ATT_KW // HD)), "att_pre_k")
    q = qn.reshape(b_, t_, ATT_HKV, ATT_G, HD).transpose(0, 2, 3, 1, 4)
    k = kn.reshape(b_, t_, ATT_HKV, HD).transpose(0, 2, 1, 3)
    v = av.reshape(b_, t_, ATT_HKV, HD).transpose(0, 2, 1, 3)
    o_lat = _attention(q[:, :, :, T_CTX:], k, v, "att_lat", ATT_TQ)
    o_ctx = _attention(q[:, :, :, :T_CTX], k[:, :, :T_CTX], v[:, :, :T_CTX], "att_ctx", T_CTX)
    o = jnp.concatenate([o_ctx, o_lat], axis=3)
    return o.transpose(0, 3, 1, 2, 4).reshape(b_, t_, ATT_W)


def _pair_major(w):
    lead = w.shape[:-1]
    return w.reshape(lead + (3, GDN_H // 2, 2 * HD)).swapaxes(-3, -2).reshape(lead + (GDN_QKV,))


def _permute_w_in(wt):
    qkv = wt[:GDN_QKV].reshape(3, GDN_H // 2, 2 * HD, D).swapaxes(0, 1).reshape(GDN_QKV, D)
    return jnp.concatenate([qkv, wt[GDN_QKV:1536], wt[1560:], wt[1536:1560],
                            jnp.zeros((IN_PAD - IN_COLS, D), wt.dtype)], axis=0)


PROJ_CUTS = (0, 1152, 1536, 1920, 2048, 2176, 2432, 2560)


@jax.custom_vjp
def _split_proj(proj):
    return tuple(proj[..., a:b] for a, b in zip(PROJ_CUTS[:-1], PROJ_CUTS[1:]))


_split_proj.defvjp(lambda proj: (_split_proj(proj), None), lambda _, d: (jnp.concatenate(d, axis=-1),))


def _mixer_sublayer(xt, mrow, lw, p, g, b):
    h = _modulate(xt, mrow[3], mrow[4]).reshape(-1, D)
    proj = _matmul_t("mix_in", h, _permute_w_in(lw['w_in']), F32).reshape(xt.shape[:2] + (IN_PAD,))
    qkv, z, aq, ak, av, su, ba = _split_proj(proj)
    o_gdn = _gdn_group(qkv, z, ba, p)
    o_att = _att_group(aq, ak, av, p)
    o_s5 = _s5_group(su, p)
    cat = jnp.concatenate([o_gdn, o_att.astype(BF16), o_s5], axis=-1).reshape(-1, D)
    y = _matmul("mix_out", cat, lw['w_out'], F32).reshape(xt.shape)
    return _post_norm(xt, y, mrow[5], _rows_const(g), _rows_const(b), 1.0)


def _local_loss(x, mod, modc, big, small, ctx, target):
    xt = jnp.concatenate([ctx, x], axis=1)
    for l in range(DEPTH):
        mrow = [_rows_of(modc[l, k * D:(k + 1) * D], mod[l, :, k * D:(k + 1) * D]) for k in range(N_MOD)]
        p = {n: small[n][l] for n in small}
        lw = big[l]
        xt = _ffn_sublayer(xt, mrow[0:3], lw['ffn_w1'][0], lw['ffn_w3'][0], lw['ffn_w2'][0], p['ln_g'][0], p['ln_b'][0])
        xt = _mixer_sublayer(xt, mrow, lw, p, p['ln_g'][1], p['ln_b'][1])
        xt = _ffn_sublayer(xt, mrow[6:9], lw['ffn_w1'][1], lw['ffn_w3'][1], lw['ffn_w2'][1], p['ln_g'][2], p['ln_b'][2])
    part = _loss_rows(xt[:, T_CTX:], target)
    return (0.5 / D) * jnp.sum(part)


def _silu_plain(x):
    return x * jax.nn.sigmoid(x)


def _dsilu_plain(x):
    s = jax.nn.sigmoid(x)
    return s * (1.0 + x * (1.0 - s))


def _small_shapes():
    return {'ln_g': (DEPTH, 3, D), 'ln_b': (DEPTH, 3, D), 'gdn_conv_w': (DEPTH, 5, GDN_QKV), 'glu_w': (DEPTH, S5_W, S5_W),
            'gdn_a_log': (DEPTH, 2, GDN_H), 'gdn_dt_bias': (DEPTH, 2, GDN_H), 'gdn_norm_w': (DEPTH, HD),
            'q_norm_w': (DEPTH, HD), 'k_norm_w': (DEPTH, HD), 's5_lam_re': (DEPTH, 2, S5_G, S5_P),
            's5_lam_im': (DEPTH, 2, S5_G, S5_P), 's5_log_dt': (DEPTH, 2, S5_G),
            's5_b_re': (DEPTH, 2, S5_G, S5_P, S5_H), 's5_b_im': (DEPTH, 2, S5_G, S5_P, S5_H),
            's5_c_re': (DEPTH, 2, S5_G, S5_H, S5_P), 's5_c_im': (DEPTH, 2, S5_G, S5_H, S5_P),
            's5_d': (DEPTH, S5_W), 'glu_b': (DEPTH, S5_W)}


def _gather_small_sharded(gathered, name):
    if name == 'glu_w':
        return gathered.transpose(1, 0, 2, 3).reshape(DEPTH, S5_W, S5_W)
    lead = gathered.shape[1:-1]
    return jnp.moveaxis(gathered, 0, -2).reshape(lead + (-1,))


def _my_small_shard(full, name, me):
    if name == 'glu_w':
        return lax.dynamic_slice_in_dim(full, me * (S5_W // NDEV), S5_W // NDEV, axis=1)
    n = full.shape[-1] // NDEV
    return lax.dynamic_slice_in_dim(full, me * n, n, axis=full.ndim - 1)


def kernel(x, c, ctx, c_ctx, w_ada, b_ada, ln_g, ln_b, ffn_w1, ffn_w3, ffn_w2, w_in, w_out, gdn_conv_w, gdn_a_log, gdn_dt_bias, gdn_norm_w, q_norm_w, k_norm_w, s5_lam_re, s5_lam_im, s5_log_dt, s5_b_re, s5_b_im, s5_c_re, s5_c_im, s5_d, glu_w, glu_b, loss_target, m_c_ctx, m_w_ada, m_b_ada, m_ln_g, m_ln_b, m_ffn_w1, m_ffn_w3, m_ffn_w2, m_w_in, m_w_out, m_gdn_conv_w, m_gdn_a_log, m_gdn_dt_bias, m_gdn_norm_w, m_q_norm_w, m_k_norm_w, m_s5_lam_re, m_s5_lam_im, m_s5_log_dt, m_s5_b_re, m_s5_b_im, m_s5_c_re, m_s5_c_im, m_s5_d, m_glu_w, m_glu_b, v_c_ctx, v_w_ada, v_b_ada, v_ln_g, v_ln_b, v_ffn_w1, v_ffn_w3, v_ffn_w2, v_w_in, v_w_out, v_gdn_conv_w, v_gdn_a_log, v_gdn_dt_bias, v_gdn_norm_w, v_q_norm_w, v_k_norm_w, v_s5_lam_re, v_s5_lam_im, v_s5_log_dt, v_s5_b_re, v_s5_b_im, v_s5_c_re, v_s5_c_im, v_s5_d, v_glu_w, v_glu_b):
    a = dict(locals())
    me = _my_index()
    ada_cols = N_MOD * D // NDEV

    sc = _silu_plain(a['c'])
    scc = _silu_plain(a['c_ctx'])
    small_in = [sc] + [a[n] for n in SMALL_SHARDED]
    got = _all_gather(_pack_flat(small_in, 128, 8), "gather_small")
    parts = _unpack_flat(got, [t.shape for t in small_in], lead=(NDEV,))
    sc_all = parts[0].reshape(NDEV * B_LOC, D)
    small = {n: _gather_small_sharded(parts[1 + i], n) for i, n in enumerate(SMALL_SHARDED)}
    for n in SMALL_REPL:
        small[n] = a[n]

    nb = NDEV * B_LOC
    rows_pad = 8
    sc_rows = jnp.concatenate([sc_all, scc[None], jnp.zeros((rows_pad - 1, D), F32)], axis=0)
    mod_part = jnp.stack([_mm_call("ada_fwd", sc_rows, a['w_ada'][l], "nn", F32) for l in range(DEPTH)])
    mod_all = _all_gather(mod_part.reshape(DEPTH * (nb + rows_pad), ada_cols), "gather_mod")
    mod_all = mod_all.reshape(NDEV, DEPTH, nb + rows_pad, ada_cols).transpose(1, 2, 0, 3).reshape(DEPTH, nb + rows_pad, N_MOD * D)
    mod_all = mod_all + a['b_ada'][:, None, :]
    mod = lax.dynamic_slice_in_dim(mod_all, me * B_LOC, B_LOC, axis=1)
    modc = mod_all[:, nb]

    big = _full_from_gathered(_all_gather(_pack_big_shards(a, BF16), "gather_weights"))

    loss_part, grads = jax.value_and_grad(_local_loss, argnums=(0, 1, 2, 3, 4))(
        a['x'], mod, modc, big, small, a['ctx'], a['loss_target'])
    gx, gmod, gmodc, gbig, gsmall = grads

    gm_rows = jnp.concatenate([gmod, gmodc[:, None], jnp.zeros((DEPTH, rows_pad - B_LOC - 1, N_MOD * D), F32)], axis=1)
    gm_all = _all_gather(gm_rows.reshape(DEPTH * rows_pad, N_MOD * D), "gather_dmod").reshape(NDEV, DEPTH, rows_pad, N_MOD * D)
    gm_all = gm_all.transpose(1, 0, 2, 3).reshape(DEPTH, NDEV * rows_pad, N_MOD * D)
    g_b_ada = jnp.sum(gm_all, axis=1)
    sc_dev = jnp.concatenate([sc_all.reshape(NDEV, B_LOC, D), jnp.broadcast_to(scc[None, None], (NDEV, 1, D)),
                              jnp.zeros((NDEV, rows_pad - B_LOC - 1, D), F32)], axis=1).reshape(NDEV * rows_pad, D)
    gm_mine = lax.dynamic_slice_in_dim(gm_all, me * ada_cols, ada_cols, axis=2)
    g_w_ada = jnp.stack([_mm_call("ada_dw", sc_dev, gm_mine[l], "tn", F32) for l in range(DEPTH)])
    gmc = gm_mine.reshape(DEPTH, NDEV, rows_pad, ada_cols)[:, :, B_LOC].sum(axis=1)
    gmc = jnp.concatenate([gmc[:, None], jnp.zeros((DEPTH, 7, ada_cols), F32)], axis=1)
    dscc_part = sum(_mm_call("ada_dx", gmc[l], a['w_ada'][l], "nt", F32)[0] for l in range(DEPTH))

    small_names = SMALL_SHARDED + SMALL_REPL
    sums_in = [loss_part.reshape(1), dscc_part] + [gsmall[n] for n in small_names]
    tot = _sum_pieces(_all_gather(_pack_flat(sums_in, 128, 512), "gather_sums"), "sum_small")
    tparts = _unpack_flat(tot, [t.shape for t in sums_in])
    loss = tparts[0].reshape(())
    g_c_ctx = tparts[1] * _dsilu_plain(a['c_ctx'])
    g = {'c_ctx': g_c_ctx, 'b_ada': g_b_ada, 'w_ada': g_w_ada}
    for i, n in enumerate(small_names):
        g[n] = _my_small_shard(tparts[2 + i], n, me) if n in SMALL_SHARDED else tparts[2 + i]

    pieces = _pieces_from_full(gbig)
    recv = _scatter_chip(_pair_sum(pieces, _scatter_pair(pieces, "scatter_pair"), "scatter_add"), "scatter_chip")
    gb, db, mb, vb = _adamw_call("adamw_big", _pack_big_shards(a, F32), recv,
                                 _pack_big_shards({n: a['m_' + n] for n in BIG}, F32),
                                 _pack_big_shards({n: a['v_' + n] for n in BIG}, F32), True)
    res = {'g': {}, 'd': {}, 'm': {}, 'v': {}}
    for key, packed in (('g', gb), ('d', db), ('m', mb), ('v', vb)):
        res[key].update(_unpack_big_shards(packed))

    shp = a['w_ada'].shape
    flat2 = lambda t: t.reshape(-1, shp[-1])
    ga, da, ma, va = _adamw_call("adamw_ada", flat2(a['w_ada']), flat2(g['w_ada']), flat2(a['m_w_ada']), flat2(a['v_w_ada']), False)
    for key, val in (('g', ga), ('d', da), ('m', ma), ('v', va)):
        res[key]['w_ada'] = val.reshape(shp)
    rest = [n for n in WEIGHTS if n not in BIG and n != 'w_ada']
    shapes = [a[n].shape for n in rest]
    pk = lambda d: _pack_flat([d[n] for n in rest], 128, 256)
    outs = _adamw_call("adamw_small", pk(a), pk(g), pk({n: a['m_' + n] for n in rest}), pk({n: a['v_' + n] for n in rest}), False)
    for key, val in zip(('g', 'd', 'm', 'v'), outs):
        for n, t in zip(rest, _unpack_flat(val, shapes)):
            res[key][n] = t

    out = [loss, gx]
    for key in ('g', 'd', 'm', 'v'):
        out += [res[key][n] for n in WEIGHTS]
    return tuple(out)
```
